```python
import math
import jax, jax.numpy as jnp
from jax import lax
import numpy as np

D_MODEL = 2048
BATCH = 2
SEQ = 16384
DEPTH = 1

HEAD_DIM_ATTN = 64
N_Q_ATTN = 16
N_KV_ATTN = 2
WINDOW = 128
ATTN_BLOCK = 128
D_ATTN = N_Q_ATTN * HEAD_DIM_ATTN
D_KV = N_KV_ATTN * HEAD_DIM_ATTN
N_RET = 8
HEAD_DIM_RET = 128
RET_CHUNK = 128
D_RET = N_RET * HEAD_DIM_RET
D_MIX = D_ATTN + D_RET
D_IN = D_ATTN + 2 * D_KV + 4 * D_RET
N_MEM = 256
N_MEM_HEADS = 4
HEAD_DIM_MEM = D_MODEL // N_MEM_HEADS
N_EXPERTS = 32
TOP_K = 4
D_EXPERT = D_MODEL
SWIGLU_LIMIT = 7.0
SWIGLU_ALPHA = 1.702
MOE_BLOCK = 256
LN_EPS = 1e-5
DN_ALPHA = (2 * DEPTH) ** 0.25
DN_BETA = (8 * DEPTH) ** -0.25
NEG_INF = -1e30

kernel_name = "hymba_swa_retention_memxattn_moe_deepnorm"


def _layer_norm(x, w, b):
    xf = x.astype(jnp.float32)
    mu = xf.mean(-1, keepdims=True)
    var = jnp.square(xf - mu).mean(-1, keepdims=True)
    y = (xf - mu) * lax.rsqrt(var + LN_EPS)
    return (y * w.astype(jnp.float32) + b.astype(jnp.float32)).astype(x.dtype)


def _rms_norm(x, w):
    xf = x.astype(jnp.float32)
    y = xf * lax.rsqrt(jnp.square(xf).mean(-1, keepdims=True) + LN_EPS)
    return (y * w.astype(jnp.float32)).astype(x.dtype)


def _alibi_slopes(n):
    return jnp.exp2(-8.0 * (jnp.arange(n, dtype=jnp.float32) + 1.0) / n)


def _sliding_window_attention(q, k, v, sinks):
    B, S, _, dh = q.shape
    nb = S // ATTN_BLOCK
    G = N_Q_ATTN // N_KV_ATTN
    qb = q.reshape(B, nb, ATTN_BLOCK, N_KV_ATTN, G, dh)
    kb = k.reshape(B, nb, ATTN_BLOCK, N_KV_ATTN, dh)
    vb = v.reshape(B, nb, ATTN_BLOCK, N_KV_ATTN, dh)
    pad = ((0, 0), (1, 0), (0, 0), (0, 0), (0, 0))
    kw = jnp.concatenate([jnp.pad(kb, pad)[:, :-1], kb], axis=2)
    vw = jnp.concatenate([jnp.pad(vb, pad)[:, :-1], vb], axis=2)
    s = jnp.einsum('bnqhgd,bnkhd->bnhgqk', qb, kw,
                   preferred_element_type=jnp.float32) * (dh ** -0.5)
    qpos = jnp.arange(ATTN_BLOCK)[:, None] + ATTN_BLOCK
    kpos = jnp.arange(2 * ATTN_BLOCK)[None, :]
    dist = qpos - kpos
    blk = jnp.arange(nb)
    valid = ((dist >= 0) & (dist < WINDOW))[None] & \
        ((blk[:, None, None] > 0) | (kpos[None] >= ATTN_BLOCK))
    slopes = _alibi_slopes(N_Q_ATTN).reshape(N_KV_ATTN, G)
    s = s - slopes[:, :, None, None] * dist.astype(jnp.float32)
    s = jnp.where(valid[None, :, None, None], s, NEG_INF)
    sink = sinks.astype(jnp.float32).reshape(N_KV_ATTN, G)[None, None, :, :, None, None]
    m = jnp.maximum(s.max(-1, keepdims=True), sink)
    p = jnp.exp(s - m)
    p = p / (p.sum(-1, keepdims=True) + jnp.exp(sink - m))
    o = jnp.einsum('bnhgqk,bnkhd->bnqhgd', p.astype(v.dtype), vw)
    return o.reshape(B, S, N_Q_ATTN * dh)


def _retention(q, k, v):
    B, S, H, d = q.shape
    C = RET_CHUNK
    nc = S // C
    log_g = jnp.log1p(-jnp.exp2(-5.0 - jnp.arange(H, dtype=jnp.float32)))
    qc = q.reshape(B, nc, C, H, d).astype(jnp.float32)
    kc = k.reshape(B, nc, C, H, d).astype(jnp.float32) * (d ** -0.5)
    vc = v.reshape(B, nc, C, H, d).astype(jnp.float32)
    idx = jnp.arange(C, dtype=jnp.float32)
    diff = idx[:, None] - idx[None, :]
    dmask = jnp.where(diff >= 0, jnp.exp(log_g[:, None, None] * jnp.maximum(diff, 0.0)), 0.0)
    s = jnp.einsum('bnqhd,bnkhd->bnhqk', qc, kc) * dmask[None, None]
    inner = jnp.einsum('bnhqk,bnkhe->bnqhe', s, vc)
    w_key = jnp.exp(log_g[None, :] * (C - 1.0 - idx)[:, None])
    U = jnp.einsum('bnkhd,bnkhe->bnhde', kc * w_key[None, None, :, :, None], vc)
    g_C = jnp.exp(log_g * C)[None, :, None, None]

    def step(R, U_c):
        return g_C * R + U_c, R

    R0 = jnp.zeros((B, H, d, d), jnp.float32)
    _, R_prev = lax.scan(step, R0, jnp.moveaxis(U, 1, 0))
    R_prev = jnp.moveaxis(R_prev, 0, 1)
    w_q = jnp.exp(log_g[None, :] * (idx + 1.0)[:, None])
    cross = jnp.einsum('bnqhd,bnhde->bnqhe', qc * w_q[None, None, :, :, None], R_prev)
    return (inner + cross).reshape(B, S, H, d)


def _head_group_norm(o, w):
    B, S, H, d = o.shape
    mu = o.mean(-1, keepdims=True)
    var = jnp.square(o - mu).mean(-1, keepdims=True)
    y = (o - mu) * lax.rsqrt(var + LN_EPS)
    return y.reshape(B, S, H * d) * w.astype(jnp.float32)


def _hybrid_mixer(h, w_in, attn_sinks, attn_norm_w, ret_norm_w, w_out):
    B, S, _ = h.shape
    proj = jnp.einsum('bsd,de->bse', h, w_in)
    offs = [D_ATTN, D_ATTN + D_KV, D_ATTN + 2 * D_KV,
            D_ATTN + 2 * D_KV + D_RET, D_ATTN + 2 * D_KV + 2 * D_RET,
            D_ATTN + 2 * D_KV + 3 * D_RET]
    q_a, k_a, v_a, q_r, k_r, v_r, g_r = jnp.split(proj, offs, axis=-1)
    a = _sliding_window_attention(
        q_a.reshape(B, S, N_Q_ATTN, HEAD_DIM_ATTN),
        k_a.reshape(B, S, N_KV_ATTN, HEAD_DIM_ATTN),
        v_a.reshape(B, S, N_KV_ATTN, HEAD_DIM_ATTN), attn_sinks)
    a = _rms_norm(a, attn_norm_w)
    r = _retention(q_r.reshape(B, S, N_RET, HEAD_DIM_RET),
                   k_r.reshape(B, S, N_RET, HEAD_DIM_RET),
                   v_r.reshape(B, S, N_RET, HEAD_DIM_RET))
    r = (_head_group_norm(r, ret_norm_w) * jax.nn.silu(g_r.astype(jnp.float32))).astype(h.dtype)
    y = jnp.concatenate([a.astype(h.dtype), r], axis=-1)
    return jnp.einsum('bse,ed->bsd', y, w_out)


def _memory_attention(h, mem, w_q, w_kv, w_o):
    B, S, _ = h.shape
    M = mem.shape[1]
    q = jnp.einsum('bsd,de->bse', h, w_q).reshape(B, S, N_MEM_HEADS, HEAD_DIM_MEM)
    k, v = jnp.split(jnp.einsum('bmd,de->bme', mem, w_kv), 2, axis=-1)
    k = k.reshape(B, M, N_MEM_HEADS, HEAD_DIM_MEM)
    v = v.reshape(B, M, N_MEM_HEADS, HEAD_DIM_MEM)
    s = jnp.einsum('bshd,bmhd->bhsm', q, k, preferred_element_type=jnp.float32) * (HEAD_DIM_MEM ** -0.5)
    p = jax.nn.softmax(s, axis=-1)
    o = jnp.einsum('bhsm,bmhd->bshd', p.astype(v.dtype), v).reshape(B, S, D_MODEL)
    return jnp.einsum('bse,ed->bsd', o, w_o)


def _moe(h, w_router, b_router, w_gate_up, b_gate_up, w_down, b_down):
    B, S, D = h.shape
    T = B * S
    A = T * TOP_K
    xt = h.reshape(T, D)
    logits = (jnp.einsum('td,de->te', xt, w_router) + b_router).astype(jnp.float32)
    top_v, top_i = lax.top_k(logits, TOP_K)
    gates = jax.nn.softmax(top_v, axis=-1)
    e_flat = top_i.reshape(A).astype(jnp.int32)
    tok_flat = jnp.arange(A, dtype=jnp.int32) // TOP_K
    g_flat = gates.reshape(A)
    order = jnp.argsort(e_flat)
    e_s, tok_s, g_s = e_flat[order], tok_flat[order], g_flat[order]
    counts = jnp.zeros((N_EXPERTS,), jnp.int32).at[e_flat].add(1)
    start = jnp.cumsum(counts) - counts
    padded = ((counts + MOE_BLOCK - 1) // MOE_BLOCK) * MOE_BLOCK
    pad_end = jnp.cumsum(padded)
    pad_start = pad_end - padded
    dest = pad_start[e_s] + (jnp.arange(A, dtype=jnp.int32) - start[e_s])
    n_blocks = (A + N_EXPERTS * (MOE_BLOCK - 1) + MOE_BLOCK - 1) // MOE_BLOCK
    P = n_blocks * MOE_BLOCK
    slot_tok = jnp.zeros((P,), jnp.int32).at[dest].set(tok_s)
    slot_gate = jnp.zeros((P,), jnp.float32).at[dest].set(g_s)
    block_expert = jnp.minimum(
        jnp.searchsorted(pad_end, jnp.arange(n_blocks, dtype=jnp.int32) * MOE_BLOCK, side='right'),
        N_EXPERTS - 1).astype(jnp.int32)

    def body(acc, blk):
        tok, gate, e = blk
        xb = xt[tok]
        gu = xb @ w_gate_up[e] + b_gate_up[e]
        gl, up = jnp.split(gu, 2, axis=-1)
        gl = jnp.minimum(gl, SWIGLU_LIMIT)
        up = jnp.clip(up, -SWIGLU_LIMIT, SWIGLU_LIMIT)
        act = (up + 1.0) * (gl * jax.nn.sigmoid(SWIGLU_ALPHA * gl))
        yb = (act @ w_down[e] + b_down[e]).astype(jnp.float32)
        return acc.at[tok].add(yb * gate[:, None]), None

    acc0 = jnp.zeros((T, D), jnp.float32)
    out, _ = lax.scan(body, acc0, (slot_tok.reshape(n_blocks, MOE_BLOCK),
                                   slot_gate.reshape(n_blocks, MOE_BLOCK), block_expert))
    return out.reshape(B, S, D).astype(h.dtype)


def setup_inputs(seed: int = 0) -> dict:
    key = jax.random.key(seed)
    ks = jax.random.split(key, 22)
    f32 = jnp.float32
    L = DEPTH

    def nrm(k, shape, scale):
        return jax.random.normal(k, shape, f32) * scale

    return {
        "x": nrm(ks[0], (BATCH, SEQ, D_MODEL), 1.0),
        "mem": nrm(ks[1], (BATCH, N_MEM, D_MODEL), 1.0),
        "w_in": nrm(ks[2], (L, D_MODEL, D_IN), D_MODEL ** -0.5),
        "attn_sinks": nrm(ks[3], (L, N_Q_ATTN), 0.5),
        "attn_norm_w": 1.0 + nrm(ks[4], (L, D_ATTN), 0.02),
        "ret_norm_w": 1.0 + nrm(ks[5], (L, D_RET), 0.02),
        "w_mix_out": nrm(ks[6], (L, D_MIX, D_MODEL), DN_BETA * D_MIX ** -0.5),
        "ln_mix_w": 1.0 + nrm(ks[7], (L, D_MODEL), 0.02),
        "ln_mix_b": nrm(ks[8], (L, D_MODEL), 0.02),
        "w_mem_q": nrm(ks[9], (L, D_MODEL, D_MODEL), D_MODEL ** -0.5),
        "w_mem_kv": nrm(ks[10], (L, D_MODEL, 2 * D_MODEL), D_MODEL ** -0.5),
        "w_mem_out": nrm(ks[11], (L, D_MODEL, D_MODEL), DN_BETA * D_MODEL ** -0.5),
        "ln_mem_w": 1.0 + nrm(ks[12], (L, D_MODEL), 0.02),
        "ln_mem_b": nrm(ks[13], (L, D_MODEL), 0.02),
        "w_router": nrm(ks[14], (L, D_MODEL, N_EXPERTS), D_MODEL ** -0.5),
        "b_router": nrm(ks[15], (L, N_EXPERTS), 0.01),
        "w_gate_up": nrm(ks[16], (L, N_EXPERTS, D_MODEL, 2 * D_EXPERT), D_MODEL ** -0.5),
        "b_gate_up": nrm(ks[17], (L, N_EXPERTS, 2 * D_EXPERT), 0.01),
        "w_down": nrm(ks[18], (L, N_EXPERTS, D_EXPERT, D_MODEL), DN_BETA * D_EXPERT ** -0.5),
        "b_down": nrm(ks[19], (L, N_EXPERTS, D_MODEL), 0.01),
        "ln_moe_w": 1.0 + nrm(ks[20], (L, D_MODEL), 0.02),
        "ln_moe_b": nrm(ks[21], (L, D_MODEL), 0.02),
    }


def reference(x, mem, w_in, attn_sinks, attn_norm_w, ret_norm_w, w_mix_out, ln_mix_w, ln_mix_b,
              w_mem_q, w_mem_kv, w_mem_out, ln_mem_w, ln_mem_b, w_router, b_router,
              w_gate_up, b_gate_up, w_down, b_down, ln_moe_w, ln_moe_b):
    for l in range(DEPTH):
        h = _hybrid_mixer(x, w_in[l], attn_sinks[l], attn_norm_w[l], ret_norm_w[l], w_mix_out[l])
        x = _layer_norm(DN_ALPHA * x + h, ln_mix_w[l], ln_mix_b[l])
        h = _memory_attention(x, mem, w_mem_q[l], w_mem_kv[l], w_mem_out[l])
        x = _layer_norm(DN_ALPHA * x + h, ln_mem_w[l], ln_mem_b[l])
        h = _moe(x, w_router[l], b_router[l], w_gate_up[l], b_gate_up[l], w_down[l], b_down[l])
        x = _layer_norm(DN_ALPHA * x + h, ln_moe_w[l], ln_moe_b[l])
    return x
```

```python
import functools
import math

import jax
import jax.numpy as jnp
from jax import lax
from jax.experimental import pallas as pl
from jax.experimental.pallas import tpu as pltpu

F32 = jnp.float32
BF16 = jnp.bfloat16

D_MODEL = 2048
BATCH = 2
SEQ = 16384
TOKENS = BATCH * SEQ

HEAD_DIM_ATTN = 64
N_Q_ATTN = 16
N_KV_ATTN = 2
GQA_GROUP = N_Q_ATTN // N_KV_ATTN
ATTN_BLOCK = 128
D_ATTN = N_Q_ATTN * HEAD_DIM_ATTN
D_KV = N_KV_ATTN * HEAD_DIM_ATTN
N_RET = 8
HEAD_DIM_RET = 128
RET_CHUNK = 128
D_RET = N_RET * HEAD_DIM_RET
D_IN = D_ATTN + 2 * D_KV + 4 * D_RET
N_MEM = 256
N_MEM_HEADS = 4
HEAD_DIM_MEM = D_MODEL // N_MEM_HEADS
N_EXPERTS = 32
TOP_K = 4
D_EXPERT = D_MODEL
SWIGLU_LIMIT = 7.0
SWIGLU_ALPHA = 1.702
LN_EPS = 1e-5
DN_ALPHA = 2.0 ** 0.25
NEG_INF = -1e30

LANES = 128
VMEM_LIMIT = 56 * 1024 * 1024

PROJ_TM = 1024
PROJ_TN = 1792
LN_TM = 512
MEM_TM = 512
ROUTE_TM = 512
ROW_TILE = 256
MOE_TM = 1024
MOE_TF = 256
N_VISITS = (TOKENS * TOP_K + N_EXPERTS * (MOE_TM - 1) + MOE_TM - 1) // MOE_TM
MOE_ROWS = N_VISITS * MOE_TM

COL_QA, COL_QR, COL_KR, COL_VR, COL_GR = 0, 1, 2, 3, 4
COL_KA = 5 * D_RET // D_KV
COL_VA = COL_KA + 1


def _cparams(sem, **kw):
    return pltpu.CompilerParams(dimension_semantics=sem, vmem_limit_bytes=VMEM_LIMIT, **kw)


def _layer_norm(z, w, b):
    mu = jnp.mean(z, axis=-1, keepdims=True)
    zc = z - mu
    var = jnp.mean(zc * zc, axis=-1, keepdims=True)
    return zc * lax.rsqrt(var + LN_EPS) * w + b


def _dot_nt(a, b):
    return lax.dot_general(a, b, (((1,), (1,)), ((), ())), preferred_element_type=F32)


def _proj_kernel(x_ref, w_ref, o_ref, xb_ref):
    @pl.when(pl.program_id(1) == 0)
    def _():
        xb_ref[...] = x_ref[...].astype(BF16)

    o_ref[...] = jnp.dot(xb_ref[...], w_ref[...], preferred_element_type=F32).astype(o_ref.dtype)


def _project(x, w, tm, tn):
    m, k = x.shape
    n = w.shape[1]
    return pl.pallas_call(
        _proj_kernel,
        grid=(m // tm, n // tn),
        in_specs=[pl.BlockSpec((tm, k), lambda i, j: (i, 0)),
                  pl.BlockSpec((k, tn), lambda i, j: (0, j))],
        out_specs=pl.BlockSpec((tm, tn), lambda i, j: (i, j)),
        out_shape=jax.ShapeDtypeStruct((m, n), BF16),
        scratch_shapes=[pltpu.VMEM((tm, k), BF16)],
        compiler_params=_cparams(("parallel", "arbitrary")),
        name="project",
    )(x, w)


def _mixer_kernel(sink_ref, qa_ref, kp_ref, kc_ref, vp_ref, vc_ref, qr_ref, kr_ref, vr_ref, gr_ref,
                  anw_ref, rnw_ref, y_ref, state_ref, a_ref):
    n = pl.program_id(1)
    row = lax.broadcasted_iota(jnp.int32, (ATTN_BLOCK, ATTN_BLOCK), 0)
    col = lax.broadcasted_iota(jnp.int32, (ATTN_BLOCK, ATTN_BLOCK), 1)
    diff = row - col
    difff = diff.astype(F32)
    cur_valid = diff >= 0
    prev_valid = jnp.logical_and(diff < 0, n > 0)

    scale = HEAD_DIM_ATTN ** -0.5
    for h in range(N_Q_ATTN):
        j = h // GQA_GROUP
        lo, hi = h * HEAD_DIM_ATTN, (h + 1) * HEAD_DIM_ATTN
        klo, khi = j * HEAD_DIM_ATTN, (j + 1) * HEAD_DIM_ATTN
        slope = 2.0 ** (-8.0 * (h + 1) / N_Q_ATTN)
        sink = sink_ref[h]
        q = qa_ref[:, lo:hi]
        s_cur = _dot_nt(q, kc_ref[:, klo:khi]) * scale - slope * difff
        s_prev = _dot_nt(q, kp_ref[:, klo:khi]) * scale - slope * (difff + float(ATTN_BLOCK))
        s_cur = jnp.where(cur_valid, s_cur, NEG_INF)
        s_prev = jnp.where(prev_valid, s_prev, NEG_INF)
        m = jnp.maximum(jnp.max(s_cur, axis=-1, keepdims=True), jnp.max(s_prev, axis=-1, keepdims=True))
        m = jnp.maximum(m, sink)
        p_cur = jnp.exp(s_cur - m)
        p_prev = jnp.exp(s_prev - m)
        denom = (jnp.sum(p_cur, axis=-1, keepdims=True) + jnp.sum(p_prev, axis=-1, keepdims=True)
                 + jnp.exp(sink - m))
        o = (jnp.dot(p_cur.astype(BF16), vc_ref[:, klo:khi], preferred_element_type=F32)
             + jnp.dot(p_prev.astype(BF16), vp_ref[:, klo:khi], preferred_element_type=F32))
        a_ref[:, lo:hi] = o / denom
    a = a_ref[...]
    a = a * lax.rsqrt(jnp.mean(a * a, axis=-1, keepdims=True) + LN_EPS) * anw_ref[...]
    y_ref[:, :D_ATTN] = a.astype(y_ref.dtype)

    @pl.when(n == 0)
    def _():
        state_ref[...] = jnp.zeros_like(state_ref)

    pos = lax.broadcasted_iota(jnp.int32, (RET_CHUNK, 1), 0).astype(F32)
    kscale = HEAD_DIM_RET ** -0.5
    for h in range(N_RET):
        lo, hi = h * HEAD_DIM_RET, (h + 1) * HEAD_DIM_RET
        log_g = math.log1p(-(2.0 ** (-5.0 - h)))
        dmask = jnp.where(diff >= 0, jnp.exp(log_g * jnp.maximum(difff, 0.0)), 0.0)
        w_key = jnp.exp(log_g * (RET_CHUNK - 1.0 - pos))
        w_q = jnp.exp(log_g * (pos + 1.0))
        g_c = math.exp(log_g * RET_CHUNK)
        q = qr_ref[:, lo:hi]
        k = kr_ref[:, lo:hi]
        v = vr_ref[:, lo:hi]
        s = _dot_nt(q, k) * (dmask * kscale)
        inner = jnp.dot(s.astype(BF16), v, preferred_element_type=F32)
        state = state_ref[h]
        qw = (q.astype(F32) * w_q).astype(BF16)
        cross = jnp.dot(qw, state.astype(BF16), preferred_element_type=F32)
        kw_t = (k.astype(F32) * (w_key * kscale)).T.astype(BF16)
        state_ref[h] = g_c * state + jnp.dot(kw_t, v, preferred_element_type=F32)
        o = inner + cross
        mu = jnp.mean(o, axis=-1, keepdims=True)
        oc = o - mu
        var = jnp.mean(oc * oc, axis=-1, keepdims=True)
        o = oc * lax.rsqrt(var + LN_EPS) * rnw_ref[:, lo:hi]
        g = gr_ref[:, lo:hi].astype(F32)
        o = o * (g * jax.nn.sigmoid(g))
        y_ref[:, D_ATTN + lo:D_ATTN + hi] = o.astype(y_ref.dtype)


def _mixer(proj, sinks, attn_norm_w, ret_norm_w):
    nc = SEQ // ATTN_BLOCK
    blk = ATTN_BLOCK

    def wide(colblock):
        return pl.BlockSpec((blk, D_RET), lambda b, n: (b * nc + n, colblock))

    def kv_cur(colblock):
        return pl.BlockSpec((blk, D_KV), lambda b, n: (b * nc + n, colblock))

    def kv_prev(colblock):
        return pl.BlockSpec((blk, D_KV), lambda b, n: (b * nc + jnp.maximum(n - 1, 0), colblock))

    vec = pl.BlockSpec((1, D_RET), lambda b, n: (0, 0))
    return pl.pallas_call(
        _mixer_kernel,
        grid=(BATCH, nc),
        in_specs=[pl.BlockSpec(memory_space=pltpu.SMEM),
                  wide(COL_QA), kv_prev(COL_KA), kv_cur(COL_KA), kv_prev(COL_VA), kv_cur(COL_VA),
                  wide(COL_QR), wide(COL_KR), wide(COL_VR), wide(COL_GR), vec, vec],
        out_specs=pl.BlockSpec((blk, D_MODEL), lambda b, n: (b * nc + n, 0)),
        out_shape=jax.ShapeDtypeStruct((TOKENS, D_MODEL), BF16),
        scratch_shapes=[pltpu.VMEM((N_RET, HEAD_DIM_RET, HEAD_DIM_RET), F32),
                        pltpu.VMEM((blk, D_ATTN), F32)],
        compiler_params=_cparams(("parallel", "arbitrary")),
        name="mixer",
    )(sinks, proj, proj, proj, proj, proj, proj, proj, proj, proj, attn_norm_w, ret_norm_w)


def _out_ln_kernel(y_ref, w_ref, x_ref, lnw_ref, lnb_ref, o_ref):
    h = jnp.dot(y_ref[...], w_ref[...], preferred_element_type=F32)
    o_ref[...] = _layer_norm(DN_ALPHA * x_ref[...] + h, lnw_ref[...], lnb_ref[...])


def _out_ln(y, w, x, lnw, lnb):
    tm = LN_TM
    row = pl.BlockSpec((tm, D_MODEL), lambda i: (i, 0))
    vec = pl.BlockSpec((1, D_MODEL), lambda i: (0, 0))
    return pl.pallas_call(
        _out_ln_kernel,
        grid=(TOKENS // tm,),
        in_specs=[row,
                  pl.BlockSpec((D_MODEL, D_MODEL), lambda i: (0, 0), pipeline_mode=pl.Buffered(1)),
                  row, vec, vec],
        out_specs=row,
        out_shape=jax.ShapeDtypeStruct((TOKENS, D_MODEL), F32),
        compiler_params=_cparams(("parallel",)),
        name="out_ln",
    )(y, w, x, lnw, lnb)


def _mem_attn_kernel(x_ref, wq_ref, k_ref, v_ref, o_ref):
    q = jnp.dot(x_ref[...].astype(BF16), wq_ref[...], preferred_element_type=F32)
    q = (q * (HEAD_DIM_MEM ** -0.5)).astype(BF16)
    for h in range(N_MEM_HEADS):
        lo, hi = h * HEAD_DIM_MEM, (h + 1) * HEAD_DIM_MEM
        s = _dot_nt(q[:, lo:hi], k_ref[:, lo:hi])
        m = jnp.max(s, axis=-1, keepdims=True)
        p = jnp.exp(s - m)
        denom = jnp.sum(p, axis=-1, keepdims=True)
        o = jnp.dot(p.astype(BF16), v_ref[:, lo:hi], preferred_element_type=F32)
        o_ref[:, lo:hi] = (o / denom).astype(o_ref.dtype)


def _mem_attn(x, wq, kv):
    tm = MEM_TM
    nt = SEQ // tm
    row = pl.BlockSpec((tm, D_MODEL), lambda b, i: (b * nt + i, 0))
    return pl.pallas_call(
        _mem_attn_kernel,
        grid=(BATCH, nt),
        in_specs=[row,
                  pl.BlockSpec((D_MODEL, D_MODEL), lambda b, i: (0, 0), pipeline_mode=pl.Buffered(1)),
                  pl.BlockSpec((N_MEM, D_MODEL), lambda b, i: (b, 0)),
                  pl.BlockSpec((N_MEM, D_MODEL), lambda b, i: (b, 1))],
        out_specs=row,
        out_shape=jax.ShapeDtypeStruct((TOKENS, D_MODEL), BF16),
        compiler_params=_cparams(("parallel", "parallel")),
        name="mem_attn",
    )(x, wq, kv, kv)


def _route_kernel(x_ref, wr_ref, br_ref, idx_ref, gate_ref, rank_ref, count_ref, carry_ref):
    tm = x_ref.shape[0]

    @pl.when(pl.program_id(0) == 0)
    def _():
        carry_ref[...] = jnp.zeros_like(carry_ref)

    logits = jnp.dot(x_ref[...], wr_ref[...], preferred_element_type=F32,
                     precision=lax.Precision.HIGHEST) + br_ref[...]
    lane = lax.broadcasted_iota(jnp.int32, (tm, LANES), 1)
    lane_f = lane.astype(F32)
    work = logits
    sels, vals, idxs = [], [], []
    for _ in range(TOP_K):
        v = jnp.max(work, axis=-1, keepdims=True)
        i = jnp.min(jnp.where(work == v, lane_f, float(LANES)), axis=-1, keepdims=True)
        sel = lane_f == i
        work = jnp.where(sel, -jnp.inf, work)
        sels.append(sel)
        vals.append(v)
        idxs.append(i)
    exps = [jnp.exp(v - vals[0]) for v in vals]
    total = exps[0] + exps[1] + exps[2] + exps[3]

    onehot = jnp.zeros((tm, LANES), F32)
    for sel in sels:
        onehot = jnp.where(sel, 1.0, onehot)
    r = lax.broadcasted_iota(jnp.int32, (tm, tm), 0)
    c = lax.broadcasted_iota(jnp.int32, (tm, tm), 1)
    below = jnp.where(c < r, 1.0, 0.0).astype(BF16)
    prefix = jnp.dot(below, onehot.astype(BF16), preferred_element_type=F32)
    rankmat = prefix + carry_ref[...]
    carry_ref[...] = carry_ref[...] + jnp.sum(onehot, axis=0, keepdims=True)
    count_ref[...] = carry_ref[...]

    idx_out = jnp.zeros((tm, LANES), jnp.int32)
    gate_out = jnp.zeros((tm, LANES), F32)
    rank_out = jnp.zeros((tm, LANES), jnp.int32)
    for k in range(TOP_K):
        rk = jnp.sum(jnp.where(sels[k], rankmat, 0.0), axis=-1, keepdims=True).astype(jnp.int32)
        idx_out = jnp.where(lane == k, idxs[k].astype(jnp.int32), idx_out)
        gate_out = jnp.where(lane == k, exps[k] / total, gate_out)
        rank_out = jnp.where(lane == k, rk, rank_out)
    idx_ref[...] = idx_out
    gate_ref[...] = gate_out
    rank_ref[...] = rank_out


def _route(x, wr, br):
    tm = ROUTE_TM
    wide = pl.BlockSpec((tm, LANES), lambda i: (i, 0))
    return pl.pallas_call(
        _route_kernel,
        grid=(TOKENS // tm,),
        in_specs=[pl.BlockSpec((tm, D_MODEL), lambda i: (i, 0)),
                  pl.BlockSpec((D_MODEL, LANES), lambda i: (0, 0)),
                  pl.BlockSpec((1, LANES), lambda i: (0, 0))],
        out_specs=[wide, wide, wide, pl.BlockSpec((1, LANES), lambda i: (0, 0))],
        out_shape=[jax.ShapeDtypeStruct((TOKENS, LANES), jnp.int32),
                   jax.ShapeDtypeStruct((TOKENS, LANES), F32),
                   jax.ShapeDtypeStruct((TOKENS, LANES), jnp.int32),
                   jax.ShapeDtypeStruct((1, LANES), F32)],
        scratch_shapes=[pltpu.VMEM((1, LANES), F32)],
        compiler_params=_cparams(("arbitrary",)),
        name="route",
    )(x, wr, br)


def _row_copies(src_ref, dst_ref, sem):
    return pltpu.make_async_copy(src_ref.at[pl.ds(0, ROW_TILE), :], dst_ref.at[pl.ds(0, ROW_TILE), :], sem)


def _scatter_kernel(dest_ref, x_ref, xs_ref, sem):
    def body(r, carry):
        for k in range(TOP_K):
            d = dest_ref[0, 0, r * TOP_K + k]
            pltpu.make_async_copy(x_ref.at[pl.ds(r, 1), :], xs_ref.at[pl.ds(d, 1), :], sem).start()
        return carry

    lax.fori_loop(0, ROW_TILE, body, 0)
    for k in range(TOP_K):
        _row_copies(x_ref, xs_ref, sem).wait()


def _scatter_rows(x, dest):
    nt = TOKENS // ROW_TILE
    return pl.pallas_call(
        _scatter_kernel,
        grid=(nt,),
        in_specs=[pl.BlockSpec((1, 1, ROW_TILE * TOP_K), lambda i: (i, 0, 0), memory_space=pltpu.SMEM),
                  pl.BlockSpec((ROW_TILE, D_MODEL), lambda i: (i, 0))],
        out_specs=pl.BlockSpec(memory_space=pl.ANY),
        out_shape=jax.ShapeDtypeStruct((MOE_ROWS, D_MODEL), F32),
        scratch_shapes=[pltpu.SemaphoreType.DMA(())],
        compiler_params=_cparams(("arbitrary",), has_side_effects=True),
        name="scatter_rows",
    )(dest.reshape(nt, 1, ROW_TILE * TOP_K), x)


def _experts_kernel(ve_ref, vn_ref, xs_ref, wg_ref, wu_ref, bg_ref, bu_ref, wd_ref, bd_ref, ys_ref, xb_ref):
    v = pl.program_id(0)
    f = pl.program_id(1)
    nvalid = vn_ref[v]

    @pl.when(nvalid > 0)
    def _():
        @pl.when(f == 0)
        def _():
            rows = lax.broadcasted_iota(jnp.int32, (MOE_TM, 1), 0)
            xb_ref[...] = jnp.where(rows < nvalid, xs_ref[...], 0.0).astype(BF16)
            ys_ref[...] = jnp.broadcast_to(bd_ref[...], ys_ref.shape)

        xb = xb_ref[...]
        gl = jnp.dot(xb, wg_ref[...], preferred_element_type=F32) + bg_ref[...]
        up = jnp.dot(xb, wu_ref[...], preferred_element_type=F32) + bu_ref[...]
        gl = jnp.minimum(gl, SWIGLU_LIMIT)
        up = jnp.clip(up, -SWIGLU_LIMIT, SWIGLU_LIMIT)
        act = (up + 1.0) * (gl * jax.nn.sigmoid(SWIGLU_ALPHA * gl))
        ys_ref[...] += jnp.dot(act.astype(BF16), wd_ref[...], preferred_element_type=F32)


def _experts(xs, visit_expert, visit_nvalid, w_gate_up, b_gate_up, w_down, b_down):
    nf = D_EXPERT // MOE_TF

    def f_eff(v, f, vn):
        return jnp.where(vn[v] > 0, f, nf - 1)

    grid_spec = pltpu.PrefetchScalarGridSpec(
        num_scalar_prefetch=2,
        grid=(N_VISITS, nf),
        in_specs=[
            pl.BlockSpec((MOE_TM, D_MODEL), lambda v, f, ve, vn: (v, 0)),
            pl.BlockSpec((None, D_MODEL, MOE_TF), lambda v, f, ve, vn: (ve[v], 0, f_eff(v, f, vn))),
            pl.BlockSpec((None, D_MODEL, MOE_TF), lambda v, f, ve, vn: (ve[v], 0, nf + f_eff(v, f, vn))),
            pl.BlockSpec((None, 1, MOE_TF), lambda v, f, ve, vn: (ve[v], 0, f_eff(v, f, vn))),
            pl.BlockSpec((None, 1, MOE_TF), lambda v, f, ve, vn: (ve[v], 0, nf + f_eff(v, f, vn))),
            pl.BlockSpec((None, MOE_TF, D_MODEL), lambda v, f, ve, vn: (ve[v], f_eff(v, f, vn), 0)),
            pl.BlockSpec((None, 1, D_MODEL), lambda v, f, ve, vn: (ve[v], 0, 0)),
        ],
        out_specs=pl.BlockSpec((MOE_TM, D_MODEL), lambda v, f, ve, vn: (v, 0)),
        scratch_shapes=[pltpu.VMEM((MOE_TM, D_MODEL), BF16)],
    )
    return pl.pallas_call(
        _experts_kernel,
        grid_spec=grid_spec,
        out_shape=jax.ShapeDtypeStruct((MOE_ROWS, D_MODEL), F32),
        compiler_params=_cparams(("arbitrary", "arbitrary")),
        name="experts",
    )(visit_expert, visit_nvalid, xs, w_gate_up, w_gate_up, b_gate_up, b_gate_up, w_down, b_down)


def _combine_kernel(dest_ref, ys_ref, gate_ref, x_ref, lnw_ref, lnb_ref, o_ref, buf_ref, sem):
    def body(r, carry):
        for k in range(TOP_K):
            d = dest_ref[0, 0, r * TOP_K + k]
            pltpu.make_async_copy(ys_ref.at[pl.ds(d, 1), :], buf_ref.at[k, pl.ds(r, 1), :], sem).start()
        return carry

    lax.fori_loop(0, ROW_TILE, body, 0)
    for k in range(TOP_K):
        _row_copies(ys_ref, buf_ref.at[k], sem).wait()
    gates = gate_ref[...]
    h = gates[:, 0:1] * buf_ref[0]
    for k in range(1, TOP_K):
        h = h + gates[:, k:k + 1] * buf_ref[k]
    o_ref[...] = _layer_norm(DN_ALPHA * x_ref[...] + h, lnw_ref[...], lnb_ref[...])


def _combine(ys, dest, gates, x, lnw, lnb):
    nt = TOKENS // ROW_TILE
    row = pl.BlockSpec((ROW_TILE, D_MODEL), lambda i: (i, 0))
    vec = pl.BlockSpec((1, D_MODEL), lambda i: (0, 0))
    return pl.pallas_call(
        _combine_kernel,
        grid=(nt,),
        in_specs=[pl.BlockSpec((1, 1, ROW_TILE * TOP_K), lambda i: (i, 0, 0), memory_space=pltpu.SMEM),
                  pl.BlockSpec(memory_space=pl.ANY),
                  pl.BlockSpec((ROW_TILE, LANES), lambda i: (i, 0)),
                  row, vec, vec],
        out_specs=row,
        out_shape=jax.ShapeDtypeStruct((TOKENS, D_MODEL), F32),
        scratch_shapes=[pltpu.VMEM((TOP_K, ROW_TILE, D_MODEL), F32), pltpu.SemaphoreType.DMA(())],
        compiler_params=_cparams(("arbitrary",)),
        name="combine",
    )(dest.reshape(nt, 1, ROW_TILE * TOP_K), ys, gates, x, lnw, lnb)


def _layer(x, mem, w_in, attn_sinks, attn_norm_w, ret_norm_w, w_mix_out, ln_mix_w, ln_mix_b,
           w_mem_q, w_mem_kv, w_mem_out, ln_mem_w, ln_mem_b, w_router, b_router,
           w_gate_up, b_gate_up, w_down, b_down, ln_moe_w, ln_moe_b):
    o_k, o_v, o_qr = D_ATTN, D_ATTN + D_KV, D_ATTN + 2 * D_KV
    w_in_p = jnp.concatenate([w_in[:, :o_k], w_in[:, o_qr:], w_in[:, o_k:o_v], w_in[:, o_v:o_qr]],
                             axis=1).astype(BF16)
    proj = _project(x, w_in_p, PROJ_TM, PROJ_TN)
    y = _mixer(proj, attn_sinks.astype(F32), attn_norm_w.reshape(1, D_ATTN), ret_norm_w.reshape(1, D_RET))
    x1 = _out_ln(y, w_mix_out.astype(BF16), x, ln_mix_w.reshape(1, D_MODEL), ln_mix_b.reshape(1, D_MODEL))

    kv = _project(mem.reshape(BATCH * N_MEM, D_MODEL), w_mem_kv.astype(BF16), BATCH * N_MEM, D_MODEL)
    o = _mem_attn(x1, w_mem_q.astype(BF16), kv)
    x2 = _out_ln(o, w_mem_out.astype(BF16), x1, ln_mem_w.reshape(1, D_MODEL), ln_mem_b.reshape(1, D_MODEL))

    wr = jnp.pad(w_router, ((0, 0), (0, LANES - N_EXPERTS)))
    br = jnp.pad(b_router.reshape(1, N_EXPERTS), ((0, 0), (0, LANES - N_EXPERTS)), constant_values=NEG_INF)
    idx, gates, rank, counts = _route(x2, wr, br)

    counts = counts[0, :N_EXPERTS].astype(jnp.int32)
    padded = ((counts + MOE_TM - 1) // MOE_TM) * MOE_TM
    pad_end = jnp.cumsum(padded)
    pad_start = pad_end - padded
    dest = pad_start[idx[:, :TOP_K]] + rank[:, :TOP_K]
    visit_row = jnp.arange(N_VISITS, dtype=jnp.int32) * MOE_TM
    visit_expert = jnp.minimum(jnp.searchsorted(pad_end, visit_row, side='right'),
                               N_EXPERTS - 1).astype(jnp.int32)
    visit_nvalid = jnp.clip(counts[visit_expert] - (visit_row - pad_start[visit_expert]),
                            0, MOE_TM).astype(jnp.int32)

    xs = _scatter_rows(x2, dest)
    ys = _experts(xs, visit_expert, visit_nvalid,
                  w_gate_up.astype(BF16), b_gate_up.reshape(N_EXPERTS, 1, 2 * D_EXPERT),
                  w_down.astype(BF16), b_down.reshape(N_EXPERTS, 1, D_MODEL))
    return _combine(ys, dest, gates, x2, ln_moe_w.reshape(1, D_MODEL), ln_moe_b.reshape(1, D_MODEL))


def kernel(x, mem, w_in, attn_sinks, attn_norm_w, ret_norm_w, w_mix_out, ln_mix_w, ln_mix_b, w_mem_q, w_mem_kv, w_mem_out, ln_mem_w, ln_mem_b, w_router, b_router, w_gate_up, b_gate_up, w_down, b_down, ln_moe_w, ln_moe_b):
    xt = x.reshape(TOKENS, D_MODEL)
    out = _layer(xt, mem, w_in[0], attn_sinks[0], attn_norm_w[0], ret_norm_w[0], w_mix_out[0],
                 ln_mix_w[0], ln_mix_b[0], w_mem_q[0], w_mem_kv[0], w_mem_out[0], ln_mem_w[0], ln_mem_b[0],
                 w_router[0], b_router[0], w_gate_up[0], b_gate_up[0], w_down[0], b_down[0],
                 ln_moe_w[0], ln_moe_b[0])
    return out.reshape(BATCH, SEQ, D_MODEL)
```

```python
import functools
import math

import jax
import jax.numpy as jnp
from jax import lax
from jax.experimental import pallas as pl
from jax.experimental.pallas import tpu as pltpu

F32 = jnp.float32
BF16 = jnp.bfloat16

D_MODEL = 2048
BATCH = 2
SEQ = 16384
TOKENS = BATCH * SEQ

HEAD_DIM_ATTN = 64
N_Q_ATTN = 16
N_KV_ATTN = 2
GQA_GROUP = N_Q_ATTN // N_KV_ATTN
ATTN_BLOCK = 128
D_ATTN = N_Q_ATTN * HEAD_DIM_ATTN
D_KV = N_KV_ATTN * HEAD_DIM_ATTN
N_RET = 8
HEAD_DIM_RET = 128
RET_CHUNK = 128
D_RET = N_RET * HEAD_DIM_RET
D_IN = D_ATTN + 2 * D_KV + 4 * D_RET
N_MEM = 256
N_MEM_HEADS = 4
HEAD_DIM_MEM = D_MODEL // N_MEM_HEADS
N_EXPERTS = 32
TOP_K = 4
D_EXPERT = D_MODEL
SWIGLU_LIMIT = 7.0
SWIGLU_ALPHA = 1.702
LN_EPS = 1e-5
DN_ALPHA = 2.0 ** 0.25
NEG_INF = -1e30

LANES = 128
VMEM_LIMIT = 56 * 1024 * 1024

PROJ_TM = 1024
PROJ_TN = 1792
LN_TM = 512
MEM_TM = 512
ROUTE_TM = 512
ROW_TILE = 256
MOE_TM = 1024
MOE_TF = 256
N_VISITS = (TOKENS * TOP_K + N_EXPERTS * (MOE_TM - 1) + MOE_TM - 1) // MOE_TM
MOE_ROWS = N_VISITS * MOE_TM

COL_QA, COL_QR, COL_KR, COL_VR, COL_GR = 0, 1, 2, 3, 4
COL_KA = 5 * D_RET // D_KV
COL_VA = COL_KA + 1


def _cparams(sem, **kw):
    return pltpu.CompilerParams(dimension_semantics=sem, vmem_limit_bytes=VMEM_LIMIT, **kw)


def _layer_norm(z, w, b):
    mu = jnp.mean(z, axis=-1, keepdims=True)
    zc = z - mu
    var = jnp.mean(zc * zc, axis=-1, keepdims=True)
    return zc * lax.rsqrt(var + LN_EPS) * w + b


def _dot_nt(a, b):
    return lax.dot_general(a, b, (((1,), (1,)), ((), ())), preferred_element_type=F32)


def _proj_kernel(x_ref, w_ref, o_ref, xb_ref):
    @pl.when(pl.program_id(1) == 0)
    def _():
        xb_ref[...] = x_ref[...].astype(BF16)

    o_ref[...] = jnp.dot(xb_ref[...], w_ref[...], preferred_element_type=F32).astype(o_ref.dtype)


def _project(x, w, tm, tn):
    m, k = x.shape
    n = w.shape[1]
    return pl.pallas_call(
        _proj_kernel,
        grid=(m // tm, n // tn),
        in_specs=[pl.BlockSpec((tm, k), lambda i, j: (i, 0)),
                  pl.BlockSpec((k, tn), lambda i, j: (0, j))],
        out_specs=pl.BlockSpec((tm, tn), lambda i, j: (i, j)),
        out_shape=jax.ShapeDtypeStruct((m, n), BF16),
        scratch_shapes=[pltpu.VMEM((tm, k), BF16)],
        compiler_params=_cparams(("parallel", "arbitrary")),
        name="project",
    )(x, w)


def _mixer_kernel(sinkrow_ref, bias_cur_ref, bias_prev_ref, qa_ref, kp_ref, kc_ref, vp_ref, vc_ref,
                  qr_ref, kr_ref, vr_ref, gr_ref, anw_ref, rnw_ref, y_ref, state_ref, a_ref):
    n = pl.program_id(1)
    row = lax.broadcasted_iota(jnp.int32, (ATTN_BLOCK, ATTN_BLOCK), 0)
    col = lax.broadcasted_iota(jnp.int32, (ATTN_BLOCK, ATTN_BLOCK), 1)
    diff = row - col
    difff = diff.astype(F32)

    scale = HEAD_DIM_ATTN ** -0.5
    for j in range(N_KV_ATTN):
        klo, khi = j * HEAD_DIM_ATTN, (j + 1) * HEAD_DIM_ATTN
        heads = range(j * GQA_GROUP, (j + 1) * GQA_GROUP)
        q_rows = jnp.concatenate([qa_ref[:, h * HEAD_DIM_ATTN:(h + 1) * HEAD_DIM_ATTN] for h in heads], axis=0)
        sink = sinkrow_ref[j]
        s_cur = _dot_nt(kc_ref[:, klo:khi], q_rows) * scale + bias_cur_ref[j]
        s_prev = _dot_nt(kp_ref[:, klo:khi], q_rows) * scale + bias_prev_ref[j]
        s_prev = jnp.where(n > 0, s_prev, NEG_INF)
        m = jnp.maximum(jnp.max(s_cur, axis=0, keepdims=True), jnp.max(s_prev, axis=0, keepdims=True))
        m = jnp.maximum(m, sink)
        p_cur = jnp.exp(s_cur - m)
        p_prev = jnp.exp(s_prev - m)
        denom = (jnp.sum(p_cur, axis=0, keepdims=True) + jnp.sum(p_prev, axis=0, keepdims=True)
                 + jnp.exp(sink - m))
        v_cur_t = vc_ref[:, klo:khi].astype(F32).T.astype(BF16)
        v_prev_t = vp_ref[:, klo:khi].astype(F32).T.astype(BF16)
        o_t = (jnp.dot(v_cur_t, p_cur.astype(BF16), preferred_element_type=F32)
               + jnp.dot(v_prev_t, p_prev.astype(BF16), preferred_element_type=F32))
        o_t = o_t * (1.0 / denom)
        for g, h in enumerate(heads):
            a_ref[h * HEAD_DIM_ATTN:(h + 1) * HEAD_DIM_ATTN, :] = o_t[:, g * ATTN_BLOCK:(g + 1) * ATTN_BLOCK]
    a_t = a_ref[...]
    a_t = a_t * lax.rsqrt(jnp.mean(a_t * a_t, axis=0, keepdims=True) + LN_EPS)
    y_ref[:, :D_ATTN] = (a_t.T * anw_ref[...]).astype(y_ref.dtype)

    @pl.when(n == 0)
    def _():
        state_ref[...] = jnp.zeros_like(state_ref)

    pos = lax.broadcasted_iota(jnp.int32, (RET_CHUNK, 1), 0).astype(F32)
    kscale = HEAD_DIM_RET ** -0.5
    for h in range(N_RET):
        lo, hi = h * HEAD_DIM_RET, (h + 1) * HEAD_DIM_RET
        log_g = math.log1p(-(2.0 ** (-5.0 - h)))
        dmask = jnp.where(diff >= 0, jnp.exp(log_g * jnp.maximum(difff, 0.0)), 0.0)
        w_key = jnp.exp(log_g * (RET_CHUNK - 1.0 - pos))
        w_q = jnp.exp(log_g * (pos + 1.0))
        g_c = math.exp(log_g * RET_CHUNK)
        q = qr_ref[:, lo:hi]
        k = kr_ref[:, lo:hi]
        v = vr_ref[:, lo:hi]
        s = _dot_nt(q, k) * (dmask * kscale)
        inner = jnp.dot(s.astype(BF16), v, preferred_element_type=F32)
        state = state_ref[h]
        qw = (q.astype(F32) * w_q).astype(BF16)
        cross = jnp.dot(qw, state.astype(BF16), preferred_element_type=F32)
        kw_t = (k.astype(F32) * (w_key * kscale)).T.astype(BF16)
        state_ref[h] = g_c * state + jnp.dot(kw_t, v, preferred_element_type=F32)
        o = inner + cross
        mu = jnp.mean(o, axis=-1, keepdims=True)
        oc = o - mu
        var = jnp.mean(oc * oc, axis=-1, keepdims=True)
        o = oc * lax.rsqrt(var + LN_EPS) * rnw_ref[:, lo:hi]
        g = gr_ref[:, lo:hi].astype(F32)
        o = o * (g * jax.nn.sigmoid(g))
        y_ref[:, D_ATTN + lo:D_ATTN + hi] = o.astype(y_ref.dtype)


def _mixer(proj, sinks, attn_norm_w, ret_norm_w):
    nc = SEQ // ATTN_BLOCK
    blk = ATTN_BLOCK

    def wide(colblock):
        return pl.BlockSpec((blk, D_RET), lambda b, n: (b * nc + n, colblock))

    def kv_cur(colblock):
        return pl.BlockSpec((blk, D_KV), lambda b, n: (b * nc + n, colblock))

    def kv_prev(colblock):
        return pl.BlockSpec((blk, D_KV), lambda b, n: (b * nc + jnp.maximum(n - 1, 0), colblock))

    vec = pl.BlockSpec((1, D_RET), lambda b, n: (0, 0))
    gq = GQA_GROUP * blk

    def table(rows):
        return pl.BlockSpec((N_KV_ATTN, rows, gq), lambda b, n: (0, 0, 0))

    slopes = jnp.exp2(-8.0 * (jnp.arange(N_Q_ATTN, dtype=F32) + 1.0) / N_Q_ATTN).reshape(N_KV_ATTN, 1, GQA_GROUP, 1)
    key = jnp.arange(blk, dtype=F32).reshape(1, blk, 1, 1)
    query = jnp.arange(blk, dtype=F32).reshape(1, 1, 1, blk)
    dist_cur = jnp.broadcast_to(query - key, (N_KV_ATTN, blk, GQA_GROUP, blk))
    bias_cur = jnp.where(dist_cur >= 0, -slopes * dist_cur, NEG_INF).reshape(N_KV_ATTN, blk, gq)
    bias_prev = jnp.where(dist_cur < 0, -slopes * (dist_cur + blk), NEG_INF).reshape(N_KV_ATTN, blk, gq)
    sink_rows = jnp.repeat(sinks.astype(F32).reshape(N_KV_ATTN, 1, GQA_GROUP), blk, axis=2)
    return pl.pallas_call(
        _mixer_kernel,
        grid=(BATCH, nc),
        in_specs=[table(1), table(blk), table(blk),
                  wide(COL_QA), kv_prev(COL_KA), kv_cur(COL_KA), kv_prev(COL_VA), kv_cur(COL_VA),
                  wide(COL_QR), wide(COL_KR), wide(COL_VR), wide(COL_GR), vec, vec],
        out_specs=pl.BlockSpec((blk, D_MODEL), lambda b, n: (b * nc + n, 0)),
        out_shape=jax.ShapeDtypeStruct((TOKENS, D_MODEL), BF16),
        scratch_shapes=[pltpu.VMEM((N_RET, HEAD_DIM_RET, HEAD_DIM_RET), F32),
                        pltpu.VMEM((D_ATTN, blk), F32)],
        compiler_params=_cparams(("parallel", "arbitrary")),
        name="mixer",
    )(sink_rows, bias_cur, bias_prev, proj, proj, proj, proj, proj, proj, proj, proj, proj,
      attn_norm_w, ret_norm_w)


def _out_ln_kernel(y_ref, w_ref, x_ref, lnw_ref, lnb_ref, o_ref):
    h = jnp.dot(y_ref[...], w_ref[...], preferred_element_type=F32)
    o_ref[...] = _layer_norm(DN_ALPHA * x_ref[...] + h, lnw_ref[...], lnb_ref[...])


def _out_ln(y, w, x, lnw, lnb):
    tm = LN_TM
    row = pl.BlockSpec((tm, D_MODEL), lambda i: (i, 0))
    vec = pl.BlockSpec((1, D_MODEL), lambda i: (0, 0))
    return pl.pallas_call(
        _out_ln_kernel,
        grid=(TOKENS // tm,),
        in_specs=[row,
                  pl.BlockSpec((D_MODEL, D_MODEL), lambda i: (0, 0), pipeline_mode=pl.Buffered(1)),
                  row, vec, vec],
        out_specs=row,
        out_shape=jax.ShapeDtypeStruct((TOKENS, D_MODEL), F32),
        compiler_params=_cparams(("parallel",)),
        name="out_ln",
    )(y, w, x, lnw, lnb)


def _mem_attn_kernel(x_ref, wq_ref, k_ref, v_ref, o_ref):
    q = jnp.dot(x_ref[...].astype(BF16), wq_ref[...], preferred_element_type=F32)
    q = (q * (HEAD_DIM_MEM ** -0.5)).astype(BF16)
    for h in range(N_MEM_HEADS):
        lo, hi = h * HEAD_DIM_MEM, (h + 1) * HEAD_DIM_MEM
        s = _dot_nt(q[:, lo:hi], k_ref[:, lo:hi])
        m = jnp.max(s, axis=-1, keepdims=True)
        p = jnp.exp(s - m)
        denom = jnp.sum(p, axis=-1, keepdims=True)
        o = jnp.dot(p.astype(BF16), v_ref[:, lo:hi], preferred_element_type=F32)
        o_ref[:, lo:hi] = (o / denom).astype(o_ref.dtype)


def _mem_attn(x, wq, kv):
    tm = MEM_TM
    nt = SEQ // tm
    row = pl.BlockSpec((tm, D_MODEL), lambda b, i: (b * nt + i, 0))
    return pl.pallas_call(
        _mem_attn_kernel,
        grid=(BATCH, nt),
        in_specs=[row,
                  pl.BlockSpec((D_MODEL, D_MODEL), lambda b, i: (0, 0), pipeline_mode=pl.Buffered(1)),
                  pl.BlockSpec((N_MEM, D_MODEL), lambda b, i: (b, 0)),
                  pl.BlockSpec((N_MEM, D_MODEL), lambda b, i: (b, 1))],
        out_specs=row,
        out_shape=jax.ShapeDtypeStruct((TOKENS, D_MODEL), BF16),
        compiler_params=_cparams(("parallel", "parallel")),
        name="mem_attn",
    )(x, wq, kv, kv)


def _route_kernel(x_ref, wr_ref, br_ref, idx_ref, gate_ref, rank_ref, count_ref, carry_ref):
    tm = x_ref.shape[0]

    @pl.when(pl.program_id(0) == 0)
    def _():
        carry_ref[...] = jnp.zeros_like(carry_ref)

    logits = jnp.dot(x_ref[...], wr_ref[...], preferred_element_type=F32,
                     precision=lax.Precision.HIGHEST) + br_ref[...]
    lane = lax.broadcasted_iota(jnp.int32, (tm, LANES), 1)
    lane_f = lane.astype(F32)
    work = logits
    sels, vals, idxs = [], [], []
    for _ in range(TOP_K):
        v = jnp.max(work, axis=-1, keepdims=True)
        i = jnp.min(jnp.where(work == v, lane_f, float(LANES)), axis=-1, keepdims=True)
        sel = lane_f == i
        work = jnp.where(sel, -jnp.inf, work)
        sels.append(sel)
        vals.append(v)
        idxs.append(i)
    exps = [jnp.exp(v - vals[0]) for v in vals]
    total = exps[0] + exps[1] + exps[2] + exps[3]

    onehot = jnp.zeros((tm, LANES), F32)
    for sel in sels:
        onehot = jnp.where(sel, 1.0, onehot)
    r = lax.broadcasted_iota(jnp.int32, (tm, tm), 0)
    c = lax.broadcasted_iota(jnp.int32, (tm, tm), 1)
    below = jnp.where(c < r, 1.0, 0.0).astype(BF16)
    prefix = jnp.dot(below, onehot.astype(BF16), preferred_element_type=F32)
    rankmat = prefix + carry_ref[...]
    carry_ref[...] = carry_ref[...] + jnp.sum(onehot, axis=0, keepdims=True)
    count_ref[...] = carry_ref[...]

    idx_out = jnp.zeros((tm, LANES), jnp.int32)
    gate_out = jnp.zeros((tm, LANES), F32)
    rank_out = jnp.zeros((tm, LANES), jnp.int32)
    for k in range(TOP_K):
        rk = jnp.sum(jnp.where(sels[k], rankmat, 0.0), axis=-1, keepdims=True).astype(jnp.int32)
        idx_out = jnp.where(lane == k, idxs[k].astype(jnp.int32), idx_out)
        gate_out = jnp.where(lane == k, exps[k] / total, gate_out)
        rank_out = jnp.where(lane == k, rk, rank_out)
    idx_ref[...] = idx_out
    gate_ref[...] = gate_out
    rank_ref[...] = rank_out


def _route(x, wr, br):
    tm = ROUTE_TM
    wide = pl.BlockSpec((tm, LANES), lambda i: (i, 0))
    return pl.pallas_call(
        _route_kernel,
        grid=(TOKENS // tm,),
        in_specs=[pl.BlockSpec((tm, D_MODEL), lambda i: (i, 0)),
                  pl.BlockSpec((D_MODEL, LANES), lambda i: (0, 0)),
                  pl.BlockSpec((1, LANES), lambda i: (0, 0))],
        out_specs=[wide, wide, wide, pl.BlockSpec((1, LANES), lambda i: (0, 0))],
        out_shape=[jax.ShapeDtypeStruct((TOKENS, LANES), jnp.int32),
                   jax.ShapeDtypeStruct((TOKENS, LANES), F32),
                   jax.ShapeDtypeStruct((TOKENS, LANES), jnp.int32),
                   jax.ShapeDtypeStruct((1, LANES), F32)],
        scratch_shapes=[pltpu.VMEM((1, LANES), F32)],
        compiler_params=_cparams(("arbitrary",)),
        name="route",
    )(x, wr, br)


def _row_copies(src_ref, dst_ref, sem):
    return pltpu.make_async_copy(src_ref.at[pl.ds(0, ROW_TILE), :], dst_ref.at[pl.ds(0, ROW_TILE), :], sem)


def _scatter_kernel(dest_ref, x_ref, xs_ref, sem):
    def body(r, carry):
        for k in range(TOP_K):
            d = dest_ref[0, 0, r * TOP_K + k]
            pltpu.make_async_copy(x_ref.at[pl.ds(r, 1), :], xs_ref.at[pl.ds(d, 1), :], sem).start()
        return carry

    lax.fori_loop(0, ROW_TILE, body, 0)
    for k in range(TOP_K):
        _row_copies(x_ref, xs_ref, sem).wait()


def _scatter_rows(x, dest):
    nt = TOKENS // ROW_TILE
    return pl.pallas_call(
        _scatter_kernel,
        grid=(nt,),
        in_specs=[pl.BlockSpec((1, 1, ROW_TILE * TOP_K), lambda i: (i, 0, 0), memory_space=pltpu.SMEM),
                  pl.BlockSpec((ROW_TILE, D_MODEL), lambda i: (i, 0))],
        out_specs=pl.BlockSpec(memory_space=pl.ANY),
        out_shape=jax.ShapeDtypeStruct((MOE_ROWS, D_MODEL), F32),
        scratch_shapes=[pltpu.SemaphoreType.DMA(())],
        compiler_params=_cparams(("arbitrary",), has_side_effects=True),
        name="scatter_rows",
    )(dest.reshape(nt, 1, ROW_TILE * TOP_K), x)


def _experts_kernel(ve_ref, vn_ref, xs_ref, wg_ref, wu_ref, bg_ref, bu_ref, wd_ref, bd_ref, ys_ref, xb_ref):
    v = pl.program_id(0)
    f = pl.program_id(1)
    nvalid = vn_ref[v]

    @pl.when(nvalid > 0)
    def _():
        @pl.when(f == 0)
        def _():
            rows = lax.broadcasted_iota(jnp.int32, (MOE_TM, 1), 0)
            xb_ref[...] = jnp.where(rows < nvalid, xs_ref[...], 0.0).astype(BF16)
            ys_ref[...] = jnp.broadcast_to(bd_ref[...], ys_ref.shape)

        xb = xb_ref[...]
        gl = jnp.dot(xb, wg_ref[...], preferred_element_type=F32) + bg_ref[...]
        up = jnp.dot(xb, wu_ref[...], preferred_element_type=F32) + bu_ref[...]
        gl = jnp.minimum(gl, SWIGLU_LIMIT)
        up = jnp.clip(up, -SWIGLU_LIMIT, SWIGLU_LIMIT)
        act = (up + 1.0) * (gl * jax.nn.sigmoid(SWIGLU_ALPHA * gl))
        ys_ref[...] += jnp.dot(act.astype(BF16), wd_ref[...], preferred_element_type=F32)


def _experts(xs, visit_expert, visit_nvalid, w_gate_up, b_gate_up, w_down, b_down):
    nf = D_EXPERT // MOE_TF

    def f_eff(v, f, vn):
        return jnp.where(vn[v] > 0, f, nf - 1)

    grid_spec = pltpu.PrefetchScalarGridSpec(
        num_scalar_prefetch=2,
        grid=(N_VISITS, nf),
        in_specs=[
            pl.BlockSpec((MOE_TM, D_MODEL), lambda v, f, ve, vn: (v, 0)),
            pl.BlockSpec((None, D_MODEL, MOE_TF), lambda v, f, ve, vn: (ve[v], 0, f_eff(v, f, vn))),
            pl.BlockSpec((None, D_MODEL, MOE_TF), lambda v, f, ve, vn: (ve[v], 0, nf + f_eff(v, f, vn))),
            pl.BlockSpec((None, 1, MOE_TF), lambda v, f, ve, vn: (ve[v], 0, f_eff(v, f, vn))),
            pl.BlockSpec((None, 1, MOE_TF), lambda v, f, ve, vn: (ve[v], 0, nf + f_eff(v, f, vn))),
            pl.BlockSpec((None, MOE_TF, D_MODEL), lambda v, f, ve, vn: (ve[v], f_eff(v, f, vn), 0)),
            pl.BlockSpec((None, 1, D_MODEL), lambda v, f, ve, vn: (ve[v], 0, 0)),
        ],
        out_specs=pl.BlockSpec((MOE_TM, D_MODEL), lambda v, f, ve, vn: (v, 0)),
        scratch_shapes=[pltpu.VMEM((MOE_TM, D_MODEL), BF16)],
    )
    return pl.pallas_call(
        _experts_kernel,
        grid_spec=grid_spec,
        out_shape=jax.ShapeDtypeStruct((MOE_ROWS, D_MODEL), F32),
        compiler_params=_cparams(("arbitrary", "arbitrary")),
        name="experts",
    )(visit_expert, visit_nvalid, xs, w_gate_up, w_gate_up, b_gate_up, b_gate_up, w_down, b_down)


def _combine_kernel(dest_ref, ys_ref, gate_ref, x_ref, lnw_ref, lnb_ref, o_ref, buf_ref, sem):
    def body(r, carry):
        for k in range(TOP_K):
            d = dest_ref[0, 0, r * TOP_K + k]
            pltpu.make_async_copy(ys_ref.at[pl.ds(d, 1), :], buf_ref.at[k, pl.ds(r, 1), :], sem).start()
        return carry

    lax.fori_loop(0, ROW_TILE, body, 0)
    for k in range(TOP_K):
        _row_copies(ys_ref, buf_ref.at[k], sem).wait()
    gates = gate_ref[...]
    h = gates[:, 0:1] * buf_ref[0]
    for k in range(1, TOP_K):
        h = h + gates[:, k:k + 1] * buf_ref[k]
    o_ref[...] = _layer_norm(DN_ALPHA * x_ref[...] + h, lnw_ref[...], lnb_ref[...])


def _combine(ys, dest, gates, x, lnw, lnb):
    nt = TOKENS // ROW_TILE
    row = pl.BlockSpec((ROW_TILE, D_MODEL), lambda i: (i, 0))
    vec = pl.BlockSpec((1, D_MODEL), lambda i: (0, 0))
    return pl.pallas_call(
        _combine_kernel,
        grid=(nt,),
        in_specs=[pl.BlockSpec((1, 1, ROW_TILE * TOP_K), lambda i: (i, 0, 0), memory_space=pltpu.SMEM),
                  pl.BlockSpec(memory_space=pl.ANY),
                  pl.BlockSpec((ROW_TILE, LANES), lambda i: (i, 0)),
                  row, vec, vec],
        out_specs=row,
        out_shape=jax.ShapeDtypeStruct((TOKENS, D_MODEL), F32),
        scratch_shapes=[pltpu.VMEM((TOP_K, ROW_TILE, D_MODEL), F32), pltpu.SemaphoreType.DMA(())],
        compiler_params=_cparams(("arbitrary",)),
        name="combine",
    )(dest.reshape(nt, 1, ROW_TILE * TOP_K), ys, gates, x, lnw, lnb)


def _layer(x, mem, w_in, attn_sinks, attn_norm_w, ret_norm_w, w_mix_out, ln_mix_w, ln_mix_b,
           w_mem_q, w_mem_kv, w_mem_out, ln_mem_w, ln_mem_b, w_router, b_router,
           w_gate_up, b_gate_up, w_down, b_down, ln_moe_w, ln_moe_b):
    o_k, o_v, o_qr = D_ATTN, D_ATTN + D_KV, D_ATTN + 2 * D_KV
    w_in_p = jnp.concatenate([w_in[:, :o_k], w_in[:, o_qr:], w_in[:, o_k:o_v], w_in[:, o_v:o_qr]],
                             axis=1).astype(BF16)
    proj = _project(x, w_in_p, PROJ_TM, PROJ_TN)
    y = _mixer(proj, attn_sinks.astype(F32), attn_norm_w.reshape(1, D_ATTN), ret_norm_w.reshape(1, D_RET))
    x1 = _out_ln(y, w_mix_out.astype(BF16), x, ln_mix_w.reshape(1, D_MODEL), ln_mix_b.reshape(1, D_MODEL))

    kv = _project(mem.reshape(BATCH * N_MEM, D_MODEL), w_mem_kv.astype(BF16), BATCH * N_MEM, D_MODEL)
    o = _mem_attn(x1, w_mem_q.astype(BF16), kv)
    x2 = _out_ln(o, w_mem_out.astype(BF16), x1, ln_mem_w.reshape(1, D_MODEL), ln_mem_b.reshape(1, D_MODEL))

    wr = jnp.pad(w_router, ((0, 0), (0, LANES - N_EXPERTS)))
    br = jnp.pad(b_router.reshape(1, N_EXPERTS), ((0, 0), (0, LANES - N_EXPERTS)), constant_values=NEG_INF)
    idx, gates, rank, counts = _route(x2, wr, br)

    counts = counts[0, :N_EXPERTS].astype(jnp.int32)
    padded = ((counts + MOE_TM - 1) // MOE_TM) * MOE_TM
    eid = jnp.arange(N_EXPERTS, dtype=jnp.int32)
    pad_end = jnp.sum(jnp.where(eid[None, :] <= eid[:, None], padded[None, :], 0), axis=1)
    pad_start = pad_end - padded
    dest = pad_start[idx[:, :TOP_K]] + rank[:, :TOP_K]
    visit_row = jnp.arange(N_VISITS, dtype=jnp.int32) * MOE_TM
    visit_expert = jnp.minimum(jnp.sum((pad_end[None, :] <= visit_row[:, None]).astype(jnp.int32), axis=1),
                               N_EXPERTS - 1)
    visit_nvalid = jnp.clip(counts[visit_expert] - (visit_row - pad_start[visit_expert]),
                            0, MOE_TM).astype(jnp.int32)

    xs = _scatter_rows(x2, dest)
    ys = _experts(xs, visit_expert, visit_nvalid,
                  w_gate_up.astype(BF16), b_gate_up.reshape(N_EXPERTS, 1, 2 * D_EXPERT),
                  w_down.astype(BF16), b_down.reshape(N_EXPERTS, 1, D_MODEL))
    return _combine(ys, dest, gates, x2, ln_moe_w.reshape(1, D_MODEL), ln_moe_b.reshape(1, D_MODEL))


def kernel(x, mem, w_in, attn_sinks, attn_norm_w, ret_norm_w, w_mix_out, ln_mix_w, ln_mix_b, w_mem_q, w_mem_kv, w_mem_out, ln_mem_w, ln_mem_b, w_router, b_router, w_gate_up, b_gate_up, w_down, b_down, ln_moe_w, ln_moe_b):
    xt = x.reshape(TOKENS, D_MODEL)
    out = _layer(xt, mem, w_in[0], attn_sinks[0], attn_norm_w[0], ret_norm_w[0], w_mix_out[0],
                 ln_mix_w[0], ln_mix_b[0], w_mem_q[0], w_mem_kv[0], w_mem_out[0], ln_mem_w[0], ln_mem_b[0],
                 w_router[0], b_router[0], w_gate_up[0], b_gate_up[0], w_down[0], b_down[0],
                 ln_moe_w[0], ln_moe_b[0])
    return out.reshape(BATCH, SEQ, D_MODEL)
```

```python
import functools
import math

import jax
import jax.numpy as jnp
from jax import lax
from jax.experimental import pallas as pl
from jax.experimental.pallas import tpu as pltpu

F32 = jnp.float32
BF16 = jnp.bfloat16

D_MODEL = 2048
BATCH = 2
SEQ = 16384
TOKENS = BATCH * SEQ

HEAD_DIM_ATTN = 64
N_Q_ATTN = 16
N_KV_ATTN = 2
GQA_GROUP = N_Q_ATTN // N_KV_ATTN
ATTN_BLOCK = 128
D_ATTN = N_Q_ATTN * HEAD_DIM_ATTN
D_KV = N_KV_ATTN * HEAD_DIM_ATTN
N_RET = 8
HEAD_DIM_RET = 128
RET_CHUNK = 128
D_RET = N_RET * HEAD_DIM_RET
D_IN = D_ATTN + 2 * D_KV + 4 * D_RET
N_MEM = 256
N_MEM_HEADS = 4
HEAD_DIM_MEM = D_MODEL // N_MEM_HEADS
N_EXPERTS = 32
TOP_K = 4
D_EXPERT = D_MODEL
SWIGLU_LIMIT = 7.0
SWIGLU_ALPHA = 1.702
LN_EPS = 1e-5
DN_ALPHA = 2.0 ** 0.25
NEG_INF = -1e30

LANES = 128
VMEM_LIMIT = 56 * 1024 * 1024

PROJ_TM = 1024
PROJ_TN = 1792
LN_TM = 512
MEM_TM = 512
ROUTE_TM = 512
ROW_TILE = 256
MOE_TM = 1024
MOE_TF = 256
N_VISITS = (TOKENS * TOP_K + N_EXPERTS * (MOE_TM - 1) + MOE_TM - 1) // MOE_TM
MOE_ROWS = N_VISITS * MOE_TM

COL_QA, COL_QR, COL_KR, COL_VR, COL_GR = 0, 1, 2, 3, 4
COL_KA = 5 * D_RET // D_KV
COL_VA = COL_KA + 1


def _cparams(sem, **kw):
    return pltpu.CompilerParams(dimension_semantics=sem, vmem_limit_bytes=VMEM_LIMIT, **kw)


def _layer_norm(z, w, b):
    mu = jnp.mean(z, axis=-1, keepdims=True)
    zc = z - mu
    var = jnp.mean(zc * zc, axis=-1, keepdims=True)
    return zc * lax.rsqrt(var + LN_EPS) * w + b


def _dot_nt(a, b):
    return lax.dot_general(a, b, (((1,), (1,)), ((), ())), preferred_element_type=F32)


def _proj_kernel(x_ref, w_ref, o_ref, xb_ref):
    @pl.when(pl.program_id(1) == 0)
    def _():
        xb_ref[...] = x_ref[...].astype(BF16)

    o_ref[...] = jnp.dot(xb_ref[...], w_ref[...], preferred_element_type=F32).astype(o_ref.dtype)


def _project(x, w, tm, tn):
    m, k = x.shape
    n = w.shape[1]
    return pl.pallas_call(
        _proj_kernel,
        grid=(m // tm, n // tn),
        in_specs=[pl.BlockSpec((tm, k), lambda i, j: (i, 0)),
                  pl.BlockSpec((k, tn), lambda i, j: (0, j))],
        out_specs=pl.BlockSpec((tm, tn), lambda i, j: (i, j)),
        out_shape=jax.ShapeDtypeStruct((m, n), BF16),
        scratch_shapes=[pltpu.VMEM((tm, k), BF16)],
        compiler_params=_cparams(("parallel", "arbitrary")),
        name="project",
    )(x, w)


def _mixer_kernel(sinkrow_ref, bias_cur_ref, bias_prev_ref, qa_ref, kp_ref, kc_ref, vp_ref, vc_ref,
                  qr_ref, kr_ref, vr_ref, gr_ref, anw_ref, rnw_ref, y_ref, state_ref, a_ref):
    n = pl.program_id(1)
    row = lax.broadcasted_iota(jnp.int32, (ATTN_BLOCK, ATTN_BLOCK), 0)
    col = lax.broadcasted_iota(jnp.int32, (ATTN_BLOCK, ATTN_BLOCK), 1)
    diff = row - col
    difff = diff.astype(F32)

    scale = HEAD_DIM_ATTN ** -0.5
    for j in range(N_KV_ATTN):
        klo, khi = j * HEAD_DIM_ATTN, (j + 1) * HEAD_DIM_ATTN
        heads = range(j * GQA_GROUP, (j + 1) * GQA_GROUP)
        q_rows = jnp.concatenate([qa_ref[:, h * HEAD_DIM_ATTN:(h + 1) * HEAD_DIM_ATTN] for h in heads], axis=0)
        sink = sinkrow_ref[j]
        s_cur = _dot_nt(kc_ref[:, klo:khi], q_rows) * scale + bias_cur_ref[j]
        s_prev = _dot_nt(kp_ref[:, klo:khi], q_rows) * scale + bias_prev_ref[j]
        s_prev = jnp.where(n > 0, s_prev, NEG_INF)
        m = jnp.maximum(jnp.max(s_cur, axis=0, keepdims=True), jnp.max(s_prev, axis=0, keepdims=True))
        m = jnp.maximum(m, sink)
        p_cur = jnp.exp(s_cur - m)
        p_prev = jnp.exp(s_prev - m)
        denom = (jnp.sum(p_cur, axis=0, keepdims=True) + jnp.sum(p_prev, axis=0, keepdims=True)
                 + jnp.exp(sink - m))
        v_cur_t = vc_ref[:, klo:khi].astype(F32).T.astype(BF16)
        v_prev_t = vp_ref[:, klo:khi].astype(F32).T.astype(BF16)
        o_t = (jnp.dot(v_cur_t, p_cur.astype(BF16), preferred_element_type=F32)
               + jnp.dot(v_prev_t, p_prev.astype(BF16), preferred_element_type=F32))
        o_t = o_t * (1.0 / denom)
        for g, h in enumerate(heads):
            a_ref[h * HEAD_DIM_ATTN:(h + 1) * HEAD_DIM_ATTN, :] = o_t[:, g * ATTN_BLOCK:(g + 1) * ATTN_BLOCK]
    a_t = a_ref[...]
    a_t = a_t * lax.rsqrt(jnp.mean(a_t * a_t, axis=0, keepdims=True) + LN_EPS)
    y_ref[:, :D_ATTN] = (a_t.T * anw_ref[...]).astype(y_ref.dtype)

    @pl.when(n == 0)
    def _():
        state_ref[...] = jnp.zeros_like(state_ref)

    pos = lax.broadcasted_iota(jnp.int32, (RET_CHUNK, 1), 0).astype(F32)
    kscale = HEAD_DIM_RET ** -0.5
    for h in range(N_RET):
        lo, hi = h * HEAD_DIM_RET, (h + 1) * HEAD_DIM_RET
        log_g = math.log1p(-(2.0 ** (-5.0 - h)))
        dmask = jnp.where(diff >= 0, jnp.exp(log_g * jnp.maximum(difff, 0.0)), 0.0)
        w_key = jnp.exp(log_g * (RET_CHUNK - 1.0 - pos))
        w_q = jnp.exp(log_g * (pos + 1.0))
        g_c = math.exp(log_g * RET_CHUNK)
        q = qr_ref[:, lo:hi]
        k = kr_ref[:, lo:hi]
        v = vr_ref[:, lo:hi]
        s = _dot_nt(q, k) * (dmask * kscale)
        inner = jnp.dot(s.astype(BF16), v, preferred_element_type=F32)
        state = state_ref[h]
        qw = (q.astype(F32) * w_q).astype(BF16)
        cross = jnp.dot(qw, state.astype(BF16), preferred_element_type=F32)
        kw_t = (k.astype(F32) * (w_key * kscale)).T.astype(BF16)
        state_ref[h] = g_c * state + jnp.dot(kw_t, v, preferred_element_type=F32)
        o = inner + cross
        mu = jnp.mean(o, axis=-1, keepdims=True)
        oc = o - mu
        var = jnp.mean(oc * oc, axis=-1, keepdims=True)
        o = oc * lax.rsqrt(var + LN_EPS) * rnw_ref[:, lo:hi]
        g = gr_ref[:, lo:hi].astype(F32)
        o = o * (g * jax.nn.sigmoid(g))
        y_ref[:, D_ATTN + lo:D_ATTN + hi] = o.astype(y_ref.dtype)


def _mixer(proj, sinks, attn_norm_w, ret_norm_w):
    nc = SEQ // ATTN_BLOCK
    blk = ATTN_BLOCK

    def wide(colblock):
        return pl.BlockSpec((blk, D_RET), lambda b, n: (b * nc + n, colblock))

    def kv_cur(colblock):
        return pl.BlockSpec((blk, D_KV), lambda b, n: (b * nc + n, colblock))

    def kv_prev(colblock):
        return pl.BlockSpec((blk, D_KV), lambda b, n: (b * nc + jnp.maximum(n - 1, 0), colblock))

    vec = pl.BlockSpec((1, D_RET), lambda b, n: (0, 0))
    gq = GQA_GROUP * blk

    def table(rows):
        return pl.BlockSpec((N_KV_ATTN, rows, gq), lambda b, n: (0, 0, 0))

    slopes = jnp.exp2(-8.0 * (jnp.arange(N_Q_ATTN, dtype=F32) + 1.0) / N_Q_ATTN).reshape(N_KV_ATTN, 1, GQA_GROUP, 1)
    key = jnp.arange(blk, dtype=F32).reshape(1, blk, 1, 1)
    query = jnp.arange(blk, dtype=F32).reshape(1, 1, 1, blk)
    dist_cur = jnp.broadcast_to(query - key, (N_KV_ATTN, blk, GQA_GROUP, blk))
    bias_cur = jnp.where(dist_cur >= 0, -slopes * dist_cur, NEG_INF).reshape(N_KV_ATTN, blk, gq)
    bias_prev = jnp.where(dist_cur < 0, -slopes * (dist_cur + blk), NEG_INF).reshape(N_KV_ATTN, blk, gq)
    sink_rows = jnp.repeat(sinks.astype(F32).reshape(N_KV_ATTN, 1, GQA_GROUP), blk, axis=2)
    return pl.pallas_call(
        _mixer_kernel,
        grid=(BATCH, nc),
        in_specs=[table(1), table(blk), table(blk),
                  wide(COL_QA), kv_prev(COL_KA), kv_cur(COL_KA), kv_prev(COL_VA), kv_cur(COL_VA),
                  wide(COL_QR), wide(COL_KR), wide(COL_VR), wide(COL_GR), vec, vec],
        out_specs=pl.BlockSpec((blk, D_MODEL), lambda b, n: (b * nc + n, 0)),
        out_shape=jax.ShapeDtypeStruct((TOKENS, D_MODEL), BF16),
        scratch_shapes=[pltpu.VMEM((N_RET, HEAD_DIM_RET, HEAD_DIM_RET), F32),
                        pltpu.VMEM((D_ATTN, blk), F32)],
        compiler_params=_cparams(("parallel", "arbitrary")),
        name="mixer",
    )(sink_rows, bias_cur, bias_prev, proj, proj, proj, proj, proj, proj, proj, proj, proj,
      attn_norm_w, ret_norm_w)


def _out_ln_kernel(y_ref, w_ref, x_ref, lnw_ref, lnb_ref, o_ref):
    h = jnp.dot(y_ref[...], w_ref[...], preferred_element_type=F32)
    o_ref[...] = _layer_norm(DN_ALPHA * x_ref[...] + h, lnw_ref[...], lnb_ref[...])


def _out_ln(y, w, x, lnw, lnb):
    tm = LN_TM
    row = pl.BlockSpec((tm, D_MODEL), lambda i: (i, 0))
    vec = pl.BlockSpec((1, D_MODEL), lambda i: (0, 0))
    return pl.pallas_call(
        _out_ln_kernel,
        grid=(TOKENS // tm,),
        in_specs=[row,
                  pl.BlockSpec((D_MODEL, D_MODEL), lambda i: (0, 0), pipeline_mode=pl.Buffered(1)),
                  row, vec, vec],
        out_specs=row,
        out_shape=jax.ShapeDtypeStruct((TOKENS, D_MODEL), F32),
        compiler_params=_cparams(("parallel",)),
        name="out_ln",
    )(y, w, x, lnw, lnb)


def _mem_attn_kernel(x_ref, wq_ref, k_ref, v_ref, o_ref):
    q = jnp.dot(x_ref[...].astype(BF16), wq_ref[...], preferred_element_type=F32)
    q = (q * (HEAD_DIM_MEM ** -0.5)).astype(BF16)
    for h in range(N_MEM_HEADS):
        lo, hi = h * HEAD_DIM_MEM, (h + 1) * HEAD_DIM_MEM
        s = _dot_nt(q[:, lo:hi], k_ref[:, lo:hi])
        m = jnp.max(s, axis=-1, keepdims=True)
        p = jnp.exp(s - m)
        denom = jnp.sum(p, axis=-1, keepdims=True)
        o = jnp.dot(p.astype(BF16), v_ref[:, lo:hi], preferred_element_type=F32)
        o_ref[:, lo:hi] = (o / denom).astype(o_ref.dtype)


def _mem_attn(x, wq, kv):
    tm = MEM_TM
    nt = SEQ // tm
    row = pl.BlockSpec((tm, D_MODEL), lambda b, i: (b * nt + i, 0))
    return pl.pallas_call(
        _mem_attn_kernel,
        grid=(BATCH, nt),
        in_specs=[row,
                  pl.BlockSpec((D_MODEL, D_MODEL), lambda b, i: (0, 0), pipeline_mode=pl.Buffered(1)),
                  pl.BlockSpec((N_MEM, D_MODEL), lambda b, i: (b, 0)),
                  pl.BlockSpec((N_MEM, D_MODEL), lambda b, i: (b, 1))],
        out_specs=row,
        out_shape=jax.ShapeDtypeStruct((TOKENS, D_MODEL), BF16),
        compiler_params=_cparams(("parallel", "parallel")),
        name="mem_attn",
    )(x, wq, kv, kv)


def _route_kernel(x_ref, wr_ref, br_ref, idx_ref, gate_ref, rank_ref, count_ref, carry_ref):
    tm = x_ref.shape[0]

    @pl.when(pl.program_id(0) == 0)
    def _():
        carry_ref[...] = jnp.zeros_like(carry_ref)

    logits = jnp.dot(x_ref[...], wr_ref[...], preferred_element_type=F32,
                     precision=lax.Precision.HIGHEST) + br_ref[...]
    lane = lax.broadcasted_iota(jnp.int32, (tm, LANES), 1)
    lane_f = lane.astype(F32)
    work = logits
    sels, vals, idxs = [], [], []
    for _ in range(TOP_K):
        v = jnp.max(work, axis=-1, keepdims=True)
        i = jnp.min(jnp.where(work == v, lane_f, float(LANES)), axis=-1, keepdims=True)
        sel = lane_f == i
        work = jnp.where(sel, -jnp.inf, work)
        sels.append(sel)
        vals.append(v)
        idxs.append(i)
    exps = [jnp.exp(v - vals[0]) for v in vals]
    total = exps[0] + exps[1] + exps[2] + exps[3]

    onehot = jnp.zeros((tm, LANES), F32)
    for sel in sels:
        onehot = jnp.where(sel, 1.0, onehot)
    r = lax.broadcasted_iota(jnp.int32, (tm, tm), 0)
    c = lax.broadcasted_iota(jnp.int32, (tm, tm), 1)
    below = jnp.where(c < r, 1.0, 0.0).astype(BF16)
    prefix = jnp.dot(below, onehot.astype(BF16), preferred_element_type=F32)
    rankmat = prefix + carry_ref[...]
    carry_ref[...] = carry_ref[...] + jnp.sum(onehot, axis=0, keepdims=True)
    count_ref[...] = carry_ref[...]

    idx_out = jnp.zeros((tm, LANES), jnp.int32)
    gate_out = jnp.zeros((tm, LANES), F32)
    rank_out = jnp.zeros((tm, LANES), jnp.int32)
    for k in range(TOP_K):
        rk = jnp.sum(jnp.where(sels[k], rankmat, 0.0), axis=-1, keepdims=True).astype(jnp.int32)
        idx_out = jnp.where(lane == k, idxs[k].astype(jnp.int32), idx_out)
        gate_out = jnp.where(lane == k, exps[k] / total, gate_out)
        rank_out = jnp.where(lane == k, rk, rank_out)
    idx_ref[...] = idx_out
    gate_ref[...] = gate_out
    rank_ref[...] = rank_out


def _route(x, wr, br):
    tm = ROUTE_TM
    wide = pl.BlockSpec((tm, LANES), lambda i: (i, 0))
    return pl.pallas_call(
        _route_kernel,
        grid=(TOKENS // tm,),
        in_specs=[pl.BlockSpec((tm, D_MODEL), lambda i: (i, 0)),
                  pl.BlockSpec((D_MODEL, LANES), lambda i: (0, 0)),
                  pl.BlockSpec((1, LANES), lambda i: (0, 0))],
        out_specs=[wide, wide, wide, pl.BlockSpec((1, LANES), lambda i: (0, 0))],
        out_shape=[jax.ShapeDtypeStruct((TOKENS, LANES), jnp.int32),
                   jax.ShapeDtypeStruct((TOKENS, LANES), F32),
                   jax.ShapeDtypeStruct((TOKENS, LANES), jnp.int32),
                   jax.ShapeDtypeStruct((1, LANES), F32)],
        scratch_shapes=[pltpu.VMEM((1, LANES), F32)],
        compiler_params=_cparams(("arbitrary",)),
        name="route",
    )(x, wr, br)


HALF = D_MODEL // 2


def _pack_rows(x):
    lo = lax.bitcast_convert_type(x[:, :HALF].astype(BF16).astype(F32), jnp.int32)
    hi = lax.bitcast_convert_type(x[:, HALF:].astype(BF16).astype(F32), jnp.int32)
    return hi | lax.shift_right_logical(lo, 16)


def _unpack_rows(w):
    lo = lax.bitcast_convert_type(lax.shift_left(w, 16), F32)
    hi = lax.bitcast_convert_type(w & jnp.int32(-65536), F32)
    return lo, hi


def _tile_copies(src_ref, dst_ref, sem):
    return pltpu.make_async_copy(src_ref.at[pl.ds(0, ROW_TILE), :], dst_ref.at[pl.ds(0, ROW_TILE), :], sem)


def _scatter_kernel(dest_ref, x_ref, xs_ref, packed_ref, sem):
    i = pl.program_id(0)
    slot = i % 2
    packed_ref[slot] = _pack_rows(x_ref[...])

    def body(r, carry):
        for k in range(TOP_K):
            d = dest_ref[0, 0, r * TOP_K + k]
            pltpu.make_async_copy(packed_ref.at[slot, pl.ds(r, 1), :], xs_ref.at[pl.ds(d, 1), :],
                                  sem.at[slot]).start()
        return carry

    lax.fori_loop(0, ROW_TILE, body, 0)

    @pl.when(i > 0)
    def _():
        for k in range(TOP_K):
            _tile_copies(packed_ref.at[1 - slot], xs_ref, sem.at[1 - slot]).wait()

    @pl.when(i == pl.num_programs(0) - 1)
    def _():
        for k in range(TOP_K):
            _tile_copies(packed_ref.at[slot], xs_ref, sem.at[slot]).wait()


def _scatter_rows(x, dest):
    nt = TOKENS // ROW_TILE
    return pl.pallas_call(
        _scatter_kernel,
        grid=(nt,),
        in_specs=[pl.BlockSpec((1, 1, ROW_TILE * TOP_K), lambda i: (i, 0, 0), memory_space=pltpu.SMEM),
                  pl.BlockSpec((ROW_TILE, D_MODEL), lambda i: (i, 0))],
        out_specs=pl.BlockSpec(memory_space=pl.ANY),
        out_shape=jax.ShapeDtypeStruct((MOE_ROWS, HALF), jnp.int32),
        scratch_shapes=[pltpu.VMEM((2, ROW_TILE, HALF), jnp.int32), pltpu.SemaphoreType.DMA((2,))],
        compiler_params=_cparams(("arbitrary",), has_side_effects=True),
        name="scatter_rows",
    )(dest.reshape(nt, 1, ROW_TILE * TOP_K), x)


def _experts_kernel(ve_ref, vn_ref, xs_ref, wg_ref, wu_ref, bg_ref, bu_ref, wd_ref, bd_ref, ys_ref,
                    xb_ref, acc_ref):
    v = pl.program_id(0)
    f = pl.program_id(1)
    nvalid = vn_ref[v]

    @pl.when(jnp.logical_and(nvalid == 0, f == 0))
    def _():
        ys_ref[...] = jnp.zeros_like(ys_ref)

    @pl.when(nvalid > 0)
    def _():
        @pl.when(f == 0)
        def _():
            rows = lax.broadcasted_iota(jnp.int32, (MOE_TM, 1), 0)
            lo, hi = _unpack_rows(jnp.where(rows < nvalid, xs_ref[...], 0))
            xb_ref[:, :HALF] = lo.astype(BF16)
            xb_ref[:, HALF:] = hi.astype(BF16)
            acc_ref[...] = jnp.broadcast_to(bd_ref[...], acc_ref.shape)

        xb = xb_ref[...]
        gl = jnp.dot(xb, wg_ref[...].astype(BF16), preferred_element_type=F32) + bg_ref[...]
        up = jnp.dot(xb, wu_ref[...].astype(BF16), preferred_element_type=F32) + bu_ref[...]
        gl = jnp.minimum(gl, SWIGLU_LIMIT)
        up = jnp.clip(up, -SWIGLU_LIMIT, SWIGLU_LIMIT)
        act = (up + 1.0) * (gl * jax.nn.sigmoid(SWIGLU_ALPHA * gl))
        acc_ref[...] += jnp.dot(act.astype(BF16), wd_ref[...].astype(BF16), preferred_element_type=F32)

        @pl.when(f == pl.num_programs(1) - 1)
        def _():
            ys_ref[...] = _pack_rows(acc_ref[...])


def _experts(xs, visit_expert, visit_nvalid, w_gate_up, b_gate_up, w_down, b_down):
    nf = D_EXPERT // MOE_TF

    def f_eff(v, f, vn):
        return jnp.where(vn[v] > 0, f, nf - 1)

    grid_spec = pltpu.PrefetchScalarGridSpec(
        num_scalar_prefetch=2,
        grid=(N_VISITS, nf),
        in_specs=[
            pl.BlockSpec((MOE_TM, HALF), lambda v, f, ve, vn: (v, 0)),
            pl.BlockSpec((None, D_MODEL, MOE_TF), lambda v, f, ve, vn: (ve[v], 0, f_eff(v, f, vn))),
            pl.BlockSpec((None, D_MODEL, MOE_TF), lambda v, f, ve, vn: (ve[v], 0, nf + f_eff(v, f, vn))),
            pl.BlockSpec((None, 1, MOE_TF), lambda v, f, ve, vn: (ve[v], 0, f_eff(v, f, vn))),
            pl.BlockSpec((None, 1, MOE_TF), lambda v, f, ve, vn: (ve[v], 0, nf + f_eff(v, f, vn))),
            pl.BlockSpec((None, MOE_TF, D_MODEL), lambda v, f, ve, vn: (ve[v], f_eff(v, f, vn), 0)),
            pl.BlockSpec((None, 1, D_MODEL), lambda v, f, ve, vn: (ve[v], 0, 0)),
        ],
        out_specs=pl.BlockSpec((MOE_TM, HALF), lambda v, f, ve, vn: (v, 0)),
        scratch_shapes=[pltpu.VMEM((MOE_TM, D_MODEL), BF16), pltpu.VMEM((MOE_TM, D_MODEL), F32)],
    )
    return pl.pallas_call(
        _experts_kernel,
        grid_spec=grid_spec,
        out_shape=jax.ShapeDtypeStruct((MOE_ROWS, HALF), jnp.int32),
        compiler_params=_cparams(("arbitrary", "arbitrary")),
        name="experts",
    )(visit_expert, visit_nvalid, xs, w_gate_up, w_gate_up, b_gate_up, b_gate_up, w_down, b_down)


def _combine_kernel(dest_ref, dest_next_ref, ys_ref, gate_ref, x_ref, lnw_ref, lnb_ref, o_ref, buf_ref, sem):
    i = pl.program_id(0)
    slot = i % 2

    def issue(idx_ref, s):
        def body(r, carry):
            for k in range(TOP_K):
                d = idx_ref[0, 0, r * TOP_K + k]
                pltpu.make_async_copy(ys_ref.at[pl.ds(d, 1), :], buf_ref.at[s, k, pl.ds(r, 1), :],
                                      sem.at[s]).start()
            return carry

        lax.fori_loop(0, ROW_TILE, body, 0)

    @pl.when(i == 0)
    def _():
        issue(dest_ref, slot)

    @pl.when(i + 1 < pl.num_programs(0))
    def _():
        issue(dest_next_ref, 1 - slot)

    for k in range(TOP_K):
        _tile_copies(ys_ref, buf_ref.at[slot, k], sem.at[slot]).wait()
    gates = gate_ref[...]
    h_lo = jnp.zeros((ROW_TILE, HALF), F32)
    h_hi = jnp.zeros((ROW_TILE, HALF), F32)
    for k in range(TOP_K):
        lo, hi = _unpack_rows(buf_ref[slot, k])
        h_lo = h_lo + gates[:, k:k + 1] * lo
        h_hi = h_hi + gates[:, k:k + 1] * hi
    h = jnp.concatenate([h_lo, h_hi], axis=1)
    o_ref[...] = _layer_norm(DN_ALPHA * x_ref[...] + h, lnw_ref[...], lnb_ref[...])


def _combine(ys, dest, gates, x, lnw, lnb):
    nt = TOKENS // ROW_TILE
    row = pl.BlockSpec((ROW_TILE, D_MODEL), lambda i: (i, 0))
    vec = pl.BlockSpec((1, D_MODEL), lambda i: (0, 0))
    dest_tiles = dest.reshape(nt, 1, ROW_TILE * TOP_K)
    return pl.pallas_call(
        _combine_kernel,
        grid=(nt,),
        in_specs=[pl.BlockSpec((1, 1, ROW_TILE * TOP_K), lambda i: (i, 0, 0), memory_space=pltpu.SMEM),
                  pl.BlockSpec((1, 1, ROW_TILE * TOP_K), lambda i: (jnp.minimum(i + 1, nt - 1), 0, 0),
                               memory_space=pltpu.SMEM),
                  pl.BlockSpec(memory_space=pl.ANY),
                  pl.BlockSpec((ROW_TILE, LANES), lambda i: (i, 0)),
                  row, vec, vec],
        out_specs=row,
        out_shape=jax.ShapeDtypeStruct((TOKENS, D_MODEL), F32),
        scratch_shapes=[pltpu.VMEM((2, TOP_K, ROW_TILE, HALF), jnp.int32), pltpu.SemaphoreType.DMA((2,))],
        compiler_params=_cparams(("arbitrary",)),
        name="combine",
    )(dest_tiles, dest_tiles, ys, gates, x, lnw, lnb)


def _layer(x, mem, w_in, attn_sinks, attn_norm_w, ret_norm_w, w_mix_out, ln_mix_w, ln_mix_b,
           w_mem_q, w_mem_kv, w_mem_out, ln_mem_w, ln_mem_b, w_router, b_router,
           w_gate_up, b_gate_up, w_down, b_down, ln_moe_w, ln_moe_b):
    o_k, o_v, o_qr = D_ATTN, D_ATTN + D_KV, D_ATTN + 2 * D_KV
    w_in_p = jnp.concatenate([w_in[:, :o_k], w_in[:, o_qr:], w_in[:, o_k:o_v], w_in[:, o_v:o_qr]],
                             axis=1).astype(BF16)
    proj = _project(x, w_in_p, PROJ_TM, PROJ_TN)
    y = _mixer(proj, attn_sinks.astype(F32), attn_norm_w.reshape(1, D_ATTN), ret_norm_w.reshape(1, D_RET))
    x1 = _out_ln(y, w_mix_out.astype(BF16), x, ln_mix_w.reshape(1, D_MODEL), ln_mix_b.reshape(1, D_MODEL))

    kv = _project(mem.reshape(BATCH * N_MEM, D_MODEL), w_mem_kv.astype(BF16), BATCH * N_MEM, D_MODEL)
    o = _mem_attn(x1, w_mem_q.astype(BF16), kv)
    x2 = _out_ln(o, w_mem_out.astype(BF16), x1, ln_mem_w.reshape(1, D_MODEL), ln_mem_b.reshape(1, D_MODEL))

    wr = jnp.pad(w_router, ((0, 0), (0, LANES - N_EXPERTS)))
    br = jnp.pad(b_router.reshape(1, N_EXPERTS), ((0, 0), (0, LANES - N_EXPERTS)), constant_values=NEG_INF)
    idx, gates, rank, counts = _route(x2, wr, br)

    counts = counts[0, :N_EXPERTS].astype(jnp.int32)
    padded = ((counts + MOE_TM - 1) // MOE_TM) * MOE_TM
    eid = jnp.arange(N_EXPERTS, dtype=jnp.int32)
    pad_end = jnp.sum(jnp.where(eid[None, :] <= eid[:, None], padded[None, :], 0), axis=1)
    pad_start = pad_end - padded
    dest = pad_start[idx[:, :TOP_K]] + rank[:, :TOP_K]
    visit_row = jnp.arange(N_VISITS, dtype=jnp.int32) * MOE_TM
    visit_expert = jnp.minimum(jnp.sum((pad_end[None, :] <= visit_row[:, None]).astype(jnp.int32), axis=1),
                               N_EXPERTS - 1)
    visit_nvalid = jnp.clip(counts[visit_expert] - (visit_row - pad_start[visit_expert]),
                            0, MOE_TM).astype(jnp.int32)

    xs = _scatter_rows(x2, dest)
    ys = _experts(xs, visit_expert, visit_nvalid,
                  w_gate_up, b_gate_up.reshape(N_EXPERTS, 1, 2 * D_EXPERT),
                  w_down, b_down.reshape(N_EXPERTS, 1, D_MODEL))
    return _combine(ys, dest, gates, x2, ln_moe_w.reshape(1, D_MODEL), ln_moe_b.reshape(1, D_MODEL))


def kernel(x, mem, w_in, attn_sinks, attn_norm_w, ret_norm_w, w_mix_out, ln_mix_w, ln_mix_b, w_mem_q, w_mem_kv, w_mem_out, ln_mem_w, ln_mem_b, w_router, b_router, w_gate_up, b_gate_up, w_down, b_down, ln_moe_w, ln_moe_b):
    xt = x.reshape(TOKENS, D_MODEL)
    out = _layer(xt, mem, w_in[0], attn_sinks[0], attn_norm_w[0], ret_norm_w[0], w_mix_out[0],
                 ln_mix_w[0], ln_mix_b[0], w_mem_q[0], w_mem_kv[0], w_mem_out[0], ln_mem_w[0], ln_mem_b[0],
                 w_router[0], b_router[0], w_gate_up[0], b_gate_up[0], w_down[0], b_down[0],
                 ln_moe_w[0], ln_moe_b[0])
    return out.reshape(BATCH, SEQ, D_MODEL)
```

```python
import functools
import math

import jax
import jax.numpy as jnp
from jax import lax
from jax.experimental import pallas as pl
from jax.experimental.pallas import tpu as pltpu

F32 = jnp.float32
BF16 = jnp.bfloat16

D_MODEL = 2048
BATCH = 2
SEQ = 16384
TOKENS = BATCH * SEQ

HEAD_DIM_ATTN = 64
N_Q_ATTN = 16
N_KV_ATTN = 2
GQA_GROUP = N_Q_ATTN // N_KV_ATTN
ATTN_BLOCK = 128
D_ATTN = N_Q_ATTN * HEAD_DIM_ATTN
D_KV = N_KV_ATTN * HEAD_DIM_ATTN
N_RET = 8
HEAD_DIM_RET = 128
RET_CHUNK = 128
D_RET = N_RET * HEAD_DIM_RET
D_IN = D_ATTN + 2 * D_KV + 4 * D_RET
N_MEM = 256
N_MEM_HEADS = 4
HEAD_DIM_MEM = D_MODEL // N_MEM_HEADS
N_EXPERTS = 32
TOP_K = 4
D_EXPERT = D_MODEL
SWIGLU_LIMIT = 7.0
SWIGLU_ALPHA = 1.702
LN_EPS = 1e-5
DN_ALPHA = 2.0 ** 0.25
NEG_INF = -1e30

LANES = 128
VMEM_LIMIT = 56 * 1024 * 1024

PROJ_TM = 1024
PROJ_TN = 1792
LN_TM = 512
MEM_TM = 512
ROUTE_TM = 512
ROW_TILE = 256
MOE_TM = 1024
MOE_TF = 256
N_VISITS = (TOKENS * TOP_K + N_EXPERTS * (MOE_TM - 1) + MOE_TM - 1) // MOE_TM
MOE_ROWS = N_VISITS * MOE_TM

COL_QA, COL_QR, COL_KR, COL_VR, COL_GR = 0, 1, 2, 3, 4
COL_KA = 5 * D_RET // D_KV
COL_VA = COL_KA + 1


def _cparams(sem, **kw):
    return pltpu.CompilerParams(dimension_semantics=sem, vmem_limit_bytes=VMEM_LIMIT, **kw)


def _layer_norm(z, w, b):
    mu = jnp.mean(z, axis=-1, keepdims=True)
    zc = z - mu
    var = jnp.mean(zc * zc, axis=-1, keepdims=True)
    return zc * lax.rsqrt(var + LN_EPS) * w + b


def _dot_nt(a, b):
    return lax.dot_general(a, b, (((1,), (1,)), ((), ())), preferred_element_type=F32)


def _proj_kernel(x_ref, w_ref, o_ref, xb_ref):
    @pl.when(pl.program_id(1) == 0)
    def _():
        xb_ref[...] = x_ref[...].astype(BF16)

    o_ref[...] = jnp.dot(xb_ref[...], w_ref[...], preferred_element_type=F32).astype(o_ref.dtype)


def _project(x, w, tm, tn):
    m, k = x.shape
    n = w.shape[1]
    return pl.pallas_call(
        _proj_kernel,
        grid=(m // tm, n // tn),
        in_specs=[pl.BlockSpec((tm, k), lambda i, j: (i, 0)),
                  pl.BlockSpec((k, tn), lambda i, j: (0, j))],
        out_specs=pl.BlockSpec((tm, tn), lambda i, j: (i, j)),
        out_shape=jax.ShapeDtypeStruct((m, n), BF16),
        scratch_shapes=[pltpu.VMEM((tm, k), BF16)],
        compiler_params=_cparams(("parallel", "arbitrary")),
        name="project",
    )(x, w)


def _mixer_kernel(sinkrow_ref, bias_cur_ref, bias_prev_ref, qa_ref, kp_ref, kc_ref, vp_ref, vc_ref,
                  qr_ref, kr_ref, vr_ref, gr_ref, anw_ref, rnw_ref, y_ref, state_ref, a_ref):
    n = pl.program_id(1)
    row = lax.broadcasted_iota(jnp.int32, (ATTN_BLOCK, ATTN_BLOCK), 0)
    col = lax.broadcasted_iota(jnp.int32, (ATTN_BLOCK, ATTN_BLOCK), 1)
    diff = row - col
    difff = diff.astype(F32)

    scale = HEAD_DIM_ATTN ** -0.5
    for j in range(N_KV_ATTN):
        klo, khi = j * HEAD_DIM_ATTN, (j + 1) * HEAD_DIM_ATTN
        heads = range(j * GQA_GROUP, (j + 1) * GQA_GROUP)
        q_rows = jnp.concatenate([qa_ref[:, h * HEAD_DIM_ATTN:(h + 1) * HEAD_DIM_ATTN] for h in heads], axis=0)
        sink = sinkrow_ref[j]
        s_cur = _dot_nt(kc_ref[:, klo:khi], q_rows) * scale + bias_cur_ref[j]
        s_prev = _dot_nt(kp_ref[:, klo:khi], q_rows) * scale + bias_prev_ref[j]
        s_prev = jnp.where(n > 0, s_prev, NEG_INF)
        m = jnp.maximum(jnp.max(s_cur, axis=0, keepdims=True), jnp.max(s_prev, axis=0, keepdims=True))
        m = jnp.maximum(m, sink)
        p_cur = jnp.exp(s_cur - m)
        p_prev = jnp.exp(s_prev - m)
        denom = (jnp.sum(p_cur, axis=0, keepdims=True) + jnp.sum(p_prev, axis=0, keepdims=True)
                 + jnp.exp(sink - m))
        v_cur_t = vc_ref[:, klo:khi].astype(F32).T.astype(BF16)
        v_prev_t = vp_ref[:, klo:khi].astype(F32).T.astype(BF16)
        o_t = (jnp.dot(v_cur_t, p_cur.astype(BF16), preferred_element_type=F32)
               + jnp.dot(v_prev_t, p_prev.astype(BF16), preferred_element_type=F32))
        o_t = o_t * (1.0 / denom)
        for g, h in enumerate(heads):
            a_ref[h * HEAD_DIM_ATTN:(h + 1) * HEAD_DIM_ATTN, :] = o_t[:, g * ATTN_BLOCK:(g + 1) * ATTN_BLOCK]
    a_t = a_ref[...]
    a_t = a_t * lax.rsqrt(jnp.mean(a_t * a_t, axis=0, keepdims=True) + LN_EPS)
    y_ref[:, :D_ATTN] = (a_t.T * anw_ref[...]).astype(y_ref.dtype)

    @pl.when(n == 0)
    def _():
        state_ref[...] = jnp.zeros_like(state_ref)

    pos = lax.broadcasted_iota(jnp.int32, (RET_CHUNK, 1), 0).astype(F32)
    kscale = HEAD_DIM_RET ** -0.5
    for h in range(N_RET):
        lo, hi = h * HEAD_DIM_RET, (h + 1) * HEAD_DIM_RET
        log_g = math.log1p(-(2.0 ** (-5.0 - h)))
        dmask = jnp.where(diff >= 0, jnp.exp(log_g * jnp.maximum(difff, 0.0)), 0.0)
        w_key = jnp.exp(log_g * (RET_CHUNK - 1.0 - pos))
        w_q = jnp.exp(log_g * (pos + 1.0))
        g_c = math.exp(log_g * RET_CHUNK)
        q = qr_ref[:, lo:hi]
        k = kr_ref[:, lo:hi]
        v = vr_ref[:, lo:hi]
        s = _dot_nt(q, k) * (dmask * kscale)
        inner = jnp.dot(s.astype(BF16), v, preferred_element_type=F32)
        state = state_ref[h]
        qw = (q.astype(F32) * w_q).astype(BF16)
        cross = jnp.dot(qw, state.astype(BF16), preferred_element_type=F32)
        kw_t = (k.astype(F32) * (w_key * kscale)).T.astype(BF16)
        state_ref[h] = g_c * state + jnp.dot(kw_t, v, preferred_element_type=F32)
        o = inner + cross
        mu = jnp.mean(o, axis=-1, keepdims=True)
        oc = o - mu
        var = jnp.mean(oc * oc, axis=-1, keepdims=True)
        o = oc * lax.rsqrt(var + LN_EPS) * rnw_ref[:, lo:hi]
        g = gr_ref[:, lo:hi].astype(F32)
        o = o * (g * jax.nn.sigmoid(g))
        y_ref[:, D_ATTN + lo:D_ATTN + hi] = o.astype(y_ref.dtype)


def _mixer(proj, sinks, attn_norm_w, ret_norm_w):
    nc = SEQ // ATTN_BLOCK
    blk = ATTN_BLOCK

    def wide(colblock):
        return pl.BlockSpec((blk, D_RET), lambda b, n: (b * nc + n, colblock))

    def kv_cur(colblock):
        return pl.BlockSpec((blk, D_KV), lambda b, n: (b * nc + n, colblock))

    def kv_prev(colblock):
        return pl.BlockSpec((blk, D_KV), lambda b, n: (b * nc + jnp.maximum(n - 1, 0), colblock))

    vec = pl.BlockSpec((1, D_RET), lambda b, n: (0, 0))
    gq = GQA_GROUP * blk

    def table(rows):
        return pl.BlockSpec((N_KV_ATTN, rows, gq), lambda b, n: (0, 0, 0))

    slopes = jnp.exp2(-8.0 * (jnp.arange(N_Q_ATTN, dtype=F32) + 1.0) / N_Q_ATTN).reshape(N_KV_ATTN, 1, GQA_GROUP, 1)
    key = jnp.arange(blk, dtype=F32).reshape(1, blk, 1, 1)
    query = jnp.arange(blk, dtype=F32).reshape(1, 1, 1, blk)
    dist_cur = jnp.broadcast_to(query - key, (N_KV_ATTN, blk, GQA_GROUP, blk))
    bias_cur = jnp.where(dist_cur >= 0, -slopes * dist_cur, NEG_INF).reshape(N_KV_ATTN, blk, gq)
    bias_prev = jnp.where(dist_cur < 0, -slopes * (dist_cur + blk), NEG_INF).reshape(N_KV_ATTN, blk, gq)
    sink_rows = jnp.repeat(sinks.astype(F32).reshape(N_KV_ATTN, 1, GQA_GROUP), blk, axis=2)
    return pl.pallas_call(
        _mixer_kernel,
        grid=(BATCH, nc),
        in_specs=[table(1), table(blk), table(blk),
                  wide(COL_QA), kv_prev(COL_KA), kv_cur(COL_KA), kv_prev(COL_VA), kv_cur(COL_VA),
                  wide(COL_QR), wide(COL_KR), wide(COL_VR), wide(COL_GR), vec, vec],
        out_specs=pl.BlockSpec((blk, D_MODEL), lambda b, n: (b * nc + n, 0)),
        out_shape=jax.ShapeDtypeStruct((TOKENS, D_MODEL), BF16),
        scratch_shapes=[pltpu.VMEM((N_RET, HEAD_DIM_RET, HEAD_DIM_RET), F32),
                        pltpu.VMEM((D_ATTN, blk), F32)],
        compiler_params=_cparams(("parallel", "arbitrary")),
        name="mixer",
    )(sink_rows, bias_cur, bias_prev, proj, proj, proj, proj, proj, proj, proj, proj, proj,
      attn_norm_w, ret_norm_w)


def _out_ln_kernel(y_ref, w_ref, x_ref, lnw_ref, lnb_ref, o_ref):
    h = jnp.dot(y_ref[...], w_ref[...], preferred_element_type=F32)
    o_ref[...] = _layer_norm(DN_ALPHA * x_ref[...] + h, lnw_ref[...], lnb_ref[...])


def _out_ln(y, w, x, lnw, lnb):
    tm = LN_TM
    row = pl.BlockSpec((tm, D_MODEL), lambda i: (i, 0))
    vec = pl.BlockSpec((1, D_MODEL), lambda i: (0, 0))
    return pl.pallas_call(
        _out_ln_kernel,
        grid=(TOKENS // tm,),
        in_specs=[row,
                  pl.BlockSpec((D_MODEL, D_MODEL), lambda i: (0, 0), pipeline_mode=pl.Buffered(1)),
                  row, vec, vec],
        out_specs=row,
        out_shape=jax.ShapeDtypeStruct((TOKENS, D_MODEL), F32),
        compiler_params=_cparams(("parallel",)),
        name="out_ln",
    )(y, w, x, lnw, lnb)


def _mem_attn_kernel(x_ref, wq_ref, k_ref, v_ref, o_ref):
    q = jnp.dot(x_ref[...].astype(BF16), wq_ref[...], preferred_element_type=F32)
    q = (q * (HEAD_DIM_MEM ** -0.5)).astype(BF16)
    for h in range(N_MEM_HEADS):
        lo, hi = h * HEAD_DIM_MEM, (h + 1) * HEAD_DIM_MEM
        s = _dot_nt(q[:, lo:hi], k_ref[:, lo:hi])
        m = jnp.max(s, axis=-1, keepdims=True)
        p = jnp.exp(s - m)
        denom = jnp.sum(p, axis=-1, keepdims=True)
        o = jnp.dot(p.astype(BF16), v_ref[:, lo:hi], preferred_element_type=F32)
        o_ref[:, lo:hi] = (o / denom).astype(o_ref.dtype)


def _mem_attn(x, wq, kv):
    tm = MEM_TM
    nt = SEQ // tm
    row = pl.BlockSpec((tm, D_MODEL), lambda b, i: (b * nt + i, 0))
    return pl.pallas_call(
        _mem_attn_kernel,
        grid=(BATCH, nt),
        in_specs=[row,
                  pl.BlockSpec((D_MODEL, D_MODEL), lambda b, i: (0, 0), pipeline_mode=pl.Buffered(1)),
                  pl.BlockSpec((N_MEM, D_MODEL), lambda b, i: (b, 0)),
                  pl.BlockSpec((N_MEM, D_MODEL), lambda b, i: (b, 1))],
        out_specs=row,
        out_shape=jax.ShapeDtypeStruct((TOKENS, D_MODEL), BF16),
        compiler_params=_cparams(("parallel", "parallel")),
        name="mem_attn",
    )(x, wq, kv, kv)


def _route_kernel(x_ref, wr_ref, br_ref, idx_ref, gate_ref, rank_ref, count_ref, carry_ref):
    tm = x_ref.shape[0]

    @pl.when(pl.program_id(0) == 0)
    def _():
        carry_ref[...] = jnp.zeros_like(carry_ref)

    logits = jnp.dot(x_ref[...], wr_ref[...], preferred_element_type=F32,
                     precision=lax.Precision.HIGHEST) + br_ref[...]
    lane = lax.broadcasted_iota(jnp.int32, (tm, LANES), 1)
    lane_f = lane.astype(F32)
    work = logits
    sels, vals, idxs = [], [], []
    for _ in range(TOP_K):
        v = jnp.max(work, axis=-1, keepdims=True)
        i = jnp.min(jnp.where(work == v, lane_f, float(LANES)), axis=-1, keepdims=True)
        sel = lane_f == i
        work = jnp.where(sel, -jnp.inf, work)
        sels.append(sel)
        vals.append(v)
        idxs.append(i)
    exps = [jnp.exp(v - vals[0]) for v in vals]
    total = exps[0] + exps[1] + exps[2] + exps[3]

    onehot = jnp.zeros((tm, LANES), F32)
    for sel in sels:
        onehot = jnp.where(sel, 1.0, onehot)
    r = lax.broadcasted_iota(jnp.int32, (tm, tm), 0)
    c = lax.broadcasted_iota(jnp.int32, (tm, tm), 1)
    below = jnp.where(c < r, 1.0, 0.0).astype(BF16)
    prefix = jnp.dot(below, onehot.astype(BF16), preferred_element_type=F32)
    rankmat = prefix + carry_ref[...]
    carry_ref[...] = carry_ref[...] + jnp.sum(onehot, axis=0, keepdims=True)
    count_ref[...] = carry_ref[...]

    idx_out = jnp.zeros((tm, LANES), jnp.int32)
    gate_out = jnp.zeros((tm, LANES), F32)
    rank_out = jnp.zeros((tm, LANES), jnp.int32)
    for k in range(TOP_K):
        rk = jnp.sum(jnp.where(sels[k], rankmat, 0.0), axis=-1, keepdims=True).astype(jnp.int32)
        idx_out = jnp.where(lane == k, idxs[k].astype(jnp.int32), idx_out)
        gate_out = jnp.where(lane == k, exps[k] / total, gate_out)
        rank_out = jnp.where(lane == k, rk, rank_out)
    idx_ref[...] = idx_out
    gate_ref[...] = gate_out
    rank_ref[...] = rank_out


def _route(x, wr, br):
    tm = ROUTE_TM
    wide = pl.BlockSpec((tm, LANES), lambda i: (i, 0))
    return pl.pallas_call(
        _route_kernel,
        grid=(TOKENS // tm,),
        in_specs=[pl.BlockSpec((tm, D_MODEL), lambda i: (i, 0)),
                  pl.BlockSpec((D_MODEL, LANES), lambda i: (0, 0)),
                  pl.BlockSpec((1, LANES), lambda i: (0, 0))],
        out_specs=[wide, wide, wide, pl.BlockSpec((1, LANES), lambda i: (0, 0))],
        out_shape=[jax.ShapeDtypeStruct((TOKENS, LANES), jnp.int32),
                   jax.ShapeDtypeStruct((TOKENS, LANES), F32),
                   jax.ShapeDtypeStruct((TOKENS, LANES), jnp.int32),
                   jax.ShapeDtypeStruct((1, LANES), F32)],
        scratch_shapes=[pltpu.VMEM((1, LANES), F32)],
        compiler_params=_cparams(("arbitrary",)),
        name="route",
    )(x, wr, br)


HALF = D_MODEL // 2


def _pack_rows(x):
    return _pack_pair(x[:, :HALF], x[:, HALF:])


def _pack_pair(lo, hi):
    lo = lax.bitcast_convert_type(lo.astype(BF16).astype(F32), jnp.int32)
    hi = lax.bitcast_convert_type(hi.astype(BF16).astype(F32), jnp.int32)
    return hi | lax.shift_right_logical(lo, 16)


def _unpack_rows(w):
    lo = lax.bitcast_convert_type(lax.shift_left(w, 16), F32)
    hi = lax.bitcast_convert_type(w & jnp.int32(-65536), F32)
    return lo, hi


def _tile_copies(src_ref, dst_ref, sem):
    return pltpu.make_async_copy(src_ref.at[pl.ds(0, ROW_TILE), :], dst_ref.at[pl.ds(0, ROW_TILE), :], sem)


def _scatter_kernel(dest_ref, x_ref, xs_ref, packed_ref, sem):
    i = pl.program_id(0)
    slot = i % 2
    packed_ref[slot] = _pack_rows(x_ref[...])

    def body(r, carry):
        for k in range(TOP_K):
            d = dest_ref[0, 0, r * TOP_K + k]
            pltpu.make_async_copy(packed_ref.at[slot, pl.ds(r, 1), :], xs_ref.at[pl.ds(d, 1), :],
                                  sem.at[slot]).start()
        return carry

    lax.fori_loop(0, ROW_TILE, body, 0)

    @pl.when(i > 0)
    def _():
        for k in range(TOP_K):
            _tile_copies(packed_ref.at[1 - slot], xs_ref, sem.at[1 - slot]).wait()

    @pl.when(i == pl.num_programs(0) - 1)
    def _():
        for k in range(TOP_K):
            _tile_copies(packed_ref.at[slot], xs_ref, sem.at[slot]).wait()


def _scatter_rows(x, dest):
    nt = TOKENS // ROW_TILE
    return pl.pallas_call(
        _scatter_kernel,
        grid=(nt,),
        in_specs=[pl.BlockSpec((1, 1, ROW_TILE * TOP_K), lambda i: (i, 0, 0), memory_space=pltpu.SMEM),
                  pl.BlockSpec((ROW_TILE, D_MODEL), lambda i: (i, 0))],
        out_specs=pl.BlockSpec(memory_space=pl.ANY),
        out_shape=jax.ShapeDtypeStruct((MOE_ROWS, HALF), jnp.int32),
        scratch_shapes=[pltpu.VMEM((2, ROW_TILE, HALF), jnp.int32), pltpu.SemaphoreType.DMA((2,))],
        compiler_params=_cparams(("arbitrary",), has_side_effects=True),
        name="scatter_rows",
    )(dest.reshape(nt, 1, ROW_TILE * TOP_K), x)


MOE_NF = D_EXPERT // MOE_TF


def _experts_kernel(ge_ref, gn_ref, de_ref, dn_ref, xs_ref, wg_ref, wu_ref, bg_ref, bu_ref, wd_ref, bd_ref,
                    ys_ref, xb_ref, act_ref, hold_ref):
    v = pl.program_id(0)
    f = pl.program_id(1)
    gn = gn_ref[v]
    dn = dn_ref[v]
    slot = v % 2
    half_nf = MOE_NF // 2

    @pl.when(jnp.logical_and(v == 0, f == 0))
    def _():
        hold_ref[...] = jnp.zeros_like(hold_ref)

    @pl.when(jnp.logical_and(gn > 0, f == 0))
    def _():
        r = lax.broadcasted_iota(jnp.int32, (MOE_TM, 1), 0)
        lo, hi = _unpack_rows(jnp.where(r < gn, xs_ref[...], 0))
        xb_ref[:, :HALF] = lo.astype(BF16)
        xb_ref[:, HALF:] = hi.astype(BF16)

    def gate_phase(rows):
        xb = xb_ref[:rows, :]
        gl = jnp.dot(xb, wg_ref[...].astype(BF16), preferred_element_type=F32) + bg_ref[...]
        up = jnp.dot(xb, wu_ref[...].astype(BF16), preferred_element_type=F32) + bu_ref[...]
        gl = jnp.minimum(gl, SWIGLU_LIMIT)
        up = jnp.clip(up, -SWIGLU_LIMIT, SWIGLU_LIMIT)
        act = ((up + 1.0) * (gl * jax.nn.sigmoid(SWIGLU_ALPHA * gl))).astype(BF16)
        act_ref[slot, :rows, pl.ds(pl.multiple_of(f * MOE_TF, MOE_TF), MOE_TF)] = act

    def down_phase(rows):
        y = jnp.dot(act_ref[1 - slot, :rows, :], wd_ref[...].astype(BF16),
                    preferred_element_type=F32) + bd_ref[...]
        h = f % half_nf
        ys_ref[:rows, :] = _pack_pair(hold_ref[h, :rows, :], y)
        hold_ref[h, :rows, :] = y
        if rows < MOE_TM:
            ys_ref[rows:, :] = jnp.zeros((MOE_TM - rows, MOE_TF), jnp.int32)

    for lower, rows in ((MOE_TM // 2, MOE_TM), (0, MOE_TM // 2)):
        pl.when(jnp.logical_and(gn > lower, gn <= rows))(functools.partial(gate_phase, rows))
        pl.when(jnp.logical_and(dn > lower, dn <= rows))(functools.partial(down_phase, rows))

    @pl.when(dn == 0)
    def _():
        ys_ref[...] = jnp.zeros_like(ys_ref)


def _experts(xs, visit_expert, visit_nvalid, w_gate_up, b_gate_up, w_down, b_down):
    nf = MOE_NF
    gate_e = jnp.concatenate([visit_expert, visit_expert[-1:]])
    gate_n = jnp.concatenate([visit_nvalid, jnp.zeros((1,), jnp.int32)])
    down_e = jnp.concatenate([visit_expert[:1], visit_expert])
    down_n = jnp.concatenate([jnp.zeros((1,), jnp.int32), visit_nvalid])

    def tile(f, n):
        return jnp.where(n > 0, f, nf - 1)

    grid_spec = pltpu.PrefetchScalarGridSpec(
        num_scalar_prefetch=4,
        grid=(N_VISITS + 1, nf),
        in_specs=[
            pl.BlockSpec((MOE_TM, HALF), lambda v, f, ge, gn, de, dn: (jnp.minimum(v, N_VISITS - 1), 0)),
            pl.BlockSpec((None, D_MODEL, MOE_TF), lambda v, f, ge, gn, de, dn: (ge[v], 0, tile(f, gn[v]))),
            pl.BlockSpec((None, D_MODEL, MOE_TF), lambda v, f, ge, gn, de, dn: (ge[v], 0, nf + tile(f, gn[v]))),
            pl.BlockSpec((None, 1, MOE_TF), lambda v, f, ge, gn, de, dn: (ge[v], 0, tile(f, gn[v]))),
            pl.BlockSpec((None, 1, MOE_TF), lambda v, f, ge, gn, de, dn: (ge[v], 0, nf + tile(f, gn[v]))),
            pl.BlockSpec((None, D_EXPERT, MOE_TF), lambda v, f, ge, gn, de, dn: (de[v], 0, tile(f, dn[v]))),
            pl.BlockSpec((None, 1, MOE_TF), lambda v, f, ge, gn, de, dn: (de[v], 0, tile(f, dn[v]))),
        ],
        out_specs=pl.BlockSpec((MOE_TM, MOE_TF),
                               lambda v, f, ge, gn, de, dn: (jnp.where(v == 0, N_VISITS, v - 1),
                                                             jnp.maximum(f - nf // 2, 0))),
        scratch_shapes=[pltpu.VMEM((MOE_TM, D_MODEL), BF16),
                        pltpu.VMEM((2, MOE_TM, D_EXPERT), BF16),
                        pltpu.VMEM((nf // 2, MOE_TM, MOE_TF), F32)],
    )
    return pl.pallas_call(
        _experts_kernel,
        grid_spec=grid_spec,
        out_shape=jax.ShapeDtypeStruct((MOE_ROWS + MOE_TM, HALF), jnp.int32),
        compiler_params=_cparams(("arbitrary", "arbitrary")),
        name="experts",
    )(gate_e, gate_n, down_e, down_n, xs, w_gate_up, w_gate_up, b_gate_up, b_gate_up, w_down, b_down)


def _combine_kernel(dest_ref, dest_next_ref, ys_ref, gate_ref, x_ref, lnw_ref, lnb_ref, o_ref, buf_ref, sem):
    i = pl.program_id(0)
    slot = i % 2

    def issue(idx_ref, s):
        def body(r, carry):
            for k in range(TOP_K):
                d = idx_ref[0, 0, r * TOP_K + k]
                pltpu.make_async_copy(ys_ref.at[pl.ds(d, 1), :], buf_ref.at[s, k, pl.ds(r, 1), :],
                                      sem.at[s]).start()
            return carry

        lax.fori_loop(0, ROW_TILE, body, 0)

    @pl.when(i == 0)
    def _():
        issue(dest_ref, slot)

    @pl.when(i + 1 < pl.num_programs(0))
    def _():
        issue(dest_next_ref, 1 - slot)

    for k in range(TOP_K):
        _tile_copies(ys_ref, buf_ref.at[slot, k], sem.at[slot]).wait()
    gates = gate_ref[...]
    h_lo = jnp.zeros((ROW_TILE, HALF), F32)
    h_hi = jnp.zeros((ROW_TILE, HALF), F32)
    for k in range(TOP_K):
        lo, hi = _unpack_rows(buf_ref[slot, k])
        h_lo = h_lo + gates[:, k:k + 1] * lo
        h_hi = h_hi + gates[:, k:k + 1] * hi
    h = jnp.concatenate([h_lo, h_hi], axis=1)
    o_ref[...] = _layer_norm(DN_ALPHA * x_ref[...] + h, lnw_ref[...], lnb_ref[...])


def _combine(ys, dest, gates, x, lnw, lnb):
    nt = TOKENS // ROW_TILE
    row = pl.BlockSpec((ROW_TILE, D_MODEL), lambda i: (i, 0))
    vec = pl.BlockSpec((1, D_MODEL), lambda i: (0, 0))
    dest_tiles = dest.reshape(nt, 1, ROW_TILE * TOP_K)
    return pl.pallas_call(
        _combine_kernel,
        grid=(nt,),
        in_specs=[pl.BlockSpec((1, 1, ROW_TILE * TOP_K), lambda i: (i, 0, 0), memory_space=pltpu.SMEM),
                  pl.BlockSpec((1, 1, ROW_TILE * TOP_K), lambda i: (jnp.minimum(i + 1, nt - 1), 0, 0),
                               memory_space=pltpu.SMEM),
                  pl.BlockSpec(memory_space=pl.ANY),
                  pl.BlockSpec((ROW_TILE, LANES), lambda i: (i, 0)),
                  row, vec, vec],
        out_specs=row,
        out_shape=jax.ShapeDtypeStruct((TOKENS, D_MODEL), F32),
        scratch_shapes=[pltpu.VMEM((2, TOP_K, ROW_TILE, HALF), jnp.int32), pltpu.SemaphoreType.DMA((2,))],
        compiler_params=_cparams(("arbitrary",)),
        name="combine",
    )(dest_tiles, dest_tiles, ys, gates, x, lnw, lnb)


def _layer(x, mem, w_in, attn_sinks, attn_norm_w, ret_norm_w, w_mix_out, ln_mix_w, ln_mix_b,
           w_mem_q, w_mem_kv, w_mem_out, ln_mem_w, ln_mem_b, w_router, b_router,
           w_gate_up, b_gate_up, w_down, b_down, ln_moe_w, ln_moe_b):
    o_k, o_v, o_qr = D_ATTN, D_ATTN + D_KV, D_ATTN + 2 * D_KV
    w_in_p = jnp.concatenate([w_in[:, :o_k], w_in[:, o_qr:], w_in[:, o_k:o_v], w_in[:, o_v:o_qr]],
                             axis=1).astype(BF16)
    proj = _project(x, w_in_p, PROJ_TM, PROJ_TN)
    y = _mixer(proj, attn_sinks.astype(F32), attn_norm_w.reshape(1, D_ATTN), ret_norm_w.reshape(1, D_RET))
    x1 = _out_ln(y, w_mix_out.astype(BF16), x, ln_mix_w.reshape(1, D_MODEL), ln_mix_b.reshape(1, D_MODEL))

    kv = _project(mem.reshape(BATCH * N_MEM, D_MODEL), w_mem_kv.astype(BF16), BATCH * N_MEM, D_MODEL)
    o = _mem_attn(x1, w_mem_q.astype(BF16), kv)
    x2 = _out_ln(o, w_mem_out.astype(BF16), x1, ln_mem_w.reshape(1, D_MODEL), ln_mem_b.reshape(1, D_MODEL))

    wr = jnp.pad(w_router, ((0, 0), (0, LANES - N_EXPERTS)))
    br = jnp.pad(b_router.reshape(1, N_EXPERTS), ((0, 0), (0, LANES - N_EXPERTS)), constant_values=NEG_INF)
    idx, gates, rank, counts = _route(x2, wr, br)

    counts = counts[0, :N_EXPERTS].astype(jnp.int32)
    padded = ((counts + MOE_TM - 1) // MOE_TM) * MOE_TM
    eid = jnp.arange(N_EXPERTS, dtype=jnp.int32)
    pad_end = jnp.sum(jnp.where(eid[None, :] <= eid[:, None], padded[None, :], 0), axis=1)
    pad_start = pad_end - padded
    dest = pad_start[idx[:, :TOP_K]] + rank[:, :TOP_K]
    visit_row = jnp.arange(N_VISITS, dtype=jnp.int32) * MOE_TM
    visit_expert = jnp.minimum(jnp.sum((pad_end[None, :] <= visit_row[:, None]).astype(jnp.int32), axis=1),
                               N_EXPERTS - 1)
    visit_nvalid = jnp.clip(counts[visit_expert] - (visit_row - pad_start[visit_expert]),
                            0, MOE_TM).astype(jnp.int32)

    xs = _scatter_rows(x2, dest)
    ys = _experts(xs, visit_expert, visit_nvalid,
                  w_gate_up, b_gate_up.reshape(N_EXPERTS, 1, 2 * D_EXPERT),
                  w_down, b_down.reshape(N_EXPERTS, 1, D_MODEL))
    return _combine(ys, dest, gates, x2, ln_moe_w.reshape(1, D_MODEL), ln_moe_b.reshape(1, D_MODEL))


def kernel(x, mem, w_in, attn_sinks, attn_norm_w, ret_norm_w, w_mix_out, ln_mix_w, ln_mix_b, w_mem_q, w_mem_kv, w_mem_out, ln_mem_w, ln_mem_b, w_router, b_router, w_gate_up, b_gate_up, w_down, b_down, ln_moe_w, ln_moe_b):
    xt = x.reshape(TOKENS, D_MODEL)
    out = _layer(xt, mem, w_in[0], attn_sinks[0], attn_norm_w[0], ret_norm_w[0], w_mix_out[0],
                 ln_mix_w[0], ln_mix_b[0], w_mem_q[0], w_mem_kv[0], w_mem_out[0], ln_mem_w[0], ln_mem_b[0],
                 w_router[0], b_router[0], w_gate_up[0], b_gate_up[0], w_down[0], b_down[0],
                 ln_moe_w[0], ln_moe_b[0])
    return out.reshape(BATCH, SEQ, D_MODEL)
```

```python
import functools
import math

import jax
import jax.numpy as jnp
from jax import lax
from jax.experimental import pallas as pl
from jax.experimental.pallas import tpu as pltpu

F32 = jnp.float32
BF16 = jnp.bfloat16

D_MODEL = 2048
BATCH = 2
SEQ = 16384
TOKENS = BATCH * SEQ

HEAD_DIM_ATTN = 64
N_Q_ATTN = 16
N_KV_ATTN = 2
GQA_GROUP = N_Q_ATTN // N_KV_ATTN
ATTN_BLOCK = 128
D_ATTN = N_Q_ATTN * HEAD_DIM_ATTN
D_KV = N_KV_ATTN * HEAD_DIM_ATTN
N_RET = 8
HEAD_DIM_RET = 128
RET_CHUNK = 128
D_RET = N_RET * HEAD_DIM_RET
D_IN = D_ATTN + 2 * D_KV + 4 * D_RET
N_MEM = 256
N_MEM_HEADS = 4
HEAD_DIM_MEM = D_MODEL // N_MEM_HEADS
N_EXPERTS = 32
TOP_K = 4
D_EXPERT = D_MODEL
SWIGLU_LIMIT = 7.0
SWIGLU_ALPHA = 1.702
LN_EPS = 1e-5
DN_ALPHA = 2.0 ** 0.25
NEG_INF = -1e30

LANES = 128
VMEM_LIMIT = 56 * 1024 * 1024

PROJ_TM = 1024
PROJ_TN = 1792
LN_TM = 512
MEM_TM = 512
ROUTE_TM = 512
ROW_TILE = 256
COMBINE_SLAB = 8
MOE_TM = 1024
MOE_TF = 256
N_VISITS = (TOKENS * TOP_K + N_EXPERTS * (MOE_TM - 1) + MOE_TM - 1) // MOE_TM
MOE_ROWS = N_VISITS * MOE_TM

COL_QA, COL_QR, COL_KR, COL_VR, COL_GR = 0, 1, 2, 3, 4
COL_KA = 5 * D_RET // D_KV
COL_VA = COL_KA + 1


def _cparams(sem, **kw):
    return pltpu.CompilerParams(dimension_semantics=sem, vmem_limit_bytes=VMEM_LIMIT, **kw)


def _layer_norm(z, w, b):
    mu = jnp.mean(z, axis=-1, keepdims=True)
    zc = z - mu
    var = jnp.mean(zc * zc, axis=-1, keepdims=True)
    return zc * lax.rsqrt(var + LN_EPS) * w + b


def _dot_nt(a, b):
    return lax.dot_general(a, b, (((1,), (1,)), ((), ())), preferred_element_type=F32)


def _proj_kernel(x_ref, w_ref, o_ref, xb_ref):
    @pl.when(pl.program_id(1) == 0)
    def _():
        xb_ref[...] = x_ref[...].astype(BF16)

    o_ref[...] = jnp.dot(xb_ref[...], w_ref[...], preferred_element_type=F32).astype(o_ref.dtype)


def _project(x, w, tm, tn):
    m, k = x.shape
    n = w.shape[1]
    return pl.pallas_call(
        _proj_kernel,
        grid=(m // tm, n // tn),
        in_specs=[pl.BlockSpec((tm, k), lambda i, j: (i, 0)),
                  pl.BlockSpec((k, tn), lambda i, j: (0, j))],
        out_specs=pl.BlockSpec((tm, tn), lambda i, j: (i, j)),
        out_shape=jax.ShapeDtypeStruct((m, n), BF16),
        scratch_shapes=[pltpu.VMEM((tm, k), BF16)],
        compiler_params=_cparams(("parallel", "arbitrary")),
        name="project",
    )(x, w)


def _mixer_kernel(sinkrow_ref, bias_cur_ref, bias_prev_ref, qa_ref, kp_ref, kc_ref, vp_ref, vc_ref,
                  qr_ref, kr_ref, vr_ref, gr_ref, anw_ref, rnw_ref, y_ref, state_ref, a_ref):
    n = pl.program_id(1)
    row = lax.broadcasted_iota(jnp.int32, (ATTN_BLOCK, ATTN_BLOCK), 0)
    col = lax.broadcasted_iota(jnp.int32, (ATTN_BLOCK, ATTN_BLOCK), 1)
    diff = row - col
    difff = diff.astype(F32)

    scale = HEAD_DIM_ATTN ** -0.5
    for j in range(N_KV_ATTN):
        klo, khi = j * HEAD_DIM_ATTN, (j + 1) * HEAD_DIM_ATTN
        heads = range(j * GQA_GROUP, (j + 1) * GQA_GROUP)
        q_rows = jnp.concatenate([qa_ref[:, h * HEAD_DIM_ATTN:(h + 1) * HEAD_DIM_ATTN] for h in heads], axis=0)
        sink = sinkrow_ref[j]
        s_cur = _dot_nt(kc_ref[:, klo:khi], q_rows) * scale + bias_cur_ref[j]
        s_prev = _dot_nt(kp_ref[:, klo:khi], q_rows) * scale + bias_prev_ref[j]
        s_prev = jnp.where(n > 0, s_prev, NEG_INF)
        m = jnp.maximum(jnp.max(s_cur, axis=0, keepdims=True), jnp.max(s_prev, axis=0, keepdims=True))
        m = jnp.maximum(m, sink)
        p_cur = jnp.exp(s_cur - m)
        p_prev = jnp.exp(s_prev - m)
        denom = (jnp.sum(p_cur, axis=0, keepdims=True) + jnp.sum(p_prev, axis=0, keepdims=True)
                 + jnp.exp(sink - m))
        v_cur_t = vc_ref[:, klo:khi].astype(F32).T.astype(BF16)
        v_prev_t = vp_ref[:, klo:khi].astype(F32).T.astype(BF16)
        o_t = (jnp.dot(v_cur_t, p_cur.astype(BF16), preferred_element_type=F32)
               + jnp.dot(v_prev_t, p_prev.astype(BF16), preferred_element_type=F32))
        o_t = o_t * (1.0 / denom)
        for g, h in enumerate(heads):
            a_ref[h * HEAD_DIM_ATTN:(h + 1) * HEAD_DIM_ATTN, :] = o_t[:, g * ATTN_BLOCK:(g + 1) * ATTN_BLOCK]
    a_t = a_ref[...]
    a_t = a_t * lax.rsqrt(jnp.mean(a_t * a_t, axis=0, keepdims=True) + LN_EPS)
    y_ref[:, :D_ATTN] = (a_t.T * anw_ref[...]).astype(y_ref.dtype)

    @pl.when(n == 0)
    def _():
        state_ref[...] = jnp.zeros_like(state_ref)

    pos = lax.broadcasted_iota(jnp.int32, (RET_CHUNK, 1), 0).astype(F32)
    kscale = HEAD_DIM_RET ** -0.5
    for h in range(N_RET):
        lo, hi = h * HEAD_DIM_RET, (h + 1) * HEAD_DIM_RET
        log_g = math.log1p(-(2.0 ** (-5.0 - h)))
        dmask = jnp.where(diff >= 0, jnp.exp(log_g * jnp.maximum(difff, 0.0)), 0.0)
        w_key = jnp.exp(log_g * (RET_CHUNK - 1.0 - pos))
        w_q = jnp.exp(log_g * (pos + 1.0))
        g_c = math.exp(log_g * RET_CHUNK)
        q = qr_ref[:, lo:hi]
        k = kr_ref[:, lo:hi]
        v = vr_ref[:, lo:hi]
        s = _dot_nt(q, k) * (dmask * kscale)
        inner = jnp.dot(s.astype(BF16), v, preferred_element_type=F32)
        state = state_ref[h]
        qw = (q.astype(F32) * w_q).astype(BF16)
        cross = jnp.dot(qw, state.astype(BF16), preferred_element_type=F32)
        kw_t = (k.astype(F32) * (w_key * kscale)).T.astype(BF16)
        state_ref[h] = g_c * state + jnp.dot(kw_t, v, preferred_element_type=F32)
        o = inner + cross
        mu = jnp.mean(o, axis=-1, keepdims=True)
        oc = o - mu
        var = jnp.mean(oc * oc, axis=-1, keepdims=True)
        o = oc * lax.rsqrt(var + LN_EPS) * rnw_ref[:, lo:hi]
        g = gr_ref[:, lo:hi].astype(F32)
        o = o * (g * jax.nn.sigmoid(g))
        y_ref[:, D_ATTN + lo:D_ATTN + hi] = o.astype(y_ref.dtype)


def _mixer(proj, sinks, attn_norm_w, ret_norm_w):
    nc = SEQ // ATTN_BLOCK
    blk = ATTN_BLOCK

    def wide(colblock):
        return pl.BlockSpec((blk, D_RET), lambda b, n: (b * nc + n, colblock))

    def kv_cur(colblock):
        return pl.BlockSpec((blk, D_KV), lambda b, n: (b * nc + n, colblock))

    def kv_prev(colblock):
        return pl.BlockSpec((blk, D_KV), lambda b, n: (b * nc + jnp.maximum(n - 1, 0), colblock))

    vec = pl.BlockSpec((1, D_RET), lambda b, n: (0, 0))
    gq = GQA_GROUP * blk

    def table(rows):
        return pl.BlockSpec((N_KV_ATTN, rows, gq), lambda b, n: (0, 0, 0))

    slopes = jnp.exp2(-8.0 * (jnp.arange(N_Q_ATTN, dtype=F32) + 1.0) / N_Q_ATTN).reshape(N_KV_ATTN, 1, GQA_GROUP, 1)
    key = jnp.arange(blk, dtype=F32).reshape(1, blk, 1, 1)
    query = jnp.arange(blk, dtype=F32).reshape(1, 1, 1, blk)
    dist_cur = jnp.broadcast_to(query - key, (N_KV_ATTN, blk, GQA_GROUP, blk))
    bias_cur = jnp.where(dist_cur >= 0, -slopes * dist_cur, NEG_INF).reshape(N_KV_ATTN, blk, gq)
    bias_prev = jnp.where(dist_cur < 0, -slopes * (dist_cur + blk), NEG_INF).reshape(N_KV_ATTN, blk, gq)
    sink_rows = jnp.repeat(sinks.astype(F32).reshape(N_KV_ATTN, 1, GQA_GROUP), blk, axis=2)
    return pl.pallas_call(
        _mixer_kernel,
        grid=(BATCH, nc),
        in_specs=[table(1), table(blk), table(blk),
                  wide(COL_QA), kv_prev(COL_KA), kv_cur(COL_KA), kv_prev(COL_VA), kv_cur(COL_VA),
                  wide(COL_QR), wide(COL_KR), wide(COL_VR), wide(COL_GR), vec, vec],
        out_specs=pl.BlockSpec((blk, D_MODEL), lambda b, n: (b * nc + n, 0)),
        out_shape=jax.ShapeDtypeStruct((TOKENS, D_MODEL), BF16),
        scratch_shapes=[pltpu.VMEM((N_RET, HEAD_DIM_RET, HEAD_DIM_RET), F32),
                        pltpu.VMEM((D_ATTN, blk), F32)],
        compiler_params=_cparams(("parallel", "arbitrary")),
        name="mixer",
    )(sink_rows, bias_cur, bias_prev, proj, proj, proj, proj, proj, proj, proj, proj, proj,
      attn_norm_w, ret_norm_w)


def _out_ln_kernel(y_ref, w_ref, x_ref, lnw_ref, lnb_ref, o_ref):
    h = jnp.dot(y_ref[...], w_ref[...], preferred_element_type=F32)
    o_ref[...] = _layer_norm(DN_ALPHA * x_ref[...] + h, lnw_ref[...], lnb_ref[...])


def _out_ln(y, w, x, lnw, lnb):
    tm = LN_TM
    row = pl.BlockSpec((tm, D_MODEL), lambda i: (i, 0))
    vec = pl.BlockSpec((1, D_MODEL), lambda i: (0, 0))
    return pl.pallas_call(
        _out_ln_kernel,
        grid=(TOKENS // tm,),
        in_specs=[row,
                  pl.BlockSpec((D_MODEL, D_MODEL), lambda i: (0, 0), pipeline_mode=pl.Buffered(1)),
                  row, vec, vec],
        out_specs=row,
        out_shape=jax.ShapeDtypeStruct((TOKENS, D_MODEL), F32),
        compiler_params=_cparams(("parallel",)),
        name="out_ln",
    )(y, w, x, lnw, lnb)


def _mem_attn_kernel(x_ref, wq_ref, k_ref, v_ref, o_ref):
    q = jnp.dot(x_ref[...].astype(BF16), wq_ref[...], preferred_element_type=F32)
    q = (q * (HEAD_DIM_MEM ** -0.5)).astype(BF16)
    for h in range(N_MEM_HEADS):
        lo, hi = h * HEAD_DIM_MEM, (h + 1) * HEAD_DIM_MEM
        s = _dot_nt(q[:, lo:hi], k_ref[:, lo:hi])
        m = jnp.max(s, axis=-1, keepdims=True)
        p = jnp.exp(s - m)
        denom = jnp.sum(p, axis=-1, keepdims=True)
        o = jnp.dot(p.astype(BF16), v_ref[:, lo:hi], preferred_element_type=F32)
        o_ref[:, lo:hi] = (o / denom).astype(o_ref.dtype)


def _mem_attn(x, wq, kv):
    tm = MEM_TM
    nt = SEQ // tm
    row = pl.BlockSpec((tm, D_MODEL), lambda b, i: (b * nt + i, 0))
    return pl.pallas_call(
        _mem_attn_kernel,
        grid=(BATCH, nt),
        in_specs=[row,
                  pl.BlockSpec((D_MODEL, D_MODEL), lambda b, i: (0, 0), pipeline_mode=pl.Buffered(1)),
                  pl.BlockSpec((N_MEM, D_MODEL), lambda b, i: (b, 0)),
                  pl.BlockSpec((N_MEM, D_MODEL), lambda b, i: (b, 1))],
        out_specs=row,
        out_shape=jax.ShapeDtypeStruct((TOKENS, D_MODEL), BF16),
        compiler_params=_cparams(("parallel", "parallel")),
        name="mem_attn",
    )(x, wq, kv, kv)


def _route_kernel(x_ref, wr_ref, br_ref, idx_ref, gate_ref, rank_ref, count_ref, carry_ref):
    tm = x_ref.shape[0]

    @pl.when(pl.program_id(0) == 0)
    def _():
        carry_ref[...] = jnp.zeros_like(carry_ref)

    x = x_ref[...]
    x_hi = x.astype(BF16)
    x_lo = (x - x_hi.astype(F32)).astype(BF16)
    part_hi = jnp.dot(x_hi, wr_ref[...], preferred_element_type=F32)
    part_lo = jnp.dot(x_lo, wr_ref[...], preferred_element_type=F32)
    logits = (part_hi[:, :LANES] + (part_hi[:, LANES:] + part_lo[:, :LANES]) + part_lo[:, LANES:]
              + br_ref[...])
    lane = lax.broadcasted_iota(jnp.int32, (tm, LANES), 1)
    lane_f = lane.astype(F32)
    work = logits
    sels, vals, idxs = [], [], []
    for _ in range(TOP_K):
        v = jnp.max(work, axis=-1, keepdims=True)
        i = jnp.min(jnp.where(work == v, lane_f, float(LANES)), axis=-1, keepdims=True)
        sel = lane_f == i
        work = jnp.where(sel, -jnp.inf, work)
        sels.append(sel)
        vals.append(v)
        idxs.append(i)
    exps = [jnp.exp(v - vals[0]) for v in vals]
    total = exps[0] + exps[1] + exps[2] + exps[3]

    onehot = jnp.zeros((tm, LANES), F32)
    for sel in sels:
        onehot = jnp.where(sel, 1.0, onehot)
    r = lax.broadcasted_iota(jnp.int32, (tm, tm), 0)
    c = lax.broadcasted_iota(jnp.int32, (tm, tm), 1)
    below = jnp.where(c < r, 1.0, 0.0).astype(BF16)
    prefix = jnp.dot(below, onehot.astype(BF16), preferred_element_type=F32)
    rankmat = prefix + carry_ref[...]
    carry_ref[...] = carry_ref[...] + jnp.sum(onehot, axis=0, keepdims=True)
    count_ref[...] = carry_ref[...]

    idx_out = jnp.zeros((tm, LANES), jnp.int32)
    gate_out = jnp.zeros((tm, LANES), F32)
    rank_out = jnp.zeros((tm, LANES), jnp.int32)
    for k in range(TOP_K):
        rk = jnp.sum(jnp.where(sels[k], rankmat, 0.0), axis=-1, keepdims=True).astype(jnp.int32)
        idx_out = jnp.where(lane == k, idxs[k].astype(jnp.int32), idx_out)
        gate_out = jnp.where(lane == k, exps[k] / total, gate_out)
        rank_out = jnp.where(lane == k, rk, rank_out)
    idx_ref[...] = idx_out
    gate_ref[...] = gate_out
    rank_ref[...] = rank_out


def _route(x, wr, br):
    tm = ROUTE_TM
    wide = pl.BlockSpec((tm, LANES), lambda i: (i, 0))
    return pl.pallas_call(
        _route_kernel,
        grid=(TOKENS // tm,),
        in_specs=[pl.BlockSpec((tm, D_MODEL), lambda i: (i, 0)),
                  pl.BlockSpec((D_MODEL, 2 * LANES), lambda i: (0, 0)),
                  pl.BlockSpec((1, LANES), lambda i: (0, 0))],
        out_specs=[wide, wide, wide, pl.BlockSpec((1, LANES), lambda i: (0, 0))],
        out_shape=[jax.ShapeDtypeStruct((TOKENS, LANES), jnp.int32),
                   jax.ShapeDtypeStruct((TOKENS, LANES), F32),
                   jax.ShapeDtypeStruct((TOKENS, LANES), jnp.int32),
                   jax.ShapeDtypeStruct((1, LANES), F32)],
        scratch_shapes=[pltpu.VMEM((1, LANES), F32)],
        compiler_params=_cparams(("arbitrary",)),
        name="route",
    )(x, wr, br)


HALF = D_MODEL // 2


def _pack_rows(x):
    return _pack_pair(x[:, :HALF], x[:, HALF:])


def _pack_pair(lo, hi):
    lo = lax.bitcast_convert_type(lo.astype(BF16).astype(F32), jnp.int32)
    hi = lax.bitcast_convert_type(hi.astype(BF16).astype(F32), jnp.int32)
    return hi | lax.shift_right_logical(lo, 16)


def _unpack_rows(w):
    lo = lax.bitcast_convert_type(lax.shift_left(w, 16), F32)
    hi = lax.bitcast_convert_type(w & jnp.int32(-65536), F32)
    return lo, hi


def _tile_copies(src_ref, dst_ref, sem):
    return pltpu.make_async_copy(src_ref.at[pl.ds(0, ROW_TILE), :], dst_ref.at[pl.ds(0, ROW_TILE), :], sem)


def _scatter_kernel(dest_ref, x_ref, xs_ref, packed_ref, sem):
    i = pl.program_id(0)
    slot = i % 2
    packed_ref[slot] = _pack_rows(x_ref[...])

    def body(r, carry):
        for k in range(TOP_K):
            d = dest_ref[0, 0, r * TOP_K + k]
            pltpu.make_async_copy(packed_ref.at[slot, pl.ds(r, 1), :], xs_ref.at[pl.ds(d, 1), :],
                                  sem.at[slot]).start()
        return carry

    lax.fori_loop(0, ROW_TILE, body, 0)

    @pl.when(i > 0)
    def _():
        for k in range(TOP_K):
            _tile_copies(packed_ref.at[1 - slot], xs_ref, sem.at[1 - slot]).wait()

    @pl.when(i == pl.num_programs(0) - 1)
    def _():
        for k in range(TOP_K):
            _tile_copies(packed_ref.at[slot], xs_ref, sem.at[slot]).wait()


def _scatter_rows(x, dest):
    nt = TOKENS // ROW_TILE
    return pl.pallas_call(
        _scatter_kernel,
        grid=(nt,),
        in_specs=[pl.BlockSpec((1, 1, ROW_TILE * TOP_K), lambda i: (i, 0, 0), memory_space=pltpu.SMEM),
                  pl.BlockSpec((ROW_TILE, D_MODEL), lambda i: (i, 0))],
        out_specs=pl.BlockSpec(memory_space=pl.ANY),
        out_shape=jax.ShapeDtypeStruct((MOE_ROWS, HALF), jnp.int32),
        scratch_shapes=[pltpu.VMEM((2, ROW_TILE, HALF), jnp.int32), pltpu.SemaphoreType.DMA((2,))],
        compiler_params=_cparams(("arbitrary",), has_side_effects=True),
        name="scatter_rows",
    )(dest.reshape(nt, 1, ROW_TILE * TOP_K), x)


MOE_NF = D_EXPERT // MOE_TF
MOE_GATE_CHUNK = 256


def _experts_kernel(ge_ref, gn_ref, de_ref, dn_ref, xs_ref, wg_ref, wu_ref, bg_ref, bu_ref, wd_ref, bd_ref,
                    ys_ref, xb_ref, act_ref, hold_ref):
    v = pl.program_id(0)
    f = pl.program_id(1)
    gn = gn_ref[v]
    dn = dn_ref[v]
    slot = v % 2
    half_nf = MOE_NF // 2

    @pl.when(jnp.logical_and(v == 0, f == 0))
    def _():
        hold_ref[...] = jnp.zeros_like(hold_ref)

    @pl.when(jnp.logical_and(gn > 0, f == 0))
    def _():
        r = lax.broadcasted_iota(jnp.int32, (MOE_TM, 1), 0)
        lo, hi = _unpack_rows(jnp.where(r < gn, xs_ref[...], 0))
        xb_ref[:, :HALF] = lo.astype(BF16)
        xb_ref[:, HALF:] = hi.astype(BF16)

    def gate_phase(rows):
        wg = wg_ref[...].astype(BF16)
        wu = wu_ref[...].astype(BF16)
        cols = pl.ds(pl.multiple_of(f * MOE_TF, MOE_TF), MOE_TF)
        for r0 in range(0, rows, MOE_GATE_CHUNK):
            xb = xb_ref[r0:r0 + MOE_GATE_CHUNK, :]
            gl = jnp.dot(xb, wg, preferred_element_type=F32) + bg_ref[...]
            up = jnp.dot(xb, wu, preferred_element_type=F32) + bu_ref[...]
            gl = jnp.minimum(gl, SWIGLU_LIMIT)
            up = jnp.clip(up, -SWIGLU_LIMIT, SWIGLU_LIMIT)
            act = ((up + 1.0) * (gl * jax.nn.sigmoid(SWIGLU_ALPHA * gl))).astype(BF16)
            act_ref[slot, r0:r0 + MOE_GATE_CHUNK, cols] = act

    def down_phase(rows):
        y = jnp.dot(act_ref[1 - slot, :rows, :], wd_ref[...].astype(BF16),
                    preferred_element_type=F32) + bd_ref[...]
        h = f % half_nf
        ys_ref[:rows, :] = _pack_pair(hold_ref[h, :rows, :], y)
        hold_ref[h, :rows, :] = y
        if rows < MOE_TM:
            ys_ref[rows:, :] = jnp.zeros((MOE_TM - rows, MOE_TF), jnp.int32)

    for lower, rows in ((MOE_TM // 2, MOE_TM), (0, MOE_TM // 2)):
        pl.when(jnp.logical_and(gn > lower, gn <= rows))(functools.partial(gate_phase, rows))
        pl.when(jnp.logical_and(dn > lower, dn <= rows))(functools.partial(down_phase, rows))

    @pl.when(dn == 0)
    def _():
        ys_ref[...] = jnp.zeros_like(ys_ref)


def _experts(xs, visit_expert, visit_nvalid, w_gate_up, b_gate_up, w_down, b_down):
    nf = MOE_NF
    gate_e = jnp.concatenate([visit_expert, visit_expert[-1:]])
    gate_n = jnp.concatenate([visit_nvalid, jnp.zeros((1,), jnp.int32)])
    down_e = jnp.concatenate([visit_expert[:1], visit_expert])
    down_n = jnp.concatenate([jnp.zeros((1,), jnp.int32), visit_nvalid])

    def tile(f, n):
        return jnp.where(n > 0, f, nf - 1)

    grid_spec = pltpu.PrefetchScalarGridSpec(
        num_scalar_prefetch=4,
        grid=(N_VISITS + 1, nf),
        in_specs=[
            pl.BlockSpec((MOE_TM, HALF), lambda v, f, ge, gn, de, dn: (jnp.minimum(v, N_VISITS - 1), 0)),
            pl.BlockSpec((None, D_MODEL, MOE_TF), lambda v, f, ge, gn, de, dn: (ge[v], 0, tile(f, gn[v]))),
            pl.BlockSpec((None, D_MODEL, MOE_TF), lambda v, f, ge, gn, de, dn: (ge[v], 0, nf + tile(f, gn[v]))),
            pl.BlockSpec((None, 1, MOE_TF), lambda v, f, ge, gn, de, dn: (ge[v], 0, tile(f, gn[v]))),
            pl.BlockSpec((None, 1, MOE_TF), lambda v, f, ge, gn, de, dn: (ge[v], 0, nf + tile(f, gn[v]))),
            pl.BlockSpec((None, D_EXPERT, MOE_TF), lambda v, f, ge, gn, de, dn: (de[v], 0, tile(f, dn[v]))),
            pl.BlockSpec((None, 1, MOE_TF), lambda v, f, ge, gn, de, dn: (de[v], 0, tile(f, dn[v]))),
        ],
        out_specs=pl.BlockSpec((MOE_TM, MOE_TF),
                               lambda v, f, ge, gn, de, dn: (jnp.where(v == 0, N_VISITS, v - 1),
                                                             jnp.maximum(f - nf // 2, 0))),
        scratch_shapes=[pltpu.VMEM((MOE_TM, D_MODEL), BF16),
                        pltpu.VMEM((2, MOE_TM, D_EXPERT), BF16),
                        pltpu.VMEM((nf // 2, MOE_TM, MOE_TF), F32)],
    )
    return pl.pallas_call(
        _experts_kernel,
        grid_spec=grid_spec,
        out_shape=jax.ShapeDtypeStruct((MOE_ROWS + MOE_TM, HALF), jnp.int32),
        compiler_params=_cparams(("arbitrary", "arbitrary")),
        name="experts",
    )(gate_e, gate_n, down_e, down_n, xs, w_gate_up, w_gate_up, b_gate_up, b_gate_up, w_down, b_down)


def _combine_kernel(dest_ref, dest_next_ref, ys_ref, gate_ref, x_ref, lnw_ref, lnb_ref, o_ref,
                    buf_ref, z_ref, sem):
    i = pl.program_id(0)
    slot = i % 2

    def issue(idx_ref, s):
        def body(r, carry):
            for k in range(TOP_K):
                d = idx_ref[0, 0, r * TOP_K + k]
                pltpu.make_async_copy(ys_ref.at[pl.ds(d, 1), :], buf_ref.at[s, k, pl.ds(r, 1), :],
                                      sem.at[s]).start()
            return carry

        lax.fori_loop(0, ROW_TILE, body, 0)

    @pl.when(i == 0)
    def _():
        issue(dest_ref, slot)

    for k in range(TOP_K):
        _tile_copies(ys_ref, buf_ref.at[slot, k], sem.at[slot]).wait()

    def slab(j, carry, prefetch):
        r0 = pl.multiple_of(j * COMBINE_SLAB, COMBINE_SLAB)
        if prefetch:
            for rr in range(COMBINE_SLAB):
                for k in range(TOP_K):
                    d = dest_next_ref[0, 0, (r0 + rr) * TOP_K + k]
                    pltpu.make_async_copy(ys_ref.at[pl.ds(d, 1), :],
                                          buf_ref.at[1 - slot, k, pl.ds(r0 + rr, 1), :],
                                          sem.at[1 - slot]).start()
        rows = pl.ds(r0, COMBINE_SLAB)
        gates = gate_ref[rows, :]
        z_lo = DN_ALPHA * x_ref[rows, :HALF]
        z_hi = DN_ALPHA * x_ref[rows, HALF:]
        for k in range(TOP_K):
            lo, hi = _unpack_rows(buf_ref[slot, k, rows, :])
            z_lo = z_lo + gates[:, k:k + 1] * lo
            z_hi = z_hi + gates[:, k:k + 1] * hi
        z_ref[rows, :HALF] = z_lo
        z_ref[rows, HALF:] = z_hi
        return carry

    last = i + 1 == pl.num_programs(0)

    @pl.when(jnp.logical_not(last))
    def _():
        lax.fori_loop(0, ROW_TILE // COMBINE_SLAB, functools.partial(slab, prefetch=True), 0)

    @pl.when(last)
    def _():
        lax.fori_loop(0, ROW_TILE // COMBINE_SLAB, functools.partial(slab, prefetch=False), 0)

    o_ref[...] = _layer_norm(z_ref[...], lnw_ref[...], lnb_ref[...])


def _combine(ys, dest, gates, x, lnw, lnb):
    nt = TOKENS // ROW_TILE
    row = pl.BlockSpec((ROW_TILE, D_MODEL), lambda i: (i, 0))
    vec = pl.BlockSpec((1, D_MODEL), lambda i: (0, 0))
    dest_tiles = dest.reshape(nt, 1, ROW_TILE * TOP_K)
    return pl.pallas_call(
        _combine_kernel,
        grid=(nt,),
        in_specs=[pl.BlockSpec((1, 1, ROW_TILE * TOP_K), lambda i: (i, 0, 0), memory_space=pltpu.SMEM),
                  pl.BlockSpec((1, 1, ROW_TILE * TOP_K), lambda i: (jnp.minimum(i + 1, nt - 1), 0, 0),
                               memory_space=pltpu.SMEM),
                  pl.BlockSpec(memory_space=pl.ANY),
                  pl.BlockSpec((ROW_TILE, LANES), lambda i: (i, 0)),
                  row, vec, vec],
        out_specs=row,
        out_shape=jax.ShapeDtypeStruct((TOKENS, D_MODEL), F32),
        scratch_shapes=[pltpu.VMEM((2, TOP_K, ROW_TILE, HALF), jnp.int32),
                        pltpu.VMEM((ROW_TILE, D_MODEL), F32), pltpu.SemaphoreType.DMA((2,))],
        compiler_params=_cparams(("arbitrary",)),
        name="combine",
    )(dest_tiles, dest_tiles, ys, gates, x, lnw, lnb)


def _layer(x, mem, w_in, attn_sinks, attn_norm_w, ret_norm_w, w_mix_out, ln_mix_w, ln_mix_b,
           w_mem_q, w_mem_kv, w_mem_out, ln_mem_w, ln_mem_b, w_router, b_router,
           w_gate_up, b_gate_up, w_down, b_down, ln_moe_w, ln_moe_b):
    o_k, o_v, o_qr = D_ATTN, D_ATTN + D_KV, D_ATTN + 2 * D_KV
    w_in_p = jnp.concatenate([w_in[:, :o_k], w_in[:, o_qr:], w_in[:, o_k:o_v], w_in[:, o_v:o_qr]],
                             axis=1).astype(BF16)
    proj = _project(x, w_in_p, PROJ_TM, PROJ_TN)
    y = _mixer(proj, attn_sinks.astype(F32), attn_norm_w.reshape(1, D_ATTN), ret_norm_w.reshape(1, D_RET))
    x1 = _out_ln(y, w_mix_out.astype(BF16), x, ln_mix_w.reshape(1, D_MODEL), ln_mix_b.reshape(1, D_MODEL))

    kv = _project(mem.reshape(BATCH * N_MEM, D_MODEL), w_mem_kv.astype(BF16), BATCH * N_MEM, D_MODEL)
    o = _mem_attn(x1, w_mem_q.astype(BF16), kv)
    x2 = _out_ln(o, w_mem_out.astype(BF16), x1, ln_mem_w.reshape(1, D_MODEL), ln_mem_b.reshape(1, D_MODEL))

    wr = jnp.pad(w_router, ((0, 0), (0, LANES - N_EXPERTS)))
    wr_hi = wr.astype(BF16)
    wr = jnp.concatenate([wr_hi, (wr - wr_hi.astype(F32)).astype(BF16)], axis=1)
    br =jnp.pad(b_router.reshape(1, N_EXPERTS), ((0, 0), (0, LANES - N_EXPERTS)), constant_values=NEG_INF)
    idx, gates, rank, counts = _route(x2, wr, br)

    counts = counts[0, :N_EXPERTS].astype(jnp.int32)
    padded = ((counts + MOE_TM - 1) // MOE_TM) * MOE_TM
    eid = jnp.arange(N_EXPERTS, dtype=jnp.int32)
    pad_end = jnp.sum(jnp.where(eid[None, :] <= eid[:, None], padded[None, :], 0), axis=1)
    pad_start = pad_end - padded
    dest = pad_start[idx[:, :TOP_K]] + rank[:, :TOP_K]
    visit_row = jnp.arange(N_VISITS, dtype=jnp.int32) * MOE_TM
    visit_expert = jnp.minimum(jnp.sum((pad_end[None, :] <= visit_row[:, None]).astype(jnp.int32), axis=1),
                               N_EXPERTS - 1)
    visit_nvalid = jnp.clip(counts[visit_expert] - (visit_row - pad_start[visit_expert]),
                            0, MOE_TM).astype(jnp.int32)

    xs = _scatter_rows(x2, dest)
    ys = _experts(xs, visit_expert, visit_nvalid,
                  w_gate_up, b_gate_up.reshape(N_EXPERTS, 1, 2 * D_EXPERT),
                  w_down, b_down.reshape(N_EXPERTS, 1, D_MODEL))
    return _combine(ys, dest, gates, x2, ln_moe_w.reshape(1, D_MODEL), ln_moe_b.reshape(1, D_MODEL))


def kernel(x, mem, w_in, attn_sinks, attn_norm_w, ret_norm_w, w_mix_out, ln_mix_w, ln_mix_b, w_mem_q, w_mem_kv, w_mem_out, ln_mem_w, ln_mem_b, w_router, b_router, w_gate_up, b_gate_up, w_down, b_down, ln_moe_w, ln_moe_b):
    xt = x.reshape(TOKENS, D_MODEL)
    out = _layer(xt, mem, w_in[0], attn_sinks[0], attn_norm_w[0], ret_norm_w[0], w_mix_out[0],
                 ln_mix_w[0], ln_mix_b[0], w_mem_q[0], w_mem_kv[0], w_mem_out[0], ln_mem_w[0], ln_mem_b[0],
                 w_router[0], b_router[0], w_gate_up[0], b_gate_up[0], w_down[0], b_down[0],
                 ln_moe_w[0], ln_moe_b[0])
    return out.reshape(BATCH, SEQ, D_MODEL)
```

```python
import functools
import math

import jax
import jax.numpy as jnp
from jax import lax
from jax.experimental import pallas as pl
from jax.experimental.pallas import tpu as pltpu

F32 = jnp.float32
BF16 = jnp.bfloat16

D_MODEL = 2048
BATCH = 2
SEQ = 16384
TOKENS = BATCH * SEQ

HEAD_DIM_ATTN = 64
N_Q_ATTN = 16
N_KV_ATTN = 2
GQA_GROUP = N_Q_ATTN // N_KV_ATTN
ATTN_BLOCK = 128
D_ATTN = N_Q_ATTN * HEAD_DIM_ATTN
D_KV = N_KV_ATTN * HEAD_DIM_ATTN
N_RET = 8
HEAD_DIM_RET = 128
RET_CHUNK = 128
D_RET = N_RET * HEAD_DIM_RET
D_IN = D_ATTN + 2 * D_KV + 4 * D_RET
N_MEM = 256
N_MEM_HEADS = 4
HEAD_DIM_MEM = D_MODEL // N_MEM_HEADS
N_EXPERTS = 32
TOP_K = 4
D_EXPERT = D_MODEL
SWIGLU_LIMIT = 7.0
SWIGLU_ALPHA = 1.702
LN_EPS = 1e-5
DN_ALPHA = 2.0 ** 0.25
NEG_INF = -1e30

LANES = 128
VMEM_LIMIT = 56 * 1024 * 1024

PROJ_TM = 1024
PROJ_TN = 1792
LN_TM = 512
MEM_TM = 512
ROUTE_TM = 512
ROW_TILE = 256
COMBINE_SLAB = 8
MOE_TM = 1024
MOE_TF = 256
N_VISITS = (TOKENS * TOP_K + N_EXPERTS * (MOE_TM - 1) + MOE_TM - 1) // MOE_TM
MOE_ROWS = N_VISITS * MOE_TM

COL_QA, COL_QR, COL_KR, COL_VR, COL_GR = 0, 1, 2, 3, 4
COL_KA = 5 * D_RET // D_KV
COL_VA = COL_KA + 1


def _cparams(sem, **kw):
    return pltpu.CompilerParams(dimension_semantics=sem, vmem_limit_bytes=VMEM_LIMIT, **kw)


def _layer_norm(z, w, b):
    mu = jnp.mean(z, axis=-1, keepdims=True)
    zc = z - mu
    var = jnp.mean(zc * zc, axis=-1, keepdims=True)
    return zc * lax.rsqrt(var + LN_EPS) * w + b


def _dot_nt(a, b):
    return lax.dot_general(a, b, (((1,), (1,)), ((), ())), preferred_element_type=F32)


def _proj_kernel(x_ref, w_ref, o_ref, xb_ref):
    @pl.when(pl.program_id(1) == 0)
    def _():
        xb_ref[...] = x_ref[...].astype(BF16)

    o_ref[...] = jnp.dot(xb_ref[...], w_ref[...], preferred_element_type=F32).astype(o_ref.dtype)


def _project(x, w, tm, tn):
    m, k = x.shape
    n = w.shape[1]
    return pl.pallas_call(
        _proj_kernel,
        grid=(m // tm, n // tn),
        in_specs=[pl.BlockSpec((tm, k), lambda i, j: (i, 0)),
                  pl.BlockSpec((k, tn), lambda i, j: (0, j))],
        out_specs=pl.BlockSpec((tm, tn), lambda i, j: (i, j)),
        out_shape=jax.ShapeDtypeStruct((m, n), BF16),
        scratch_shapes=[pltpu.VMEM((tm, k), BF16)],
        compiler_params=_cparams(("parallel", "arbitrary")),
        name="project",
    )(x, w)


def _mixer_kernel(sinkrow_ref, bias_cur_ref, bias_prev_ref, qa_ref, kp_ref, kc_ref, vp_ref, vc_ref,
                  qr_ref, kr_ref, vr_ref, gr_ref, anw_ref, rnw_ref, y_ref, state_ref, a_ref):
    n = pl.program_id(1)
    row = lax.broadcasted_iota(jnp.int32, (ATTN_BLOCK, ATTN_BLOCK), 0)
    col = lax.broadcasted_iota(jnp.int32, (ATTN_BLOCK, ATTN_BLOCK), 1)
    diff = row - col
    difff = diff.astype(F32)

    scale = HEAD_DIM_ATTN ** -0.5
    for j in range(N_KV_ATTN):
        klo, khi = j * HEAD_DIM_ATTN, (j + 1) * HEAD_DIM_ATTN
        heads = range(j * GQA_GROUP, (j + 1) * GQA_GROUP)
        q_rows = jnp.concatenate([qa_ref[:, h * HEAD_DIM_ATTN:(h + 1) * HEAD_DIM_ATTN] for h in heads], axis=0)
        q_rows = q_rows * scale
        sink = sinkrow_ref[j]
        s_cur = _dot_nt(kc_ref[:, klo:khi], q_rows) + bias_cur_ref[j]
        s_prev = _dot_nt(kp_ref[:, klo:khi], q_rows) + bias_prev_ref[j]
        s_prev = jnp.where(n > 0, s_prev, NEG_INF)
        m = jnp.maximum(jnp.max(s_cur, axis=0, keepdims=True), jnp.max(s_prev, axis=0, keepdims=True))
        m = jnp.maximum(m, sink)
        p_cur = jnp.exp(s_cur - m)
        p_prev = jnp.exp(s_prev - m)
        denom = (jnp.sum(p_cur, axis=0, keepdims=True) + jnp.sum(p_prev, axis=0, keepdims=True)
                 + jnp.exp(sink - m))
        v_cur_t = vc_ref[:, klo:khi].astype(F32).T.astype(BF16)
        v_prev_t = vp_ref[:, klo:khi].astype(F32).T.astype(BF16)
        o_t = (jnp.dot(v_cur_t, p_cur.astype(BF16), preferred_element_type=F32)
               + jnp.dot(v_prev_t, p_prev.astype(BF16), preferred_element_type=F32))
        o_t = o_t * (1.0 / denom)
        for g, h in enumerate(heads):
            a_ref[h * HEAD_DIM_ATTN:(h + 1) * HEAD_DIM_ATTN, :] = o_t[:, g * ATTN_BLOCK:(g + 1) * ATTN_BLOCK]
    a_t = a_ref[...]
    a_t = a_t * lax.rsqrt(jnp.mean(a_t * a_t, axis=0, keepdims=True) + LN_EPS)
    y_ref[:, :D_ATTN] = (a_t.T * anw_ref[...]).astype(y_ref.dtype)

    @pl.when(n == 0)
    def _():
        state_ref[...] = jnp.zeros_like(state_ref)

    pos = lax.broadcasted_iota(jnp.int32, (RET_CHUNK, 1), 0).astype(F32)
    kscale = HEAD_DIM_RET ** -0.5
    for h in range(N_RET):
        lo, hi = h * HEAD_DIM_RET, (h + 1) * HEAD_DIM_RET
        log_g = math.log1p(-(2.0 ** (-5.0 - h)))
        dmask = jnp.where(diff >= 0, jnp.exp(log_g * jnp.maximum(difff, 0.0)), 0.0)
        w_key = jnp.exp(log_g * (RET_CHUNK - 1.0 - pos))
        w_q = jnp.exp(log_g * (pos + 1.0))
        g_c = math.exp(log_g * RET_CHUNK)
        q = qr_ref[:, lo:hi]
        k = kr_ref[:, lo:hi]
        v = vr_ref[:, lo:hi]
        s = _dot_nt(q, k) * (dmask * kscale)
        inner = jnp.dot(s.astype(BF16), v, preferred_element_type=F32)
        state = state_ref[h]
        qw = (q.astype(F32) * w_q).astype(BF16)
        cross = jnp.dot(qw, state.astype(BF16), preferred_element_type=F32)
        kw_t = (k.astype(F32) * (w_key * kscale)).T.astype(BF16)
        state_ref[h] = g_c * state + jnp.dot(kw_t, v, preferred_element_type=F32)
        o = inner + cross
        mu = jnp.mean(o, axis=-1, keepdims=True)
        oc = o - mu
        var = jnp.mean(oc * oc, axis=-1, keepdims=True)
        o = oc * lax.rsqrt(var + LN_EPS) * rnw_ref[:, lo:hi]
        g = gr_ref[:, lo:hi].astype(F32)
        o = o * (g * jax.nn.sigmoid(g))
        y_ref[:, D_ATTN + lo:D_ATTN + hi] = o.astype(y_ref.dtype)


def _mixer(proj, sinks, attn_norm_w, ret_norm_w):
    nc = SEQ // ATTN_BLOCK
    blk = ATTN_BLOCK

    def wide(colblock):
        return pl.BlockSpec((blk, D_RET), lambda b, n: (b * nc + n, colblock))

    def kv_cur(colblock):
        return pl.BlockSpec((blk, D_KV), lambda b, n: (b * nc + n, colblock))

    def kv_prev(colblock):
        return pl.BlockSpec((blk, D_KV), lambda b, n: (b * nc + jnp.maximum(n - 1, 0), colblock))

    vec = pl.BlockSpec((1, D_RET), lambda b, n: (0, 0))
    gq = GQA_GROUP * blk

    def table(rows):
        return pl.BlockSpec((N_KV_ATTN, rows, gq), lambda b, n: (0, 0, 0))

    slopes = jnp.exp2(-8.0 * (jnp.arange(N_Q_ATTN, dtype=F32) + 1.0) / N_Q_ATTN).reshape(N_KV_ATTN, 1, GQA_GROUP, 1)
    key = jnp.arange(blk, dtype=F32).reshape(1, blk, 1, 1)
    query = jnp.arange(blk, dtype=F32).reshape(1, 1, 1, blk)
    dist_cur = jnp.broadcast_to(query - key, (N_KV_ATTN, blk, GQA_GROUP, blk))
    bias_cur = jnp.where(dist_cur >= 0, -slopes * dist_cur, NEG_INF).reshape(N_KV_ATTN, blk, gq)
    bias_prev = jnp.where(dist_cur < 0, -slopes * (dist_cur + blk), NEG_INF).reshape(N_KV_ATTN, blk, gq)
    sink_rows = jnp.repeat(sinks.astype(F32).reshape(N_KV_ATTN, 1, GQA_GROUP), blk, axis=2)
    return pl.pallas_call(
        _mixer_kernel,
        grid=(BATCH, nc),
        in_specs=[table(1), table(blk), table(blk),
                  wide(COL_QA), kv_prev(COL_KA), kv_cur(COL_KA), kv_prev(COL_VA), kv_cur(COL_VA),
                  wide(COL_QR), wide(COL_KR), wide(COL_VR), wide(COL_GR), vec, vec],
        out_specs=pl.BlockSpec((blk, D_MODEL), lambda b, n: (b * nc + n, 0)),
        out_shape=jax.ShapeDtypeStruct((TOKENS, D_MODEL), BF16),
        scratch_shapes=[pltpu.VMEM((N_RET, HEAD_DIM_RET, HEAD_DIM_RET), F32),
                        pltpu.VMEM((D_ATTN, blk), F32)],
        compiler_params=_cparams(("parallel", "arbitrary")),
        name="mixer",
    )(sink_rows, bias_cur, bias_prev, proj, proj, proj, proj, proj, proj, proj, proj, proj,
      attn_norm_w, ret_norm_w)


def _out_ln_kernel(y_ref, w_ref, x_ref, lnw_ref, lnb_ref, o_ref):
    h = jnp.dot(y_ref[...], w_ref[...], preferred_element_type=F32)
    o_ref[...] = _layer_norm(DN_ALPHA * x_ref[...] + h, lnw_ref[...], lnb_ref[...])


def _out_ln(y, w, x, lnw, lnb):
    tm = LN_TM
    row = pl.BlockSpec((tm, D_MODEL), lambda i: (i, 0))
    vec = pl.BlockSpec((1, D_MODEL), lambda i: (0, 0))
    return pl.pallas_call(
        _out_ln_kernel,
        grid=(TOKENS // tm,),
        in_specs=[row,
                  pl.BlockSpec((D_MODEL, D_MODEL), lambda i: (0, 0), pipeline_mode=pl.Buffered(1)),
                  row, vec, vec],
        out_specs=row,
        out_shape=jax.ShapeDtypeStruct((TOKENS, D_MODEL), F32),
        compiler_params=_cparams(("parallel",)),
        name="out_ln",
    )(y, w, x, lnw, lnb)


def _mem_attn_kernel(x_ref, wq_ref, k_ref, v_ref, o_ref):
    q = jnp.dot(x_ref[...].astype(BF16), wq_ref[...], preferred_element_type=F32)
    q = (q * (HEAD_DIM_MEM ** -0.5)).astype(BF16)
    for h in range(N_MEM_HEADS):
        lo, hi = h * HEAD_DIM_MEM, (h + 1) * HEAD_DIM_MEM
        s = _dot_nt(q[:, lo:hi], k_ref[:, lo:hi])
        m = jnp.max(s, axis=-1, keepdims=True)
        p = jnp.exp(s - m)
        denom = jnp.sum(p, axis=-1, keepdims=True)
        o = jnp.dot(p.astype(BF16), v_ref[:, lo:hi], preferred_element_type=F32)
        o_ref[:, lo:hi] = (o / denom).astype(o_ref.dtype)


def _mem_attn(x, wq, kv):
    tm = MEM_TM
    nt = SEQ // tm
    row = pl.BlockSpec((tm, D_MODEL), lambda b, i: (b * nt + i, 0))
    return pl.pallas_call(
        _mem_attn_kernel,
        grid=(BATCH, nt),
        in_specs=[row,
                  pl.BlockSpec((D_MODEL, D_MODEL), lambda b, i: (0, 0), pipeline_mode=pl.Buffered(1)),
                  pl.BlockSpec((N_MEM, D_MODEL), lambda b, i: (b, 0)),
                  pl.BlockSpec((N_MEM, D_MODEL), lambda b, i: (b, 1))],
        out_specs=row,
        out_shape=jax.ShapeDtypeStruct((TOKENS, D_MODEL), BF16),
        compiler_params=_cparams(("parallel", "parallel")),
        name="mem_attn",
    )(x, wq, kv, kv)


def _route_kernel(x_ref, wr_ref, br_ref, idx_ref, gate_ref, rank_ref, count_ref, carry_ref):
    tm = x_ref.shape[0]

    @pl.when(pl.program_id(0) == 0)
    def _():
        carry_ref[...] = jnp.zeros_like(carry_ref)

    x = x_ref[...]
    x_hi = x.astype(BF16)
    x_lo = (x - x_hi.astype(F32)).astype(BF16)
    part_hi = jnp.dot(x_hi, wr_ref[...], preferred_element_type=F32)
    part_lo = jnp.dot(x_lo, wr_ref[...], preferred_element_type=F32)
    logits = (part_hi[:, :LANES] + (part_hi[:, LANES:] + part_lo[:, :LANES]) + part_lo[:, LANES:]
              + br_ref[...])
    lane = lax.broadcasted_iota(jnp.int32, (tm, LANES), 1)
    lane_f = lane.astype(F32)
    work = logits
    sels, vals, idxs = [], [], []
    for _ in range(TOP_K):
        v = jnp.max(work, axis=-1, keepdims=True)
        i = jnp.min(jnp.where(work == v, lane_f, float(LANES)), axis=-1, keepdims=True)
        sel = lane_f == i
        work = jnp.where(sel, -jnp.inf, work)
        sels.append(sel)
        vals.append(v)
        idxs.append(i)
    exps = [jnp.exp(v - vals[0]) for v in vals]
    total = exps[0] + exps[1] + exps[2] + exps[3]

    onehot = jnp.zeros((tm, LANES), F32)
    for sel in sels:
        onehot = jnp.where(sel, 1.0, onehot)
    r = lax.broadcasted_iota(jnp.int32, (tm, tm), 0)
    c = lax.broadcasted_iota(jnp.int32, (tm, tm), 1)
    below = jnp.where(c < r, 1.0, 0.0).astype(BF16)
    prefix = jnp.dot(below, onehot.astype(BF16), preferred_element_type=F32)
    rankmat = prefix + carry_ref[...]
    carry_ref[...] = carry_ref[...] + jnp.sum(onehot, axis=0, keepdims=True)
    count_ref[...] = carry_ref[...]

    idx_out = jnp.zeros((tm, LANES), jnp.int32)
    gate_out = jnp.zeros((tm, LANES), F32)
    rank_out = jnp.zeros((tm, LANES), jnp.int32)
    for k in range(TOP_K):
        rk = jnp.sum(jnp.where(sels[k], rankmat, 0.0), axis=-1, keepdims=True).astype(jnp.int32)
        idx_out = jnp.where(lane == k, idxs[k].astype(jnp.int32), idx_out)
        gate_out = jnp.where(lane == k, exps[k] / total, gate_out)
        rank_out = jnp.where(lane == k, rk, rank_out)
    idx_ref[...] = idx_out
    gate_ref[...] = gate_out
    rank_ref[...] = rank_out


def _route(x, wr, br):
    tm = ROUTE_TM
    wide = pl.BlockSpec((tm, LANES), lambda i: (i, 0))
    return pl.pallas_call(
        _route_kernel,
        grid=(TOKENS // tm,),
        in_specs=[pl.BlockSpec((tm, D_MODEL), lambda i: (i, 0)),
                  pl.BlockSpec((D_MODEL, 2 * LANES), lambda i: (0, 0)),
                  pl.BlockSpec((1, LANES), lambda i: (0, 0))],
        out_specs=[wide, wide, wide, pl.BlockSpec((1, LANES), lambda i: (0, 0))],
        out_shape=[jax.ShapeDtypeStruct((TOKENS, LANES), jnp.int32),
                   jax.ShapeDtypeStruct((TOKENS, LANES), F32),
                   jax.ShapeDtypeStruct((TOKENS, LANES), jnp.int32),
                   jax.ShapeDtypeStruct((1, LANES), F32)],
        scratch_shapes=[pltpu.VMEM((1, LANES), F32)],
        compiler_params=_cparams(("arbitrary",)),
        name="route",
    )(x, wr, br)


HALF = D_MODEL // 2


def _pack_rows(x):
    return _pack_pair(x[:, :HALF], x[:, HALF:])


def _pack_pair(lo, hi):
    lo = lax.bitcast_convert_type(lo.astype(BF16).astype(F32), jnp.int32)
    hi = lax.bitcast_convert_type(hi.astype(BF16).astype(F32), jnp.int32)
    return hi | lax.shift_right_logical(lo, 16)


def _unpack_rows(w):
    lo = lax.bitcast_convert_type(lax.shift_left(w, 16), F32)
    hi = lax.bitcast_convert_type(w & jnp.int32(-65536), F32)
    return lo, hi


def _tile_copies(src_ref, dst_ref, sem):
    return pltpu.make_async_copy(src_ref.at[pl.ds(0, ROW_TILE), :], dst_ref.at[pl.ds(0, ROW_TILE), :], sem)


def _scatter_kernel(dest_ref, x_ref, xs_ref, packed_ref, sem):
    i = pl.program_id(0)
    slot = i % 2
    packed_ref[slot] = _pack_rows(x_ref[...])

    def body(r, carry):
        for k in range(TOP_K):
            d = dest_ref[0, 0, r * TOP_K + k]
            pltpu.make_async_copy(packed_ref.at[slot, pl.ds(r, 1), :], xs_ref.at[pl.ds(d, 1), :],
                                  sem.at[slot]).start()
        return carry

    lax.fori_loop(0, ROW_TILE, body, 0)

    @pl.when(i > 0)
    def _():
        for k in range(TOP_K):
            _tile_copies(packed_ref.at[1 - slot], xs_ref, sem.at[1 - slot]).wait()

    @pl.when(i == pl.num_programs(0) - 1)
    def _():
        for k in range(TOP_K):
            _tile_copies(packed_ref.at[slot], xs_ref, sem.at[slot]).wait()


def _scatter_rows(x, dest):
    nt = TOKENS // ROW_TILE
    return pl.pallas_call(
        _scatter_kernel,
        grid=(nt,),
        in_specs=[pl.BlockSpec((1, 1, ROW_TILE * TOP_K), lambda i: (i, 0, 0), memory_space=pltpu.SMEM),
                  pl.BlockSpec((ROW_TILE, D_MODEL), lambda i: (i, 0))],
        out_specs=pl.BlockSpec(memory_space=pl.ANY),
        out_shape=jax.ShapeDtypeStruct((MOE_ROWS, HALF), jnp.int32),
        scratch_shapes=[pltpu.VMEM((2, ROW_TILE, HALF), jnp.int32), pltpu.SemaphoreType.DMA((2,))],
        compiler_params=_cparams(("arbitrary",), has_side_effects=True),
        name="scatter_rows",
    )(dest.reshape(nt, 1, ROW_TILE * TOP_K), x)


MOE_NF = D_EXPERT // MOE_TF
MOE_ROW_STEP = 256


def _experts_kernel(ge_ref, gn_ref, de_ref, dn_ref, xs_ref, wg_ref, wu_ref, bg_ref, bu_ref, wd_ref, bd_ref,
                    ys_ref, xb_ref, act_ref, hold_ref):
    v = pl.program_id(0)
    f = pl.program_id(1)
    gn = gn_ref[v]
    dn = dn_ref[v]
    slot = v % 2
    half_nf = MOE_NF // 2

    @pl.when(jnp.logical_and(v == 0, f == 0))
    def _():
        hold_ref[...] = jnp.zeros_like(hold_ref)

    @pl.when(jnp.logical_and(gn > 0, f == 0))
    def _():
        r = lax.broadcasted_iota(jnp.int32, (MOE_TM, 1), 0)
        lo, hi = _unpack_rows(jnp.where(r < gn, xs_ref[...], 0))
        xb_ref[:, :HALF] = lo.astype(BF16)
        xb_ref[:, HALF:] = hi.astype(BF16)

    def gate_phase(rows):
        xb = xb_ref[:rows, :]
        gl = jnp.dot(xb, wg_ref[...].astype(BF16), preferred_element_type=F32) + bg_ref[...]
        up = jnp.dot(xb, wu_ref[...].astype(BF16), preferred_element_type=F32) + bu_ref[...]
        gl = jnp.minimum(gl, SWIGLU_LIMIT)
        up = jnp.clip(up, -SWIGLU_LIMIT, SWIGLU_LIMIT)
        act = ((up + 1.0) * (gl * jax.nn.sigmoid(SWIGLU_ALPHA * gl))).astype(BF16)
        act_ref[slot, :rows, pl.ds(pl.multiple_of(f * MOE_TF, MOE_TF), MOE_TF)] = act

    def down_phase(rows):
        y = jnp.dot(act_ref[1 - slot, :rows, :], wd_ref[...].astype(BF16),
                    preferred_element_type=F32) + bd_ref[...]
        h = f % half_nf
        ys_ref[:rows, :] = _pack_pair(hold_ref[h, :rows, :], y)
        hold_ref[h, :rows, :] = y
        if rows < MOE_TM:
            ys_ref[rows:, :] = jnp.zeros((MOE_TM - rows, MOE_TF), jnp.int32)

    for rows in range(MOE_TM, 0, -MOE_ROW_STEP):
        lower = rows - MOE_ROW_STEP
        pl.when(jnp.logical_and(gn > lower, gn <= rows))(functools.partial(gate_phase, rows))
        pl.when(jnp.logical_and(dn > lower, dn <= rows))(functools.partial(down_phase, rows))

    @pl.when(dn == 0)
    def _():
        ys_ref[...] = jnp.zeros_like(ys_ref)


def _experts(xs, visit_expert, visit_nvalid, w_gate_up, b_gate_up, w_down, b_down):
    nf = MOE_NF
    gate_e = jnp.concatenate([visit_expert, visit_expert[-1:]])
    gate_n = jnp.concatenate([visit_nvalid, jnp.zeros((1,), jnp.int32)])
    down_e = jnp.concatenate([visit_expert[:1], visit_expert])
    down_n = jnp.concatenate([jnp.zeros((1,), jnp.int32), visit_nvalid])

    def tile(f, n):
        return jnp.where(n > 0, f, nf - 1)

    grid_spec = pltpu.PrefetchScalarGridSpec(
        num_scalar_prefetch=4,
        grid=(N_VISITS + 1, nf),
        in_specs=[
            pl.BlockSpec((MOE_TM, HALF), lambda v, f, ge, gn, de, dn: (jnp.minimum(v, N_VISITS - 1), 0)),
            pl.BlockSpec((None, D_MODEL, MOE_TF), lambda v, f, ge, gn, de, dn: (ge[v], 0, tile(f, gn[v]))),
            pl.BlockSpec((None, D_MODEL, MOE_TF), lambda v, f, ge, gn, de, dn: (ge[v], 0, nf + tile(f, gn[v]))),
            pl.BlockSpec((None, 1, MOE_TF), lambda v, f, ge, gn, de, dn: (ge[v], 0, tile(f, gn[v]))),
            pl.BlockSpec((None, 1, MOE_TF), lambda v, f, ge, gn, de, dn: (ge[v], 0, nf + tile(f, gn[v]))),
            pl.BlockSpec((None, D_EXPERT, MOE_TF), lambda v, f, ge, gn, de, dn: (de[v], 0, tile(f, dn[v]))),
            pl.BlockSpec((None, 1, MOE_TF), lambda v, f, ge, gn, de, dn: (de[v], 0, tile(f, dn[v]))),
        ],
        out_specs=pl.BlockSpec((MOE_TM, MOE_TF),
                               lambda v, f, ge, gn, de, dn: (jnp.where(v == 0, N_VISITS, v - 1),
                                                             jnp.maximum(f - nf // 2, 0))),
        scratch_shapes=[pltpu.VMEM((MOE_TM, D_MODEL), BF16),
                        pltpu.VMEM((2, MOE_TM, D_EXPERT), BF16),
                        pltpu.VMEM((nf // 2, MOE_TM, MOE_TF), F32)],
    )
    return pl.pallas_call(
        _experts_kernel,
        grid_spec=grid_spec,
        out_shape=jax.ShapeDtypeStruct((MOE_ROWS + MOE_TM, HALF), jnp.int32),
        compiler_params=_cparams(("arbitrary", "arbitrary")),
        name="experts",
    )(gate_e, gate_n, down_e, down_n, xs, w_gate_up, w_gate_up, b_gate_up, b_gate_up, w_down, b_down)


def _combine_kernel(dest_ref, dest_next_ref, ys_ref, gate_ref, x_ref, lnw_ref, lnb_ref, o_ref,
                    buf_ref, z_ref, sem):
    i = pl.program_id(0)
    slot = i % 2

    def issue(idx_ref, s):
        def body(r, carry):
            for k in range(TOP_K):
                d = idx_ref[0, 0, r * TOP_K + k]
                pltpu.make_async_copy(ys_ref.at[pl.ds(d, 1), :], buf_ref.at[s, k, pl.ds(r, 1), :],
                                      sem.at[s]).start()
            return carry

        lax.fori_loop(0, ROW_TILE, body, 0)

    @pl.when(i == 0)
    def _():
        issue(dest_ref, slot)

    for k in range(TOP_K):
        _tile_copies(ys_ref, buf_ref.at[slot, k], sem.at[slot]).wait()

    def slab(j, carry, prefetch):
        r0 = pl.multiple_of(j * COMBINE_SLAB, COMBINE_SLAB)
        if prefetch:
            for rr in range(COMBINE_SLAB):
                for k in range(TOP_K):
                    d = dest_next_ref[0, 0, (r0 + rr) * TOP_K + k]
                    pltpu.make_async_copy(ys_ref.at[pl.ds(d, 1), :],
                                          buf_ref.at[1 - slot, k, pl.ds(r0 + rr, 1), :],
                                          sem.at[1 - slot]).start()
        rows = pl.ds(r0, COMBINE_SLAB)
        gates = gate_ref[rows, :]
        z_lo = DN_ALPHA * x_ref[rows, :HALF]
        z_hi = DN_ALPHA * x_ref[rows, HALF:]
        for k in range(TOP_K):
            lo, hi = _unpack_rows(buf_ref[slot, k, rows, :])
            z_lo = z_lo + gates[:, k:k + 1] * lo
            z_hi = z_hi + gates[:, k:k + 1] * hi
        z_ref[rows, :HALF] = z_lo
        z_ref[rows, HALF:] = z_hi
        return carry

    last = i + 1 == pl.num_programs(0)

    @pl.when(jnp.logical_not(last))
    def _():
        lax.fori_loop(0, ROW_TILE // COMBINE_SLAB, functools.partial(slab, prefetch=True), 0)

    @pl.when(last)
    def _():
        lax.fori_loop(0, ROW_TILE // COMBINE_SLAB, functools.partial(slab, prefetch=False), 0)

    o_ref[...] = _layer_norm(z_ref[...], lnw_ref[...], lnb_ref[...])


def _combine(ys, dest, gates, x, lnw, lnb):
    nt = TOKENS // ROW_TILE
    row = pl.BlockSpec((ROW_TILE, D_MODEL), lambda i: (i, 0))
    vec = pl.BlockSpec((1, D_MODEL), lambda i: (0, 0))
    dest_tiles = dest.reshape(nt, 1, ROW_TILE * TOP_K)
    return pl.pallas_call(
        _combine_kernel,
        grid=(nt,),
        in_specs=[pl.BlockSpec((1, 1, ROW_TILE * TOP_K), lambda i: (i, 0, 0), memory_space=pltpu.SMEM),
                  pl.BlockSpec((1, 1, ROW_TILE * TOP_K), lambda i: (jnp.minimum(i + 1, nt - 1), 0, 0),
                               memory_space=pltpu.SMEM),
                  pl.BlockSpec(memory_space=pl.ANY),
                  pl.BlockSpec((ROW_TILE, LANES), lambda i: (i, 0)),
                  row, vec, vec],
        out_specs=row,
        out_shape=jax.ShapeDtypeStruct((TOKENS, D_MODEL), F32),
        scratch_shapes=[pltpu.VMEM((2, TOP_K, ROW_TILE, HALF), jnp.int32),
                        pltpu.VMEM((ROW_TILE, D_MODEL), F32), pltpu.SemaphoreType.DMA((2,))],
        compiler_params=_cparams(("arbitrary",)),
        name="combine",
    )(dest_tiles, dest_tiles, ys, gates, x, lnw, lnb)


def _layer(x, mem, w_in, attn_sinks, attn_norm_w, ret_norm_w, w_mix_out, ln_mix_w, ln_mix_b,
           w_mem_q, w_mem_kv, w_mem_out, ln_mem_w, ln_mem_b, w_router, b_router,
           w_gate_up, b_gate_up, w_down, b_down, ln_moe_w, ln_moe_b):
    o_k, o_v, o_qr = D_ATTN, D_ATTN + D_KV, D_ATTN + 2 * D_KV
    w_in_p = jnp.concatenate([w_in[:, :o_k], w_in[:, o_qr:], w_in[:, o_k:o_v], w_in[:, o_v:o_qr]],
                             axis=1).astype(BF16)
    proj = _project(x, w_in_p, PROJ_TM, PROJ_TN)
    y = _mixer(proj, attn_sinks.astype(F32), attn_norm_w.reshape(1, D_ATTN), ret_norm_w.reshape(1, D_RET))
    x1 = _out_ln(y, w_mix_out.astype(BF16), x, ln_mix_w.reshape(1, D_MODEL), ln_mix_b.reshape(1, D_MODEL))

    kv = _project(mem.reshape(BATCH * N_MEM, D_MODEL), w_mem_kv.astype(BF16), BATCH * N_MEM, D_MODEL)
    o = _mem_attn(x1, w_mem_q.astype(BF16), kv)
    x2 = _out_ln(o, w_mem_out.astype(BF16), x1, ln_mem_w.reshape(1, D_MODEL), ln_mem_b.reshape(1, D_MODEL))

    wr = jnp.pad(w_router, ((0, 0), (0, LANES - N_EXPERTS)))
    wr_hi = wr.astype(BF16)
    wr = jnp.concatenate([wr_hi, (wr - wr_hi.astype(F32)).astype(BF16)], axis=1)
    br =jnp.pad(b_router.reshape(1, N_EXPERTS), ((0, 0), (0, LANES - N_EXPERTS)), constant_values=NEG_INF)
    idx, gates, rank, counts = _route(x2, wr, br)

    counts = counts[0, :N_EXPERTS].astype(jnp.int32)
    padded = ((counts + MOE_TM - 1) // MOE_TM) * MOE_TM
    eid = jnp.arange(N_EXPERTS, dtype=jnp.int32)
    pad_end = jnp.sum(jnp.where(eid[None, :] <= eid[:, None], padded[None, :], 0), axis=1)
    pad_start = pad_end - padded
    chosen = idx[:, :TOP_K, None] == eid[None, None, :]
    dest = jnp.sum(jnp.where(chosen, pad_start[None, None, :], 0), axis=-1) + rank[:, :TOP_K]
    visit_row = jnp.arange(N_VISITS, dtype=jnp.int32) * MOE_TM
    visit_expert = jnp.minimum(jnp.sum((pad_end[None, :] <= visit_row[:, None]).astype(jnp.int32), axis=1),
                               N_EXPERTS - 1)
    visit_nvalid = jnp.clip(counts[visit_expert] - (visit_row - pad_start[visit_expert]),
                            0, MOE_TM).astype(jnp.int32)

    xs = _scatter_rows(x2, dest)
    ys = _experts(xs, visit_expert, visit_nvalid,
                  w_gate_up, b_gate_up.reshape(N_EXPERTS, 1, 2 * D_EXPERT),
                  w_down, b_down.reshape(N_EXPERTS, 1, D_MODEL))
    return _combine(ys, dest, gates, x2, ln_moe_w.reshape(1, D_MODEL), ln_moe_b.reshape(1, D_MODEL))


def kernel(x, mem, w_in, attn_sinks, attn_norm_w, ret_norm_w, w_mix_out, ln_mix_w, ln_mix_b, w_mem_q, w_mem_kv, w_mem_out, ln_mem_w, ln_mem_b, w_router, b_router, w_gate_up, b_gate_up, w_down, b_down, ln_moe_w, ln_moe_b):
    xt = x.reshape(TOKENS, D_MODEL)
    out = _layer(xt, mem, w_in[0], attn_sinks[0], attn_norm_w[0], ret_norm_w[0], w_mix_out[0],
                 ln_mix_w[0], ln_mix_b[0], w_mem_q[0], w_mem_kv[0], w_mem_out[0], ln_mem_w[0], ln_mem_b[0],
                 w_router[0], b_router[0], w_gate_up[0], b_gate_up[0], w_down[0], b_down[0],
                 ln_moe_w[0], ln_moe_b[0])
    return out.reshape(BATCH, SEQ, D_MODEL)
```

```python
import functools
import math

import jax
import jax.numpy as jnp
from jax import lax
from jax.experimental import pallas as pl
from jax.experimental.pallas import tpu as pltpu

F32 = jnp.float32
BF16 = jnp.bfloat16

D_MODEL = 2048
BATCH = 2
SEQ = 16384
TOKENS = BATCH * SEQ

HEAD_DIM_ATTN = 64
N_Q_ATTN = 16
N_KV_ATTN = 2
GQA_GROUP = N_Q_ATTN // N_KV_ATTN
ATTN_BLOCK = 128
D_ATTN = N_Q_ATTN * HEAD_DIM_ATTN
D_KV = N_KV_ATTN * HEAD_DIM_ATTN
N_RET = 8
HEAD_DIM_RET = 128
RET_CHUNK = 128
D_RET = N_RET * HEAD_DIM_RET
D_IN = D_ATTN + 2 * D_KV + 4 * D_RET
N_MEM = 256
N_MEM_HEADS = 4
HEAD_DIM_MEM = D_MODEL // N_MEM_HEADS
N_EXPERTS = 32
TOP_K = 4
D_EXPERT = D_MODEL
SWIGLU_LIMIT = 7.0
SWIGLU_ALPHA = 1.702
LN_EPS = 1e-5
DN_ALPHA = 2.0 ** 0.25
NEG_INF = -1e30

LANES = 128
VMEM_LIMIT = 56 * 1024 * 1024

PROJ_TM = 1024
PROJ_TN = 1792
LN_TM = 512
MEM_TM = 512
ROUTE_TM = 512
ROW_TILE = 256
COMBINE_SLAB = 8
MOE_TM = 1024
MOE_TF = 256
N_VISITS = (TOKENS * TOP_K + N_EXPERTS * (MOE_TM - 1) + MOE_TM - 1) // MOE_TM
MOE_ROWS = N_VISITS * MOE_TM

COL_QA, COL_QR, COL_KR, COL_VR, COL_GR = 0, 1, 2, 3, 4
COL_KA = 5 * D_RET // D_KV
COL_VA = COL_KA + 1


def _cparams(sem, **kw):
    return pltpu.CompilerParams(dimension_semantics=sem, vmem_limit_bytes=VMEM_LIMIT, **kw)


def _layer_norm(z, w, b):
    mu = jnp.mean(z, axis=-1, keepdims=True)
    zc = z - mu
    var = jnp.mean(zc * zc, axis=-1, keepdims=True)
    return zc * lax.rsqrt(var + LN_EPS) * w + b


def _dot_nt(a, b):
    return lax.dot_general(a, b, (((1,), (1,)), ((), ())), preferred_element_type=F32)


def _proj_kernel(x_ref, w_ref, o_ref, xb_ref):
    @pl.when(pl.program_id(1) == 0)
    def _():
        xb_ref[...] = x_ref[...].astype(BF16)

    o_ref[...] = jnp.dot(xb_ref[...], w_ref[...], preferred_element_type=F32).astype(o_ref.dtype)


def _project(x, w, tm, tn):
    m, k = x.shape
    n = w.shape[1]
    return pl.pallas_call(
        _proj_kernel,
        grid=(m // tm, n // tn),
        in_specs=[pl.BlockSpec((tm, k), lambda i, j: (i, 0)),
                  pl.BlockSpec((k, tn), lambda i, j: (0, j))],
        out_specs=pl.BlockSpec((tm, tn), lambda i, j: (i, j)),
        out_shape=jax.ShapeDtypeStruct((m, n), BF16),
        scratch_shapes=[pltpu.VMEM((tm, k), BF16)],
        compiler_params=_cparams(("parallel", "arbitrary")),
        name="project",
    )(x, w)


def _mixer_kernel(sinkrow_ref, bias_cur_ref, bias_prev_ref, qa_ref, kp_ref, kc_ref, vp_ref, vc_ref,
                  qr_ref, kr_ref, vr_ref, gr_ref, anw_ref, rnw_ref, y_ref, state_ref, a_ref):
    n = pl.program_id(1)
    row = lax.broadcasted_iota(jnp.int32, (ATTN_BLOCK, ATTN_BLOCK), 0)
    col = lax.broadcasted_iota(jnp.int32, (ATTN_BLOCK, ATTN_BLOCK), 1)
    diff = row - col
    difff = diff.astype(F32)

    scale = HEAD_DIM_ATTN ** -0.5
    for j in range(N_KV_ATTN):
        klo, khi = j * HEAD_DIM_ATTN, (j + 1) * HEAD_DIM_ATTN
        heads = range(j * GQA_GROUP, (j + 1) * GQA_GROUP)
        q_rows = jnp.concatenate([qa_ref[:, h * HEAD_DIM_ATTN:(h + 1) * HEAD_DIM_ATTN] for h in heads], axis=0)
        q_rows = q_rows * scale
        sink = sinkrow_ref[j]
        s_cur = _dot_nt(kc_ref[:, klo:khi], q_rows) + bias_cur_ref[j]
        s_prev = _dot_nt(kp_ref[:, klo:khi], q_rows) + bias_prev_ref[j]
        s_prev = jnp.where(n > 0, s_prev, NEG_INF)
        m = jnp.maximum(jnp.max(s_cur, axis=0, keepdims=True), jnp.max(s_prev, axis=0, keepdims=True))
        m = jnp.maximum(m, sink)
        p_cur = jnp.exp(s_cur - m)
        p_prev = jnp.exp(s_prev - m)
        denom = (jnp.sum(p_cur, axis=0, keepdims=True) + jnp.sum(p_prev, axis=0, keepdims=True)
                 + jnp.exp(sink - m))
        v_cur_t = vc_ref[:, klo:khi].astype(F32).T.astype(BF16)
        v_prev_t = vp_ref[:, klo:khi].astype(F32).T.astype(BF16)
        o_t = (jnp.dot(v_cur_t, p_cur.astype(BF16), preferred_element_type=F32)
               + jnp.dot(v_prev_t, p_prev.astype(BF16), preferred_element_type=F32))
        o_t = o_t * (1.0 / denom)
        for g, h in enumerate(heads):
            a_ref[h * HEAD_DIM_ATTN:(h + 1) * HEAD_DIM_ATTN, :] = o_t[:, g * ATTN_BLOCK:(g + 1) * ATTN_BLOCK]
    a_t = a_ref[...]
    a_t = a_t * lax.rsqrt(jnp.mean(a_t * a_t, axis=0, keepdims=True) + LN_EPS)
    y_ref[:, :D_ATTN] = (a_t.T * anw_ref[...]).astype(y_ref.dtype)

    @pl.when(n == 0)
    def _():
        state_ref[...] = jnp.zeros_like(state_ref)

    pos = lax.broadcasted_iota(jnp.int32, (RET_CHUNK, 1), 0).astype(F32)
    kscale = HEAD_DIM_RET ** -0.5
    for h in range(N_RET):
        lo, hi = h * HEAD_DIM_RET, (h + 1) * HEAD_DIM_RET
        log_g = math.log1p(-(2.0 ** (-5.0 - h)))
        dmask = jnp.where(diff >= 0, jnp.exp(log_g * jnp.maximum(difff, 0.0)), 0.0)
        w_key = jnp.exp(log_g * (RET_CHUNK - 1.0 - pos))
        w_q = jnp.exp(log_g * (pos + 1.0))
        g_c = math.exp(log_g * RET_CHUNK)
        q = qr_ref[:, lo:hi]
        k = kr_ref[:, lo:hi]
        v = vr_ref[:, lo:hi]
        s = _dot_nt(q, k) * (dmask * kscale)
        inner = jnp.dot(s.astype(BF16), v, preferred_element_type=F32)
        state = state_ref[h]
        qw = (q.astype(F32) * w_q).astype(BF16)
        cross = jnp.dot(qw, state.astype(BF16), preferred_element_type=F32)
        kw_t = (k.astype(F32) * (w_key * kscale)).T.astype(BF16)
        state_ref[h] = g_c * state + jnp.dot(kw_t, v, preferred_element_type=F32)
        o = inner + cross
        mu = jnp.mean(o, axis=-1, keepdims=True)
        oc = o - mu
        var = jnp.mean(oc * oc, axis=-1, keepdims=True)
        o = oc * lax.rsqrt(var + LN_EPS) * rnw_ref[:, lo:hi]
        g = gr_ref[:, lo:hi].astype(F32)
        o = o * (g * jax.nn.sigmoid(g))
        y_ref[:, D_ATTN + lo:D_ATTN + hi] = o.astype(y_ref.dtype)


def _mixer(proj, sinks, attn_norm_w, ret_norm_w):
    nc = SEQ // ATTN_BLOCK
    blk = ATTN_BLOCK

    def wide(colblock):
        return pl.BlockSpec((blk, D_RET), lambda b, n: (b * nc + n, colblock))

    def kv_cur(colblock):
        return pl.BlockSpec((blk, D_KV), lambda b, n: (b * nc + n, colblock))

    def kv_prev(colblock):
        return pl.BlockSpec((blk, D_KV), lambda b, n: (b * nc + jnp.maximum(n - 1, 0), colblock))

    vec = pl.BlockSpec((1, D_RET), lambda b, n: (0, 0))
    gq = GQA_GROUP * blk

    def table(rows):
        return pl.BlockSpec((N_KV_ATTN, rows, gq), lambda b, n: (0, 0, 0))

    slopes = jnp.exp2(-8.0 * (jnp.arange(N_Q_ATTN, dtype=F32) + 1.0) / N_Q_ATTN).reshape(N_KV_ATTN, 1, GQA_GROUP, 1)
    key = jnp.arange(blk, dtype=F32).reshape(1, blk, 1, 1)
    query = jnp.arange(blk, dtype=F32).reshape(1, 1, 1, blk)
    dist_cur = jnp.broadcast_to(query - key, (N_KV_ATTN, blk, GQA_GROUP, blk))
    bias_cur = jnp.where(dist_cur >= 0, -slopes * dist_cur, NEG_INF).reshape(N_KV_ATTN, blk, gq)
    bias_prev = jnp.where(dist_cur < 0, -slopes * (dist_cur + blk), NEG_INF).reshape(N_KV_ATTN, blk, gq)
    sink_rows = jnp.repeat(sinks.astype(F32).reshape(N_KV_ATTN, 1, GQA_GROUP), blk, axis=2)
    return pl.pallas_call(
        _mixer_kernel,
        grid=(BATCH, nc),
        in_specs=[table(1), table(blk), table(blk),
                  wide(COL_QA), kv_prev(COL_KA), kv_cur(COL_KA), kv_prev(COL_VA), kv_cur(COL_VA),
                  wide(COL_QR), wide(COL_KR), wide(COL_VR), wide(COL_GR), vec, vec],
        out_specs=pl.BlockSpec((blk, D_MODEL), lambda b, n: (b * nc + n, 0)),
        out_shape=jax.ShapeDtypeStruct((TOKENS, D_MODEL), BF16),
        scratch_shapes=[pltpu.VMEM((N_RET, HEAD_DIM_RET, HEAD_DIM_RET), F32),
                        pltpu.VMEM((D_ATTN, blk), F32)],
        compiler_params=_cparams(("parallel", "arbitrary")),
        name="mixer",
    )(sink_rows, bias_cur, bias_prev, proj, proj, proj, proj, proj, proj, proj, proj, proj,
      attn_norm_w, ret_norm_w)


def _out_ln_kernel(y_ref, w_ref, x_ref, lnw_ref, lnb_ref, o_ref):
    h = jnp.dot(y_ref[...], w_ref[...], preferred_element_type=F32)
    o_ref[...] = _layer_norm(DN_ALPHA * x_ref[...] + h, lnw_ref[...], lnb_ref[...])


def _out_ln(y, w, x, lnw, lnb):
    tm = LN_TM
    row = pl.BlockSpec((tm, D_MODEL), lambda i: (i, 0))
    vec = pl.BlockSpec((1, D_MODEL), lambda i: (0, 0))
    return pl.pallas_call(
        _out_ln_kernel,
        grid=(TOKENS // tm,),
        in_specs=[row,
                  pl.BlockSpec((D_MODEL, D_MODEL), lambda i: (0, 0), pipeline_mode=pl.Buffered(1)),
                  row, vec, vec],
        out_specs=row,
        out_shape=jax.ShapeDtypeStruct((TOKENS, D_MODEL), F32),
        compiler_params=_cparams(("parallel",)),
        name="out_ln",
    )(y, w, x, lnw, lnb)


def _mem_attn_kernel(x_ref, wq_ref, k_ref, v_ref, o_ref):
    q = jnp.dot(x_ref[...].astype(BF16), wq_ref[...], preferred_element_type=F32)
    q = (q * (HEAD_DIM_MEM ** -0.5)).astype(BF16)
    for h in range(N_MEM_HEADS):
        lo, hi = h * HEAD_DIM_MEM, (h + 1) * HEAD_DIM_MEM
        s = _dot_nt(q[:, lo:hi], k_ref[:, lo:hi])
        m = jnp.max(s, axis=-1, keepdims=True)
        p = jnp.exp(s - m)
        denom = jnp.sum(p, axis=-1, keepdims=True)
        o = jnp.dot(p.astype(BF16), v_ref[:, lo:hi], preferred_element_type=F32)
        o_ref[:, lo:hi] = (o / denom).astype(o_ref.dtype)


def _mem_attn(x, wq, kv):
    tm = MEM_TM
    nt = SEQ // tm
    row = pl.BlockSpec((tm, D_MODEL), lambda b, i: (b * nt + i, 0))
    return pl.pallas_call(
        _mem_attn_kernel,
        grid=(BATCH, nt),
        in_specs=[row,
                  pl.BlockSpec((D_MODEL, D_MODEL), lambda b, i: (0, 0), pipeline_mode=pl.Buffered(1)),
                  pl.BlockSpec((N_MEM, D_MODEL), lambda b, i: (b, 0)),
                  pl.BlockSpec((N_MEM, D_MODEL), lambda b, i: (b, 1))],
        out_specs=row,
        out_shape=jax.ShapeDtypeStruct((TOKENS, D_MODEL), BF16),
        compiler_params=_cparams(("parallel", "parallel")),
        name="mem_attn",
    )(x, wq, kv, kv)


def _route_kernel(x_ref, wr_ref, br_ref, idx_ref, gate_ref, rank_ref, count_ref, carry_ref):
    tm = x_ref.shape[0]

    @pl.when(pl.program_id(0) == 0)
    def _():
        carry_ref[...] = jnp.zeros_like(carry_ref)

    x = x_ref[...]
    x_hi = x.astype(BF16)
    x_lo = (x - x_hi.astype(F32)).astype(BF16)
    part_hi = jnp.dot(x_hi, wr_ref[...], preferred_element_type=F32)
    part_lo = jnp.dot(x_lo, wr_ref[...], preferred_element_type=F32)
    logits = (part_hi[:, :LANES] + (part_hi[:, LANES:] + part_lo[:, :LANES]) + part_lo[:, LANES:]
              + br_ref[...])
    lane = lax.broadcasted_iota(jnp.int32, (tm, LANES), 1)
    lane_f = lane.astype(F32)
    work = logits
    sels, vals, idxs = [], [], []
    for _ in range(TOP_K):
        v = jnp.max(work, axis=-1, keepdims=True)
        i = jnp.min(jnp.where(work == v, lane_f, float(LANES)), axis=-1, keepdims=True)
        sel = lane_f == i
        work = jnp.where(sel, -jnp.inf, work)
        sels.append(sel)
        vals.append(v)
        idxs.append(i)
    exps = [jnp.exp(v - vals[0]) for v in vals]
    total = exps[0] + exps[1] + exps[2] + exps[3]

    onehot = jnp.zeros((tm, LANES), F32)
    for sel in sels:
        onehot = jnp.where(sel, 1.0, onehot)
    r = lax.broadcasted_iota(jnp.int32, (tm, tm), 0)
    c = lax.broadcasted_iota(jnp.int32, (tm, tm), 1)
    below = jnp.where(c < r, 1.0, 0.0).astype(BF16)
    prefix = jnp.dot(below, onehot.astype(BF16), preferred_element_type=F32)
    rankmat = prefix + carry_ref[...]
    carry_ref[...] = carry_ref[...] + jnp.sum(onehot, axis=0, keepdims=True)
    count_ref[...] = carry_ref[...]

    idx_out = jnp.zeros((tm, LANES), jnp.int32)
    gate_out = jnp.zeros((tm, LANES), F32)
    rank_out = jnp.zeros((tm, LANES), jnp.int32)
    for k in range(TOP_K):
        rk = jnp.sum(jnp.where(sels[k], rankmat, 0.0), axis=-1, keepdims=True).astype(jnp.int32)
        idx_out = jnp.where(lane == k, idxs[k].astype(jnp.int32), idx_out)
        gate_out = jnp.where(lane == k, exps[k] / total, gate_out)
        rank_out = jnp.where(lane == k, rk, rank_out)
    idx_ref[...] = idx_out
    gate_ref[...] = gate_out
    rank_ref[...] = rank_out


def _route(x, wr, br):
    tm = ROUTE_TM
    wide = pl.BlockSpec((tm, LANES), lambda i: (i, 0))
    return pl.pallas_call(
        _route_kernel,
        grid=(TOKENS // tm,),
        in_specs=[pl.BlockSpec((tm, D_MODEL), lambda i: (i, 0)),
                  pl.BlockSpec((D_MODEL, 2 * LANES), lambda i: (0, 0)),
                  pl.BlockSpec((1, LANES), lambda i: (0, 0))],
        out_specs=[wide, wide, wide, pl.BlockSpec((1, LANES), lambda i: (0, 0))],
        out_shape=[jax.ShapeDtypeStruct((TOKENS, LANES), jnp.int32),
                   jax.ShapeDtypeStruct((TOKENS, LANES), F32),
                   jax.ShapeDtypeStruct((TOKENS, LANES), jnp.int32),
                   jax.ShapeDtypeStruct((1, LANES), F32)],
        scratch_shapes=[pltpu.VMEM((1, LANES), F32)],
        compiler_params=_cparams(("arbitrary",)),
        name="route",
    )(x, wr, br)


HALF = D_MODEL // 2


def _pack_rows(x):
    return _pack_pair(x[:, :HALF], x[:, HALF:])


def _pack_pair(lo, hi):
    lo = lax.bitcast_convert_type(lo.astype(BF16).astype(F32), jnp.int32)
    hi = lax.bitcast_convert_type(hi.astype(BF16).astype(F32), jnp.int32)
    return hi | lax.shift_right_logical(lo, 16)


def _unpack_rows(w):
    lo = lax.bitcast_convert_type(lax.shift_left(w, 16), F32)
    hi = lax.bitcast_convert_type(w & jnp.int32(-65536), F32)
    return lo, hi


def _tile_copies(src_ref, dst_ref, sem):
    return pltpu.make_async_copy(src_ref.at[pl.ds(0, ROW_TILE)], dst_ref.at[pl.ds(0, ROW_TILE)], sem)


ROW_SUB = HALF // LANES


def _scatter_kernel(dest_ref, x_ref, xs_ref, packed_ref, sem):
    i = pl.program_id(0)
    slot = i % 2
    words = _pack_rows(x_ref[...])
    for c in range(ROW_SUB):
        packed_ref[slot, :, c, :] = words[:, c * LANES:(c + 1) * LANES]

    def body(r, carry):
        for k in range(TOP_K):
            d = dest_ref[0, 0, r * TOP_K + k]
            pltpu.make_async_copy(packed_ref.at[slot, r], xs_ref.at[d], sem.at[slot]).start()
        return carry

    lax.fori_loop(0, ROW_TILE, body, 0)

    @pl.when(i > 0)
    def _():
        for k in range(TOP_K):
            _tile_copies(packed_ref.at[1 - slot], xs_ref, sem.at[1 - slot]).wait()

    @pl.when(i == pl.num_programs(0) - 1)
    def _():
        for k in range(TOP_K):
            _tile_copies(packed_ref.at[slot], xs_ref, sem.at[slot]).wait()


def _scatter_rows(x, dest):
    nt = TOKENS // ROW_TILE
    return pl.pallas_call(
        _scatter_kernel,
        grid=(nt,),
        in_specs=[pl.BlockSpec((1, 1, ROW_TILE * TOP_K), lambda i: (i, 0, 0), memory_space=pltpu.SMEM),
                  pl.BlockSpec((ROW_TILE, D_MODEL), lambda i: (i, 0))],
        out_specs=pl.BlockSpec(memory_space=pl.ANY),
        out_shape=jax.ShapeDtypeStruct((MOE_ROWS, ROW_SUB, LANES), jnp.int32),
        scratch_shapes=[pltpu.VMEM((2, ROW_TILE, ROW_SUB, LANES), jnp.int32), pltpu.SemaphoreType.DMA((2,))],
        compiler_params=_cparams(("arbitrary",), has_side_effects=True),
        name="scatter_rows",
    )(dest.reshape(nt, 1, ROW_TILE * TOP_K), x)


MOE_NF = D_EXPERT // MOE_TF
MOE_ROW_STEP = 256


def _experts_kernel(ge_ref, gn_ref, de_ref, dn_ref, xs_ref, wg_ref, wu_ref, bg_ref, bu_ref, wd_ref, bd_ref,
                    ys_ref, xb_ref, act_ref, hold_ref, words_ref, xsem):
    v = pl.program_id(0)
    f = pl.program_id(1)
    gn = gn_ref[v]
    dn = dn_ref[v]
    slot = v % 2
    half_nf = MOE_NF // 2

    @pl.when(jnp.logical_and(v == 0, f == 0))
    def _():
        hold_ref[...] = jnp.zeros_like(hold_ref)

    def xs_copies(visit, s):
        row0 = pl.multiple_of(visit * MOE_TM, MOE_TM)
        return [pltpu.make_async_copy(xs_ref.at[pl.ds(row0, MOE_TM), c, :],
                                      words_ref.at[s, :, c * LANES:(c + 1) * LANES], xsem.at[s])
                for c in range(ROW_SUB)]

    @pl.when(f == 0)
    def _():
        @pl.when(v == 0)
        def _():
            for cp in xs_copies(0, 0):
                cp.start()

        @pl.when(v < N_VISITS)
        def _():
            for cp in xs_copies(v, slot):
                cp.wait()

        @pl.when(v + 1 < N_VISITS)
        def _():
            for cp in xs_copies(v + 1, 1 - slot):
                cp.start()

    @pl.when(jnp.logical_and(gn > 0, f == 0))
    def _():
        r = lax.broadcasted_iota(jnp.int32, (MOE_TM, 1), 0)
        lo, hi = _unpack_rows(jnp.where(r < gn, words_ref[slot], 0))
        xb_ref[:, :HALF] = lo.astype(BF16)
        xb_ref[:, HALF:] = hi.astype(BF16)

    def gate_phase(rows):
        xb = xb_ref[:rows, :]
        gl = jnp.dot(xb, wg_ref[...].astype(BF16), preferred_element_type=F32) + bg_ref[...]
        up = jnp.dot(xb, wu_ref[...].astype(BF16), preferred_element_type=F32) + bu_ref[...]
        gl = jnp.minimum(gl, SWIGLU_LIMIT)
        up = jnp.clip(up, -SWIGLU_LIMIT, SWIGLU_LIMIT)
        act = ((up + 1.0) * (gl * jax.nn.sigmoid(SWIGLU_ALPHA * gl))).astype(BF16)
        act_ref[slot, :rows, pl.ds(pl.multiple_of(f * MOE_TF, MOE_TF), MOE_TF)] = act

    def down_phase(rows):
        y = jnp.dot(act_ref[1 - slot, :rows, :], wd_ref[...].astype(BF16),
                    preferred_element_type=F32) + bd_ref[...]
        h = f % half_nf
        ys_ref[:rows, :] = _pack_pair(hold_ref[h, :rows, :], y)
        hold_ref[h, :rows, :] = y
        if rows < MOE_TM:
            ys_ref[rows:, :] = jnp.zeros((MOE_TM - rows, MOE_TF), jnp.int32)

    for rows in range(MOE_TM, 0, -MOE_ROW_STEP):
        lower = rows - MOE_ROW_STEP
        pl.when(jnp.logical_and(gn > lower, gn <= rows))(functools.partial(gate_phase, rows))
        pl.when(jnp.logical_and(dn > lower, dn <= rows))(functools.partial(down_phase, rows))

    @pl.when(dn == 0)
    def _():
        ys_ref[...] = jnp.zeros_like(ys_ref)


def _experts(xs, visit_expert, visit_nvalid, w_gate_up, b_gate_up, w_down, b_down):
    nf = MOE_NF
    gate_e = jnp.concatenate([visit_expert, visit_expert[-1:]])
    gate_n = jnp.concatenate([visit_nvalid, jnp.zeros((1,), jnp.int32)])
    down_e = jnp.concatenate([visit_expert[:1], visit_expert])
    down_n = jnp.concatenate([jnp.zeros((1,), jnp.int32), visit_nvalid])

    def tile(f, n):
        return jnp.where(n > 0, f, nf - 1)

    grid_spec = pltpu.PrefetchScalarGridSpec(
        num_scalar_prefetch=4,
        grid=(N_VISITS + 1, nf),
        in_specs=[
            pl.BlockSpec(memory_space=pl.ANY),
            pl.BlockSpec((None, D_MODEL, MOE_TF), lambda v, f, ge, gn, de, dn: (ge[v], 0, tile(f, gn[v]))),
            pl.BlockSpec((None, D_MODEL, MOE_TF), lambda v, f, ge, gn, de, dn: (ge[v], 0, nf + tile(f, gn[v]))),
            pl.BlockSpec((None, 1, MOE_TF), lambda v, f, ge, gn, de, dn: (ge[v], 0, tile(f, gn[v]))),
            pl.BlockSpec((None, 1, MOE_TF), lambda v, f, ge, gn, de, dn: (ge[v], 0, nf + tile(f, gn[v]))),
            pl.BlockSpec((None, D_EXPERT, MOE_TF), lambda v, f, ge, gn, de, dn: (de[v], 0, tile(f, dn[v]))),
            pl.BlockSpec((None, 1, MOE_TF), lambda v, f, ge, gn, de, dn: (de[v], 0, tile(f, dn[v]))),
        ],
        out_specs=pl.BlockSpec((MOE_TM, MOE_TF),
                               lambda v, f, ge, gn, de, dn: (jnp.where(v == 0, N_VISITS, v - 1),
                                                             jnp.maximum(f - nf // 2, 0))),
        scratch_shapes=[pltpu.VMEM((MOE_TM, D_MODEL), BF16),
                        pltpu.VMEM((2, MOE_TM, D_EXPERT), BF16),
                        pltpu.VMEM((nf // 2, MOE_TM, MOE_TF), F32),
                        pltpu.VMEM((2, MOE_TM, HALF), jnp.int32),
                        pltpu.SemaphoreType.DMA((2,))],
    )
    return pl.pallas_call(
        _experts_kernel,
        grid_spec=grid_spec,
        out_shape=jax.ShapeDtypeStruct((MOE_ROWS + MOE_TM, HALF), jnp.int32),
        compiler_params=_cparams(("arbitrary", "arbitrary")),
        name="experts",
    )(gate_e, gate_n, down_e, down_n, xs, w_gate_up, w_gate_up, b_gate_up, b_gate_up, w_down, b_down)


def _combine_kernel(dest_ref, dest_next_ref, ys_ref, gate_ref, x_ref, lnw_ref, lnb_ref, o_ref,
                    buf_ref, z_ref, sem):
    i = pl.program_id(0)
    slot = i % 2

    def issue(idx_ref, s):
        def body(r, carry):
            for k in range(TOP_K):
                d = idx_ref[0, 0, r * TOP_K + k]
                pltpu.make_async_copy(ys_ref.at[pl.ds(d, 1), :], buf_ref.at[s, k, pl.ds(r, 1), :],
                                      sem.at[s]).start()
            return carry

        lax.fori_loop(0, ROW_TILE, body, 0)

    @pl.when(i == 0)
    def _():
        issue(dest_ref, slot)

    for k in range(TOP_K):
        _tile_copies(ys_ref, buf_ref.at[slot, k], sem.at[slot]).wait()

    def slab(j, carry, prefetch):
        r0 = pl.multiple_of(j * COMBINE_SLAB, COMBINE_SLAB)
        if prefetch:
            for rr in range(COMBINE_SLAB):
                for k in range(TOP_K):
                    d = dest_next_ref[0, 0, (r0 + rr) * TOP_K + k]
                    pltpu.make_async_copy(ys_ref.at[pl.ds(d, 1), :],
                                          buf_ref.at[1 - slot, k, pl.ds(r0 + rr, 1), :],
                                          sem.at[1 - slot]).start()
        rows = pl.ds(r0, COMBINE_SLAB)
        gates = gate_ref[rows, :]
        z_lo = DN_ALPHA * x_ref[rows, :HALF]
        z_hi = DN_ALPHA * x_ref[rows, HALF:]
        for k in range(TOP_K):
            lo, hi = _unpack_rows(buf_ref[slot, k, rows, :])
            z_lo = z_lo + gates[:, k:k + 1] * lo
            z_hi = z_hi + gates[:, k:k + 1] * hi
        z_ref[rows, :HALF] = z_lo
        z_ref[rows, HALF:] = z_hi
        return carry

    last = i + 1 == pl.num_programs(0)

    @pl.when(jnp.logical_not(last))
    def _():
        lax.fori_loop(0, ROW_TILE // COMBINE_SLAB, functools.partial(slab, prefetch=True), 0)

    @pl.when(last)
    def _():
        lax.fori_loop(0, ROW_TILE // COMBINE_SLAB, functools.partial(slab, prefetch=False), 0)

    o_ref[...] = _layer_norm(z_ref[...], lnw_ref[...], lnb_ref[...])


def _combine(ys, dest, gates, x, lnw, lnb):
    nt = TOKENS // ROW_TILE
    row = pl.BlockSpec((ROW_TILE, D_MODEL), lambda i: (i, 0))
    vec = pl.BlockSpec((1, D_MODEL), lambda i: (0, 0))
    dest_tiles = dest.reshape(nt, 1, ROW_TILE * TOP_K)
    return pl.pallas_call(
        _combine_kernel,
        grid=(nt,),
        in_specs=[pl.BlockSpec((1, 1, ROW_TILE * TOP_K), lambda i: (i, 0, 0), memory_space=pltpu.SMEM),
                  pl.BlockSpec((1, 1, ROW_TILE * TOP_K), lambda i: (jnp.minimum(i + 1, nt - 1), 0, 0),
                               memory_space=pltpu.SMEM),
                  pl.BlockSpec(memory_space=pl.ANY),
                  pl.BlockSpec((ROW_TILE, LANES), lambda i: (i, 0)),
                  row, vec, vec],
        out_specs=row,
        out_shape=jax.ShapeDtypeStruct((TOKENS, D_MODEL), F32),
        scratch_shapes=[pltpu.VMEM((2, TOP_K, ROW_TILE, HALF), jnp.int32),
                        pltpu.VMEM((ROW_TILE, D_MODEL), F32), pltpu.SemaphoreType.DMA((2,))],
        compiler_params=_cparams(("arbitrary",)),
        name="combine",
    )(dest_tiles, dest_tiles, ys, gates, x, lnw, lnb)


def _layer(x, mem, w_in, attn_sinks, attn_norm_w, ret_norm_w, w_mix_out, ln_mix_w, ln_mix_b,
           w_mem_q, w_mem_kv, w_mem_out, ln_mem_w, ln_mem_b, w_router, b_router,
           w_gate_up, b_gate_up, w_down, b_down, ln_moe_w, ln_moe_b):
    o_k, o_v, o_qr = D_ATTN, D_ATTN + D_KV, D_ATTN + 2 * D_KV
    w_in_p = jnp.concatenate([w_in[:, :o_k], w_in[:, o_qr:], w_in[:, o_k:o_v], w_in[:, o_v:o_qr]],
                             axis=1).astype(BF16)
    proj = _project(x, w_in_p, PROJ_TM, PROJ_TN)
    y = _mixer(proj, attn_sinks.astype(F32), attn_norm_w.reshape(1, D_ATTN), ret_norm_w.reshape(1, D_RET))
    x1 = _out_ln(y, w_mix_out.astype(BF16), x, ln_mix_w.reshape(1, D_MODEL), ln_mix_b.reshape(1, D_MODEL))

    kv = _project(mem.reshape(BATCH * N_MEM, D_MODEL), w_mem_kv.astype(BF16), BATCH * N_MEM, D_MODEL)
    o = _mem_attn(x1, w_mem_q.astype(BF16), kv)
    x2 = _out_ln(o, w_mem_out.astype(BF16), x1, ln_mem_w.reshape(1, D_MODEL), ln_mem_b.reshape(1, D_MODEL))

    wr = jnp.pad(w_router, ((0, 0), (0, LANES - N_EXPERTS)))
    wr_hi = wr.astype(BF16)
    wr = jnp.concatenate([wr_hi, (wr - wr_hi.astype(F32)).astype(BF16)], axis=1)
    br =jnp.pad(b_router.reshape(1, N_EXPERTS), ((0, 0), (0, LANES - N_EXPERTS)), constant_values=NEG_INF)
    idx, gates, rank, counts = _route(x2, wr, br)

    counts = counts[0, :N_EXPERTS].astype(jnp.int32)
    padded = ((counts + MOE_TM - 1) // MOE_TM) * MOE_TM
    eid = jnp.arange(N_EXPERTS, dtype=jnp.int32)
    pad_end = jnp.sum(jnp.where(eid[None, :] <= eid[:, None], padded[None, :], 0), axis=1)
    pad_start = pad_end - padded
    chosen = idx[:, :TOP_K, None] == eid[None, None, :]
    dest = jnp.sum(jnp.where(chosen, pad_start[None, None, :], 0), axis=-1) + rank[:, :TOP_K]
    visit_row = jnp.arange(N_VISITS, dtype=jnp.int32) * MOE_TM
    visit_expert = jnp.minimum(jnp.sum((pad_end[None, :] <= visit_row[:, None]).astype(jnp.int32), axis=1),
                               N_EXPERTS - 1)
    visit_nvalid = jnp.clip(counts[visit_expert] - (visit_row - pad_start[visit_expert]),
                            0, MOE_TM).astype(jnp.int32)

    xs = _scatter_rows(x2, dest)
    ys = _experts(xs, visit_expert, visit_nvalid,
                  w_gate_up, b_gate_up.reshape(N_EXPERTS, 1, 2 * D_EXPERT),
                  w_down, b_down.reshape(N_EXPERTS, 1, D_MODEL))
    return _combine(ys, dest, gates, x2, ln_moe_w.reshape(1, D_MODEL), ln_moe_b.reshape(1, D_MODEL))


def kernel(x, mem, w_in, attn_sinks, attn_norm_w, ret_norm_w, w_mix_out, ln_mix_w, ln_mix_b, w_mem_q, w_mem_kv, w_mem_out, ln_mem_w, ln_mem_b, w_router, b_router, w_gate_up, b_gate_up, w_down, b_down, ln_moe_w, ln_moe_b):
    xt = x.reshape(TOKENS, D_MODEL)
    out = _layer(xt, mem, w_in[0], attn_sinks[0], attn_norm_w[0], ret_norm_w[0], w_mix_out[0],
                 ln_mix_w[0], ln_mix_b[0], w_mem_q[0], w_mem_kv[0], w_mem_out[0], ln_mem_w[0], ln_mem_b[0],
                 w_router[0], b_router[0], w_gate_up[0], b_gate_up[0], w_down[0], b_down[0],
                 ln_moe_w[0], ln_moe_b[0])
    return out.reshape(BATCH, SEQ, D_MODEL)
```

```python
import functools
import math

import jax
import jax.numpy as jnp
from jax import lax
from jax.experimental import pallas as pl
from jax.experimental.pallas import tpu as pltpu

F32 = jnp.float32
BF16 = jnp.bfloat16

D_MODEL = 2048
BATCH = 2
SEQ = 16384
TOKENS = BATCH * SEQ

HEAD_DIM_ATTN = 64
N_Q_ATTN = 16
N_KV_ATTN = 2
GQA_GROUP = N_Q_ATTN // N_KV_ATTN
ATTN_BLOCK = 128
D_ATTN = N_Q_ATTN * HEAD_DIM_ATTN
D_KV = N_KV_ATTN * HEAD_DIM_ATTN
N_RET = 8
HEAD_DIM_RET = 128
RET_CHUNK = 128
D_RET = N_RET * HEAD_DIM_RET
D_IN = D_ATTN + 2 * D_KV + 4 * D_RET
N_MEM = 256
N_MEM_HEADS = 4
HEAD_DIM_MEM = D_MODEL // N_MEM_HEADS
N_EXPERTS = 32
TOP_K = 4
D_EXPERT = D_MODEL
SWIGLU_LIMIT = 7.0
SWIGLU_ALPHA = 1.702
LN_EPS = 1e-5
DN_ALPHA = 2.0 ** 0.25
NEG_INF = -1e30

LANES = 128
VMEM_LIMIT = 56 * 1024 * 1024

PROJ_TM = 1024
PROJ_TN = 1792
LN_TM = 512
MEM_TM = 512
ROUTE_TM = 512
ROW_TILE = 256
COMBINE_SLAB = 8
MOE_TM = 1024
MOE_TF = 256
N_VISITS = (TOKENS * TOP_K + N_EXPERTS * (MOE_TM - 1) + MOE_TM - 1) // MOE_TM
MOE_ROWS = N_VISITS * MOE_TM

COL_QA, COL_QR, COL_KR, COL_VR, COL_GR = 0, 1, 2, 3, 4
COL_KA = 5 * D_RET // D_KV
COL_VA = COL_KA + 1


def _cparams(sem, **kw):
    return pltpu.CompilerParams(dimension_semantics=sem, vmem_limit_bytes=VMEM_LIMIT, **kw)


def _layer_norm(z, w, b):
    mu = jnp.mean(z, axis=-1, keepdims=True)
    zc = z - mu
    var = jnp.mean(zc * zc, axis=-1, keepdims=True)
    return zc * lax.rsqrt(var + LN_EPS) * w + b


def _dot_nt(a, b):
    return lax.dot_general(a, b, (((1,), (1,)), ((), ())), preferred_element_type=F32)


def _proj_kernel(x_ref, w_ref, o_ref, xb_ref):
    @pl.when(pl.program_id(1) == 0)
    def _():
        xb_ref[...] = x_ref[...].astype(BF16)

    o_ref[...] = jnp.dot(xb_ref[...], w_ref[...], preferred_element_type=F32).astype(o_ref.dtype)


def _project(x, w, tm, tn):
    m, k = x.shape
    n = w.shape[1]
    return pl.pallas_call(
        _proj_kernel,
        grid=(m // tm, n // tn),
        in_specs=[pl.BlockSpec((tm, k), lambda i, j: (i, 0)),
                  pl.BlockSpec((k, tn), lambda i, j: (0, j))],
        out_specs=pl.BlockSpec((tm, tn), lambda i, j: (i, j)),
        out_shape=jax.ShapeDtypeStruct((m, n), BF16),
        scratch_shapes=[pltpu.VMEM((tm, k), BF16)],
        compiler_params=_cparams(("parallel", "arbitrary")),
        name="project",
    )(x, w)


def _mixer_kernel(sinkrow_ref, bias_cur_ref, bias_prev_ref, qa_ref, kp_ref, kc_ref, vp_ref, vc_ref,
                  qr_ref, kr_ref, vr_ref, gr_ref, anw_ref, rnw_ref, y_ref, state_ref, a_ref):
    n = pl.program_id(1)
    row = lax.broadcasted_iota(jnp.int32, (ATTN_BLOCK, ATTN_BLOCK), 0)
    col = lax.broadcasted_iota(jnp.int32, (ATTN_BLOCK, ATTN_BLOCK), 1)
    diff = row - col
    difff = diff.astype(F32)

    scale = HEAD_DIM_ATTN ** -0.5
    for j in range(N_KV_ATTN):
        klo, khi = j * HEAD_DIM_ATTN, (j + 1) * HEAD_DIM_ATTN
        heads = range(j * GQA_GROUP, (j + 1) * GQA_GROUP)
        q_rows = jnp.concatenate([qa_ref[:, h * HEAD_DIM_ATTN:(h + 1) * HEAD_DIM_ATTN] for h in heads], axis=0)
        q_rows = q_rows * scale
        sink = sinkrow_ref[j]
        s_cur = _dot_nt(kc_ref[:, klo:khi], q_rows) + bias_cur_ref[j]
        s_prev = _dot_nt(kp_ref[:, klo:khi], q_rows) + bias_prev_ref[j]
        s_prev = jnp.where(n > 0, s_prev, NEG_INF)
        m = jnp.maximum(jnp.max(s_cur, axis=0, keepdims=True), jnp.max(s_prev, axis=0, keepdims=True))
        m = jnp.maximum(m, sink)
        p_cur = jnp.exp(s_cur - m)
        p_prev = jnp.exp(s_prev - m)
        denom = (jnp.sum(p_cur, axis=0, keepdims=True) + jnp.sum(p_prev, axis=0, keepdims=True)
                 + jnp.exp(sink - m))
        v_cur_t = vc_ref[:, klo:khi].astype(F32).T.astype(BF16)
        v_prev_t = vp_ref[:, klo:khi].astype(F32).T.astype(BF16)
        o_t = (jnp.dot(v_cur_t, p_cur.astype(BF16), preferred_element_type=F32)
               + jnp.dot(v_prev_t, p_prev.astype(BF16), preferred_element_type=F32))
        o_t = o_t * (1.0 / denom)
        for g, h in enumerate(heads):
            a_ref[h * HEAD_DIM_ATTN:(h + 1) * HEAD_DIM_ATTN, :] = o_t[:, g * ATTN_BLOCK:(g + 1) * ATTN_BLOCK]
    a_t = a_ref[...]
    a_t = a_t * lax.rsqrt(jnp.mean(a_t * a_t, axis=0, keepdims=True) + LN_EPS)
    y_ref[:, :D_ATTN] = (a_t.T * anw_ref[...]).astype(y_ref.dtype)

    @pl.when(n == 0)
    def _():
        state_ref[...] = jnp.zeros_like(state_ref)

    pos = lax.broadcasted_iota(jnp.int32, (RET_CHUNK, 1), 0).astype(F32)
    kscale = HEAD_DIM_RET ** -0.5
    for h in range(N_RET):
        lo, hi = h * HEAD_DIM_RET, (h + 1) * HEAD_DIM_RET
        log_g = math.log1p(-(2.0 ** (-5.0 - h)))
        dmask = jnp.where(diff >= 0, jnp.exp(log_g * jnp.maximum(difff, 0.0)), 0.0)
        w_key = jnp.exp(log_g * (RET_CHUNK - 1.0 - pos))
        w_q = jnp.exp(log_g * (pos + 1.0))
        g_c = math.exp(log_g * RET_CHUNK)
        q = qr_ref[:, lo:hi]
        k = kr_ref[:, lo:hi]
        v = vr_ref[:, lo:hi]
        s = _dot_nt(q, k) * (dmask * kscale)
        inner = jnp.dot(s.astype(BF16), v, preferred_element_type=F32)
        state = state_ref[h]
        qw = (q.astype(F32) * w_q).astype(BF16)
        cross = jnp.dot(qw, state.astype(BF16), preferred_element_type=F32)
        kw_t = (k.astype(F32) * (w_key * kscale)).T.astype(BF16)
        state_ref[h] = g_c * state + jnp.dot(kw_t, v, preferred_element_type=F32)
        o = inner + cross
        mu = jnp.mean(o, axis=-1, keepdims=True)
        oc = o - mu
        var = jnp.mean(oc * oc, axis=-1, keepdims=True)
        o = oc * lax.rsqrt(var + LN_EPS) * rnw_ref[:, lo:hi]
        g = gr_ref[:, lo:hi].astype(F32)
        o = o * (g * jax.nn.sigmoid(g))
        y_ref[:, D_ATTN + lo:D_ATTN + hi] = o.astype(y_ref.dtype)


def _mixer(proj, sinks, attn_norm_w, ret_norm_w):
    nc = SEQ // ATTN_BLOCK
    blk = ATTN_BLOCK

    def wide(colblock):
        return pl.BlockSpec((blk, D_RET), lambda b, n: (b * nc + n, colblock))

    def kv_cur(colblock):
        return pl.BlockSpec((blk, D_KV), lambda b, n: (b * nc + n, colblock))

    def kv_prev(colblock):
        return pl.BlockSpec((blk, D_KV), lambda b, n: (b * nc + jnp.maximum(n - 1, 0), colblock))

    vec = pl.BlockSpec((1, D_RET), lambda b, n: (0, 0))
    gq = GQA_GROUP * blk

    def table(rows):
        return pl.BlockSpec((N_KV_ATTN, rows, gq), lambda b, n: (0, 0, 0))

    slopes = jnp.exp2(-8.0 * (jnp.arange(N_Q_ATTN, dtype=F32) + 1.0) / N_Q_ATTN).reshape(N_KV_ATTN, 1, GQA_GROUP, 1)
    key = jnp.arange(blk, dtype=F32).reshape(1, blk, 1, 1)
    query = jnp.arange(blk, dtype=F32).reshape(1, 1, 1, blk)
    dist_cur = jnp.broadcast_to(query - key, (N_KV_ATTN, blk, GQA_GROUP, blk))
    bias_cur = jnp.where(dist_cur >= 0, -slopes * dist_cur, NEG_INF).reshape(N_KV_ATTN, blk, gq)
    bias_prev = jnp.where(dist_cur < 0, -slopes * (dist_cur + blk), NEG_INF).reshape(N_KV_ATTN, blk, gq)
    sink_rows = jnp.repeat(sinks.astype(F32).reshape(N_KV_ATTN, 1, GQA_GROUP), blk, axis=2)
    return pl.pallas_call(
        _mixer_kernel,
        grid=(BATCH, nc),
        in_specs=[table(1), table(blk), table(blk),
                  wide(COL_QA), kv_prev(COL_KA), kv_cur(COL_KA), kv_prev(COL_VA), kv_cur(COL_VA),
                  wide(COL_QR), wide(COL_KR), wide(COL_VR), wide(COL_GR), vec, vec],
        out_specs=pl.BlockSpec((blk, D_MODEL), lambda b, n: (b * nc + n, 0)),
        out_shape=jax.ShapeDtypeStruct((TOKENS, D_MODEL), BF16),
        scratch_shapes=[pltpu.VMEM((N_RET, HEAD_DIM_RET, HEAD_DIM_RET), F32),
                        pltpu.VMEM((D_ATTN, blk), F32)],
        compiler_params=_cparams(("parallel", "arbitrary")),
        name="mixer",
    )(sink_rows, bias_cur, bias_prev, proj, proj, proj, proj, proj, proj, proj, proj, proj,
      attn_norm_w, ret_norm_w)


def _out_ln_kernel(y_ref, w_ref, x_ref, lnw_ref, lnb_ref, o_ref):
    h = jnp.dot(y_ref[...], w_ref[...], preferred_element_type=F32)
    o_ref[...] = _layer_norm(DN_ALPHA * x_ref[...] + h, lnw_ref[...], lnb_ref[...])


def _out_ln(y, w, x, lnw, lnb):
    tm = LN_TM
    row = pl.BlockSpec((tm, D_MODEL), lambda i: (i, 0))
    vec = pl.BlockSpec((1, D_MODEL), lambda i: (0, 0))
    return pl.pallas_call(
        _out_ln_kernel,
        grid=(TOKENS // tm,),
        in_specs=[row,
                  pl.BlockSpec((D_MODEL, D_MODEL), lambda i: (0, 0), pipeline_mode=pl.Buffered(1)),
                  row, vec, vec],
        out_specs=row,
        out_shape=jax.ShapeDtypeStruct((TOKENS, D_MODEL), F32),
        compiler_params=_cparams(("parallel",)),
        name="out_ln",
    )(y, w, x, lnw, lnb)


def _mem_attn_kernel(x_ref, wq_ref, k_ref, v_ref, o_ref):
    q = jnp.dot(x_ref[...].astype(BF16), wq_ref[...], preferred_element_type=F32)
    q = (q * (HEAD_DIM_MEM ** -0.5)).astype(BF16)
    for h in range(N_MEM_HEADS):
        lo, hi = h * HEAD_DIM_MEM, (h + 1) * HEAD_DIM_MEM
        s = _dot_nt(q[:, lo:hi], k_ref[:, lo:hi])
        m = jnp.max(s, axis=-1, keepdims=True)
        p = jnp.exp(s - m)
        denom = jnp.sum(p, axis=-1, keepdims=True)
        o = jnp.dot(p.astype(BF16), v_ref[:, lo:hi], preferred_element_type=F32)
        o_ref[:, lo:hi] = (o / denom).astype(o_ref.dtype)


def _mem_attn(x, wq, kv):
    tm = MEM_TM
    nt = SEQ // tm
    row = pl.BlockSpec((tm, D_MODEL), lambda b, i: (b * nt + i, 0))
    return pl.pallas_call(
        _mem_attn_kernel,
        grid=(BATCH, nt),
        in_specs=[row,
                  pl.BlockSpec((D_MODEL, D_MODEL), lambda b, i: (0, 0), pipeline_mode=pl.Buffered(1)),
                  pl.BlockSpec((N_MEM, D_MODEL), lambda b, i: (b, 0)),
                  pl.BlockSpec((N_MEM, D_MODEL), lambda b, i: (b, 1))],
        out_specs=row,
        out_shape=jax.ShapeDtypeStruct((TOKENS, D_MODEL), BF16),
        compiler_params=_cparams(("parallel", "parallel")),
        name="mem_attn",
    )(x, wq, kv, kv)


def _route_kernel(x_ref, wr_ref, br_ref, idx_ref, gate_ref, rank_ref, count_ref, carry_ref):
    tm = x_ref.shape[0]

    @pl.when(pl.program_id(0) == 0)
    def _():
        carry_ref[...] = jnp.zeros_like(carry_ref)

    x = x_ref[...]
    x_hi = x.astype(BF16)
    x_lo = (x - x_hi.astype(F32)).astype(BF16)
    part_hi = jnp.dot(x_hi, wr_ref[...], preferred_element_type=F32)
    part_lo = jnp.dot(x_lo, wr_ref[...], preferred_element_type=F32)
    logits = (part_hi[:, :LANES] + (part_hi[:, LANES:] + part_lo[:, :LANES]) + part_lo[:, LANES:]
              + br_ref[...])
    lane = lax.broadcasted_iota(jnp.int32, (tm, LANES), 1)
    lane_f = lane.astype(F32)
    work = logits
    sels, vals, idxs = [], [], []
    for _ in range(TOP_K):
        v = jnp.max(work, axis=-1, keepdims=True)
        i = jnp.min(jnp.where(work == v, lane_f, float(LANES)), axis=-1, keepdims=True)
        sel = lane_f == i
        work = jnp.where(sel, -jnp.inf, work)
        sels.append(sel)
        vals.append(v)
        idxs.append(i)
    exps = [jnp.exp(v - vals[0]) for v in vals]
    total = exps[0] + exps[1] + exps[2] + exps[3]

    onehot = jnp.zeros((tm, LANES), F32)
    for sel in sels:
        onehot = jnp.where(sel, 1.0, onehot)
    r = lax.broadcasted_iota(jnp.int32, (tm, tm), 0)
    c = lax.broadcasted_iota(jnp.int32, (tm, tm), 1)
    below = jnp.where(c < r, 1.0, 0.0).astype(BF16)
    prefix = jnp.dot(below, onehot.astype(BF16), preferred_element_type=F32)
    rankmat = prefix + carry_ref[...]
    carry_ref[...] = carry_ref[...] + jnp.sum(onehot, axis=0, keepdims=True)
    count_ref[...] = carry_ref[...]

    idx_out = jnp.zeros((tm, LANES), jnp.int32)
    gate_out = jnp.zeros((tm, LANES), F32)
    rank_out = jnp.zeros((tm, LANES), jnp.int32)
    for k in range(TOP_K):
        rk = jnp.sum(jnp.where(sels[k], rankmat, 0.0), axis=-1, keepdims=True).astype(jnp.int32)
        idx_out = jnp.where(lane == k, idxs[k].astype(jnp.int32), idx_out)
        gate_out = jnp.where(lane == k, exps[k] / total, gate_out)
        rank_out = jnp.where(lane == k, rk, rank_out)
    idx_ref[...] = idx_out
    gate_ref[...] = gate_out
    rank_ref[...] = rank_out


def _route(x, wr, br):
    tm = ROUTE_TM
    wide = pl.BlockSpec((tm, LANES), lambda i: (i, 0))
    return pl.pallas_call(
        _route_kernel,
        grid=(TOKENS // tm,),
        in_specs=[pl.BlockSpec((tm, D_MODEL), lambda i: (i, 0)),
                  pl.BlockSpec((D_MODEL, 2 * LANES), lambda i: (0, 0)),
                  pl.BlockSpec((1, LANES), lambda i: (0, 0))],
        out_specs=[wide, wide, wide, pl.BlockSpec((1, LANES), lambda i: (0, 0))],
        out_shape=[jax.ShapeDtypeStruct((TOKENS, LANES), jnp.int32),
                   jax.ShapeDtypeStruct((TOKENS, LANES), F32),
                   jax.ShapeDtypeStruct((TOKENS, LANES), jnp.int32),
                   jax.ShapeDtypeStruct((1, LANES), F32)],
        scratch_shapes=[pltpu.VMEM((1, LANES), F32)],
        compiler_params=_cparams(("arbitrary",)),
        name="route",
    )(x, wr, br)


HALF = D_MODEL // 2


def _pack_rows(x):
    return _pack_pair(x[:, :HALF], x[:, HALF:])


def _pack_pair(lo, hi):
    lo = lax.bitcast_convert_type(lo.astype(BF16).astype(F32), jnp.int32)
    hi = lax.bitcast_convert_type(hi.astype(BF16).astype(F32), jnp.int32)
    return hi | lax.shift_right_logical(lo, 16)


def _unpack_rows(w):
    lo = lax.bitcast_convert_type(lax.shift_left(w, 16), F32)
    hi = lax.bitcast_convert_type(w & jnp.int32(-65536), F32)
    return lo, hi


def _tile_copies(src_ref, dst_ref, sem):
    return pltpu.make_async_copy(src_ref.at[pl.ds(0, ROW_TILE)], dst_ref.at[pl.ds(0, ROW_TILE)], sem)


ROW_SUB = HALF // LANES


FILL_SIZES = tuple(MOE_TM >> (b + 1) for b in range(MOE_TM.bit_length() - 1))


def _fill_copies(fill_start_ref, fill_len_ref, zero_ref, xs_ref, sem, act):
    def body(e, carry):
        pos = fill_start_ref[e]
        n = fill_len_ref[e]
        for size in FILL_SIZES:
            take = (n & size) != 0

            @pl.when(take)
            def _():
                cp = pltpu.make_async_copy(zero_ref.at[pl.ds(0, size)], xs_ref.at[pl.ds(pos, size)], sem)
                getattr(cp, act)()

            pos = pos + jnp.where(take, size, 0)
        return carry

    lax.fori_loop(0, N_EXPERTS, body, 0)

    def tail(j, carry):
        pos = fill_start_ref[N_EXPERTS] + j * FILL_SIZES[0]
        cp = pltpu.make_async_copy(zero_ref, xs_ref.at[pl.ds(pos, FILL_SIZES[0])], sem)
        getattr(cp, act)()
        return carry

    lax.fori_loop(0, fill_len_ref[N_EXPERTS], tail, 0)


def _scatter_kernel(fill_start_ref, fill_len_ref, dest_ref, x_ref, xs_ref, packed_ref, zero_ref, sem, fill_sem):
    i = pl.program_id(0)
    slot = i % 2

    @pl.when(i == 0)
    def _():
        zero_ref[...] = jnp.zeros_like(zero_ref)
        _fill_copies(fill_start_ref, fill_len_ref, zero_ref, xs_ref, fill_sem, "start")

    words = _pack_rows(x_ref[...])
    for c in range(ROW_SUB):
        packed_ref[slot, :, c, :] = words[:, c * LANES:(c + 1) * LANES]

    def body(r, carry):
        for k in range(TOP_K):
            d = dest_ref[0, 0, r * TOP_K + k]
            pltpu.make_async_copy(packed_ref.at[slot, r], xs_ref.at[d], sem.at[slot]).start(priority=k % 2)
        return carry

    lax.fori_loop(0, ROW_TILE, body, 0)

    @pl.when(i > 0)
    def _():
        for k in range(TOP_K):
            _tile_copies(packed_ref.at[1 - slot], xs_ref, sem.at[1 - slot]).wait()

    @pl.when(i == pl.num_programs(0) - 1)
    def _():
        for k in range(TOP_K):
            _tile_copies(packed_ref.at[slot], xs_ref, sem.at[slot]).wait()

    @pl.when(i == 0)
    def _():
        _fill_copies(fill_start_ref, fill_len_ref, zero_ref, xs_ref, fill_sem, "wait")


def _scatter_rows(x, dest, fill_start, fill_len):
    nt = TOKENS // ROW_TILE
    grid_spec = pltpu.PrefetchScalarGridSpec(
        num_scalar_prefetch=2,
        grid=(nt,),
        in_specs=[pl.BlockSpec((1, 1, ROW_TILE * TOP_K), lambda i, fs, fl: (i, 0, 0), memory_space=pltpu.SMEM),
                  pl.BlockSpec((ROW_TILE, D_MODEL), lambda i, fs, fl: (i, 0))],
        out_specs=pl.BlockSpec(memory_space=pl.ANY),
        scratch_shapes=[pltpu.VMEM((2, ROW_TILE, ROW_SUB, LANES), jnp.int32),
                        pltpu.VMEM((FILL_SIZES[0], ROW_SUB, LANES), jnp.int32),
                        pltpu.SemaphoreType.DMA((2,)), pltpu.SemaphoreType.DMA(())],
    )
    return pl.pallas_call(
        _scatter_kernel,
        grid_spec=grid_spec,
        out_shape=jax.ShapeDtypeStruct((MOE_ROWS, ROW_SUB, LANES), jnp.int32),
        compiler_params=_cparams(("arbitrary",), has_side_effects=True),
        name="scatter_rows",
    )(fill_start, fill_len, dest.reshape(nt, 1, ROW_TILE * TOP_K), x)


MOE_NF = D_EXPERT // MOE_TF
MOE_ROW_STEP = 256


def _experts_kernel(ge_ref, gn_ref, de_ref, dn_ref, xs_ref, wg_ref, wu_ref, bg_ref, bu_ref, wd_ref, bd_ref,
                    ys_ref, xb_ref, act_ref, hold_ref, words_ref, xsem):
    v = pl.program_id(0)
    f = pl.program_id(1)
    gn = gn_ref[v]
    dn = dn_ref[v]
    slot = v % 2
    half_nf = MOE_NF // 2

    @pl.when(jnp.logical_and(v == 0, f == 0))
    def _():
        hold_ref[...] = jnp.zeros_like(hold_ref)

    def xs_copies(visit, s):
        row0 = pl.multiple_of(visit * MOE_TM, MOE_TM)
        return [pltpu.make_async_copy(xs_ref.at[pl.ds(row0, MOE_TM), c, :],
                                      words_ref.at[s, :, c * LANES:(c + 1) * LANES], xsem.at[s])
                for c in range(ROW_SUB)]

    @pl.when(f == 0)
    def _():
        @pl.when(v == 0)
        def _():
            for cp in xs_copies(0, 0):
                cp.start()

        @pl.when(v < N_VISITS)
        def _():
            for cp in xs_copies(v, slot):
                cp.wait()

        @pl.when(v + 1 < N_VISITS)
        def _():
            for cp in xs_copies(v + 1, 1 - slot):
                cp.start()

    @pl.when(jnp.logical_and(gn > 0, f == 0))
    def _():
        r = lax.broadcasted_iota(jnp.int32, (MOE_TM, 1), 0)
        lo, hi = _unpack_rows(jnp.where(r < gn, words_ref[slot], 0))
        xb_ref[:, :HALF] = lo.astype(BF16)
        xb_ref[:, HALF:] = hi.astype(BF16)

    def gate_phase(rows):
        xb = xb_ref[:rows, :]
        gl = jnp.dot(xb, wg_ref[...].astype(BF16), preferred_element_type=F32) + bg_ref[...]
        up = jnp.dot(xb, wu_ref[...].astype(BF16), preferred_element_type=F32) + bu_ref[...]
        gl = jnp.minimum(gl, SWIGLU_LIMIT)
        up = jnp.clip(up, -SWIGLU_LIMIT, SWIGLU_LIMIT)
        act = ((up + 1.0) * (gl * jax.nn.sigmoid(SWIGLU_ALPHA * gl))).astype(BF16)
        act_ref[slot, :rows, pl.ds(pl.multiple_of(f * MOE_TF, MOE_TF), MOE_TF)] = act

    def down_phase(rows):
        y = jnp.dot(act_ref[1 - slot, :rows, :], wd_ref[...].astype(BF16),
                    preferred_element_type=F32) + bd_ref[...]
        h = f % half_nf
        ys_ref[:rows, :] = _pack_pair(hold_ref[h, :rows, :], y)
        hold_ref[h, :rows, :] = y
        if rows < MOE_TM:
            ys_ref[rows:, :] = jnp.zeros((MOE_TM - rows, MOE_TF), jnp.int32)

    for rows in range(MOE_TM, 0, -MOE_ROW_STEP):
        lower = rows - MOE_ROW_STEP
        pl.when(jnp.logical_and(gn > lower, gn <= rows))(functools.partial(gate_phase, rows))
        pl.when(jnp.logical_and(dn > lower, dn <= rows))(functools.partial(down_phase, rows))

    @pl.when(dn == 0)
    def _():
        ys_ref[...] = jnp.zeros_like(ys_ref)


def _experts(xs, visit_expert, visit_nvalid, w_gate_up, b_gate_up, w_down, b_down):
    nf = MOE_NF
    gate_e = jnp.concatenate([visit_expert, visit_expert[-1:]])
    gate_n = jnp.concatenate([visit_nvalid, jnp.zeros((1,), jnp.int32)])
    down_e = jnp.concatenate([visit_expert[:1], visit_expert])
    down_n = jnp.concatenate([jnp.zeros((1,), jnp.int32), visit_nvalid])

    def tile(f, n):
        return jnp.where(n > 0, f, nf - 1)

    grid_spec = pltpu.PrefetchScalarGridSpec(
        num_scalar_prefetch=4,
        grid=(N_VISITS + 1, nf),
        in_specs=[
            pl.BlockSpec(memory_space=pl.ANY),
            pl.BlockSpec((None, D_MODEL, MOE_TF), lambda v, f, ge, gn, de, dn: (ge[v], 0, tile(f, gn[v]))),
            pl.BlockSpec((None, D_MODEL, MOE_TF), lambda v, f, ge, gn, de, dn: (ge[v], 0, nf + tile(f, gn[v]))),
            pl.BlockSpec((None, 1, MOE_TF), lambda v, f, ge, gn, de, dn: (ge[v], 0, tile(f, gn[v]))),
            pl.BlockSpec((None, 1, MOE_TF), lambda v, f, ge, gn, de, dn: (ge[v], 0, nf + tile(f, gn[v]))),
            pl.BlockSpec((None, D_EXPERT, MOE_TF), lambda v, f, ge, gn, de, dn: (de[v], 0, tile(f, dn[v]))),
            pl.BlockSpec((None, 1, MOE_TF), lambda v, f, ge, gn, de, dn: (de[v], 0, tile(f, dn[v]))),
        ],
        out_specs=pl.BlockSpec((MOE_TM, MOE_TF),
                               lambda v, f, ge, gn, de, dn: (jnp.where(v == 0, N_VISITS, v - 1),
                                                             jnp.maximum(f - nf // 2, 0))),
        scratch_shapes=[pltpu.VMEM((MOE_TM, D_MODEL), BF16),
                        pltpu.VMEM((2, MOE_TM, D_EXPERT), BF16),
                        pltpu.VMEM((nf // 2, MOE_TM, MOE_TF), F32),
                        pltpu.VMEM((2, MOE_TM, HALF), jnp.int32),
                        pltpu.SemaphoreType.DMA((2,))],
    )
    return pl.pallas_call(
        _experts_kernel,
        grid_spec=grid_spec,
        out_shape=jax.ShapeDtypeStruct((MOE_ROWS + MOE_TM, HALF), jnp.int32),
        compiler_params=_cparams(("arbitrary", "arbitrary")),
        name="experts",
    )(gate_e, gate_n, down_e, down_n, xs, w_gate_up, w_gate_up, b_gate_up, b_gate_up, w_down, b_down)


def _combine_kernel(dest_ref, dest_next_ref, ys_ref, gate_ref, x_ref, lnw_ref, lnb_ref, o_ref,
                    buf_ref, z_ref, sem):
    i = pl.program_id(0)
    slot = i % 2

    def issue(idx_ref, s):
        def body(r, carry):
            for k in range(TOP_K):
                d = idx_ref[0, 0, r * TOP_K + k]
                pltpu.make_async_copy(ys_ref.at[pl.ds(d, 1), :], buf_ref.at[s, k, pl.ds(r, 1), :],
                                      sem.at[s]).start(priority=k % 2)
            return carry

        lax.fori_loop(0, ROW_TILE, body, 0)

    @pl.when(i == 0)
    def _():
        issue(dest_ref, slot)

    for k in range(TOP_K):
        _tile_copies(ys_ref, buf_ref.at[slot, k], sem.at[slot]).wait()

    def slab(j, carry, prefetch):
        r0 = pl.multiple_of(j * COMBINE_SLAB, COMBINE_SLAB)
        if prefetch:
            for rr in range(COMBINE_SLAB):
                for k in range(TOP_K):
                    d = dest_next_ref[0, 0, (r0 + rr) * TOP_K + k]
                    pltpu.make_async_copy(ys_ref.at[pl.ds(d, 1), :],
                                          buf_ref.at[1 - slot, k, pl.ds(r0 + rr, 1), :],
                                          sem.at[1 - slot]).start(priority=k % 2)
        rows = pl.ds(r0, COMBINE_SLAB)
        gates = gate_ref[rows, :]
        z_lo = DN_ALPHA * x_ref[rows, :HALF]
        z_hi = DN_ALPHA * x_ref[rows, HALF:]
        for k in range(TOP_K):
            lo, hi = _unpack_rows(buf_ref[slot, k, rows, :])
            z_lo = z_lo + gates[:, k:k + 1] * lo
            z_hi = z_hi + gates[:, k:k + 1] * hi
        z_ref[rows, :HALF] = z_lo
        z_ref[rows, HALF:] = z_hi
        return carry

    last = i + 1 == pl.num_programs(0)

    @pl.when(jnp.logical_not(last))
    def _():
        lax.fori_loop(0, ROW_TILE // COMBINE_SLAB, functools.partial(slab, prefetch=True), 0)

    @pl.when(last)
    def _():
        lax.fori_loop(0, ROW_TILE // COMBINE_SLAB, functools.partial(slab, prefetch=False), 0)

    o_ref[...] = _layer_norm(z_ref[...], lnw_ref[...], lnb_ref[...])


def _combine(ys, dest, gates, x, lnw, lnb):
    nt = TOKENS // ROW_TILE
    row = pl.BlockSpec((ROW_TILE, D_MODEL), lambda i: (i, 0))
    vec = pl.BlockSpec((1, D_MODEL), lambda i: (0, 0))
    dest_tiles = dest.reshape(nt, 1, ROW_TILE * TOP_K)
    return pl.pallas_call(
        _combine_kernel,
        grid=(nt,),
        in_specs=[pl.BlockSpec((1, 1, ROW_TILE * TOP_K), lambda i: (i, 0, 0), memory_space=pltpu.SMEM),
                  pl.BlockSpec((1, 1, ROW_TILE * TOP_K), lambda i: (jnp.minimum(i + 1, nt - 1), 0, 0),
                               memory_space=pltpu.SMEM),
                  pl.BlockSpec(memory_space=pl.ANY),
                  pl.BlockSpec((ROW_TILE, LANES), lambda i: (i, 0)),
                  row, vec, vec],
        out_specs=row,
        out_shape=jax.ShapeDtypeStruct((TOKENS, D_MODEL), F32),
        scratch_shapes=[pltpu.VMEM((2, TOP_K, ROW_TILE, HALF), jnp.int32),
                        pltpu.VMEM((ROW_TILE, D_MODEL), F32), pltpu.SemaphoreType.DMA((2,))],
        compiler_params=_cparams(("arbitrary",)),
        name="combine",
    )(dest_tiles, dest_tiles, ys, gates, x, lnw, lnb)


def _layer(x, mem, w_in, attn_sinks, attn_norm_w, ret_norm_w, w_mix_out, ln_mix_w, ln_mix_b,
           w_mem_q, w_mem_kv, w_mem_out, ln_mem_w, ln_mem_b, w_router, b_router,
           w_gate_up, b_gate_up, w_down, b_down, ln_moe_w, ln_moe_b):
    o_k, o_v, o_qr = D_ATTN, D_ATTN + D_KV, D_ATTN + 2 * D_KV
    w_in_p = jnp.concatenate([w_in[:, :o_k], w_in[:, o_qr:], w_in[:, o_k:o_v], w_in[:, o_v:o_qr]],
                             axis=1).astype(BF16)
    proj = _project(x, w_in_p, PROJ_TM, PROJ_TN)
    y = _mixer(proj, attn_sinks.astype(F32), attn_norm_w.reshape(1, D_ATTN), ret_norm_w.reshape(1, D_RET))
    x1 = _out_ln(y, w_mix_out.astype(BF16), x, ln_mix_w.reshape(1, D_MODEL), ln_mix_b.reshape(1, D_MODEL))

    kv = _project(mem.reshape(BATCH * N_MEM, D_MODEL), w_mem_kv.astype(BF16), BATCH * N_MEM, D_MODEL)
    o = _mem_attn(x1, w_mem_q.astype(BF16), kv)
    x2 = _out_ln(o, w_mem_out.astype(BF16), x1, ln_mem_w.reshape(1, D_MODEL), ln_mem_b.reshape(1, D_MODEL))

    wr = jnp.pad(w_router, ((0, 0), (0, LANES - N_EXPERTS)))
    wr_hi = wr.astype(BF16)
    wr = jnp.concatenate([wr_hi, (wr - wr_hi.astype(F32)).astype(BF16)], axis=1)
    br =jnp.pad(b_router.reshape(1, N_EXPERTS), ((0, 0), (0, LANES - N_EXPERTS)), constant_values=NEG_INF)
    idx, gates, rank, counts = _route(x2, wr, br)

    counts = counts[0, :N_EXPERTS].astype(jnp.int32)
    padded = ((counts + MOE_TM - 1) // MOE_TM) * MOE_TM
    eid = jnp.arange(N_EXPERTS, dtype=jnp.int32)
    pad_end = jnp.sum(jnp.where(eid[None, :] <= eid[:, None], padded[None, :], 0), axis=1)
    pad_start = pad_end - padded
    chosen = idx[:, :TOP_K, None] == eid[None, None, :]
    dest = jnp.sum(jnp.where(chosen, pad_start[None, None, :], 0), axis=-1) + rank[:, :TOP_K]
    visit_row = jnp.arange(N_VISITS, dtype=jnp.int32) * MOE_TM
    visit_expert = jnp.minimum(jnp.sum((pad_end[None, :] <= visit_row[:, None]).astype(jnp.int32), axis=1),
                               N_EXPERTS - 1)
    visit_nvalid = jnp.clip(counts[visit_expert] - (visit_row - pad_start[visit_expert]),
                            0, MOE_TM).astype(jnp.int32)

    fill_start = jnp.concatenate([pad_start + counts, pad_end[-1:]]).astype(jnp.int32)
    fill_len = jnp.concatenate([padded - counts, (MOE_ROWS - pad_end[-1:]) // FILL_SIZES[0]]).astype(jnp.int32)
    xs = _scatter_rows(x2, dest, fill_start, fill_len)
    ys = _experts(xs, visit_expert, visit_nvalid,
                  w_gate_up, b_gate_up.reshape(N_EXPERTS, 1, 2 * D_EXPERT),
                  w_down, b_down.reshape(N_EXPERTS, 1, D_MODEL))
    return _combine(ys, dest, gates, x2, ln_moe_w.reshape(1, D_MODEL), ln_moe_b.reshape(1, D_MODEL))


def kernel(x, mem, w_in, attn_sinks, attn_norm_w, ret_norm_w, w_mix_out, ln_mix_w, ln_mix_b, w_mem_q, w_mem_kv, w_mem_out, ln_mem_w, ln_mem_b, w_router, b_router, w_gate_up, b_gate_up, w_down, b_down, ln_moe_w, ln_moe_b):
    xt = x.reshape(TOKENS, D_MODEL)
    out = _layer(xt, mem, w_in[0], attn_sinks[0], attn_norm_w[0], ret_norm_w[0], w_mix_out[0],
                 ln_mix_w[0], ln_mix_b[0], w_mem_q[0], w_mem_kv[0], w_mem_out[0], ln_mem_w[0], ln_mem_b[0],
                 w_router[0], b_router[0], w_gate_up[0], b_gate_up[0], w_down[0], b_down[0],
                 ln_moe_w[0], ln_moe_b[0])
    return out.reshape(BATCH, SEQ, D_MODEL)
```

```python
import functools
import math

import jax
import jax.numpy as jnp
from jax import lax
from jax.experimental import pallas as pl
from jax.experimental.pallas import tpu as pltpu

F32 = jnp.float32
BF16 = jnp.bfloat16

D_MODEL = 2048
BATCH = 2
SEQ = 16384
TOKENS = BATCH * SEQ

HEAD_DIM_ATTN = 64
N_Q_ATTN = 16
N_KV_ATTN = 2
GQA_GROUP = N_Q_ATTN // N_KV_ATTN
ATTN_BLOCK = 128
D_ATTN = N_Q_ATTN * HEAD_DIM_ATTN
D_KV = N_KV_ATTN * HEAD_DIM_ATTN
N_RET = 8
HEAD_DIM_RET = 128
RET_CHUNK = 128
D_RET = N_RET * HEAD_DIM_RET
D_IN = D_ATTN + 2 * D_KV + 4 * D_RET
N_MEM = 256
N_MEM_HEADS = 4
HEAD_DIM_MEM = D_MODEL // N_MEM_HEADS
N_EXPERTS = 32
TOP_K = 4
D_EXPERT = D_MODEL
SWIGLU_LIMIT = 7.0
SWIGLU_ALPHA = 1.702
LN_EPS = 1e-5
DN_ALPHA = 2.0 ** 0.25
NEG_INF = -1e30

LANES = 128
VMEM_LIMIT = 56 * 1024 * 1024

PROJ_TM = 1024
PROJ_TN = 1792
LN_TM = 512
MEM_TM = 512
ROUTE_TM = 512
ROW_TILE = 256
MOE_TM = 1024
MOE_TF = 256
N_VISITS = (TOKENS * TOP_K + N_EXPERTS * (MOE_TM - 1) + MOE_TM - 1) // MOE_TM
MOE_ROWS = N_VISITS * MOE_TM

COL_QA, COL_QR, COL_KR, COL_VR, COL_GR = 0, 1, 2, 3, 4
COL_KA = 5 * D_RET // D_KV
COL_VA = COL_KA + 1


def _cparams(sem, **kw):
    return pltpu.CompilerParams(dimension_semantics=sem, vmem_limit_bytes=VMEM_LIMIT, **kw)


def _layer_norm(z, w, b):
    mu = jnp.mean(z, axis=-1, keepdims=True)
    zc = z - mu
    var = jnp.mean(zc * zc, axis=-1, keepdims=True)
    return zc * lax.rsqrt(var + LN_EPS) * w + b


def _dot_nt(a, b):
    return lax.dot_general(a, b, (((1,), (1,)), ((), ())), preferred_element_type=F32)


def _proj_kernel(x_ref, w_ref, o_ref, xb_ref):
    @pl.when(pl.program_id(1) == 0)
    def _():
        xb_ref[...] = x_ref[...].astype(BF16)

    o_ref[...] = jnp.dot(xb_ref[...], w_ref[...], preferred_element_type=F32).astype(o_ref.dtype)


def _project(x, w, tm, tn):
    m, k = x.shape
    n = w.shape[1]
    return pl.pallas_call(
        _proj_kernel,
        grid=(m // tm, n // tn),
        in_specs=[pl.BlockSpec((tm, k), lambda i, j: (i, 0)),
                  pl.BlockSpec((k, tn), lambda i, j: (0, j))],
        out_specs=pl.BlockSpec((tm, tn), lambda i, j: (i, j)),
        out_shape=jax.ShapeDtypeStruct((m, n), BF16),
        scratch_shapes=[pltpu.VMEM((tm, k), BF16)],
        compiler_params=_cparams(("parallel", "arbitrary")),
        name="project",
    )(x, w)


def _mixer_kernel(sinkrow_ref, bias_cur_ref, bias_prev_ref, qa_ref, kp_ref, kc_ref, vp_ref, vc_ref,
                  qr_ref, kr_ref, vr_ref, gr_ref, anw_ref, rnw_ref, y_ref, state_ref, a_ref):
    n = pl.program_id(1)
    row = lax.broadcasted_iota(jnp.int32, (ATTN_BLOCK, ATTN_BLOCK), 0)
    col = lax.broadcasted_iota(jnp.int32, (ATTN_BLOCK, ATTN_BLOCK), 1)
    diff = row - col
    difff = diff.astype(F32)

    scale = HEAD_DIM_ATTN ** -0.5
    for j in range(N_KV_ATTN):
        klo, khi = j * HEAD_DIM_ATTN, (j + 1) * HEAD_DIM_ATTN
        heads = range(j * GQA_GROUP, (j + 1) * GQA_GROUP)
        q_rows = jnp.concatenate([qa_ref[:, h * HEAD_DIM_ATTN:(h + 1) * HEAD_DIM_ATTN] for h in heads], axis=0)
        q_rows = q_rows * scale
        sink = sinkrow_ref[j]
        s_cur = _dot_nt(kc_ref[:, klo:khi], q_rows) + bias_cur_ref[j]
        s_prev = _dot_nt(kp_ref[:, klo:khi], q_rows) + bias_prev_ref[j]
        s_prev = jnp.where(n > 0, s_prev, NEG_INF)
        m = jnp.maximum(jnp.max(s_cur, axis=0, keepdims=True), jnp.max(s_prev, axis=0, keepdims=True))
        m = jnp.maximum(m, sink)
        p_cur = jnp.exp(s_cur - m)
        p_prev = jnp.exp(s_prev - m)
        denom = (jnp.sum(p_cur, axis=0, keepdims=True) + jnp.sum(p_prev, axis=0, keepdims=True)
                 + jnp.exp(sink - m))
        v_cur_t = vc_ref[:, klo:khi].astype(F32).T.astype(BF16)
        v_prev_t = vp_ref[:, klo:khi].astype(F32).T.astype(BF16)
        o_t = (jnp.dot(v_cur_t, p_cur.astype(BF16), preferred_element_type=F32)
               + jnp.dot(v_prev_t, p_prev.astype(BF16), preferred_element_type=F32))
        o_t = o_t * (1.0 / denom)
        for g, h in enumerate(heads):
            a_ref[h * HEAD_DIM_ATTN:(h + 1) * HEAD_DIM_ATTN, :] = o_t[:, g * ATTN_BLOCK:(g + 1) * ATTN_BLOCK]
    a_t = a_ref[...]
    a_t = a_t * lax.rsqrt(jnp.mean(a_t * a_t, axis=0, keepdims=True) + LN_EPS)
    y_ref[:, :D_ATTN] = (a_t.T * anw_ref[...]).astype(y_ref.dtype)

    @pl.when(n == 0)
    def _():
        state_ref[...] = jnp.zeros_like(state_ref)

    pos = lax.broadcasted_iota(jnp.int32, (RET_CHUNK, 1), 0).astype(F32)
    kscale = HEAD_DIM_RET ** -0.5
    for h in range(N_RET):
        lo, hi = h * HEAD_DIM_RET, (h + 1) * HEAD_DIM_RET
        log_g = math.log1p(-(2.0 ** (-5.0 - h)))
        dmask = jnp.where(diff >= 0, jnp.exp(log_g * jnp.maximum(difff, 0.0)), 0.0)
        w_key = jnp.exp(log_g * (RET_CHUNK - 1.0 - pos))
        w_q = jnp.exp(log_g * (pos + 1.0))
        g_c = math.exp(log_g * RET_CHUNK)
        q = qr_ref[:, lo:hi]
        k = kr_ref[:, lo:hi]
        v = vr_ref[:, lo:hi]
        s = _dot_nt(q, k) * (dmask * kscale)
        inner = jnp.dot(s.astype(BF16), v, preferred_element_type=F32)
        state = state_ref[h]
        qw = (q.astype(F32) * w_q).astype(BF16)
        cross = jnp.dot(qw, state.astype(BF16), preferred_element_type=F32)
        kw_t = (k.astype(F32) * (w_key * kscale)).T.astype(BF16)
        state_ref[h] = g_c * state + jnp.dot(kw_t, v, preferred_element_type=F32)
        o = inner + cross
        mu = jnp.mean(o, axis=-1, keepdims=True)
        oc = o - mu
        var = jnp.mean(oc * oc, axis=-1, keepdims=True)
        o = oc * lax.rsqrt(var + LN_EPS) * rnw_ref[:, lo:hi]
        g = gr_ref[:, lo:hi].astype(F32)
        o = o * (g * jax.nn.sigmoid(g))
        y_ref[:, D_ATTN + lo:D_ATTN + hi] = o.astype(y_ref.dtype)


def _mixer(proj, sinks, attn_norm_w, ret_norm_w):
    nc = SEQ // ATTN_BLOCK
    blk = ATTN_BLOCK

    def wide(colblock):
        return pl.BlockSpec((blk, D_RET), lambda b, n: (b * nc + n, colblock))

    def kv_cur(colblock):
        return pl.BlockSpec((blk, D_KV), lambda b, n: (b * nc + n, colblock))

    def kv_prev(colblock):
        return pl.BlockSpec((blk, D_KV), lambda b, n: (b * nc + jnp.maximum(n - 1, 0), colblock))

    vec = pl.BlockSpec((1, D_RET), lambda b, n: (0, 0))
    gq = GQA_GROUP * blk

    def table(rows):
        return pl.BlockSpec((N_KV_ATTN, rows, gq), lambda b, n: (0, 0, 0))

    slopes = jnp.exp2(-8.0 * (jnp.arange(N_Q_ATTN, dtype=F32) + 1.0) / N_Q_ATTN).reshape(N_KV_ATTN, 1, GQA_GROUP, 1)
    key = jnp.arange(blk, dtype=F32).reshape(1, blk, 1, 1)
    query = jnp.arange(blk, dtype=F32).reshape(1, 1, 1, blk)
    dist_cur = jnp.broadcast_to(query - key, (N_KV_ATTN, blk, GQA_GROUP, blk))
    bias_cur = jnp.where(dist_cur >= 0, -slopes * dist_cur, NEG_INF).reshape(N_KV_ATTN, blk, gq)
    bias_prev = jnp.where(dist_cur < 0, -slopes * (dist_cur + blk), NEG_INF).reshape(N_KV_ATTN, blk, gq)
    sink_rows = jnp.repeat(sinks.astype(F32).reshape(N_KV_ATTN, 1, GQA_GROUP), blk, axis=2)
    return pl.pallas_call(
        _mixer_kernel,
        grid=(BATCH, nc),
        in_specs=[table(1), table(blk), table(blk),
                  wide(COL_QA), kv_prev(COL_KA), kv_cur(COL_KA), kv_prev(COL_VA), kv_cur(COL_VA),
                  wide(COL_QR), wide(COL_KR), wide(COL_VR), wide(COL_GR), vec, vec],
        out_specs=pl.BlockSpec((blk, D_MODEL), lambda b, n: (b * nc + n, 0)),
        out_shape=jax.ShapeDtypeStruct((TOKENS, D_MODEL), BF16),
        scratch_shapes=[pltpu.VMEM((N_RET, HEAD_DIM_RET, HEAD_DIM_RET), F32),
                        pltpu.VMEM((D_ATTN, blk), F32)],
        compiler_params=_cparams(("parallel", "arbitrary")),
        name="mixer",
    )(sink_rows, bias_cur, bias_prev, proj, proj, proj, proj, proj, proj, proj, proj, proj,
      attn_norm_w, ret_norm_w)


def _out_ln_kernel(y_ref, w_ref, x_ref, lnw_ref, lnb_ref, o_ref):
    h = jnp.dot(y_ref[...], w_ref[...], preferred_element_type=F32)
    o_ref[...] = _layer_norm(DN_ALPHA * x_ref[...] + h, lnw_ref[...], lnb_ref[...])


def _out_ln(y, w, x, lnw, lnb):
    tm = LN_TM
    row = pl.BlockSpec((tm, D_MODEL), lambda i: (i, 0))
    vec = pl.BlockSpec((1, D_MODEL), lambda i: (0, 0))
    return pl.pallas_call(
        _out_ln_kernel,
        grid=(TOKENS // tm,),
        in_specs=[row,
                  pl.BlockSpec((D_MODEL, D_MODEL), lambda i: (0, 0), pipeline_mode=pl.Buffered(1)),
                  row, vec, vec],
        out_specs=row,
        out_shape=jax.ShapeDtypeStruct((TOKENS, D_MODEL), F32),
        compiler_params=_cparams(("parallel",)),
        name="out_ln",
    )(y, w, x, lnw, lnb)


def _mem_attn_kernel(x_ref, wq_ref, k_ref, v_ref, o_ref):
    q = jnp.dot(x_ref[...].astype(BF16), wq_ref[...], preferred_element_type=F32)
    q = (q * (HEAD_DIM_MEM ** -0.5)).astype(BF16)
    for h in range(N_MEM_HEADS):
        lo, hi = h * HEAD_DIM_MEM, (h + 1) * HEAD_DIM_MEM
        s = _dot_nt(q[:, lo:hi], k_ref[:, lo:hi])
        m = jnp.max(s, axis=-1, keepdims=True)
        p = jnp.exp(s - m)
        denom = jnp.sum(p, axis=-1, keepdims=True)
        o = jnp.dot(p.astype(BF16), v_ref[:, lo:hi], preferred_element_type=F32)
        o_ref[:, lo:hi] = (o / denom).astype(o_ref.dtype)


def _mem_attn(x, wq, kv):
    tm = MEM_TM
    nt = SEQ // tm
    row = pl.BlockSpec((tm, D_MODEL), lambda b, i: (b * nt + i, 0))
    return pl.pallas_call(
        _mem_attn_kernel,
        grid=(BATCH, nt),
        in_specs=[row,
                  pl.BlockSpec((D_MODEL, D_MODEL), lambda b, i: (0, 0), pipeline_mode=pl.Buffered(1)),
                  pl.BlockSpec((N_MEM, D_MODEL), lambda b, i: (b, 0)),
                  pl.BlockSpec((N_MEM, D_MODEL), lambda b, i: (b, 1))],
        out_specs=row,
        out_shape=jax.ShapeDtypeStruct((TOKENS, D_MODEL), BF16),
        compiler_params=_cparams(("parallel", "parallel")),
        name="mem_attn",
    )(x, wq, kv, kv)


def _route_kernel(x_ref, wr_ref, br_ref, idx_ref, gate_ref, rank_ref, count_ref, carry_ref):
    tm = x_ref.shape[0]

    @pl.when(pl.program_id(0) == 0)
    def _():
        carry_ref[...] = jnp.zeros_like(carry_ref)

    x = x_ref[...]
    x_hi = x.astype(BF16)
    x_lo = (x - x_hi.astype(F32)).astype(BF16)
    part_hi = jnp.dot(x_hi, wr_ref[...], preferred_element_type=F32)
    part_lo = jnp.dot(x_lo, wr_ref[...], preferred_element_type=F32)
    logits = (part_hi[:, :LANES] + (part_hi[:, LANES:] + part_lo[:, :LANES]) + part_lo[:, LANES:]
              + br_ref[...])
    lane = lax.broadcasted_iota(jnp.int32, (tm, LANES), 1)
    lane_f = lane.astype(F32)
    work = logits
    sels, vals, idxs = [], [], []
    for _ in range(TOP_K):
        v = jnp.max(work, axis=-1, keepdims=True)
        i = jnp.min(jnp.where(work == v, lane_f, float(LANES)), axis=-1, keepdims=True)
        sel = lane_f == i
        work = jnp.where(sel, -jnp.inf, work)
        sels.append(sel)
        vals.append(v)
        idxs.append(i)
    exps = [jnp.exp(v - vals[0]) for v in vals]
    total = exps[0] + exps[1] + exps[2] + exps[3]

    onehot = jnp.zeros((tm, LANES), F32)
    for sel in sels:
        onehot = jnp.where(sel, 1.0, onehot)
    r = lax.broadcasted_iota(jnp.int32, (tm, tm), 0)
    c = lax.broadcasted_iota(jnp.int32, (tm, tm), 1)
    below = jnp.where(c < r, 1.0, 0.0).astype(BF16)
    prefix = jnp.dot(below, onehot.astype(BF16), preferred_element_type=F32)
    rankmat = prefix + carry_ref[...]
    carry_ref[...] = carry_ref[...] + jnp.sum(onehot, axis=0, keepdims=True)
    count_ref[...] = carry_ref[...]

    idx_out = jnp.zeros((tm, LANES), jnp.int32)
    gate_out = jnp.zeros((tm, LANES), F32)
    rank_out = jnp.zeros((tm, LANES), jnp.int32)
    for k in range(TOP_K):
        rk = jnp.sum(jnp.where(sels[k], rankmat, 0.0), axis=-1, keepdims=True).astype(jnp.int32)
        idx_out = jnp.where(lane == k, idxs[k].astype(jnp.int32), idx_out)
        gate_out = jnp.where(lane == k, exps[k] / total, gate_out)
        rank_out = jnp.where(lane == k, rk, rank_out)
    idx_ref[...] = idx_out
    gate_ref[...] = gate_out
    rank_ref[...] = rank_out


def _route(x, wr, br):
    tm = ROUTE_TM
    wide = pl.BlockSpec((tm, LANES), lambda i: (i, 0))
    return pl.pallas_call(
        _route_kernel,
        grid=(TOKENS // tm,),
        in_specs=[pl.BlockSpec((tm, D_MODEL), lambda i: (i, 0)),
                  pl.BlockSpec((D_MODEL, 2 * LANES), lambda i: (0, 0)),
                  pl.BlockSpec((1, LANES), lambda i: (0, 0))],
        out_specs=[wide, wide, wide, pl.BlockSpec((1, LANES), lambda i: (0, 0))],
        out_shape=[jax.ShapeDtypeStruct((TOKENS, LANES), jnp.int32),
                   jax.ShapeDtypeStruct((TOKENS, LANES), F32),
                   jax.ShapeDtypeStruct((TOKENS, LANES), jnp.int32),
                   jax.ShapeDtypeStruct((1, LANES), F32)],
        scratch_shapes=[pltpu.VMEM((1, LANES), F32)],
        compiler_params=_cparams(("arbitrary",)),
        name="route",
    )(x, wr, br)


HALF = D_MODEL // 2


def _pack_rows(x):
    return _pack_pair(x[:, :HALF], x[:, HALF:])


def _pack_pair(lo, hi):
    lo = lax.bitcast_convert_type(lo.astype(BF16).astype(F32), jnp.int32)
    hi = lax.bitcast_convert_type(hi.astype(BF16).astype(F32), jnp.int32)
    return hi | lax.shift_right_logical(lo, 16)


def _unpack_rows(w):
    lo = lax.bitcast_convert_type(lax.shift_left(w, 16), F32)
    hi = lax.bitcast_convert_type(w & jnp.int32(-65536), F32)
    return lo, hi


def _wait_row_copies(rows_ref, sem):
    pltpu.make_async_copy(rows_ref.at[pl.ds(0, ROW_TILE)], rows_ref.at[pl.ds(0, ROW_TILE)], sem).wait()


ROW_SUB = HALF // LANES
SUBLANES = 8
ROW_GROUPS = ROW_TILE // SUBLANES


FILL_SIZES = tuple(MOE_TM >> (b + 1) for b in range(MOE_TM.bit_length() - 1))


def _fill_copies(fill_start_ref, fill_len_ref, zero_ref, xs_ref, sem, act):
    def body(e, carry):
        pos = fill_start_ref[e]
        n = fill_len_ref[e]
        for size in FILL_SIZES:
            take = (n & size) != 0

            @pl.when(take)
            def _():
                cp = pltpu.make_async_copy(zero_ref.at[pl.ds(0, size)], xs_ref.at[pl.ds(pos, size)], sem)
                getattr(cp, act)()

            pos = pos + jnp.where(take, size, 0)
        return carry

    lax.fori_loop(0, N_EXPERTS, body, 0)

    def tail(j, carry):
        pos = fill_start_ref[N_EXPERTS] + j * FILL_SIZES[0]
        cp = pltpu.make_async_copy(zero_ref, xs_ref.at[pl.ds(pos, FILL_SIZES[0])], sem)
        getattr(cp, act)()
        return carry

    lax.fori_loop(0, fill_len_ref[N_EXPERTS], tail, 0)


def _scatter_kernel(fill_start_ref, fill_len_ref, dest_ref, x_ref, xs_ref, packed_ref, zero_ref, sem, fill_sem):
    i = pl.program_id(0)
    slot = i % 2

    @pl.when(i == 0)
    def _():
        zero_ref[...] = jnp.zeros_like(zero_ref)
        _fill_copies(fill_start_ref, fill_len_ref, zero_ref, xs_ref, fill_sem, "start")

    words = _pack_rows(x_ref[...])
    for g in range(ROW_GROUPS):
        for c in range(ROW_SUB):
            packed_ref[slot, g, c] = words[g * SUBLANES:(g + 1) * SUBLANES, c * LANES:(c + 1) * LANES]

    def body(g, carry):
        for s in range(SUBLANES):
            for k in range(TOP_K):
                d = dest_ref[0, 0, (g * SUBLANES + s) * TOP_K + k]
                pltpu.make_async_copy(packed_ref.at[slot, g, :, s, :], xs_ref.at[d],
                                      sem.at[slot]).start(priority=k % 2)
        return carry

    lax.fori_loop(0, ROW_GROUPS, body, 0)

    @pl.when(i > 0)
    def _():
        for k in range(TOP_K):
            _wait_row_copies(xs_ref, sem.at[1 - slot])

    @pl.when(i == pl.num_programs(0) - 1)
    def _():
        for k in range(TOP_K):
            _wait_row_copies(xs_ref, sem.at[slot])

    @pl.when(i == 0)
    def _():
        _fill_copies(fill_start_ref, fill_len_ref, zero_ref, xs_ref, fill_sem, "wait")


def _scatter_rows(x, dest, fill_start, fill_len):
    nt = TOKENS // ROW_TILE
    grid_spec = pltpu.PrefetchScalarGridSpec(
        num_scalar_prefetch=2,
        grid=(nt,),
        in_specs=[pl.BlockSpec((1, 1, ROW_TILE * TOP_K), lambda i, fs, fl: (i, 0, 0), memory_space=pltpu.SMEM),
                  pl.BlockSpec((ROW_TILE, D_MODEL), lambda i, fs, fl: (i, 0))],
        out_specs=pl.BlockSpec(memory_space=pl.ANY),
        scratch_shapes=[pltpu.VMEM((2, ROW_GROUPS, ROW_SUB, SUBLANES, LANES), jnp.int32),
                        pltpu.VMEM((FILL_SIZES[0], ROW_SUB, LANES), jnp.int32),
                        pltpu.SemaphoreType.DMA((2,)), pltpu.SemaphoreType.DMA(())],
    )
    return pl.pallas_call(
        _scatter_kernel,
        grid_spec=grid_spec,
        out_shape=jax.ShapeDtypeStruct((MOE_ROWS, ROW_SUB, LANES), jnp.int32),
        compiler_params=_cparams(("arbitrary",), has_side_effects=True),
        name="scatter_rows",
    )(fill_start, fill_len, dest.reshape(nt, 1, ROW_TILE * TOP_K), x)


MOE_NF = D_EXPERT // MOE_TF
MOE_ROW_STEP = 256


def _experts_kernel(ge_ref, gn_ref, de_ref, dn_ref, xs_ref, wg_ref, wu_ref, bg_ref, bu_ref, wd_ref, bd_ref,
                    ys_ref, xb_ref, act_ref, hold_ref, words_ref, ow_ref, pend_ref, xsem, ysem):
    v = pl.program_id(0)
    f = pl.program_id(1)
    gn = gn_ref[v]
    dn = dn_ref[v]
    slot = v % 2
    half_nf = MOE_NF // 2

    sub_per_tile = MOE_TF // LANES

    def ys_copies(visit, tile):
        row0 = pl.multiple_of(visit * MOE_TM, MOE_TM)
        return [pltpu.make_async_copy(ow_ref.at[:, j * LANES:(j + 1) * LANES],
                                      ys_ref.at[pl.ds(row0, MOE_TM), tile * sub_per_tile + j, :], ysem)
                for j in range(sub_per_tile)]

    def emit_output_tile():
        for cp in ys_copies(v - 1, f - half_nf):
            cp.start()
        pend_ref[0] = 1

    def drain_output_tile():
        @pl.when(pend_ref[0] == 1)
        def _():
            for cp in ys_copies(0, 0):
                cp.wait()
            pend_ref[0] = 0

    @pl.when(jnp.logical_and(v == 0, f == 0))
    def _():
        hold_ref[...] = jnp.zeros_like(hold_ref)
        ow_ref[...] = jnp.zeros_like(ow_ref)
        pend_ref[0] = 0

    drain_output_tile()

    def xs_copies(visit, s):
        row0 = pl.multiple_of(visit * MOE_TM, MOE_TM)
        return [pltpu.make_async_copy(xs_ref.at[pl.ds(row0, MOE_TM), c, :],
                                      words_ref.at[s, :, c * LANES:(c + 1) * LANES], xsem.at[s])
                for c in range(ROW_SUB)]

    @pl.when(f == 0)
    def _():
        @pl.when(v == 0)
        def _():
            for cp in xs_copies(0, 0):
                cp.start()

        @pl.when(v < N_VISITS)
        def _():
            for cp in xs_copies(v, slot):
                cp.wait()

        @pl.when(v + 1 < N_VISITS)
        def _():
            for cp in xs_copies(v + 1, 1 - slot):
                cp.start()

    @pl.when(jnp.logical_and(gn > 0, f == 0))
    def _():
        r = lax.broadcasted_iota(jnp.int32, (MOE_TM, 1), 0)
        lo, hi = _unpack_rows(jnp.where(r < gn, words_ref[slot], 0))
        xb_ref[:, :HALF] = lo.astype(BF16)
        xb_ref[:, HALF:] = hi.astype(BF16)

    def gate_phase(rows):
        xb = xb_ref[:rows, :]
        gl = jnp.dot(xb, wg_ref[...].astype(BF16), preferred_element_type=F32) + bg_ref[...]
        up = jnp.dot(xb, wu_ref[...].astype(BF16), preferred_element_type=F32) + bu_ref[...]
        gl = jnp.minimum(gl, SWIGLU_LIMIT)
        up = jnp.clip(up, -SWIGLU_LIMIT, SWIGLU_LIMIT)
        act = ((up + 1.0) * (gl * jax.nn.sigmoid(SWIGLU_ALPHA * gl))).astype(BF16)
        act_ref[slot, :rows, pl.ds(pl.multiple_of(f * MOE_TF, MOE_TF), MOE_TF)] = act

    def down_phase(rows):
        y = jnp.dot(act_ref[1 - slot, :rows, :], wd_ref[...].astype(BF16),
                    preferred_element_type=F32) + bd_ref[...]
        h = f % half_nf

        @pl.when(f >= half_nf)
        def _():
            ow_ref[:rows, :] = _pack_pair(hold_ref[h, :rows, :], y)
            emit_output_tile()

        hold_ref[h, :rows, :] = y

    for rows in range(MOE_TM, 0, -MOE_ROW_STEP):
        lower = rows - MOE_ROW_STEP
        pl.when(jnp.logical_and(gn > lower, gn <= rows))(functools.partial(gate_phase, rows))
        pl.when(jnp.logical_and(dn > lower, dn <= rows))(functools.partial(down_phase, rows))

    @pl.when(jnp.logical_and(jnp.logical_and(dn == 0, v >= 1), f >= half_nf))
    def _():
        ow_ref[...] = jnp.zeros_like(ow_ref)
        emit_output_tile()

    @pl.when(jnp.logical_and(v == pl.num_programs(0) - 1, f == MOE_NF - 1))
    def _():
        drain_output_tile()


def _experts(xs, visit_expert, visit_nvalid, w_gate_up, b_gate_up, w_down, b_down):
    nf = MOE_NF
    gate_e = jnp.concatenate([visit_expert, visit_expert[-1:]])
    gate_n = jnp.concatenate([visit_nvalid, jnp.zeros((1,), jnp.int32)])
    down_e = jnp.concatenate([visit_expert[:1], visit_expert])
    down_n = jnp.concatenate([jnp.zeros((1,), jnp.int32), visit_nvalid])

    def tile(f, n):
        return jnp.where(n > 0, f, nf - 1)

    grid_spec = pltpu.PrefetchScalarGridSpec(
        num_scalar_prefetch=4,
        grid=(N_VISITS + 1, nf),
        in_specs=[
            pl.BlockSpec(memory_space=pl.ANY),
            pl.BlockSpec((None, D_MODEL, MOE_TF), lambda v, f, ge, gn, de, dn: (ge[v], 0, tile(f, gn[v]))),
            pl.BlockSpec((None, D_MODEL, MOE_TF), lambda v, f, ge, gn, de, dn: (ge[v], 0, nf + tile(f, gn[v]))),
            pl.BlockSpec((None, 1, MOE_TF), lambda v, f, ge, gn, de, dn: (ge[v], 0, tile(f, gn[v]))),
            pl.BlockSpec((None, 1, MOE_TF), lambda v, f, ge, gn, de, dn: (ge[v], 0, nf + tile(f, gn[v]))),
            pl.BlockSpec((None, D_EXPERT, MOE_TF), lambda v, f, ge, gn, de, dn: (de[v], 0, tile(f, dn[v]))),
            pl.BlockSpec((None, 1, MOE_TF), lambda v, f, ge, gn, de, dn: (de[v], 0, tile(f, dn[v]))),
        ],
        out_specs=pl.BlockSpec(memory_space=pl.ANY),
        scratch_shapes=[pltpu.VMEM((MOE_TM, D_MODEL), BF16),
                        pltpu.VMEM((2, MOE_TM, D_EXPERT), BF16),
                        pltpu.VMEM((nf // 2, MOE_TM, MOE_TF), F32),
                        pltpu.VMEM((2, MOE_TM, HALF), jnp.int32),
                        pltpu.VMEM((MOE_TM, MOE_TF), jnp.int32),
                        pltpu.SMEM((1,), jnp.int32),
                        pltpu.SemaphoreType.DMA((2,)),
                        pltpu.SemaphoreType.DMA(())],
    )
    return pl.pallas_call(
        _experts_kernel,
        grid_spec=grid_spec,
        out_shape=jax.ShapeDtypeStruct((MOE_ROWS, ROW_SUB, LANES), jnp.int32),
        compiler_params=_cparams(("arbitrary", "arbitrary")),
        name="experts",
    )(gate_e, gate_n, down_e, down_n, xs, w_gate_up, w_gate_up, b_gate_up, b_gate_up, w_down, b_down)


def _combine_kernel(dest_ref, dest_next_ref, ys_ref, gate_ref, x_ref, lnw_ref, lnb_ref, o_ref,
                    buf_ref, z_ref, sem):
    i = pl.program_id(0)
    slot = i % 2

    def start_group(idx_ref, s, g):
        for t in range(SUBLANES):
            for k in range(TOP_K):
                d = idx_ref[0, 0, (g * SUBLANES + t) * TOP_K + k]
                pltpu.make_async_copy(ys_ref.at[d], buf_ref.at[s, k, g, :, t, :],
                                      sem.at[s]).start(priority=k % 2)

    @pl.when(i == 0)
    def _():
        def body(g, carry):
            start_group(dest_ref, slot, g)
            return carry

        lax.fori_loop(0, ROW_GROUPS, body, 0)

    for k in range(TOP_K):
        _wait_row_copies(ys_ref, sem.at[slot])

    def group(g, carry, prefetch):
        if prefetch:
            start_group(dest_next_ref, 1 - slot, g)
        rows = pl.ds(pl.multiple_of(g * SUBLANES, SUBLANES), SUBLANES)
        gates = gate_ref[rows, :]
        gate_k = [jnp.broadcast_to(gates[:, k:k + 1], (SUBLANES, LANES)) for k in range(TOP_K)]
        for c in range(ROW_SUB):
            lo_cols = slice(c * LANES, (c + 1) * LANES)
            hi_cols = slice(HALF + c * LANES, HALF + (c + 1) * LANES)
            z_lo = DN_ALPHA * x_ref[rows, lo_cols]
            z_hi = DN_ALPHA * x_ref[rows, hi_cols]
            for k in range(TOP_K):
                lo, hi = _unpack_rows(buf_ref[slot, k, g, c])
                z_lo = z_lo + gate_k[k] * lo
                z_hi = z_hi + gate_k[k] * hi
            z_ref[rows, lo_cols] = z_lo
            z_ref[rows, hi_cols] = z_hi
        return carry

    last = i + 1 == pl.num_programs(0)

    @pl.when(jnp.logical_not(last))
    def _():
        lax.fori_loop(0, ROW_GROUPS, functools.partial(group, prefetch=True), 0)

    @pl.when(last)
    def _():
        lax.fori_loop(0, ROW_GROUPS, functools.partial(group, prefetch=False), 0)

    o_ref[...] = _layer_norm(z_ref[...], lnw_ref[...], lnb_ref[...])


def _combine(ys, dest, gates, x, lnw, lnb):
    nt = TOKENS // ROW_TILE
    row = pl.BlockSpec((ROW_TILE, D_MODEL), lambda i: (i, 0))
    vec = pl.BlockSpec((1, D_MODEL), lambda i: (0, 0))
    dest_tiles = dest.reshape(nt, 1, ROW_TILE * TOP_K)
    return pl.pallas_call(
        _combine_kernel,
        grid=(nt,),
        in_specs=[pl.BlockSpec((1, 1, ROW_TILE * TOP_K), lambda i: (i, 0, 0), memory_space=pltpu.SMEM),
                  pl.BlockSpec((1, 1, ROW_TILE * TOP_K), lambda i: (jnp.minimum(i + 1, nt - 1), 0, 0),
                               memory_space=pltpu.SMEM),
                  pl.BlockSpec(memory_space=pl.ANY),
                  pl.BlockSpec((ROW_TILE, LANES), lambda i: (i, 0)),
                  row, vec, vec],
        out_specs=row,
        out_shape=jax.ShapeDtypeStruct((TOKENS, D_MODEL), F32),
        scratch_shapes=[pltpu.VMEM((2, TOP_K, ROW_GROUPS, ROW_SUB, SUBLANES, LANES), jnp.int32),
                        pltpu.VMEM((ROW_TILE, D_MODEL), F32), pltpu.SemaphoreType.DMA((2,))],
        compiler_params=_cparams(("arbitrary",)),
        name="combine",
    )(dest_tiles, dest_tiles, ys, gates, x, lnw, lnb)


def _layer(x, mem, w_in, attn_sinks, attn_norm_w, ret_norm_w, w_mix_out, ln_mix_w, ln_mix_b,
           w_mem_q, w_mem_kv, w_mem_out, ln_mem_w, ln_mem_b, w_router, b_router,
           w_gate_up, b_gate_up, w_down, b_down, ln_moe_w, ln_moe_b):
    o_k, o_v, o_qr = D_ATTN, D_ATTN + D_KV, D_ATTN + 2 * D_KV
    w_in_p = jnp.concatenate([w_in[:, :o_k], w_in[:, o_qr:], w_in[:, o_k:o_v], w_in[:, o_v:o_qr]],
                             axis=1).astype(BF16)
    proj = _project(x, w_in_p, PROJ_TM, PROJ_TN)
    y = _mixer(proj, attn_sinks.astype(F32), attn_norm_w.reshape(1, D_ATTN), ret_norm_w.reshape(1, D_RET))
    x1 = _out_ln(y, w_mix_out.astype(BF16), x, ln_mix_w.reshape(1, D_MODEL), ln_mix_b.reshape(1, D_MODEL))

    kv = _project(mem.reshape(BATCH * N_MEM, D_MODEL), w_mem_kv.astype(BF16), BATCH * N_MEM, D_MODEL)
    o = _mem_attn(x1, w_mem_q.astype(BF16), kv)
    x2 = _out_ln(o, w_mem_out.astype(BF16), x1, ln_mem_w.reshape(1, D_MODEL), ln_mem_b.reshape(1, D_MODEL))

    wr = jnp.pad(w_router, ((0, 0), (0, LANES - N_EXPERTS)))
    wr_hi = wr.astype(BF16)
    wr = jnp.concatenate([wr_hi, (wr - wr_hi.astype(F32)).astype(BF16)], axis=1)
    br =jnp.pad(b_router.reshape(1, N_EXPERTS), ((0, 0), (0, LANES - N_EXPERTS)), constant_values=NEG_INF)
    idx, gates, rank, counts = _route(x2, wr, br)

    counts = counts[0, :N_EXPERTS].astype(jnp.int32)
    padded = ((counts + MOE_TM - 1) // MOE_TM) * MOE_TM
    eid = jnp.arange(N_EXPERTS, dtype=jnp.int32)
    pad_end = jnp.sum(jnp.where(eid[None, :] <= eid[:, None], padded[None, :], 0), axis=1)
    pad_start = pad_end - padded
    chosen = idx[:, :TOP_K, None] == eid[None, None, :]
    dest = jnp.sum(jnp.where(chosen, pad_start[None, None, :], 0), axis=-1) + rank[:, :TOP_K]
    visit_row = jnp.arange(N_VISITS, dtype=jnp.int32) * MOE_TM
    visit_expert = jnp.minimum(jnp.sum((pad_end[None, :] <= visit_row[:, None]).astype(jnp.int32), axis=1),
                               N_EXPERTS - 1)
    visit_nvalid = jnp.clip(counts[visit_expert] - (visit_row - pad_start[visit_expert]),
                            0, MOE_TM).astype(jnp.int32)

    fill_start = jnp.concatenate([pad_start + counts, pad_end[-1:]]).astype(jnp.int32)
    fill_len = jnp.concatenate([padded - counts, (MOE_ROWS - pad_end[-1:]) // FILL_SIZES[0]]).astype(jnp.int32)
    xs = _scatter_rows(x2, dest, fill_start, fill_len)
    ys = _experts(xs, visit_expert, visit_nvalid,
                  w_gate_up, b_gate_up.reshape(N_EXPERTS, 1, 2 * D_EXPERT),
                  w_down, b_down.reshape(N_EXPERTS, 1, D_MODEL))
    return _combine(ys, dest, gates, x2, ln_moe_w.reshape(1, D_MODEL), ln_moe_b.reshape(1, D_MODEL))


def kernel(x, mem, w_in, attn_sinks, attn_norm_w, ret_norm_w, w_mix_out, ln_mix_w, ln_mix_b, w_mem_q, w_mem_kv, w_mem_out, ln_mem_w, ln_mem_b, w_router, b_router, w_gate_up, b_gate_up, w_down, b_down, ln_moe_w, ln_moe_b):
    xt = x.reshape(TOKENS, D_MODEL)
    out = _layer(xt, mem, w_in[0], attn_sinks[0], attn_norm_w[0], ret_norm_w[0], w_mix_out[0],
                 ln_mix_w[0], ln_mix_b[0], w_mem_q[0], w_mem_kv[0], w_mem_out[0], ln_mem_w[0], ln_mem_b[0],
                 w_router[0], b_router[0], w_gate_up[0], b_gate_up[0], w_down[0], b_down[0],
                 ln_moe_w[0], ln_moe_b[0])
    return out.reshape(BATCH, SEQ, D_MODEL)
```

```python
import functools
import math

import jax
import jax.numpy as jnp
from jax import lax
from jax.experimental import pallas as pl
from jax.experimental.pallas import tpu as pltpu

F32 = jnp.float32
BF16 = jnp.bfloat16

D_MODEL = 2048
BATCH = 2
SEQ = 16384
TOKENS = BATCH * SEQ

HEAD_DIM_ATTN = 64
N_Q_ATTN = 16
N_KV_ATTN = 2
GQA_GROUP = N_Q_ATTN // N_KV_ATTN
ATTN_BLOCK = 128
D_ATTN = N_Q_ATTN * HEAD_DIM_ATTN
D_KV = N_KV_ATTN * HEAD_DIM_ATTN
N_RET = 8
HEAD_DIM_RET = 128
RET_CHUNK = 128
D_RET = N_RET * HEAD_DIM_RET
D_IN = D_ATTN + 2 * D_KV + 4 * D_RET
N_MEM = 256
N_MEM_HEADS = 4
HEAD_DIM_MEM = D_MODEL // N_MEM_HEADS
N_EXPERTS = 32
TOP_K = 4
D_EXPERT = D_MODEL
SWIGLU_LIMIT = 7.0
SWIGLU_ALPHA = 1.702
LN_EPS = 1e-5
DN_ALPHA = 2.0 ** 0.25
NEG_INF = -1e30

LANES = 128
VMEM_LIMIT = 56 * 1024 * 1024

PROJ_TM = 1024
PROJ_TN = 1792
LN_TM = 512
MEM_TM = 512
ROUTE_TM = 512
ROW_TILE = 256
MOE_TM = 1024
MOE_TF = 256
N_VISITS = (TOKENS * TOP_K + N_EXPERTS * (MOE_TM - 1) + MOE_TM - 1) // MOE_TM
MOE_ROWS = N_VISITS * MOE_TM

COL_QA, COL_QR, COL_KR, COL_VR, COL_GR = 0, 1, 2, 3, 4
COL_KA = 5 * D_RET // D_KV
COL_VA = COL_KA + 1


def _cparams(sem, **kw):
    return pltpu.CompilerParams(dimension_semantics=sem, vmem_limit_bytes=VMEM_LIMIT, **kw)


def _layer_norm(z, w, b):
    mu = jnp.mean(z, axis=-1, keepdims=True)
    zc = z - mu
    var = jnp.mean(zc * zc, axis=-1, keepdims=True)
    return zc * lax.rsqrt(var + LN_EPS) * w + b


def _dot_nt(a, b):
    return lax.dot_general(a, b, (((1,), (1,)), ((), ())), preferred_element_type=F32)


def _proj_kernel(x_ref, w_ref, o_ref, xb_ref):
    @pl.when(pl.program_id(1) == 0)
    def _():
        xb_ref[...] = x_ref[...].astype(BF16)

    o_ref[...] = jnp.dot(xb_ref[...], w_ref[...], preferred_element_type=F32).astype(o_ref.dtype)


def _project(x, w, tm, tn):
    m, k = x.shape
    n = w.shape[1]
    return pl.pallas_call(
        _proj_kernel,
        grid=(m // tm, n // tn),
        in_specs=[pl.BlockSpec((tm, k), lambda i, j: (i, 0)),
                  pl.BlockSpec((k, tn), lambda i, j: (0, j))],
        out_specs=pl.BlockSpec((tm, tn), lambda i, j: (i, j)),
        out_shape=jax.ShapeDtypeStruct((m, n), BF16),
        scratch_shapes=[pltpu.VMEM((tm, k), BF16)],
        compiler_params=_cparams(("parallel", "arbitrary")),
        name="project",
    )(x, w)


def _mixer_kernel(sinkrow_ref, bias_cur_ref, bias_prev_ref, qa_ref, kp_ref, kc_ref, vp_ref, vc_ref,
                  qr_ref, kr_ref, vr_ref, gr_ref, anw_ref, rnw_ref, y_ref, state_ref, a_ref):
    n = pl.program_id(1)
    row = lax.broadcasted_iota(jnp.int32, (ATTN_BLOCK, ATTN_BLOCK), 0)
    col = lax.broadcasted_iota(jnp.int32, (ATTN_BLOCK, ATTN_BLOCK), 1)
    diff = row - col
    difff = diff.astype(F32)

    scale = HEAD_DIM_ATTN ** -0.5
    for j in range(N_KV_ATTN):
        klo, khi = j * HEAD_DIM_ATTN, (j + 1) * HEAD_DIM_ATTN
        heads = range(j * GQA_GROUP, (j + 1) * GQA_GROUP)
        q_rows = jnp.concatenate([qa_ref[:, h * HEAD_DIM_ATTN:(h + 1) * HEAD_DIM_ATTN] for h in heads], axis=0)
        q_rows = q_rows * scale
        sink = sinkrow_ref[j]
        s_cur = _dot_nt(kc_ref[:, klo:khi], q_rows) + bias_cur_ref[j]
        s_prev = _dot_nt(kp_ref[:, klo:khi], q_rows) + bias_prev_ref[j]
        s_prev = jnp.where(n > 0, s_prev, NEG_INF)
        m = jnp.maximum(jnp.max(s_cur, axis=0, keepdims=True), jnp.max(s_prev, axis=0, keepdims=True))
        m = jnp.maximum(m, sink)
        p_cur = jnp.exp(s_cur - m)
        p_prev = jnp.exp(s_prev - m)
        denom = (jnp.sum(p_cur, axis=0, keepdims=True) + jnp.sum(p_prev, axis=0, keepdims=True)
                 + jnp.exp(sink - m))
        v_cur_t = vc_ref[:, klo:khi].astype(F32).T.astype(BF16)
        v_prev_t = vp_ref[:, klo:khi].astype(F32).T.astype(BF16)
        o_t = (jnp.dot(v_cur_t, p_cur.astype(BF16), preferred_element_type=F32)
               + jnp.dot(v_prev_t, p_prev.astype(BF16), preferred_element_type=F32))
        o_t = o_t * (1.0 / denom)
        for g, h in enumerate(heads):
            a_ref[h * HEAD_DIM_ATTN:(h + 1) * HEAD_DIM_ATTN, :] = o_t[:, g * ATTN_BLOCK:(g + 1) * ATTN_BLOCK]
    a_t = a_ref[...]
    a_t = a_t * lax.rsqrt(jnp.mean(a_t * a_t, axis=0, keepdims=True) + LN_EPS)
    y_ref[:, :D_ATTN] = (a_t.T * anw_ref[...]).astype(y_ref.dtype)

    @pl.when(n == 0)
    def _():
        state_ref[...] = jnp.zeros_like(state_ref)

    pos = lax.broadcasted_iota(jnp.int32, (RET_CHUNK, 1), 0).astype(F32)
    kscale = HEAD_DIM_RET ** -0.5
    for h in range(N_RET):
        lo, hi = h * HEAD_DIM_RET, (h + 1) * HEAD_DIM_RET
        log_g = math.log1p(-(2.0 ** (-5.0 - h)))
        dmask = jnp.where(diff >= 0, jnp.exp(log_g * jnp.maximum(difff, 0.0)), 0.0)
        w_key = jnp.exp(log_g * (RET_CHUNK - 1.0 - pos))
        w_q = jnp.exp(log_g * (pos + 1.0))
        g_c = math.exp(log_g * RET_CHUNK)
        q = qr_ref[:, lo:hi]
        k = kr_ref[:, lo:hi]
        v = vr_ref[:, lo:hi]
        s = _dot_nt(q, k) * (dmask * kscale)
        inner = jnp.dot(s.astype(BF16), v, preferred_element_type=F32)
        state = state_ref[h]
        qw = (q.astype(F32) * w_q).astype(BF16)
        cross = jnp.dot(qw, state.astype(BF16), preferred_element_type=F32)
        kw_t = (k.astype(F32) * (w_key * kscale)).T.astype(BF16)
        state_ref[h] = g_c * state + jnp.dot(kw_t, v, preferred_element_type=F32)
        o = inner + cross
        mu = jnp.mean(o, axis=-1, keepdims=True)
        oc = o - mu
        var = jnp.mean(oc * oc, axis=-1, keepdims=True)
        o = oc * lax.rsqrt(var + LN_EPS) * rnw_ref[:, lo:hi]
        g = gr_ref[:, lo:hi].astype(F32)
        o = o * (g * jax.nn.sigmoid(g))
        y_ref[:, D_ATTN + lo:D_ATTN + hi] = o.astype(y_ref.dtype)


def _mixer(proj, sinks, attn_norm_w, ret_norm_w):
    nc = SEQ // ATTN_BLOCK
    blk = ATTN_BLOCK

    def wide(colblock):
        return pl.BlockSpec((blk, D_RET), lambda b, n: (b * nc + n, colblock))

    def kv_cur(colblock):
        return pl.BlockSpec((blk, D_KV), lambda b, n: (b * nc + n, colblock))

    def kv_prev(colblock):
        return pl.BlockSpec((blk, D_KV), lambda b, n: (b * nc + jnp.maximum(n - 1, 0), colblock))

    vec = pl.BlockSpec((1, D_RET), lambda b, n: (0, 0))
    gq = GQA_GROUP * blk

    def table(rows):
        return pl.BlockSpec((N_KV_ATTN, rows, gq), lambda b, n: (0, 0, 0))

    slopes = jnp.exp2(-8.0 * (jnp.arange(N_Q_ATTN, dtype=F32) + 1.0) / N_Q_ATTN).reshape(N_KV_ATTN, 1, GQA_GROUP, 1)
    key = jnp.arange(blk, dtype=F32).reshape(1, blk, 1, 1)
    query = jnp.arange(blk, dtype=F32).reshape(1, 1, 1, blk)
    dist_cur = jnp.broadcast_to(query - key, (N_KV_ATTN, blk, GQA_GROUP, blk))
    bias_cur = jnp.where(dist_cur >= 0, -slopes * dist_cur, NEG_INF).reshape(N_KV_ATTN, blk, gq)
    bias_prev = jnp.where(dist_cur < 0, -slopes * (dist_cur + blk), NEG_INF).reshape(N_KV_ATTN, blk, gq)
    sink_rows = jnp.repeat(sinks.astype(F32).reshape(N_KV_ATTN, 1, GQA_GROUP), blk, axis=2)
    return pl.pallas_call(
        _mixer_kernel,
        grid=(BATCH, nc),
        in_specs=[table(1), table(blk), table(blk),
                  wide(COL_QA), kv_prev(COL_KA), kv_cur(COL_KA), kv_prev(COL_VA), kv_cur(COL_VA),
                  wide(COL_QR), wide(COL_KR), wide(COL_VR), wide(COL_GR), vec, vec],
        out_specs=pl.BlockSpec((blk, D_MODEL), lambda b, n: (b * nc + n, 0)),
        out_shape=jax.ShapeDtypeStruct((TOKENS, D_MODEL), BF16),
        scratch_shapes=[pltpu.VMEM((N_RET, HEAD_DIM_RET, HEAD_DIM_RET), F32),
                        pltpu.VMEM((D_ATTN, blk), F32)],
        compiler_params=_cparams(("parallel", "arbitrary")),
        name="mixer",
    )(sink_rows, bias_cur, bias_prev, proj, proj, proj, proj, proj, proj, proj, proj, proj,
      attn_norm_w, ret_norm_w)


def _out_ln_kernel(y_ref, w_ref, x_ref, lnw_ref, lnb_ref, o_ref):
    h = jnp.dot(y_ref[...], w_ref[...], preferred_element_type=F32)
    o_ref[...] = _layer_norm(DN_ALPHA * x_ref[...] + h, lnw_ref[...], lnb_ref[...])


def _out_ln(y, w, x, lnw, lnb):
    tm = LN_TM
    row = pl.BlockSpec((tm, D_MODEL), lambda i: (i, 0))
    vec = pl.BlockSpec((1, D_MODEL), lambda i: (0, 0))
    return pl.pallas_call(
        _out_ln_kernel,
        grid=(TOKENS // tm,),
        in_specs=[row,
                  pl.BlockSpec((D_MODEL, D_MODEL), lambda i: (0, 0), pipeline_mode=pl.Buffered(1)),
                  row, vec, vec],
        out_specs=row,
        out_shape=jax.ShapeDtypeStruct((TOKENS, D_MODEL), F32),
        compiler_params=_cparams(("parallel",)),
        name="out_ln",
    )(y, w, x, lnw, lnb)


def _mem_attn_kernel(x_ref, wq_ref, k_ref, v_ref, o_ref):
    q = jnp.dot(x_ref[...].astype(BF16), wq_ref[...], preferred_element_type=F32)
    q = (q * (HEAD_DIM_MEM ** -0.5)).astype(BF16)
    for h in range(N_MEM_HEADS):
        lo, hi = h * HEAD_DIM_MEM, (h + 1) * HEAD_DIM_MEM
        s = _dot_nt(q[:, lo:hi], k_ref[:, lo:hi])
        m = jnp.max(s, axis=-1, keepdims=True)
        p = jnp.exp(s - m)
        denom = jnp.sum(p, axis=-1, keepdims=True)
        o = jnp.dot(p.astype(BF16), v_ref[:, lo:hi], preferred_element_type=F32)
        o_ref[:, lo:hi] = (o / denom).astype(o_ref.dtype)


def _mem_attn(x, wq, kv):
    tm = MEM_TM
    nt = SEQ // tm
    row = pl.BlockSpec((tm, D_MODEL), lambda b, i: (b * nt + i, 0))
    return pl.pallas_call(
        _mem_attn_kernel,
        grid=(BATCH, nt),
        in_specs=[row,
                  pl.BlockSpec((D_MODEL, D_MODEL), lambda b, i: (0, 0), pipeline_mode=pl.Buffered(1)),
                  pl.BlockSpec((N_MEM, D_MODEL), lambda b, i: (b, 0)),
                  pl.BlockSpec((N_MEM, D_MODEL), lambda b, i: (b, 1))],
        out_specs=row,
        out_shape=jax.ShapeDtypeStruct((TOKENS, D_MODEL), BF16),
        compiler_params=_cparams(("parallel", "parallel")),
        name="mem_attn",
    )(x, wq, kv, kv)


def _route_kernel(x_ref, wr_ref, br_ref, idx_ref, gate_ref, rank_ref, count_ref, carry_ref):
    tm = x_ref.shape[0]

    @pl.when(pl.program_id(0) == 0)
    def _():
        carry_ref[...] = jnp.zeros_like(carry_ref)

    x = x_ref[...]
    x_hi = x.astype(BF16)
    x_lo = (x - x_hi.astype(F32)).astype(BF16)
    part_hi = jnp.dot(x_hi, wr_ref[...], preferred_element_type=F32)
    part_lo = jnp.dot(x_lo, wr_ref[...], preferred_element_type=F32)
    logits = (part_hi[:, :LANES] + (part_hi[:, LANES:] + part_lo[:, :LANES]) + part_lo[:, LANES:]
              + br_ref[...])
    lane = lax.broadcasted_iota(jnp.int32, (tm, LANES), 1)
    lane_f = lane.astype(F32)
    work = logits
    sels, vals, idxs = [], [], []
    for _ in range(TOP_K):
        v = jnp.max(work, axis=-1, keepdims=True)
        i = jnp.min(jnp.where(work == v, lane_f, float(LANES)), axis=-1, keepdims=True)
        sel = lane_f == i
        work = jnp.where(sel, -jnp.inf, work)
        sels.append(sel)
        vals.append(v)
        idxs.append(i)
    exps = [jnp.exp(v - vals[0]) for v in vals]
    total = exps[0] + exps[1] + exps[2] + exps[3]

    onehot = jnp.zeros((tm, LANES), F32)
    for sel in sels:
        onehot = jnp.where(sel, 1.0, onehot)
    r = lax.broadcasted_iota(jnp.int32, (tm, tm), 0)
    c = lax.broadcasted_iota(jnp.int32, (tm, tm), 1)
    below = jnp.where(c < r, 1.0, 0.0).astype(BF16)
    prefix = jnp.dot(below, onehot.astype(BF16), preferred_element_type=F32)
    rankmat = prefix + carry_ref[...]
    carry_ref[...] = carry_ref[...] + jnp.sum(onehot, axis=0, keepdims=True)
    count_ref[...] = carry_ref[...]

    idx_out = jnp.zeros((tm, LANES), jnp.int32)
    gate_out = jnp.zeros((tm, LANES), F32)
    rank_out = jnp.zeros((tm, LANES), jnp.int32)
    for k in range(TOP_K):
        rk = jnp.sum(jnp.where(sels[k], rankmat, 0.0), axis=-1, keepdims=True).astype(jnp.int32)
        idx_out = jnp.where(lane == k, idxs[k].astype(jnp.int32), idx_out)
        gate_out = jnp.where(lane == k, exps[k] / total, gate_out)
        rank_out = jnp.where(lane == k, rk, rank_out)
    idx_ref[...] = idx_out
    gate_ref[...] = gate_out
    rank_ref[...] = rank_out


def _route(x, wr, br):
    tm = ROUTE_TM
    wide = pl.BlockSpec((tm, LANES), lambda i: (i, 0))
    return pl.pallas_call(
        _route_kernel,
        grid=(TOKENS // tm,),
        in_specs=[pl.BlockSpec((tm, D_MODEL), lambda i: (i, 0)),
                  pl.BlockSpec((D_MODEL, 2 * LANES), lambda i: (0, 0)),
                  pl.BlockSpec((1, LANES), lambda i: (0, 0))],
        out_specs=[wide, wide, wide, pl.BlockSpec((1, LANES), lambda i: (0, 0))],
        out_shape=[jax.ShapeDtypeStruct((TOKENS, LANES), jnp.int32),
                   jax.ShapeDtypeStruct((TOKENS, LANES), F32),
                   jax.ShapeDtypeStruct((TOKENS, LANES), jnp.int32),
                   jax.ShapeDtypeStruct((1, LANES), F32)],
        scratch_shapes=[pltpu.VMEM((1, LANES), F32)],
        compiler_params=_cparams(("arbitrary",)),
        name="route",
    )(x, wr, br)


HALF = D_MODEL // 2


def _pack_rows(x):
    return _pack_pair(x[:, :HALF], x[:, HALF:])


def _pack_pair(lo, hi):
    lo = lax.bitcast_convert_type(lo.astype(BF16).astype(F32), jnp.int32)
    hi = lax.bitcast_convert_type(hi.astype(BF16).astype(F32), jnp.int32)
    return hi | lax.shift_right_logical(lo, 16)


def _unpack_rows(w):
    lo = lax.bitcast_convert_type(lax.shift_left(w, 16), F32)
    hi = lax.bitcast_convert_type(w & jnp.int32(-65536), F32)
    return lo, hi


def _wait_row_copies(rows_ref, sem):
    pltpu.make_async_copy(rows_ref.at[pl.ds(0, ROW_TILE)], rows_ref.at[pl.ds(0, ROW_TILE)], sem).wait()


ROW_SUB = HALF // LANES
SUBLANES = 8
ROW_GROUPS = ROW_TILE // SUBLANES


FILL_SIZES = tuple(MOE_TM >> (b + 1) for b in range(MOE_TM.bit_length() - 1))


def _fill_copies(fill_start_ref, fill_len_ref, zero_ref, xs_ref, sem, act):
    def body(e, carry):
        pos = fill_start_ref[e]
        n = fill_len_ref[e]
        for size in FILL_SIZES:
            take = (n & size) != 0

            @pl.when(take)
            def _():
                cp = pltpu.make_async_copy(zero_ref.at[pl.ds(0, size)], xs_ref.at[pl.ds(pos, size)], sem)
                getattr(cp, act)()

            pos = pos + jnp.where(take, size, 0)
        return carry

    lax.fori_loop(0, N_EXPERTS, body, 0)

    def tail(j, carry):
        pos = fill_start_ref[N_EXPERTS] + j * FILL_SIZES[0]
        cp = pltpu.make_async_copy(zero_ref, xs_ref.at[pl.ds(pos, FILL_SIZES[0])], sem)
        getattr(cp, act)()
        return carry

    lax.fori_loop(0, fill_len_ref[N_EXPERTS], tail, 0)


def _scatter_kernel(fill_start_ref, fill_len_ref, dest_ref, x_ref, xs_ref, packed_ref, zero_ref, sem, fill_sem):
    i = pl.program_id(0)
    slot = i % 2

    @pl.when(i == 0)
    def _():
        zero_ref[...] = jnp.zeros_like(zero_ref)
        _fill_copies(fill_start_ref, fill_len_ref, zero_ref, xs_ref, fill_sem, "start")

    words = _pack_rows(x_ref[...])
    for g in range(ROW_GROUPS):
        for c in range(ROW_SUB):
            packed_ref[slot, g, c] = words[g * SUBLANES:(g + 1) * SUBLANES, c * LANES:(c + 1) * LANES]

    def body(g, carry):
        for s in range(SUBLANES):
            for k in range(TOP_K):
                d = dest_ref[0, 0, (g * SUBLANES + s) * TOP_K + k]
                pltpu.make_async_copy(packed_ref.at[slot, g, :, s, :], xs_ref.at[d],
                                      sem.at[slot]).start(priority=k % 2)
        return carry

    lax.fori_loop(0, ROW_GROUPS, body, 0)

    @pl.when(i > 0)
    def _():
        for k in range(TOP_K):
            _wait_row_copies(xs_ref, sem.at[1 - slot])

    @pl.when(i == pl.num_programs(0) - 1)
    def _():
        for k in range(TOP_K):
            _wait_row_copies(xs_ref, sem.at[slot])

    @pl.when(i == 0)
    def _():
        _fill_copies(fill_start_ref, fill_len_ref, zero_ref, xs_ref, fill_sem, "wait")


def _scatter_rows(x, dest, fill_start, fill_len):
    nt = TOKENS // ROW_TILE
    grid_spec = pltpu.PrefetchScalarGridSpec(
        num_scalar_prefetch=2,
        grid=(nt,),
        in_specs=[pl.BlockSpec((1, 1, ROW_TILE * TOP_K), lambda i, fs, fl: (i, 0, 0), memory_space=pltpu.SMEM),
                  pl.BlockSpec((ROW_TILE, D_MODEL), lambda i, fs, fl: (i, 0))],
        out_specs=pl.BlockSpec(memory_space=pl.ANY),
        scratch_shapes=[pltpu.VMEM((2, ROW_GROUPS, ROW_SUB, SUBLANES, LANES), jnp.int32),
                        pltpu.VMEM((FILL_SIZES[0], ROW_SUB, LANES), jnp.int32),
                        pltpu.SemaphoreType.DMA((2,)), pltpu.SemaphoreType.DMA(())],
    )
    return pl.pallas_call(
        _scatter_kernel,
        grid_spec=grid_spec,
        out_shape=jax.ShapeDtypeStruct((MOE_ROWS, ROW_SUB, LANES), jnp.int32),
        compiler_params=_cparams(("arbitrary",), has_side_effects=True),
        name="scatter_rows",
    )(fill_start, fill_len, dest.reshape(nt, 1, ROW_TILE * TOP_K), x)


MOE_NF = D_EXPERT // MOE_TF
MOE_ROW_STEP = 256


def _experts_kernel(ge_ref, gn_ref, de_ref, dn_ref, xs_ref, wg_ref, wu_ref, bg_ref, bu_ref, wd_ref, bd_ref,
                    ys_ref, xb_ref, act_ref, hold_ref, words_ref, ow_ref, pend_ref, xsem, ysem):
    v = pl.program_id(0)
    f = pl.program_id(1)
    gn = gn_ref[v]
    dn = dn_ref[v]
    slot = v % 2
    half_nf = MOE_NF // 2

    sub_per_tile = MOE_TF // LANES

    def ys_copies(visit, tile):
        row0 = pl.multiple_of(visit * MOE_TM, MOE_TM)
        return [pltpu.make_async_copy(ow_ref.at[:, j * LANES:(j + 1) * LANES],
                                      ys_ref.at[pl.ds(row0, MOE_TM), tile * sub_per_tile + j, :], ysem)
                for j in range(sub_per_tile)]

    def emit_output_tile():
        for cp in ys_copies(v - 1, f - half_nf):
            cp.start()
        pend_ref[0] = 1

    def drain_output_tile():
        @pl.when(pend_ref[0] == 1)
        def _():
            for cp in ys_copies(0, 0):
                cp.wait()
            pend_ref[0] = 0

    @pl.when(jnp.logical_and(v == 0, f == 0))
    def _():
        hold_ref[...] = jnp.zeros_like(hold_ref)
        ow_ref[...] = jnp.zeros_like(ow_ref)
        pend_ref[0] = 0

    def xs_copies(visit, s):
        row0 = pl.multiple_of(visit * MOE_TM, MOE_TM)
        return [pltpu.make_async_copy(xs_ref.at[pl.ds(row0, MOE_TM), c, :],
                                      words_ref.at[s, :, c * LANES:(c + 1) * LANES], xsem.at[s])
                for c in range(ROW_SUB)]

    @pl.when(f == 0)
    def _():
        @pl.when(v == 0)
        def _():
            for cp in xs_copies(0, 0):
                cp.start()

        @pl.when(v < N_VISITS)
        def _():
            for cp in xs_copies(v, slot):
                cp.wait()

        @pl.when(v + 1 < N_VISITS)
        def _():
            for cp in xs_copies(v + 1, 1 - slot):
                cp.start()

    @pl.when(jnp.logical_and(gn > 0, f == 0))
    def _():
        r = lax.broadcasted_iota(jnp.int32, (MOE_TM, 1), 0)
        lo, hi = _unpack_rows(jnp.where(r < gn, words_ref[slot], 0))
        xb_ref[:, :HALF] = lo.astype(BF16)
        xb_ref[:, HALF:] = hi.astype(BF16)

    def gate_phase(rows):
        xb = xb_ref[:rows, :]
        gl = jnp.dot(xb, wg_ref[...].astype(BF16), preferred_element_type=F32) + bg_ref[...]
        up = jnp.dot(xb, wu_ref[...].astype(BF16), preferred_element_type=F32) + bu_ref[...]
        gl = jnp.minimum(gl, SWIGLU_LIMIT)
        up = jnp.clip(up, -SWIGLU_LIMIT, SWIGLU_LIMIT)
        act = ((up + 1.0) * (gl * jax.nn.sigmoid(SWIGLU_ALPHA * gl))).astype(BF16)
        act_ref[slot, :rows, pl.ds(pl.multiple_of(f * MOE_TF, MOE_TF), MOE_TF)] = act

    def down_phase(rows):
        y = jnp.dot(act_ref[1 - slot, :rows, :], wd_ref[...].astype(BF16),
                    preferred_element_type=F32) + bd_ref[...]
        h = f % half_nf

        @pl.when(f >= half_nf)
        def _():
            drain_output_tile()
            ow_ref[:rows, :] = _pack_pair(hold_ref[h, :rows, :], y)
            emit_output_tile()

        hold_ref[h, :rows, :] = y

    for rows in range(MOE_TM, 0, -MOE_ROW_STEP):
        lower = rows - MOE_ROW_STEP
        pl.when(jnp.logical_and(gn > lower, gn <= rows))(functools.partial(gate_phase, rows))
        pl.when(jnp.logical_and(dn > lower, dn <= rows))(functools.partial(down_phase, rows))

    @pl.when(jnp.logical_and(jnp.logical_and(dn == 0, v >= 1), f >= half_nf))
    def _():
        drain_output_tile()
        ow_ref[...] = jnp.zeros_like(ow_ref)
        emit_output_tile()

    @pl.when(jnp.logical_and(v == pl.num_programs(0) - 1, f == MOE_NF - 1))
    def _():
        drain_output_tile()


def _experts(xs, visit_expert, visit_nvalid, w_gate_up, b_gate_up, w_down, b_down):
    nf = MOE_NF
    gate_e = jnp.concatenate([visit_expert, visit_expert[-1:]])
    gate_n = jnp.concatenate([visit_nvalid, jnp.zeros((1,), jnp.int32)])
    down_e = jnp.concatenate([visit_expert[:1], visit_expert])
    down_n = jnp.concatenate([jnp.zeros((1,), jnp.int32), visit_nvalid])

    def tile(f, n):
        return jnp.where(n > 0, f, nf - 1)

    grid_spec = pltpu.PrefetchScalarGridSpec(
        num_scalar_prefetch=4,
        grid=(N_VISITS + 1, nf),
        in_specs=[
            pl.BlockSpec(memory_space=pl.ANY),
            pl.BlockSpec((None, D_MODEL, MOE_TF), lambda v, f, ge, gn, de, dn: (ge[v], 0, tile(f, gn[v]))),
            pl.BlockSpec((None, D_MODEL, MOE_TF), lambda v, f, ge, gn, de, dn: (ge[v], 0, nf + tile(f, gn[v]))),
            pl.BlockSpec((None, 1, MOE_TF), lambda v, f, ge, gn, de, dn: (ge[v], 0, tile(f, gn[v]))),
            pl.BlockSpec((None, 1, MOE_TF), lambda v, f, ge, gn, de, dn: (ge[v], 0, nf + tile(f, gn[v]))),
            pl.BlockSpec((None, D_EXPERT, MOE_TF), lambda v, f, ge, gn, de, dn: (de[v], 0, tile(f, dn[v]))),
            pl.BlockSpec((None, 1, MOE_TF), lambda v, f, ge, gn, de, dn: (de[v], 0, tile(f, dn[v]))),
        ],
        out_specs=pl.BlockSpec(memory_space=pl.ANY),
        scratch_shapes=[pltpu.VMEM((MOE_TM, D_MODEL), BF16),
                        pltpu.VMEM((2, MOE_TM, D_EXPERT), BF16),
                        pltpu.VMEM((nf // 2, MOE_TM, MOE_TF), F32),
                        pltpu.VMEM((2, MOE_TM, HALF), jnp.int32),
                        pltpu.VMEM((MOE_TM, MOE_TF), jnp.int32),
                        pltpu.SMEM((1,), jnp.int32),
                        pltpu.SemaphoreType.DMA((2,)),
                        pltpu.SemaphoreType.DMA(())],
    )
    return pl.pallas_call(
        _experts_kernel,
        grid_spec=grid_spec,
        out_shape=jax.ShapeDtypeStruct((MOE_ROWS, ROW_SUB, LANES), jnp.int32),
        compiler_params=_cparams(("arbitrary", "arbitrary")),
        name="experts",
    )(gate_e, gate_n, down_e, down_n, xs, w_gate_up, w_gate_up, b_gate_up, b_gate_up, w_down, b_down)


def _combine_kernel(dest_ref, dest_next_ref, ys_ref, gate_ref, x_ref, lnw_ref, lnb_ref, o_ref,
                    buf_ref, z_ref, sem):
    i = pl.program_id(0)
    slot = i % 2

    def start_group(idx_ref, s, g):
        for t in range(SUBLANES):
            for k in range(TOP_K):
                d = idx_ref[0, 0, (g * SUBLANES + t) * TOP_K + k]
                pltpu.make_async_copy(ys_ref.at[d], buf_ref.at[s, k, g, :, t, :],
                                      sem.at[s]).start(priority=k % 2)

    @pl.when(i == 0)
    def _():
        def body(g, carry):
            start_group(dest_ref, slot, g)
            return carry

        lax.fori_loop(0, ROW_GROUPS, body, 0)

    for k in range(TOP_K):
        _wait_row_copies(ys_ref, sem.at[slot])

    def group(g, carry, prefetch):
        if prefetch:
            start_group(dest_next_ref, 1 - slot, g)
        rows = pl.ds(pl.multiple_of(g * SUBLANES, SUBLANES), SUBLANES)
        gates = gate_ref[rows, :]
        gate_k = [jnp.broadcast_to(gates[:, k:k + 1], (SUBLANES, LANES)) for k in range(TOP_K)]
        for c in range(ROW_SUB):
            lo_cols = slice(c * LANES, (c + 1) * LANES)
            hi_cols = slice(HALF + c * LANES, HALF + (c + 1) * LANES)
            z_lo = DN_ALPHA * x_ref[rows, lo_cols]
            z_hi = DN_ALPHA * x_ref[rows, hi_cols]
            for k in range(TOP_K):
                lo, hi = _unpack_rows(buf_ref[slot, k, g, c])
                z_lo = z_lo + gate_k[k] * lo
                z_hi = z_hi + gate_k[k] * hi
            z_ref[rows, lo_cols] = z_lo
            z_ref[rows, hi_cols] = z_hi
        return carry

    last = i + 1 == pl.num_programs(0)

    @pl.when(jnp.logical_not(last))
    def _():
        lax.fori_loop(0, ROW_GROUPS, functools.partial(group, prefetch=True), 0)

    @pl.when(last)
    def _():
        lax.fori_loop(0, ROW_GROUPS, functools.partial(group, prefetch=False), 0)

    o_ref[...] = _layer_norm(z_ref[...], lnw_ref[...], lnb_ref[...])


def _combine(ys, dest, gates, x, lnw, lnb):
    nt = TOKENS // ROW_TILE
    row = pl.BlockSpec((ROW_TILE, D_MODEL), lambda i: (i, 0))
    vec = pl.BlockSpec((1, D_MODEL), lambda i: (0, 0))
    dest_tiles = dest.reshape(nt, 1, ROW_TILE * TOP_K)
    return pl.pallas_call(
        _combine_kernel,
        grid=(nt,),
        in_specs=[pl.BlockSpec((1, 1, ROW_TILE * TOP_K), lambda i: (i, 0, 0), memory_space=pltpu.SMEM),
                  pl.BlockSpec((1, 1, ROW_TILE * TOP_K), lambda i: (jnp.minimum(i + 1, nt - 1), 0, 0),
                               memory_space=pltpu.SMEM),
                  pl.BlockSpec(memory_space=pl.ANY),
                  pl.BlockSpec((ROW_TILE, LANES), lambda i: (i, 0)),
                  row, vec, vec],
        out_specs=row,
        out_shape=jax.ShapeDtypeStruct((TOKENS, D_MODEL), F32),
        scratch_shapes=[pltpu.VMEM((2, TOP_K, ROW_GROUPS, ROW_SUB, SUBLANES, LANES), jnp.int32),
                        pltpu.VMEM((ROW_TILE, D_MODEL), F32), pltpu.SemaphoreType.DMA((2,))],
        compiler_params=_cparams(("arbitrary",)),
        name="combine",
    )(dest_tiles, dest_tiles, ys, gates, x, lnw, lnb)


def _layer(x, mem, w_in, attn_sinks, attn_norm_w, ret_norm_w, w_mix_out, ln_mix_w, ln_mix_b,
           w_mem_q, w_mem_kv, w_mem_out, ln_mem_w, ln_mem_b, w_router, b_router,
           w_gate_up, b_gate_up, w_down, b_down, ln_moe_w, ln_moe_b):
    o_k, o_v, o_qr = D_ATTN, D_ATTN + D_KV, D_ATTN + 2 * D_KV
    w_in_p = jnp.concatenate([w_in[:, :o_k], w_in[:, o_qr:], w_in[:, o_k:o_v], w_in[:, o_v:o_qr]],
                             axis=1).astype(BF16)
    proj = _project(x, w_in_p, PROJ_TM, PROJ_TN)
    y = _mixer(proj, attn_sinks.astype(F32), attn_norm_w.reshape(1, D_ATTN), ret_norm_w.reshape(1, D_RET))
    x1 = _out_ln(y, w_mix_out.astype(BF16), x, ln_mix_w.reshape(1, D_MODEL), ln_mix_b.reshape(1, D_MODEL))

    kv = _project(mem.reshape(BATCH * N_MEM, D_MODEL), w_mem_kv.astype(BF16), BATCH * N_MEM, D_MODEL)
    o = _mem_attn(x1, w_mem_q.astype(BF16), kv)
    x2 = _out_ln(o, w_mem_out.astype(BF16), x1, ln_mem_w.reshape(1, D_MODEL), ln_mem_b.reshape(1, D_MODEL))

    wr = jnp.pad(w_router, ((0, 0), (0, LANES - N_EXPERTS)))
    wr_hi = wr.astype(BF16)
    wr = jnp.concatenate([wr_hi, (wr - wr_hi.astype(F32)).astype(BF16)], axis=1)
    br =jnp.pad(b_router.reshape(1, N_EXPERTS), ((0, 0), (0, LANES - N_EXPERTS)), constant_values=NEG_INF)
    idx, gates, rank, counts = _route(x2, wr, br)

    counts = counts[0, :N_EXPERTS].astype(jnp.int32)
    padded = ((counts + MOE_TM - 1) // MOE_TM) * MOE_TM
    eid = jnp.arange(N_EXPERTS, dtype=jnp.int32)
    pad_end = jnp.sum(jnp.where(eid[None, :] <= eid[:, None], padded[None, :], 0), axis=1)
    pad_start = pad_end - padded
    chosen = idx[:, :TOP_K, None] == eid[None, None, :]
    dest = jnp.sum(jnp.where(chosen, pad_start[None, None, :], 0), axis=-1) + rank[:, :TOP_K]
    visit_row = jnp.arange(N_VISITS, dtype=jnp.int32) * MOE_TM
    visit_expert = jnp.minimum(jnp.sum((pad_end[None, :] <= visit_row[:, None]).astype(jnp.int32), axis=1),
                               N_EXPERTS - 1)
    visit_nvalid = jnp.clip(counts[visit_expert] - (visit_row - pad_start[visit_expert]),
                            0, MOE_TM).astype(jnp.int32)

    fill_start = jnp.concatenate([pad_start + counts, pad_end[-1:]]).astype(jnp.int32)
    fill_len = jnp.concatenate([padded - counts, (MOE_ROWS - pad_end[-1:]) // FILL_SIZES[0]]).astype(jnp.int32)
    xs = _scatter_rows(x2, dest, fill_start, fill_len)
    ys = _experts(xs, visit_expert, visit_nvalid,
                  w_gate_up, b_gate_up.reshape(N_EXPERTS, 1, 2 * D_EXPERT),
                  w_down, b_down.reshape(N_EXPERTS, 1, D_MODEL))
    return _combine(ys, dest, gates, x2, ln_moe_w.reshape(1, D_MODEL), ln_moe_b.reshape(1, D_MODEL))


def kernel(x, mem, w_in, attn_sinks, attn_norm_w, ret_norm_w, w_mix_out, ln_mix_w, ln_mix_b, w_mem_q, w_mem_kv, w_mem_out, ln_mem_w, ln_mem_b, w_router, b_router, w_gate_up, b_gate_up, w_down, b_down, ln_moe_w, ln_moe_b):
    xt = x.reshape(TOKENS, D_MODEL)
    out = _layer(xt, mem, w_in[0], attn_sinks[0], attn_norm_w[0], ret_norm_w[0], w_mix_out[0],
                 ln_mix_w[0], ln_mix_b[0], w_mem_q[0], w_mem_kv[0], w_mem_out[0], ln_mem_w[0], ln_mem_b[0],
                 w_router[0], b_router[0], w_gate_up[0], b_gate_up[0], w_down[0], b_down[0],
                 ln_moe_w[0], ln_moe_b[0])
    return out.reshape(BATCH, SEQ, D_MODEL)
```

```python
import functools
import math

import jax
import jax.numpy as jnp
from jax import lax
from jax.experimental import pallas as pl
from jax.experimental.pallas import tpu as pltpu

F32 = jnp.float32
BF16 = jnp.bfloat16

D_MODEL = 2048
BATCH = 2
SEQ = 16384
TOKENS = BATCH * SEQ

HEAD_DIM_ATTN = 64
N_Q_ATTN = 16
N_KV_ATTN = 2
GQA_GROUP = N_Q_ATTN // N_KV_ATTN
ATTN_BLOCK = 128
D_ATTN = N_Q_ATTN * HEAD_DIM_ATTN
D_KV = N_KV_ATTN * HEAD_DIM_ATTN
N_RET = 8
HEAD_DIM_RET = 128
RET_CHUNK = 128
D_RET = N_RET * HEAD_DIM_RET
D_IN = D_ATTN + 2 * D_KV + 4 * D_RET
N_MEM = 256
N_MEM_HEADS = 4
HEAD_DIM_MEM = D_MODEL // N_MEM_HEADS
N_EXPERTS = 32
TOP_K = 4
D_EXPERT = D_MODEL
SWIGLU_LIMIT = 7.0
SWIGLU_ALPHA = 1.702
LN_EPS = 1e-5
DN_ALPHA = 2.0 ** 0.25
NEG_INF = -1e30

LANES = 128
VMEM_LIMIT = 56 * 1024 * 1024

PROJ_TM = 1024
PROJ_TN = 1792
LN_TM = 512
MEM_TM = 512
ROUTE_TM = 512
ROW_TILE = 256
MOE_TM = 1024
MOE_TF = 256
N_VISITS = (TOKENS * TOP_K + N_EXPERTS * (MOE_TM - 1) + MOE_TM - 1) // MOE_TM
MOE_ROWS = N_VISITS * MOE_TM

COL_QA, COL_QR, COL_KR, COL_VR, COL_GR = 0, 1, 2, 3, 4
COL_KA = 5 * D_RET // D_KV
COL_VA = COL_KA + 1


def _cparams(sem, **kw):
    return pltpu.CompilerParams(dimension_semantics=sem, vmem_limit_bytes=VMEM_LIMIT, **kw)


def _layer_norm(z, w, b):
    mu = jnp.mean(z, axis=-1, keepdims=True)
    zc = z - mu
    var = jnp.mean(zc * zc, axis=-1, keepdims=True)
    return zc * lax.rsqrt(var + LN_EPS) * w + b


def _dot_nt(a, b):
    return lax.dot_general(a, b, (((1,), (1,)), ((), ())), preferred_element_type=F32)


def _proj_kernel(x_ref, w_ref, o_ref, xb_ref):
    @pl.when(pl.program_id(1) == 0)
    def _():
        xb_ref[...] = x_ref[...].astype(BF16)

    o_ref[...] = jnp.dot(xb_ref[...], w_ref[...], preferred_element_type=F32).astype(o_ref.dtype)


def _project(x, w, tm, tn):
    m, k = x.shape
    n = w.shape[1]
    return pl.pallas_call(
        _proj_kernel,
        grid=(m // tm, n // tn),
        in_specs=[pl.BlockSpec((tm, k), lambda i, j: (i, 0)),
                  pl.BlockSpec((k, tn), lambda i, j: (0, j))],
        out_specs=pl.BlockSpec((tm, tn), lambda i, j: (i, j)),
        out_shape=jax.ShapeDtypeStruct((m, n), BF16),
        scratch_shapes=[pltpu.VMEM((tm, k), BF16)],
        compiler_params=_cparams(("parallel", "arbitrary")),
        name="project",
    )(x, w)


def _mixer_kernel(sinkrow_ref, bias_cur_ref, bias_prev_ref, qa_ref, kp_ref, kc_ref, vp_ref, vc_ref,
                  qr_ref, kr_ref, vr_ref, gr_ref, anw_ref, rnw_ref, y_ref, state_ref, a_ref):
    n = pl.program_id(1)
    row = lax.broadcasted_iota(jnp.int32, (ATTN_BLOCK, ATTN_BLOCK), 0)
    col = lax.broadcasted_iota(jnp.int32, (ATTN_BLOCK, ATTN_BLOCK), 1)
    diff = row - col
    difff = diff.astype(F32)

    scale = HEAD_DIM_ATTN ** -0.5
    for j in range(N_KV_ATTN):
        klo, khi = j * HEAD_DIM_ATTN, (j + 1) * HEAD_DIM_ATTN
        heads = range(j * GQA_GROUP, (j + 1) * GQA_GROUP)
        q_rows = jnp.concatenate([qa_ref[:, h * HEAD_DIM_ATTN:(h + 1) * HEAD_DIM_ATTN] for h in heads], axis=0)
        q_rows = q_rows * scale
        sink = sinkrow_ref[j]
        s_cur = _dot_nt(kc_ref[:, klo:khi], q_rows) + bias_cur_ref[j]
        s_prev = _dot_nt(kp_ref[:, klo:khi], q_rows) + bias_prev_ref[j]
        s_prev = jnp.where(n > 0, s_prev, NEG_INF)
        m = jnp.maximum(jnp.max(s_cur, axis=0, keepdims=True), jnp.max(s_prev, axis=0, keepdims=True))
        m = jnp.maximum(m, sink)
        p_cur = jnp.exp(s_cur - m)
        p_prev = jnp.exp(s_prev - m)
        denom = (jnp.sum(p_cur, axis=0, keepdims=True) + jnp.sum(p_prev, axis=0, keepdims=True)
                 + jnp.exp(sink - m))
        v_cur_t = vc_ref[:, klo:khi].astype(F32).T.astype(BF16)
        v_prev_t = vp_ref[:, klo:khi].astype(F32).T.astype(BF16)
        o_t = (jnp.dot(v_cur_t, p_cur.astype(BF16), preferred_element_type=F32)
               + jnp.dot(v_prev_t, p_prev.astype(BF16), preferred_element_type=F32))
        o_t = o_t * (1.0 / denom)
        for g, h in enumerate(heads):
            a_ref[h * HEAD_DIM_ATTN:(h + 1) * HEAD_DIM_ATTN, :] = o_t[:, g * ATTN_BLOCK:(g + 1) * ATTN_BLOCK]
    a_t = a_ref[...]
    a_t = a_t * lax.rsqrt(jnp.mean(a_t * a_t, axis=0, keepdims=True) + LN_EPS)
    y_ref[:, :D_ATTN] = (a_t.T * anw_ref[...]).astype(y_ref.dtype)

    @pl.when(n == 0)
    def _():
        state_ref[...] = jnp.zeros_like(state_ref)

    pos = lax.broadcasted_iota(jnp.int32, (RET_CHUNK, 1), 0).astype(F32)
    kscale = HEAD_DIM_RET ** -0.5
    for h in range(N_RET):
        lo, hi = h * HEAD_DIM_RET, (h + 1) * HEAD_DIM_RET
        log_g = math.log1p(-(2.0 ** (-5.0 - h)))
        dmask = jnp.where(diff >= 0, jnp.exp(log_g * jnp.maximum(difff, 0.0)), 0.0)
        w_key = jnp.exp(log_g * (RET_CHUNK - 1.0 - pos))
        w_q = jnp.exp(log_g * (pos + 1.0))
        g_c = math.exp(log_g * RET_CHUNK)
        q = qr_ref[:, lo:hi]
        k = kr_ref[:, lo:hi]
        v = vr_ref[:, lo:hi]
        s = _dot_nt(q, k) * (dmask * kscale)
        inner = jnp.dot(s.astype(BF16), v, preferred_element_type=F32)
        state = state_ref[h]
        qw = (q.astype(F32) * w_q).astype(BF16)
        cross = jnp.dot(qw, state.astype(BF16), preferred_element_type=F32)
        kw_t = (k.astype(F32) * (w_key * kscale)).T.astype(BF16)
        state_ref[h] = g_c * state + jnp.dot(kw_t, v, preferred_element_type=F32)
        o = inner + cross
        mu = jnp.mean(o, axis=-1, keepdims=True)
        oc = o - mu
        var = jnp.mean(oc * oc, axis=-1, keepdims=True)
        o = oc * lax.rsqrt(var + LN_EPS) * rnw_ref[:, lo:hi]
        g = gr_ref[:, lo:hi].astype(F32)
        o = o * (g * jax.nn.sigmoid(g))
        y_ref[:, D_ATTN + lo:D_ATTN + hi] = o.astype(y_ref.dtype)


def _mixer(proj, sinks, attn_norm_w, ret_norm_w):
    nc = SEQ // ATTN_BLOCK
    blk = ATTN_BLOCK

    def wide(colblock):
        return pl.BlockSpec((blk, D_RET), lambda b, n: (b * nc + n, colblock))

    def kv_cur(colblock):
        return pl.BlockSpec((blk, D_KV), lambda b, n: (b * nc + n, colblock))

    def kv_prev(colblock):
        return pl.BlockSpec((blk, D_KV), lambda b, n: (b * nc + jnp.maximum(n - 1, 0), colblock))

    vec = pl.BlockSpec((1, D_RET), lambda b, n: (0, 0))
    gq = GQA_GROUP * blk

    def table(rows):
        return pl.BlockSpec((N_KV_ATTN, rows, gq), lambda b, n: (0, 0, 0))

    slopes = jnp.exp2(-8.0 * (jnp.arange(N_Q_ATTN, dtype=F32) + 1.0) / N_Q_ATTN).reshape(N_KV_ATTN, 1, GQA_GROUP, 1)
    key = jnp.arange(blk, dtype=F32).reshape(1, blk, 1, 1)
    query = jnp.arange(blk, dtype=F32).reshape(1, 1, 1, blk)
    dist_cur = jnp.broadcast_to(query - key, (N_KV_ATTN, blk, GQA_GROUP, blk))
    bias_cur = jnp.where(dist_cur >= 0, -slopes * dist_cur, NEG_INF).reshape(N_KV_ATTN, blk, gq)
    bias_prev = jnp.where(dist_cur < 0, -slopes * (dist_cur + blk), NEG_INF).reshape(N_KV_ATTN, blk, gq)
    sink_rows = jnp.repeat(sinks.astype(F32).reshape(N_KV_ATTN, 1, GQA_GROUP), blk, axis=2)
    return pl.pallas_call(
        _mixer_kernel,
        grid=(BATCH, nc),
        in_specs=[table(1), table(blk), table(blk),
                  wide(COL_QA), kv_prev(COL_KA), kv_cur(COL_KA), kv_prev(COL_VA), kv_cur(COL_VA),
                  wide(COL_QR), wide(COL_KR), wide(COL_VR), wide(COL_GR), vec, vec],
        out_specs=pl.BlockSpec((blk, D_MODEL), lambda b, n: (b * nc + n, 0)),
        out_shape=jax.ShapeDtypeStruct((TOKENS, D_MODEL), BF16),
        scratch_shapes=[pltpu.VMEM((N_RET, HEAD_DIM_RET, HEAD_DIM_RET), F32),
                        pltpu.VMEM((D_ATTN, blk), F32)],
        compiler_params=_cparams(("parallel", "arbitrary")),
        name="mixer",
    )(sink_rows, bias_cur, bias_prev, proj, proj, proj, proj, proj, proj, proj, proj, proj,
      attn_norm_w, ret_norm_w)


def _out_ln_kernel(y_ref, w_ref, x_ref, lnw_ref, lnb_ref, o_ref):
    h = jnp.dot(y_ref[...], w_ref[...], preferred_element_type=F32)
    o_ref[...] = _layer_norm(DN_ALPHA * x_ref[...] + h, lnw_ref[...], lnb_ref[...])


def _out_ln(y, w, x, lnw, lnb):
    tm = LN_TM
    row = pl.BlockSpec((tm, D_MODEL), lambda i: (i, 0))
    vec = pl.BlockSpec((1, D_MODEL), lambda i: (0, 0))
    return pl.pallas_call(
        _out_ln_kernel,
        grid=(TOKENS // tm,),
        in_specs=[row,
                  pl.BlockSpec((D_MODEL, D_MODEL), lambda i: (0, 0), pipeline_mode=pl.Buffered(1)),
                  row, vec, vec],
        out_specs=row,
        out_shape=jax.ShapeDtypeStruct((TOKENS, D_MODEL), F32),
        compiler_params=_cparams(("parallel",)),
        name="out_ln",
    )(y, w, x, lnw, lnb)


def _mem_attn_kernel(x_ref, wq_ref, k_ref, v_ref, o_ref):
    q = jnp.dot(x_ref[...].astype(BF16), wq_ref[...], preferred_element_type=F32)
    q = (q * (HEAD_DIM_MEM ** -0.5)).astype(BF16)
    for h in range(N_MEM_HEADS):
        lo, hi = h * HEAD_DIM_MEM, (h + 1) * HEAD_DIM_MEM
        s = _dot_nt(q[:, lo:hi], k_ref[:, lo:hi])
        m = jnp.max(s, axis=-1, keepdims=True)
        p = jnp.exp(s - m)
        denom = jnp.sum(p, axis=-1, keepdims=True)
        o = jnp.dot(p.astype(BF16), v_ref[:, lo:hi], preferred_element_type=F32)
        o_ref[:, lo:hi] = (o / denom).astype(o_ref.dtype)


def _mem_attn(x, wq, kv):
    tm = MEM_TM
    nt = SEQ // tm
    row = pl.BlockSpec((tm, D_MODEL), lambda b, i: (b * nt + i, 0))
    return pl.pallas_call(
        _mem_attn_kernel,
        grid=(BATCH, nt),
        in_specs=[row,
                  pl.BlockSpec((D_MODEL, D_MODEL), lambda b, i: (0, 0), pipeline_mode=pl.Buffered(1)),
                  pl.BlockSpec((N_MEM, D_MODEL), lambda b, i: (b, 0)),
                  pl.BlockSpec((N_MEM, D_MODEL), lambda b, i: (b, 1))],
        out_specs=row,
        out_shape=jax.ShapeDtypeStruct((TOKENS, D_MODEL), BF16),
        compiler_params=_cparams(("parallel", "parallel")),
        name="mem_attn",
    )(x, wq, kv, kv)


def _route_kernel(x_ref, wr_ref, br_ref, idx_ref, gate_ref, rank_ref, count_ref, carry_ref):
    tm = x_ref.shape[0]

    @pl.when(pl.program_id(0) == 0)
    def _():
        carry_ref[...] = jnp.zeros_like(carry_ref)

    x = x_ref[...]
    x_hi = x.astype(BF16)
    x_lo = (x - x_hi.astype(F32)).astype(BF16)
    part_hi = jnp.dot(x_hi, wr_ref[...], preferred_element_type=F32)
    part_lo = jnp.dot(x_lo, wr_ref[...], preferred_element_type=F32)
    logits = (part_hi[:, :LANES] + (part_hi[:, LANES:] + part_lo[:, :LANES]) + part_lo[:, LANES:]
              + br_ref[...])
    lane = lax.broadcasted_iota(jnp.int32, (tm, LANES), 1)
    lane_f = lane.astype(F32)
    work = logits
    sels, vals, idxs = [], [], []
    for _ in range(TOP_K):
        v = jnp.max(work, axis=-1, keepdims=True)
        i = jnp.min(jnp.where(work == v, lane_f, float(LANES)), axis=-1, keepdims=True)
        sel = lane_f == i
        work = jnp.where(sel, -jnp.inf, work)
        sels.append(sel)
        vals.append(v)
        idxs.append(i)
    exps = [jnp.exp(v - vals[0]) for v in vals]
    total = exps[0] + exps[1] + exps[2] + exps[3]

    onehot = jnp.zeros((tm, LANES), F32)
    for sel in sels:
        onehot = jnp.where(sel, 1.0, onehot)
    r = lax.broadcasted_iota(jnp.int32, (tm, tm), 0)
    c = lax.broadcasted_iota(jnp.int32, (tm, tm), 1)
    below = jnp.where(c < r, 1.0, 0.0).astype(BF16)
    prefix = jnp.dot(below, onehot.astype(BF16), preferred_element_type=F32)
    rankmat = prefix + carry_ref[...]
    carry_ref[...] = carry_ref[...] + jnp.sum(onehot, axis=0, keepdims=True)
    count_ref[...] = carry_ref[...]

    idx_out = jnp.zeros((tm, LANES), jnp.int32)
    gate_out = jnp.zeros((tm, LANES), F32)
    rank_out = jnp.zeros((tm, LANES), jnp.int32)
    for k in range(TOP_K):
        rk = jnp.sum(jnp.where(sels[k], rankmat, 0.0), axis=-1, keepdims=True).astype(jnp.int32)
        idx_out = jnp.where(lane == k, idxs[k].astype(jnp.int32), idx_out)
        gate_out = jnp.where(lane == k, exps[k] / total, gate_out)
        rank_out = jnp.where(lane == k, rk, rank_out)
    idx_ref[...] = idx_out
    gate_ref[...] = gate_out
    rank_ref[...] = rank_out


def _route(x, wr, br):
    tm = ROUTE_TM
    wide = pl.BlockSpec((tm, LANES), lambda i: (i, 0))
    return pl.pallas_call(
        _route_kernel,
        grid=(TOKENS // tm,),
        in_specs=[pl.BlockSpec((tm, D_MODEL), lambda i: (i, 0)),
                  pl.BlockSpec((D_MODEL, 2 * LANES), lambda i: (0, 0)),
                  pl.BlockSpec((1, LANES), lambda i: (0, 0))],
        out_specs=[wide, wide, wide, pl.BlockSpec((1, LANES), lambda i: (0, 0))],
        out_shape=[jax.ShapeDtypeStruct((TOKENS, LANES), jnp.int32),
                   jax.ShapeDtypeStruct((TOKENS, LANES), F32),
                   jax.ShapeDtypeStruct((TOKENS, LANES), jnp.int32),
                   jax.ShapeDtypeStruct((1, LANES), F32)],
        scratch_shapes=[pltpu.VMEM((1, LANES), F32)],
        compiler_params=_cparams(("arbitrary",)),
        name="route",
    )(x, wr, br)


HALF = D_MODEL // 2


def _pack_rows(x):
    return _pack_pair(x[:, :HALF], x[:, HALF:])


def _pack_pair(lo, hi):
    lo = lax.bitcast_convert_type(lo.astype(BF16).astype(F32), jnp.int32)
    hi = lax.bitcast_convert_type(hi.astype(BF16).astype(F32), jnp.int32)
    return hi | lax.shift_right_logical(lo, 16)


def _unpack_rows(w):
    lo = lax.bitcast_convert_type(lax.shift_left(w, 16), F32)
    hi = lax.bitcast_convert_type(w & jnp.int32(-65536), F32)
    return lo, hi


def _wait_row_copies(rows_ref, sem):
    pltpu.make_async_copy(rows_ref.at[pl.ds(0, ROW_TILE)], rows_ref.at[pl.ds(0, ROW_TILE)], sem).wait()


ROW_SUB = HALF // LANES
SUBLANES = 8
ROW_GROUPS = ROW_TILE // SUBLANES


FILL_SIZES = tuple(MOE_TM >> (b + 1) for b in range(MOE_TM.bit_length() - 1))


def _fill_copies(fill_start_ref, fill_len_ref, zero_ref, xs_ref, sem, act):
    def body(e, carry):
        pos = fill_start_ref[e]
        n = fill_len_ref[e]
        for size in FILL_SIZES:
            take = (n & size) != 0

            @pl.when(take)
            def _():
                cp = pltpu.make_async_copy(zero_ref.at[pl.ds(0, size)], xs_ref.at[pl.ds(pos, size)], sem)
                getattr(cp, act)()

            pos = pos + jnp.where(take, size, 0)
        return carry

    lax.fori_loop(0, N_EXPERTS, body, 0)

    def tail(j, carry):
        pos = fill_start_ref[N_EXPERTS] + j * FILL_SIZES[0]
        cp = pltpu.make_async_copy(zero_ref, xs_ref.at[pl.ds(pos, FILL_SIZES[0])], sem)
        getattr(cp, act)()
        return carry

    lax.fori_loop(0, fill_len_ref[N_EXPERTS], tail, 0)


def _scatter_kernel(fill_start_ref, fill_len_ref, dest_ref, x_ref, xs_ref, packed_ref, zero_ref, sem, fill_sem):
    i = pl.program_id(0)
    slot = i % 2

    @pl.when(i == 0)
    def _():
        zero_ref[...] = jnp.zeros_like(zero_ref)
        _fill_copies(fill_start_ref, fill_len_ref, zero_ref, xs_ref, fill_sem, "start")

    words = _pack_rows(x_ref[...])
    for g in range(ROW_GROUPS):
        for c in range(ROW_SUB):
            packed_ref[slot, g, c] = words[g * SUBLANES:(g + 1) * SUBLANES, c * LANES:(c + 1) * LANES]

    def body(g, carry):
        for s in range(SUBLANES):
            for k in range(TOP_K):
                d = dest_ref[0, 0, (g * SUBLANES + s) * TOP_K + k]
                pltpu.make_async_copy(packed_ref.at[slot, g, :, s, :], xs_ref.at[d],
                                      sem.at[slot]).start(priority=k % 2)
        return carry

    lax.fori_loop(0, ROW_GROUPS, body, 0)

    @pl.when(i > 0)
    def _():
        for k in range(TOP_K):
            _wait_row_copies(xs_ref, sem.at[1 - slot])

    @pl.when(i == pl.num_programs(0) - 1)
    def _():
        for k in range(TOP_K):
            _wait_row_copies(xs_ref, sem.at[slot])

    @pl.when(i == 0)
    def _():
        _fill_copies(fill_start_ref, fill_len_ref, zero_ref, xs_ref, fill_sem, "wait")


def _scatter_rows(x, dest, fill_start, fill_len):
    nt = TOKENS // ROW_TILE
    grid_spec = pltpu.PrefetchScalarGridSpec(
        num_scalar_prefetch=2,
        grid=(nt,),
        in_specs=[pl.BlockSpec((1, 1, ROW_TILE * TOP_K), lambda i, fs, fl: (i, 0, 0), memory_space=pltpu.SMEM),
                  pl.BlockSpec((ROW_TILE, D_MODEL), lambda i, fs, fl: (i, 0))],
        out_specs=pl.BlockSpec(memory_space=pl.ANY),
        scratch_shapes=[pltpu.VMEM((2, ROW_GROUPS, ROW_SUB, SUBLANES, LANES), jnp.int32),
                        pltpu.VMEM((FILL_SIZES[0], ROW_SUB, LANES), jnp.int32),
                        pltpu.SemaphoreType.DMA((2,)), pltpu.SemaphoreType.DMA(())],
    )
    return pl.pallas_call(
        _scatter_kernel,
        grid_spec=grid_spec,
        out_shape=jax.ShapeDtypeStruct((MOE_ROWS, ROW_SUB, LANES), jnp.int32),
        compiler_params=_cparams(("arbitrary",), has_side_effects=True),
        name="scatter_rows",
    )(fill_start, fill_len, dest.reshape(nt, 1, ROW_TILE * TOP_K), x)


MOE_NF = D_EXPERT // MOE_TF
MOE_ROW_STEP = 256


def _experts_kernel(ge_ref, gn_ref, de_ref, dn_ref, xs_ref, wg_ref, wu_ref, bg_ref, bu_ref, wd_ref, bd_ref,
                    ys_ref, xb_ref, act_ref, hold_ref, words_ref, ow_ref, pend_ref, xsem, ysem):
    v = pl.program_id(0)
    f = pl.program_id(1)
    gn = gn_ref[v]
    dn = dn_ref[v]
    slot = v % 2
    half_nf = MOE_NF // 2

    sub_per_tile = MOE_TF // LANES

    def ys_copies(visit, tile):
        row0 = pl.multiple_of(visit * MOE_TM, MOE_TM)
        return [pltpu.make_async_copy(ow_ref.at[:, j * LANES:(j + 1) * LANES],
                                      ys_ref.at[pl.ds(row0, MOE_TM), tile * sub_per_tile + j, :], ysem)
                for j in range(sub_per_tile)]

    def emit_output_tile():
        for cp in ys_copies(v - 1, f - half_nf):
            cp.start()
        pend_ref[0] = 1

    def drain_output_tile():
        @pl.when(pend_ref[0] == 1)
        def _():
            for cp in ys_copies(0, 0):
                cp.wait()
            pend_ref[0] = 0

    @pl.when(jnp.logical_and(v == 0, f == 0))
    def _():
        hold_ref[...] = jnp.zeros_like(hold_ref)
        ow_ref[...] = jnp.zeros_like(ow_ref)
        pend_ref[0] = 0

    def xs_copies(visit, s):
        row0 = pl.multiple_of(visit * MOE_TM, MOE_TM)
        return [pltpu.make_async_copy(xs_ref.at[pl.ds(row0, MOE_TM), c, :],
                                      words_ref.at[s, :, c * LANES:(c + 1) * LANES], xsem.at[s])
                for c in range(ROW_SUB)]

    @pl.when(f == 0)
    def _():
        @pl.when(v == 0)
        def _():
            for cp in xs_copies(0, 0):
                cp.start()

        @pl.when(v < N_VISITS)
        def _():
            for cp in xs_copies(v, slot):
                cp.wait()

        @pl.when(v + 1 < N_VISITS)
        def _():
            for cp in xs_copies(v + 1, 1 - slot):
                cp.start()

    @pl.when(jnp.logical_and(gn > 0, f == 0))
    def _():
        r = lax.broadcasted_iota(jnp.int32, (MOE_TM, 1), 0)
        lo, hi = _unpack_rows(jnp.where(r < gn, words_ref[slot], 0))
        xb_ref[:, :HALF] = lo.astype(BF16)
        xb_ref[:, HALF:] = hi.astype(BF16)

    def gate_phase(rows):
        xb = xb_ref[:rows, :]
        gl = jnp.dot(xb, wg_ref[...].astype(BF16), preferred_element_type=F32) + bg_ref[...]
        up = jnp.dot(xb, wu_ref[...].astype(BF16), preferred_element_type=F32) + bu_ref[...]
        gl = jnp.minimum(gl, SWIGLU_LIMIT)
        up = jnp.clip(up, -SWIGLU_LIMIT, SWIGLU_LIMIT)
        act = ((up + 1.0) * (gl * jax.nn.sigmoid(SWIGLU_ALPHA * gl))).astype(BF16)
        act_ref[slot, :rows, pl.ds(pl.multiple_of(f * MOE_TF, MOE_TF), MOE_TF)] = act

    def down_phase(rows):
        drain_output_tile()
        y = jnp.dot(act_ref[1 - slot, :rows, :], wd_ref[...].astype(BF16),
                    preferred_element_type=F32) + bd_ref[...]
        h = f % half_nf
        ow_ref[:rows, :] = _pack_pair(hold_ref[h, :rows, :], y)
        hold_ref[h, :rows, :] = y
        pl.when(f >= half_nf)(emit_output_tile)

    for rows in range(MOE_TM, 0, -MOE_ROW_STEP):
        lower = rows - MOE_ROW_STEP
        pl.when(jnp.logical_and(gn > lower, gn <= rows))(functools.partial(gate_phase, rows))
        pl.when(jnp.logical_and(dn > lower, dn <= rows))(functools.partial(down_phase, rows))

    @pl.when(jnp.logical_and(jnp.logical_and(dn == 0, v >= 1), f >= half_nf))
    def _():
        drain_output_tile()
        ow_ref[...] = jnp.zeros_like(ow_ref)
        emit_output_tile()

    @pl.when(jnp.logical_and(v == pl.num_programs(0) - 1, f == MOE_NF - 1))
    def _():
        drain_output_tile()


def _experts(xs, visit_expert, visit_nvalid, w_gate_up, b_gate_up, w_down, b_down):
    nf = MOE_NF
    gate_e = jnp.concatenate([visit_expert, visit_expert[-1:]])
    gate_n = jnp.concatenate([visit_nvalid, jnp.zeros((1,), jnp.int32)])
    down_e = jnp.concatenate([visit_expert[:1], visit_expert])
    down_n = jnp.concatenate([jnp.zeros((1,), jnp.int32), visit_nvalid])

    def tile(f, n):
        return jnp.where(n > 0, f, nf - 1)

    grid_spec = pltpu.PrefetchScalarGridSpec(
        num_scalar_prefetch=4,
        grid=(N_VISITS + 1, nf),
        in_specs=[
            pl.BlockSpec(memory_space=pl.ANY),
            pl.BlockSpec((None, D_MODEL, MOE_TF), lambda v, f, ge, gn, de, dn: (ge[v], 0, tile(f, gn[v]))),
            pl.BlockSpec((None, D_MODEL, MOE_TF), lambda v, f, ge, gn, de, dn: (ge[v], 0, nf + tile(f, gn[v]))),
            pl.BlockSpec((None, 1, MOE_TF), lambda v, f, ge, gn, de, dn: (ge[v], 0, tile(f, gn[v]))),
            pl.BlockSpec((None, 1, MOE_TF), lambda v, f, ge, gn, de, dn: (ge[v], 0, nf + tile(f, gn[v]))),
            pl.BlockSpec((None, D_EXPERT, MOE_TF), lambda v, f, ge, gn, de, dn: (de[v], 0, tile(f, dn[v]))),
            pl.BlockSpec((None, 1, MOE_TF), lambda v, f, ge, gn, de, dn: (de[v], 0, tile(f, dn[v]))),
        ],
        out_specs=pl.BlockSpec(memory_space=pl.ANY),
        scratch_shapes=[pltpu.VMEM((MOE_TM, D_MODEL), BF16),
                        pltpu.VMEM((2, MOE_TM, D_EXPERT), BF16),
                        pltpu.VMEM((nf // 2, MOE_TM, MOE_TF), F32),
                        pltpu.VMEM((2, MOE_TM, HALF), jnp.int32),
                        pltpu.VMEM((MOE_TM, MOE_TF), jnp.int32),
                        pltpu.SMEM((1,), jnp.int32),
                        pltpu.SemaphoreType.DMA((2,)),
                        pltpu.SemaphoreType.DMA(())],
    )
    return pl.pallas_call(
        _experts_kernel,
        grid_spec=grid_spec,
        out_shape=jax.ShapeDtypeStruct((MOE_ROWS, ROW_SUB, LANES), jnp.int32),
        compiler_params=_cparams(("arbitrary", "arbitrary")),
        name="experts",
    )(gate_e, gate_n, down_e, down_n, xs, w_gate_up, w_gate_up, b_gate_up, b_gate_up, w_down, b_down)


def _combine_kernel(dest_ref, dest_next_ref, ys_ref, gate_ref, x_ref, lnw_ref, lnb_ref, o_ref,
                    buf_ref, z_ref, sem):
    i = pl.program_id(0)
    slot = i % 2

    def start_group(idx_ref, s, g):
        for t in range(SUBLANES):
            for k in range(TOP_K):
                d = idx_ref[0, 0, (g * SUBLANES + t) * TOP_K + k]
                pltpu.make_async_copy(ys_ref.at[d], buf_ref.at[s, k, g, :, t, :],
                                      sem.at[s]).start(priority=k % 2)

    @pl.when(i == 0)
    def _():
        def body(g, carry):
            start_group(dest_ref, slot, g)
            return carry

        lax.fori_loop(0, ROW_GROUPS, body, 0)

    for k in range(TOP_K):
        _wait_row_copies(ys_ref, sem.at[slot])

    def group(g, carry, prefetch):
        if prefetch:
            start_group(dest_next_ref, 1 - slot, g)
        rows = pl.ds(pl.multiple_of(g * SUBLANES, SUBLANES), SUBLANES)
        gates = gate_ref[rows, :]
        gate_k = [jnp.broadcast_to(gates[:, k:k + 1], (SUBLANES, LANES)) for k in range(TOP_K)]
        for c in range(ROW_SUB):
            lo_cols = slice(c * LANES, (c + 1) * LANES)
            hi_cols = slice(HALF + c * LANES, HALF + (c + 1) * LANES)
            z_lo = DN_ALPHA * x_ref[rows, lo_cols]
            z_hi = DN_ALPHA * x_ref[rows, hi_cols]
            for k in range(TOP_K):
                lo, hi = _unpack_rows(buf_ref[slot, k, g, c])
                z_lo = z_lo + gate_k[k] * lo
                z_hi = z_hi + gate_k[k] * hi
            z_ref[rows, lo_cols] = z_lo
            z_ref[rows, hi_cols] = z_hi
        return carry

    last = i + 1 == pl.num_programs(0)

    @pl.when(jnp.logical_not(last))
    def _():
        lax.fori_loop(0, ROW_GROUPS, functools.partial(group, prefetch=True), 0)

    @pl.when(last)
    def _():
        lax.fori_loop(0, ROW_GROUPS, functools.partial(group, prefetch=False), 0)

    o_ref[...] = _layer_norm(z_ref[...], lnw_ref[...], lnb_ref[...])


def _combine(ys, dest, gates, x, lnw, lnb):
    nt = TOKENS // ROW_TILE
    row = pl.BlockSpec((ROW_TILE, D_MODEL), lambda i: (i, 0))
    vec = pl.BlockSpec((1, D_MODEL), lambda i: (0, 0))
    dest_tiles = dest.reshape(nt, 1, ROW_TILE * TOP_K)
    return pl.pallas_call(
        _combine_kernel,
        grid=(nt,),
        in_specs=[pl.BlockSpec((1, 1, ROW_TILE * TOP_K), lambda i: (i, 0, 0), memory_space=pltpu.SMEM),
                  pl.BlockSpec((1, 1, ROW_TILE * TOP_K), lambda i: (jnp.minimum(i + 1, nt - 1), 0, 0),
                               memory_space=pltpu.SMEM),
                  pl.BlockSpec(memory_space=pl.ANY),
                  pl.BlockSpec((ROW_TILE, LANES), lambda i: (i, 0)),
                  row, vec, vec],
        out_specs=row,
        out_shape=jax.ShapeDtypeStruct((TOKENS, D_MODEL), F32),
        scratch_shapes=[pltpu.VMEM((2, TOP_K, ROW_GROUPS, ROW_SUB, SUBLANES, LANES), jnp.int32),
                        pltpu.VMEM((ROW_TILE, D_MODEL), F32), pltpu.SemaphoreType.DMA((2,))],
        compiler_params=_cparams(("arbitrary",)),
        name="combine",
    )(dest_tiles, dest_tiles, ys, gates, x, lnw, lnb)


def _layer(x, mem, w_in, attn_sinks, attn_norm_w, ret_norm_w, w_mix_out, ln_mix_w, ln_mix_b,
           w_mem_q, w_mem_kv, w_mem_out, ln_mem_w, ln_mem_b, w_router, b_router,
           w_gate_up, b_gate_up, w_down, b_down, ln_moe_w, ln_moe_b):
    o_k, o_v, o_qr = D_ATTN, D_ATTN + D_KV, D_ATTN + 2 * D_KV
    w_in_p = jnp.concatenate([w_in[:, :o_k], w_in[:, o_qr:], w_in[:, o_k:o_v], w_in[:, o_v:o_qr]],
                             axis=1).astype(BF16)
    proj = _project(x, w_in_p, PROJ_TM, PROJ_TN)
    y = _mixer(proj, attn_sinks.astype(F32), attn_norm_w.reshape(1, D_ATTN), ret_norm_w.reshape(1, D_RET))
    x1 = _out_ln(y, w_mix_out.astype(BF16), x, ln_mix_w.reshape(1, D_MODEL), ln_mix_b.reshape(1, D_MODEL))

    kv = _project(mem.reshape(BATCH * N_MEM, D_MODEL), w_mem_kv.astype(BF16), BATCH * N_MEM, D_MODEL)
    o = _mem_attn(x1, w_mem_q.astype(BF16), kv)
    x2 = _out_ln(o, w_mem_out.astype(BF16), x1, ln_mem_w.reshape(1, D_MODEL), ln_mem_b.reshape(1, D_MODEL))

    wr = jnp.pad(w_router, ((0, 0), (0, LANES - N_EXPERTS)))
    wr_hi = wr.astype(BF16)
    wr = jnp.concatenate([wr_hi, (wr - wr_hi.astype(F32)).astype(BF16)], axis=1)
    br =jnp.pad(b_router.reshape(1, N_EXPERTS), ((0, 0), (0, LANES - N_EXPERTS)), constant_values=NEG_INF)
    idx, gates, rank, counts = _route(x2, wr, br)

    counts = counts[0, :N_EXPERTS].astype(jnp.int32)
    padded = ((counts + MOE_TM - 1) // MOE_TM) * MOE_TM
    eid = jnp.arange(N_EXPERTS, dtype=jnp.int32)
    pad_end = jnp.sum(jnp.where(eid[None, :] <= eid[:, None], padded[None, :], 0), axis=1)
    pad_start = pad_end - padded
    chosen = idx[:, :TOP_K, None] == eid[None, None, :]
    dest = jnp.sum(jnp.where(chosen, pad_start[None, None, :], 0), axis=-1) + rank[:, :TOP_K]
    visit_row = jnp.arange(N_VISITS, dtype=jnp.int32) * MOE_TM
    visit_expert = jnp.minimum(jnp.sum((pad_end[None, :] <= visit_row[:, None]).astype(jnp.int32), axis=1),
                               N_EXPERTS - 1)
    visit_nvalid = jnp.clip(counts[visit_expert] - (visit_row - pad_start[visit_expert]),
                            0, MOE_TM).astype(jnp.int32)

    fill_start = jnp.concatenate([pad_start + counts, pad_end[-1:]]).astype(jnp.int32)
    fill_len = jnp.concatenate([padded - counts, (MOE_ROWS - pad_end[-1:]) // FILL_SIZES[0]]).astype(jnp.int32)
    xs = _scatter_rows(x2, dest, fill_start, fill_len)
    ys = _experts(xs, visit_expert, visit_nvalid,
                  w_gate_up, b_gate_up.reshape(N_EXPERTS, 1, 2 * D_EXPERT),
                  w_down, b_down.reshape(N_EXPERTS, 1, D_MODEL))
    return _combine(ys, dest, gates, x2, ln_moe_w.reshape(1, D_MODEL), ln_moe_b.reshape(1, D_MODEL))


def kernel(x, mem, w_in, attn_sinks, attn_norm_w, ret_norm_w, w_mix_out, ln_mix_w, ln_mix_b, w_mem_q, w_mem_kv, w_mem_out, ln_mem_w, ln_mem_b, w_router, b_router, w_gate_up, b_gate_up, w_down, b_down, ln_moe_w, ln_moe_b):
    xt = x.reshape(TOKENS, D_MODEL)
    out = _layer(xt, mem, w_in[0], attn_sinks[0], attn_norm_w[0], ret_norm_w[0], w_mix_out[0],
                 ln_mix_w[0], ln_mix_b[0], w_mem_q[0], w_mem_kv[0], w_mem_out[0], ln_mem_w[0], ln_mem_b[0],
                 w_router[0], b_router[0], w_gate_up[0], b_gate_up[0], w_down[0], b_down[0],
                 ln_moe_w[0], ln_moe_b[0])
    return out.reshape(BATCH, SEQ, D_MODEL)
```

```python
import functools
import math

import jax
import jax.numpy as jnp
from jax import lax
from jax.experimental import pallas as pl
from jax.experimental.pallas import tpu as pltpu

F32 = jnp.float32
BF16 = jnp.bfloat16

D_MODEL = 2048
BATCH = 2
SEQ = 16384
TOKENS = BATCH * SEQ

HEAD_DIM_ATTN = 64
N_Q_ATTN = 16
N_KV_ATTN = 2
GQA_GROUP = N_Q_ATTN // N_KV_ATTN
ATTN_BLOCK = 128
D_ATTN = N_Q_ATTN * HEAD_DIM_ATTN
D_KV = N_KV_ATTN * HEAD_DIM_ATTN
N_RET = 8
HEAD_DIM_RET = 128
RET_CHUNK = 128
D_RET = N_RET * HEAD_DIM_RET
D_IN = D_ATTN + 2 * D_KV + 4 * D_RET
N_MEM = 256
N_MEM_HEADS = 4
HEAD_DIM_MEM = D_MODEL // N_MEM_HEADS
N_EXPERTS = 32
TOP_K = 4
D_EXPERT = D_MODEL
SWIGLU_LIMIT = 7.0
SWIGLU_ALPHA = 1.702
LN_EPS = 1e-5
DN_ALPHA = 2.0 ** 0.25
NEG_INF = -1e30

LANES = 128
VMEM_LIMIT = 56 * 1024 * 1024

PROJ_TM = 1024
PROJ_TN = 1792
LN_TM = 512
MEM_TM = 512
ROUTE_TM = 512
ROW_TILE = 256
MOE_TM = 1024
MOE_TF = 256
N_VISITS = (TOKENS * TOP_K + N_EXPERTS * (MOE_TM - 1) + MOE_TM - 1) // MOE_TM
MOE_ROWS = N_VISITS * MOE_TM

COL_QA, COL_QR, COL_KR, COL_VR, COL_GR = 0, 1, 2, 3, 4
COL_KA = 5 * D_RET // D_KV
COL_VA = COL_KA + 1


def _cparams(sem, **kw):
    return pltpu.CompilerParams(dimension_semantics=sem, vmem_limit_bytes=VMEM_LIMIT, **kw)


def _layer_norm(z, w, b):
    mu = jnp.mean(z, axis=-1, keepdims=True)
    zc = z - mu
    var = jnp.mean(zc * zc, axis=-1, keepdims=True)
    return zc * lax.rsqrt(var + LN_EPS) * w + b


def _dot_nt(a, b):
    return lax.dot_general(a, b, (((1,), (1,)), ((), ())), preferred_element_type=F32)


def _proj_kernel(x_ref, w_ref, o_ref, xb_ref):
    @pl.when(pl.program_id(1) == 0)
    def _():
        xb_ref[...] = x_ref[...].astype(BF16)

    o_ref[...] = jnp.dot(xb_ref[...], w_ref[...], preferred_element_type=F32).astype(o_ref.dtype)


def _project(x, w, tm, tn):
    m, k = x.shape
    n = w.shape[1]
    return pl.pallas_call(
        _proj_kernel,
        grid=(m // tm, n // tn),
        in_specs=[pl.BlockSpec((tm, k), lambda i, j: (i, 0)),
                  pl.BlockSpec((k, tn), lambda i, j: (0, j))],
        out_specs=pl.BlockSpec((tm, tn), lambda i, j: (i, j)),
        out_shape=jax.ShapeDtypeStruct((m, n), BF16),
        scratch_shapes=[pltpu.VMEM((tm, k), BF16)],
        compiler_params=_cparams(("parallel", "arbitrary")),
        name="project",
    )(x, w)


def _mixer_kernel(sinkrow_ref, bias_cur_ref, bias_prev_ref, qa_ref, kp_ref, kc_ref, vp_ref, vc_ref,
                  qr_ref, kr_ref, vr_ref, gr_ref, anw_ref, rnw_ref, y_ref, state_ref, a_ref):
    n = pl.program_id(1)
    row = lax.broadcasted_iota(jnp.int32, (ATTN_BLOCK, ATTN_BLOCK), 0)
    col = lax.broadcasted_iota(jnp.int32, (ATTN_BLOCK, ATTN_BLOCK), 1)
    diff = row - col
    difff = diff.astype(F32)

    scale = HEAD_DIM_ATTN ** -0.5
    for j in range(N_KV_ATTN):
        klo, khi = j * HEAD_DIM_ATTN, (j + 1) * HEAD_DIM_ATTN
        heads = range(j * GQA_GROUP, (j + 1) * GQA_GROUP)
        q_rows = jnp.concatenate([qa_ref[:, h * HEAD_DIM_ATTN:(h + 1) * HEAD_DIM_ATTN] for h in heads], axis=0)
        q_rows = q_rows * scale
        sink = sinkrow_ref[j]
        s_cur = _dot_nt(kc_ref[:, klo:khi], q_rows) + bias_cur_ref[j]
        s_prev = _dot_nt(kp_ref[:, klo:khi], q_rows) + bias_prev_ref[j]
        s_prev = jnp.where(n > 0, s_prev, NEG_INF)
        m = jnp.maximum(jnp.max(s_cur, axis=0, keepdims=True), jnp.max(s_prev, axis=0, keepdims=True))
        m = jnp.maximum(m, sink)
        p_cur = jnp.exp(s_cur - m)
        p_prev = jnp.exp(s_prev - m)
        denom = (jnp.sum(p_cur, axis=0, keepdims=True) + jnp.sum(p_prev, axis=0, keepdims=True)
                 + jnp.exp(sink - m))
        v_cur_t = vc_ref[:, klo:khi].astype(F32).T.astype(BF16)
        v_prev_t = vp_ref[:, klo:khi].astype(F32).T.astype(BF16)
        o_t = (jnp.dot(v_cur_t, p_cur.astype(BF16), preferred_element_type=F32)
               + jnp.dot(v_prev_t, p_prev.astype(BF16), preferred_element_type=F32))
        o_t = o_t * (1.0 / denom)
        for g, h in enumerate(heads):
            a_ref[h * HEAD_DIM_ATTN:(h + 1) * HEAD_DIM_ATTN, :] = o_t[:, g * ATTN_BLOCK:(g + 1) * ATTN_BLOCK]
    a_t = a_ref[...]
    a_t = a_t * lax.rsqrt(jnp.mean(a_t * a_t, axis=0, keepdims=True) + LN_EPS)
    y_ref[:, :D_ATTN] = (a_t.T * anw_ref[...]).astype(y_ref.dtype)

    @pl.when(n == 0)
    def _():
        state_ref[...] = jnp.zeros_like(state_ref)

    pos = lax.broadcasted_iota(jnp.int32, (RET_CHUNK, 1), 0).astype(F32)
    kscale = HEAD_DIM_RET ** -0.5
    for h in range(N_RET):
        lo, hi = h * HEAD_DIM_RET, (h + 1) * HEAD_DIM_RET
        log_g = math.log1p(-(2.0 ** (-5.0 - h)))
        dmask = jnp.where(diff >= 0, jnp.exp(log_g * jnp.maximum(difff, 0.0)), 0.0)
        w_key = jnp.exp(log_g * (RET_CHUNK - 1.0 - pos))
        w_q = jnp.exp(log_g * (pos + 1.0))
        g_c = math.exp(log_g * RET_CHUNK)
        q = qr_ref[:, lo:hi]
        k = kr_ref[:, lo:hi]
        v = vr_ref[:, lo:hi]
        s = _dot_nt(q, k) * (dmask * kscale)
        inner = jnp.dot(s.astype(BF16), v, preferred_element_type=F32)
        state = state_ref[h]
        qw = (q.astype(F32) * w_q).astype(BF16)
        cross = jnp.dot(qw, state.astype(BF16), preferred_element_type=F32)
        kw_t = (k.astype(F32) * (w_key * kscale)).T.astype(BF16)
        state_ref[h] = g_c * state + jnp.dot(kw_t, v, preferred_element_type=F32)
        o = inner + cross
        mu = jnp.mean(o, axis=-1, keepdims=True)
        oc = o - mu
        var = jnp.mean(oc * oc, axis=-1, keepdims=True)
        o = oc * lax.rsqrt(var + LN_EPS) * rnw_ref[:, lo:hi]
        g = gr_ref[:, lo:hi].astype(F32)
        o = o * (g * jax.nn.sigmoid(g))
        y_ref[:, D_ATTN + lo:D_ATTN + hi] = o.astype(y_ref.dtype)


def _mixer(proj, sinks, attn_norm_w, ret_norm_w):
    nc = SEQ // ATTN_BLOCK
    blk = ATTN_BLOCK

    def wide(colblock):
        return pl.BlockSpec((blk, D_RET), lambda b, n: (b * nc + n, colblock))

    def kv_cur(colblock):
        return pl.BlockSpec((blk, D_KV), lambda b, n: (b * nc + n, colblock))

    def kv_prev(colblock):
        return pl.BlockSpec((blk, D_KV), lambda b, n: (b * nc + jnp.maximum(n - 1, 0), colblock))

    vec = pl.BlockSpec((1, D_RET), lambda b, n: (0, 0))
    gq = GQA_GROUP * blk

    def table(rows):
        return pl.BlockSpec((N_KV_ATTN, rows, gq), lambda b, n: (0, 0, 0))

    slopes = jnp.exp2(-8.0 * (jnp.arange(N_Q_ATTN, dtype=F32) + 1.0) / N_Q_ATTN).reshape(N_KV_ATTN, 1, GQA_GROUP, 1)
    key = jnp.arange(blk, dtype=F32).reshape(1, blk, 1, 1)
    query = jnp.arange(blk, dtype=F32).reshape(1, 1, 1, blk)
    dist_cur = jnp.broadcast_to(query - key, (N_KV_ATTN, blk, GQA_GROUP, blk))
    bias_cur = jnp.where(dist_cur >= 0, -slopes * dist_cur, NEG_INF).reshape(N_KV_ATTN, blk, gq)
    bias_prev = jnp.where(dist_cur < 0, -slopes * (dist_cur + blk), NEG_INF).reshape(N_KV_ATTN, blk, gq)
    sink_rows = jnp.repeat(sinks.astype(F32).reshape(N_KV_ATTN, 1, GQA_GROUP), blk, axis=2)
    return pl.pallas_call(
        _mixer_kernel,
        grid=(BATCH, nc),
        in_specs=[table(1), table(blk), table(blk),
                  wide(COL_QA), kv_prev(COL_KA), kv_cur(COL_KA), kv_prev(COL_VA), kv_cur(COL_VA),
                  wide(COL_QR), wide(COL_KR), wide(COL_VR), wide(COL_GR), vec, vec],
        out_specs=pl.BlockSpec((blk, D_MODEL), lambda b, n: (b * nc + n, 0)),
        out_shape=jax.ShapeDtypeStruct((TOKENS, D_MODEL), BF16),
        scratch_shapes=[pltpu.VMEM((N_RET, HEAD_DIM_RET, HEAD_DIM_RET), F32),
                        pltpu.VMEM((D_ATTN, blk), F32)],
        compiler_params=_cparams(("parallel", "arbitrary")),
        name="mixer",
    )(sink_rows, bias_cur, bias_prev, proj, proj, proj, proj, proj, proj, proj, proj, proj,
      attn_norm_w, ret_norm_w)


def _out_ln_kernel(y_ref, w_ref, x_ref, lnw_ref, lnb_ref, o_ref):
    h = jnp.dot(y_ref[...], w_ref[...], preferred_element_type=F32)
    o_ref[...] = _layer_norm(DN_ALPHA * x_ref[...] + h, lnw_ref[...], lnb_ref[...])


def _out_ln(y, w, x, lnw, lnb):
    tm = LN_TM
    row = pl.BlockSpec((tm, D_MODEL), lambda i: (i, 0))
    vec = pl.BlockSpec((1, D_MODEL), lambda i: (0, 0))
    return pl.pallas_call(
        _out_ln_kernel,
        grid=(TOKENS // tm,),
        in_specs=[row,
                  pl.BlockSpec((D_MODEL, D_MODEL), lambda i: (0, 0), pipeline_mode=pl.Buffered(1)),
                  row, vec, vec],
        out_specs=row,
        out_shape=jax.ShapeDtypeStruct((TOKENS, D_MODEL), F32),
        compiler_params=_cparams(("parallel",)),
        name="out_ln",
    )(y, w, x, lnw, lnb)


def _mem_attn_kernel(x_ref, wq_ref, k_ref, v_ref, o_ref):
    q = jnp.dot(x_ref[...].astype(BF16), wq_ref[...], preferred_element_type=F32)
    q = (q * (HEAD_DIM_MEM ** -0.5)).astype(BF16)
    for h in range(N_MEM_HEADS):
        lo, hi = h * HEAD_DIM_MEM, (h + 1) * HEAD_DIM_MEM
        s = _dot_nt(q[:, lo:hi], k_ref[:, lo:hi])
        m = jnp.max(s, axis=-1, keepdims=True)
        p = jnp.exp(s - m)
        denom = jnp.sum(p, axis=-1, keepdims=True)
        o = jnp.dot(p.astype(BF16), v_ref[:, lo:hi], preferred_element_type=F32)
        o_ref[:, lo:hi] = (o / denom).astype(o_ref.dtype)


def _mem_attn(x, wq, kv):
    tm = MEM_TM
    nt = SEQ // tm
    row = pl.BlockSpec((tm, D_MODEL), lambda b, i: (b * nt + i, 0))
    return pl.pallas_call(
        _mem_attn_kernel,
        grid=(BATCH, nt),
        in_specs=[row,
                  pl.BlockSpec((D_MODEL, D_MODEL), lambda b, i: (0, 0), pipeline_mode=pl.Buffered(1)),
                  pl.BlockSpec((N_MEM, D_MODEL), lambda b, i: (b, 0)),
                  pl.BlockSpec((N_MEM, D_MODEL), lambda b, i: (b, 1))],
        out_specs=row,
        out_shape=jax.ShapeDtypeStruct((TOKENS, D_MODEL), BF16),
        compiler_params=_cparams(("parallel", "parallel")),
        name="mem_attn",
    )(x, wq, kv, kv)


def _route_kernel(x_ref, wr_ref, br_ref, idx_ref, gate_ref, rank_ref, count_ref, carry_ref):
    tm = x_ref.shape[0]

    @pl.when(pl.program_id(0) == 0)
    def _():
        carry_ref[...] = jnp.zeros_like(carry_ref)

    x = x_ref[...]
    x_hi = x.astype(BF16)
    x_lo = (x - x_hi.astype(F32)).astype(BF16)
    part_hi = jnp.dot(x_hi, wr_ref[...], preferred_element_type=F32)
    part_lo = jnp.dot(x_lo, wr_ref[...], preferred_element_type=F32)
    logits = (part_hi[:, :LANES] + (part_hi[:, LANES:] + part_lo[:, :LANES]) + part_lo[:, LANES:]
              + br_ref[...])
    lane = lax.broadcasted_iota(jnp.int32, (tm, LANES), 1)
    lane_f = lane.astype(F32)
    work = logits
    sels, vals, idxs = [], [], []
    for _ in range(TOP_K):
        v = jnp.max(work, axis=-1, keepdims=True)
        i = jnp.min(jnp.where(work == v, lane_f, float(LANES)), axis=-1, keepdims=True)
        sel = lane_f == i
        work = jnp.where(sel, -jnp.inf, work)
        sels.append(sel)
        vals.append(v)
        idxs.append(i)
    exps = [jnp.exp(v - vals[0]) for v in vals]
    total = exps[0] + exps[1] + exps[2] + exps[3]

    onehot = jnp.zeros((tm, LANES), F32)
    for sel in sels:
        onehot = jnp.where(sel, 1.0, onehot)
    r = lax.broadcasted_iota(jnp.int32, (tm, tm), 0)
    c = lax.broadcasted_iota(jnp.int32, (tm, tm), 1)
    below = jnp.where(c < r, 1.0, 0.0).astype(BF16)
    prefix = jnp.dot(below, onehot.astype(BF16), preferred_element_type=F32)
    rankmat = prefix + carry_ref[...]
    carry_ref[...] = carry_ref[...] + jnp.sum(onehot, axis=0, keepdims=True)
    count_ref[...] = carry_ref[...]

    idx_out = jnp.zeros((tm, LANES), jnp.int32)
    gate_out = jnp.zeros((tm, LANES), F32)
    rank_out = jnp.zeros((tm, LANES), jnp.int32)
    for k in range(TOP_K):
        rk = jnp.sum(jnp.where(sels[k], rankmat, 0.0), axis=-1, keepdims=True).astype(jnp.int32)
        idx_out = jnp.where(lane == k, idxs[k].astype(jnp.int32), idx_out)
        gate_out = jnp.where(lane == k, exps[k] / total, gate_out)
        rank_out = jnp.where(lane == k, rk, rank_out)
    idx_ref[...] = idx_out
    gate_ref[...] = gate_out
    rank_ref[...] = rank_out


def _route(x, wr, br):
    tm = ROUTE_TM
    wide = pl.BlockSpec((tm, LANES), lambda i: (i, 0))
    return pl.pallas_call(
        _route_kernel,
        grid=(TOKENS // tm,),
        in_specs=[pl.BlockSpec((tm, D_MODEL), lambda i: (i, 0)),
                  pl.BlockSpec((D_MODEL, 2 * LANES), lambda i: (0, 0)),
                  pl.BlockSpec((1, LANES), lambda i: (0, 0))],
        out_specs=[wide, wide, wide, pl.BlockSpec((1, LANES), lambda i: (0, 0))],
        out_shape=[jax.ShapeDtypeStruct((TOKENS, LANES), jnp.int32),
                   jax.ShapeDtypeStruct((TOKENS, LANES), F32),
                   jax.ShapeDtypeStruct((TOKENS, LANES), jnp.int32),
                   jax.ShapeDtypeStruct((1, LANES), F32)],
        scratch_shapes=[pltpu.VMEM((1, LANES), F32)],
        compiler_params=_cparams(("arbitrary",)),
        name="route",
    )(x, wr, br)


HALF = D_MODEL // 2


def _pack_rows(x):
    return _pack_pair(x[:, :HALF], x[:, HALF:])


def _pack_pair(lo, hi):
    lo = lax.bitcast_convert_type(lo.astype(BF16).astype(F32), jnp.int32)
    hi = lax.bitcast_convert_type(hi.astype(BF16).astype(F32), jnp.int32)
    return hi | lax.shift_right_logical(lo, 16)


def _unpack_rows(w):
    lo = lax.bitcast_convert_type(lax.shift_left(w, 16), F32)
    hi = lax.bitcast_convert_type(w & jnp.int32(-65536), F32)
    return lo, hi


def _wait_row_copies(rows_ref, sem):
    pltpu.make_async_copy(rows_ref.at[pl.ds(0, ROW_TILE)], rows_ref.at[pl.ds(0, ROW_TILE)], sem).wait()


ROW_SUB = HALF // LANES
SUBLANES = 8
ROW_GROUPS = ROW_TILE // SUBLANES


FILL_SIZES = tuple(MOE_TM >> (b + 1) for b in range(MOE_TM.bit_length() - 1))


def _fill_copies(fill_start_ref, fill_len_ref, zero_ref, xs_ref, sem, act):
    def body(e, carry):
        pos = fill_start_ref[e]
        n = fill_len_ref[e]
        for size in FILL_SIZES:
            take = (n & size) != 0

            @pl.when(take)
            def _():
                cp = pltpu.make_async_copy(zero_ref.at[pl.ds(0, size)], xs_ref.at[pl.ds(pos, size)], sem)
                getattr(cp, act)()

            pos = pos + jnp.where(take, size, 0)
        return carry

    lax.fori_loop(0, N_EXPERTS, body, 0)

    def tail(j, carry):
        pos = fill_start_ref[N_EXPERTS] + j * FILL_SIZES[0]
        cp = pltpu.make_async_copy(zero_ref, xs_ref.at[pl.ds(pos, FILL_SIZES[0])], sem)
        getattr(cp, act)()
        return carry

    lax.fori_loop(0, fill_len_ref[N_EXPERTS], tail, 0)


def _scatter_kernel(fill_start_ref, fill_len_ref, dest_ref, x_ref, xs_ref, packed_ref, zero_ref, sem, fill_sem):
    i = pl.program_id(0)
    slot = i % 2

    @pl.when(i == 0)
    def _():
        zero_ref[...] = jnp.zeros_like(zero_ref)
        _fill_copies(fill_start_ref, fill_len_ref, zero_ref, xs_ref, fill_sem, "start")

    words = _pack_rows(x_ref[...])
    for g in range(ROW_GROUPS):
        for c in range(ROW_SUB):
            packed_ref[slot, g, c] = words[g * SUBLANES:(g + 1) * SUBLANES, c * LANES:(c + 1) * LANES]

    def body(g, carry):
        for s in range(SUBLANES):
            for k in range(TOP_K):
                d = dest_ref[0, 0, (g * SUBLANES + s) * TOP_K + k]
                pltpu.make_async_copy(packed_ref.at[slot, g, :, s, :], xs_ref.at[d],
                                      sem.at[slot]).start(priority=k % 2)
        return carry

    lax.fori_loop(0, ROW_GROUPS, body, 0)

    @pl.when(i > 0)
    def _():
        for k in range(TOP_K):
            _wait_row_copies(xs_ref, sem.at[1 - slot])

    @pl.when(i == pl.num_programs(0) - 1)
    def _():
        for k in range(TOP_K):
            _wait_row_copies(xs_ref, sem.at[slot])

    @pl.when(i == 0)
    def _():
        _fill_copies(fill_start_ref, fill_len_ref, zero_ref, xs_ref, fill_sem, "wait")


def _scatter_rows(x, dest, fill_start, fill_len):
    nt = TOKENS // ROW_TILE
    grid_spec = pltpu.PrefetchScalarGridSpec(
        num_scalar_prefetch=2,
        grid=(nt,),
        in_specs=[pl.BlockSpec((1, 1, ROW_TILE * TOP_K), lambda i, fs, fl: (i, 0, 0), memory_space=pltpu.SMEM),
                  pl.BlockSpec((ROW_TILE, D_MODEL), lambda i, fs, fl: (i, 0))],
        out_specs=pl.BlockSpec(memory_space=pl.ANY),
        scratch_shapes=[pltpu.VMEM((2, ROW_GROUPS, ROW_SUB, SUBLANES, LANES), jnp.int32),
                        pltpu.VMEM((FILL_SIZES[0], ROW_SUB, LANES), jnp.int32),
                        pltpu.SemaphoreType.DMA((2,)), pltpu.SemaphoreType.DMA(())],
    )
    return pl.pallas_call(
        _scatter_kernel,
        grid_spec=grid_spec,
        out_shape=jax.ShapeDtypeStruct((MOE_ROWS, ROW_SUB, LANES), jnp.int32),
        compiler_params=_cparams(("arbitrary",), has_side_effects=True),
        name="scatter_rows",
    )(fill_start, fill_len, dest.reshape(nt, 1, ROW_TILE * TOP_K), x)


MOE_NF = D_EXPERT // MOE_TF
MOE_ROW_STEP = 256


def _experts_kernel(ge_ref, gn_ref, de_ref, dn_ref, xs_ref, wg_ref, wu_ref, wd_ref, bgu_ref, bd_ref,
                    ys_ref, xb_ref, act_ref, hold_ref, words_ref, ow_ref, pend_ref, xsem, ysem):
    v = pl.program_id(0)
    f = pl.program_id(1)
    gn = gn_ref[v]
    dn = dn_ref[v]
    slot = v % 2
    half_nf = MOE_NF // 2

    sub_per_tile = MOE_TF // LANES

    def ys_copies(visit, tile):
        row0 = pl.multiple_of(visit * MOE_TM, MOE_TM)
        return [pltpu.make_async_copy(ow_ref.at[:, j * LANES:(j + 1) * LANES],
                                      ys_ref.at[pl.ds(row0, MOE_TM), tile * sub_per_tile + j, :], ysem)
                for j in range(sub_per_tile)]

    def emit_output_tile():
        for cp in ys_copies(v - 1, f - half_nf):
            cp.start()
        pend_ref[0] = 1

    def drain_output_tile():
        @pl.when(pend_ref[0] == 1)
        def _():
            for cp in ys_copies(0, 0):
                cp.wait()
            pend_ref[0] = 0

    @pl.when(jnp.logical_and(v == 0, f == 0))
    def _():
        hold_ref[...] = jnp.zeros_like(hold_ref)
        ow_ref[...] = jnp.zeros_like(ow_ref)
        pend_ref[0] = 0

    def xs_copies(visit, s):
        row0 = pl.multiple_of(visit * MOE_TM, MOE_TM)
        return [pltpu.make_async_copy(xs_ref.at[pl.ds(row0, MOE_TM), c, :],
                                      words_ref.at[s, :, c * LANES:(c + 1) * LANES], xsem.at[s])
                for c in range(ROW_SUB)]

    @pl.when(f == 0)
    def _():
        @pl.when(v == 0)
        def _():
            for cp in xs_copies(0, 0):
                cp.start()

        @pl.when(v < N_VISITS)
        def _():
            for cp in xs_copies(v, slot):
                cp.wait()

        @pl.when(v + 1 < N_VISITS)
        def _():
            for cp in xs_copies(v + 1, 1 - slot):
                cp.start()

    @pl.when(jnp.logical_and(gn > 0, f == 0))
    def _():
        r = lax.broadcasted_iota(jnp.int32, (MOE_TM, 1), 0)
        lo, hi = _unpack_rows(jnp.where(r < gn, words_ref[slot], 0))
        xb_ref[:, :HALF] = lo.astype(BF16)
        xb_ref[:, HALF:] = hi.astype(BF16)

    def gate_phase(rows):
        xb = xb_ref[:rows, :]
        e = pl.ds(ge_ref[v], 1)
        col = pl.multiple_of(f * MOE_TF, MOE_TF)
        gl = jnp.dot(xb, wg_ref[...].astype(BF16), preferred_element_type=F32) + bgu_ref[e, pl.ds(col, MOE_TF)]
        up = (jnp.dot(xb, wu_ref[...].astype(BF16), preferred_element_type=F32)
              + bgu_ref[e, pl.ds(D_EXPERT + col, MOE_TF)])
        gl = jnp.minimum(gl, SWIGLU_LIMIT)
        up = jnp.clip(up, -SWIGLU_LIMIT, SWIGLU_LIMIT)
        act = ((up + 1.0) * (gl * jax.nn.sigmoid(SWIGLU_ALPHA * gl))).astype(BF16)
        act_ref[slot, :rows, pl.ds(pl.multiple_of(f * MOE_TF, MOE_TF), MOE_TF)] = act

    def down_phase(rows):
        drain_output_tile()
        y = jnp.dot(act_ref[1 - slot, :rows, :], wd_ref[...].astype(BF16),
                    preferred_element_type=F32) + bd_ref[pl.ds(de_ref[v], 1),
                                                         pl.ds(pl.multiple_of(f * MOE_TF, MOE_TF), MOE_TF)]
        h = f % half_nf
        ow_ref[:rows, :] = _pack_pair(hold_ref[h, :rows, :], y)
        hold_ref[h, :rows, :] = y
        pl.when(f >= half_nf)(emit_output_tile)

    for rows in range(MOE_TM, 0, -MOE_ROW_STEP):
        lower = rows - MOE_ROW_STEP
        pl.when(jnp.logical_and(gn > lower, gn <= rows))(functools.partial(gate_phase, rows))
        pl.when(jnp.logical_and(dn > lower, dn <= rows))(functools.partial(down_phase, rows))

    @pl.when(jnp.logical_and(jnp.logical_and(dn == 0, v >= 1), f >= half_nf))
    def _():
        drain_output_tile()
        ow_ref[...] = jnp.zeros_like(ow_ref)
        emit_output_tile()

    @pl.when(jnp.logical_and(v == pl.num_programs(0) - 1, f == MOE_NF - 1))
    def _():
        drain_output_tile()


def _experts(xs, visit_expert, visit_nvalid, w_gate_up, b_gate_up, w_down, b_down):
    nf = MOE_NF
    gate_e = jnp.concatenate([visit_expert, visit_expert[-1:]])
    gate_n = jnp.concatenate([visit_nvalid, jnp.zeros((1,), jnp.int32)])
    down_e = jnp.concatenate([visit_expert[:1], visit_expert])
    down_n = jnp.concatenate([jnp.zeros((1,), jnp.int32), visit_nvalid])

    def tile(f, n):
        return jnp.where(n > 0, f, nf - 1)

    grid_spec = pltpu.PrefetchScalarGridSpec(
        num_scalar_prefetch=4,
        grid=(N_VISITS + 1, nf),
        in_specs=[
            pl.BlockSpec(memory_space=pl.ANY),
            pl.BlockSpec((None, D_MODEL, MOE_TF), lambda v, f, ge, gn, de, dn: (ge[v], 0, tile(f, gn[v]))),
            pl.BlockSpec((None, D_MODEL, MOE_TF), lambda v, f, ge, gn, de, dn: (ge[v], 0, nf + tile(f, gn[v]))),
            pl.BlockSpec((None, D_EXPERT, MOE_TF), lambda v, f, ge, gn, de, dn: (de[v], 0, tile(f, dn[v]))),
            pl.BlockSpec((N_EXPERTS, 2 * D_EXPERT), lambda v, f, ge, gn, de, dn: (0, 0)),
            pl.BlockSpec((N_EXPERTS, D_MODEL), lambda v, f, ge, gn, de, dn: (0, 0)),
        ],
        out_specs=pl.BlockSpec(memory_space=pl.ANY),
        scratch_shapes=[pltpu.VMEM((MOE_TM, D_MODEL), BF16),
                        pltpu.VMEM((2, MOE_TM, D_EXPERT), BF16),
                        pltpu.VMEM((nf // 2, MOE_TM, MOE_TF), F32),
                        pltpu.VMEM((2, MOE_TM, HALF), jnp.int32),
                        pltpu.VMEM((MOE_TM, MOE_TF), jnp.int32),
                        pltpu.SMEM((1,), jnp.int32),
                        pltpu.SemaphoreType.DMA((2,)),
                        pltpu.SemaphoreType.DMA(())],
    )
    return pl.pallas_call(
        _experts_kernel,
        grid_spec=grid_spec,
        out_shape=jax.ShapeDtypeStruct((MOE_ROWS, ROW_SUB, LANES), jnp.int32),
        compiler_params=_cparams(("arbitrary", "arbitrary")),
        name="experts",
    )(gate_e, gate_n, down_e, down_n, xs, w_gate_up, w_gate_up, w_down, b_gate_up, b_down)


def _combine_kernel(dest_ref, dest_next_ref, ys_ref, gate_ref, x_ref, lnw_ref, lnb_ref, o_ref,
                    buf_ref, z_ref, sem):
    i = pl.program_id(0)
    slot = i % 2

    def start_group(idx_ref, s, g):
        for t in range(SUBLANES):
            for k in range(TOP_K):
                d = idx_ref[0, 0, (g * SUBLANES + t) * TOP_K + k]
                pltpu.make_async_copy(ys_ref.at[d], buf_ref.at[s, k, g, :, t, :],
                                      sem.at[s]).start(priority=k % 2)

    @pl.when(i == 0)
    def _():
        def body(g, carry):
            start_group(dest_ref, slot, g)
            return carry

        lax.fori_loop(0, ROW_GROUPS, body, 0)

    for k in range(TOP_K):
        _wait_row_copies(ys_ref, sem.at[slot])

    def group(g, carry, prefetch):
        if prefetch:
            start_group(dest_next_ref, 1 - slot, g)
        rows = pl.ds(pl.multiple_of(g * SUBLANES, SUBLANES), SUBLANES)
        gates = gate_ref[rows, :]
        gate_k = [jnp.broadcast_to(gates[:, k:k + 1], (SUBLANES, LANES)) for k in range(TOP_K)]
        for c in range(ROW_SUB):
            lo_cols = slice(c * LANES, (c + 1) * LANES)
            hi_cols = slice(HALF + c * LANES, HALF + (c + 1) * LANES)
            z_lo = DN_ALPHA * x_ref[rows, lo_cols]
            z_hi = DN_ALPHA * x_ref[rows, hi_cols]
            for k in range(TOP_K):
                lo, hi = _unpack_rows(buf_ref[slot, k, g, c])
                z_lo = z_lo + gate_k[k] * lo
                z_hi = z_hi + gate_k[k] * hi
            z_ref[rows, lo_cols] = z_lo
            z_ref[rows, hi_cols] = z_hi
        return carry

    last = i + 1 == pl.num_programs(0)

    @pl.when(jnp.logical_not(last))
    def _():
        lax.fori_loop(0, ROW_GROUPS, functools.partial(group, prefetch=True), 0)

    @pl.when(last)
    def _():
        lax.fori_loop(0, ROW_GROUPS, functools.partial(group, prefetch=False), 0)

    o_ref[...] = _layer_norm(z_ref[...], lnw_ref[...], lnb_ref[...])


def _combine(ys, dest, gates, x, lnw, lnb):
    nt = TOKENS // ROW_TILE
    row = pl.BlockSpec((ROW_TILE, D_MODEL), lambda i: (i, 0))
    vec = pl.BlockSpec((1, D_MODEL), lambda i: (0, 0))
    dest_tiles = dest.reshape(nt, 1, ROW_TILE * TOP_K)
    return pl.pallas_call(
        _combine_kernel,
        grid=(nt,),
        in_specs=[pl.BlockSpec((1, 1, ROW_TILE * TOP_K), lambda i: (i, 0, 0), memory_space=pltpu.SMEM),
                  pl.BlockSpec((1, 1, ROW_TILE * TOP_K), lambda i: (jnp.minimum(i + 1, nt - 1), 0, 0),
                               memory_space=pltpu.SMEM),
                  pl.BlockSpec(memory_space=pl.ANY),
                  pl.BlockSpec((ROW_TILE, LANES), lambda i: (i, 0)),
                  row, vec, vec],
        out_specs=row,
        out_shape=jax.ShapeDtypeStruct((TOKENS, D_MODEL), F32),
        scratch_shapes=[pltpu.VMEM((2, TOP_K, ROW_GROUPS, ROW_SUB, SUBLANES, LANES), jnp.int32),
                        pltpu.VMEM((ROW_TILE, D_MODEL), F32), pltpu.SemaphoreType.DMA((2,))],
        compiler_params=_cparams(("arbitrary",)),
        name="combine",
    )(dest_tiles, dest_tiles, ys, gates, x, lnw, lnb)


def _layer(x, mem, w_in, attn_sinks, attn_norm_w, ret_norm_w, w_mix_out, ln_mix_w, ln_mix_b,
           w_mem_q, w_mem_kv, w_mem_out, ln_mem_w, ln_mem_b, w_router, b_router,
           w_gate_up, b_gate_up, w_down, b_down, ln_moe_w, ln_moe_b):
    o_k, o_v, o_qr = D_ATTN, D_ATTN + D_KV, D_ATTN + 2 * D_KV
    w_in_p = jnp.concatenate([w_in[:, :o_k], w_in[:, o_qr:], w_in[:, o_k:o_v], w_in[:, o_v:o_qr]],
                             axis=1).astype(BF16)
    proj = _project(x, w_in_p, PROJ_TM, PROJ_TN)
    y = _mixer(proj, attn_sinks.astype(F32), attn_norm_w.reshape(1, D_ATTN), ret_norm_w.reshape(1, D_RET))
    x1 = _out_ln(y, w_mix_out.astype(BF16), x, ln_mix_w.reshape(1, D_MODEL), ln_mix_b.reshape(1, D_MODEL))

    kv = _project(mem.reshape(BATCH * N_MEM, D_MODEL), w_mem_kv.astype(BF16), BATCH * N_MEM, D_MODEL)
    o = _mem_attn(x1, w_mem_q.astype(BF16), kv)
    x2 = _out_ln(o, w_mem_out.astype(BF16), x1, ln_mem_w.reshape(1, D_MODEL), ln_mem_b.reshape(1, D_MODEL))

    wr = jnp.pad(w_router, ((0, 0), (0, LANES - N_EXPERTS)))
    wr_hi = wr.astype(BF16)
    wr = jnp.concatenate([wr_hi, (wr - wr_hi.astype(F32)).astype(BF16)], axis=1)
    br =jnp.pad(b_router.reshape(1, N_EXPERTS), ((0, 0), (0, LANES - N_EXPERTS)), constant_values=NEG_INF)
    idx, gates, rank, counts = _route(x2, wr, br)

    counts = counts[0, :N_EXPERTS].astype(jnp.int32)
    padded = ((counts + MOE_TM - 1) // MOE_TM) * MOE_TM
    eid = jnp.arange(N_EXPERTS, dtype=jnp.int32)
    pad_end = jnp.sum(jnp.where(eid[None, :] <= eid[:, None], padded[None, :], 0), axis=1)
    pad_start = pad_end - padded
    chosen = idx[:, :TOP_K, None] == eid[None, None, :]
    dest = jnp.sum(jnp.where(chosen, pad_start[None, None, :], 0), axis=-1) + rank[:, :TOP_K]
    visit_row = jnp.arange(N_VISITS, dtype=jnp.int32) * MOE_TM
    visit_expert = jnp.minimum(jnp.sum((pad_end[None, :] <= visit_row[:, None]).astype(jnp.int32), axis=1),
                               N_EXPERTS - 1)
    visit_nvalid = jnp.clip(counts[visit_expert] - (visit_row - pad_start[visit_expert]),
                            0, MOE_TM).astype(jnp.int32)

    fill_start = jnp.concatenate([pad_start + counts, pad_end[-1:]]).astype(jnp.int32)
    fill_len = jnp.concatenate([padded - counts, (MOE_ROWS - pad_end[-1:]) // FILL_SIZES[0]]).astype(jnp.int32)
    xs = _scatter_rows(x2, dest, fill_start, fill_len)
    ys = _experts(xs, visit_expert, visit_nvalid,
                  w_gate_up, b_gate_up, w_down, b_down)
    return _combine(ys, dest, gates, x2, ln_moe_w.reshape(1, D_MODEL), ln_moe_b.reshape(1, D_MODEL))


def kernel(x, mem, w_in, attn_sinks, attn_norm_w, ret_norm_w, w_mix_out, ln_mix_w, ln_mix_b, w_mem_q, w_mem_kv, w_mem_out, ln_mem_w, ln_mem_b, w_router, b_router, w_gate_up, b_gate_up, w_down, b_down, ln_moe_w, ln_moe_b):
    xt = x.reshape(TOKENS, D_MODEL)
    out = _layer(xt, mem, w_in[0], attn_sinks[0], attn_norm_w[0], ret_norm_w[0], w_mix_out[0],
                 ln_mix_w[0], ln_mix_b[0], w_mem_q[0], w_mem_kv[0], w_mem_out[0], ln_mem_w[0], ln_mem_b[0],
                 w_router[0], b_router[0], w_gate_up[0], b_gate_up[0], w_down[0], b_down[0],
                 ln_moe_w[0], ln_moe_b[0])
    return out.reshape(BATCH, SEQ, D_MODEL)
```

```python
import functools
import math

import jax
import jax.numpy as jnp
from jax import lax
from jax.experimental import pallas as pl
from jax.experimental.pallas import tpu as pltpu

F32 = jnp.float32
BF16 = jnp.bfloat16

D_MODEL = 2048
BATCH = 2
SEQ = 16384
TOKENS = BATCH * SEQ

HEAD_DIM_ATTN = 64
N_Q_ATTN = 16
N_KV_ATTN = 2
GQA_GROUP = N_Q_ATTN // N_KV_ATTN
ATTN_BLOCK = 128
D_ATTN = N_Q_ATTN * HEAD_DIM_ATTN
D_KV = N_KV_ATTN * HEAD_DIM_ATTN
N_RET = 8
HEAD_DIM_RET = 128
RET_CHUNK = 128
D_RET = N_RET * HEAD_DIM_RET
D_IN = D_ATTN + 2 * D_KV + 4 * D_RET
N_MEM = 256
N_MEM_HEADS = 4
HEAD_DIM_MEM = D_MODEL // N_MEM_HEADS
N_EXPERTS = 32
TOP_K = 4
D_EXPERT = D_MODEL
SWIGLU_LIMIT = 7.0
SWIGLU_ALPHA = 1.702
LN_EPS = 1e-5
DN_ALPHA = 2.0 ** 0.25
NEG_INF = -1e30

LANES = 128
VMEM_LIMIT = 58 * 1024 * 1024

PROJ_TM = 1024
PROJ_TN = 1792
LN_TM = 512
MEM_TM = 512
ROUTE_TM = 512
ROW_TILE = 256
MOE_TM = 1024
MOE_TF = 512
N_VISITS = (TOKENS * TOP_K + N_EXPERTS * (MOE_TM - 1) + MOE_TM - 1) // MOE_TM
MOE_ROWS = N_VISITS * MOE_TM

COL_QA, COL_QR, COL_KR, COL_VR, COL_GR = 0, 1, 2, 3, 4
COL_KA = 5 * D_RET // D_KV
COL_VA = COL_KA + 1


def _cparams(sem, **kw):
    return pltpu.CompilerParams(dimension_semantics=sem, vmem_limit_bytes=VMEM_LIMIT, **kw)


def _layer_norm(z, w, b):
    mu = jnp.mean(z, axis=-1, keepdims=True)
    zc = z - mu
    var = jnp.mean(zc * zc, axis=-1, keepdims=True)
    return zc * lax.rsqrt(var + LN_EPS) * w + b


def _dot_nt(a, b):
    return lax.dot_general(a, b, (((1,), (1,)), ((), ())), preferred_element_type=F32)


def _proj_kernel(x_ref, w_ref, o_ref, xb_ref):
    @pl.when(pl.program_id(1) == 0)
    def _():
        xb_ref[...] = x_ref[...].astype(BF16)

    o_ref[...] = jnp.dot(xb_ref[...], w_ref[...], preferred_element_type=F32).astype(o_ref.dtype)


def _project(x, w, tm, tn):
    m, k = x.shape
    n = w.shape[1]
    return pl.pallas_call(
        _proj_kernel,
        grid=(m // tm, n // tn),
        in_specs=[pl.BlockSpec((tm, k), lambda i, j: (i, 0)),
                  pl.BlockSpec((k, tn), lambda i, j: (0, j))],
        out_specs=pl.BlockSpec((tm, tn), lambda i, j: (i, j)),
        out_shape=jax.ShapeDtypeStruct((m, n), BF16),
        scratch_shapes=[pltpu.VMEM((tm, k), BF16)],
        compiler_params=_cparams(("parallel", "arbitrary")),
        name="project",
    )(x, w)


def _mixer_kernel(sinkrow_ref, bias_cur_ref, bias_prev_ref, qa_ref, kp_ref, kc_ref, vp_ref, vc_ref,
                  qr_ref, kr_ref, vr_ref, gr_ref, anw_ref, rnw_ref, y_ref, state_ref, a_ref):
    n = pl.program_id(1)
    row = lax.broadcasted_iota(jnp.int32, (ATTN_BLOCK, ATTN_BLOCK), 0)
    col = lax.broadcasted_iota(jnp.int32, (ATTN_BLOCK, ATTN_BLOCK), 1)
    diff = row - col
    difff = diff.astype(F32)

    scale = HEAD_DIM_ATTN ** -0.5
    for j in range(N_KV_ATTN):
        klo, khi = j * HEAD_DIM_ATTN, (j + 1) * HEAD_DIM_ATTN
        heads = range(j * GQA_GROUP, (j + 1) * GQA_GROUP)
        q_rows = jnp.concatenate([qa_ref[:, h * HEAD_DIM_ATTN:(h + 1) * HEAD_DIM_ATTN] for h in heads], axis=0)
        q_rows = q_rows * scale
        sink = sinkrow_ref[j]
        s_cur = _dot_nt(kc_ref[:, klo:khi], q_rows) + bias_cur_ref[j]
        s_prev = _dot_nt(kp_ref[:, klo:khi], q_rows) + bias_prev_ref[j]
        s_prev = jnp.where(n > 0, s_prev, NEG_INF)
        m = jnp.maximum(jnp.max(s_cur, axis=0, keepdims=True), jnp.max(s_prev, axis=0, keepdims=True))
        m = jnp.maximum(m, sink)
        p_cur = jnp.exp(s_cur - m)
        p_prev = jnp.exp(s_prev - m)
        denom = (jnp.sum(p_cur, axis=0, keepdims=True) + jnp.sum(p_prev, axis=0, keepdims=True)
                 + jnp.exp(sink - m))
        v_cur_t = vc_ref[:, klo:khi].astype(F32).T.astype(BF16)
        v_prev_t = vp_ref[:, klo:khi].astype(F32).T.astype(BF16)
        o_t = (jnp.dot(v_cur_t, p_cur.astype(BF16), preferred_element_type=F32)
               + jnp.dot(v_prev_t, p_prev.astype(BF16), preferred_element_type=F32))
        o_t = o_t * (1.0 / denom)
        for g, h in enumerate(heads):
            a_ref[h * HEAD_DIM_ATTN:(h + 1) * HEAD_DIM_ATTN, :] = o_t[:, g * ATTN_BLOCK:(g + 1) * ATTN_BLOCK]
    a_t = a_ref[...]
    a_t = a_t * lax.rsqrt(jnp.mean(a_t * a_t, axis=0, keepdims=True) + LN_EPS)
    y_ref[:, :D_ATTN] = (a_t.T * anw_ref[...]).astype(y_ref.dtype)

    @pl.when(n == 0)
    def _():
        state_ref[...] = jnp.zeros_like(state_ref)

    pos = lax.broadcasted_iota(jnp.int32, (RET_CHUNK, 1), 0).astype(F32)
    kscale = HEAD_DIM_RET ** -0.5
    for h in range(N_RET):
        lo, hi = h * HEAD_DIM_RET, (h + 1) * HEAD_DIM_RET
        log_g = math.log1p(-(2.0 ** (-5.0 - h)))
        dmask = jnp.where(diff >= 0, jnp.exp(log_g * jnp.maximum(difff, 0.0)), 0.0)
        w_key = jnp.exp(log_g * (RET_CHUNK - 1.0 - pos))
        w_q = jnp.exp(log_g * (pos + 1.0))
        g_c = math.exp(log_g * RET_CHUNK)
        q = qr_ref[:, lo:hi]
        k = kr_ref[:, lo:hi]
        v = vr_ref[:, lo:hi]
        s = _dot_nt(q, k) * (dmask * kscale)
        inner = jnp.dot(s.astype(BF16), v, preferred_element_type=F32)
        state = state_ref[h]
        qw = (q.astype(F32) * w_q).astype(BF16)
        cross = jnp.dot(qw, state.astype(BF16), preferred_element_type=F32)
        kw_t = (k.astype(F32) * (w_key * kscale)).T.astype(BF16)
        state_ref[h] = g_c * state + jnp.dot(kw_t, v, preferred_element_type=F32)
        o = inner + cross
        mu = jnp.mean(o, axis=-1, keepdims=True)
        oc = o - mu
        var = jnp.mean(oc * oc, axis=-1, keepdims=True)
        o = oc * lax.rsqrt(var + LN_EPS) * rnw_ref[:, lo:hi]
        g = gr_ref[:, lo:hi].astype(F32)
        o = o * (g * jax.nn.sigmoid(g))
        y_ref[:, D_ATTN + lo:D_ATTN + hi] = o.astype(y_ref.dtype)


def _mixer(proj, sinks, attn_norm_w, ret_norm_w):
    nc = SEQ // ATTN_BLOCK
    blk = ATTN_BLOCK

    def wide(colblock):
        return pl.BlockSpec((blk, D_RET), lambda b, n: (b * nc + n, colblock))

    def kv_cur(colblock):
        return pl.BlockSpec((blk, D_KV), lambda b, n: (b * nc + n, colblock))

    def kv_prev(colblock):
        return pl.BlockSpec((blk, D_KV), lambda b, n: (b * nc + jnp.maximum(n - 1, 0), colblock))

    vec = pl.BlockSpec((1, D_RET), lambda b, n: (0, 0))
    gq = GQA_GROUP * blk

    def table(rows):
        return pl.BlockSpec((N_KV_ATTN, rows, gq), lambda b, n: (0, 0, 0))

    slopes = jnp.exp2(-8.0 * (jnp.arange(N_Q_ATTN, dtype=F32) + 1.0) / N_Q_ATTN).reshape(N_KV_ATTN, 1, GQA_GROUP, 1)
    key = jnp.arange(blk, dtype=F32).reshape(1, blk, 1, 1)
    query = jnp.arange(blk, dtype=F32).reshape(1, 1, 1, blk)
    dist_cur = jnp.broadcast_to(query - key, (N_KV_ATTN, blk, GQA_GROUP, blk))
    bias_cur = jnp.where(dist_cur >= 0, -slopes * dist_cur, NEG_INF).reshape(N_KV_ATTN, blk, gq)
    bias_prev = jnp.where(dist_cur < 0, -slopes * (dist_cur + blk), NEG_INF).reshape(N_KV_ATTN, blk, gq)
    sink_rows = jnp.repeat(sinks.astype(F32).reshape(N_KV_ATTN, 1, GQA_GROUP), blk, axis=2)
    return pl.pallas_call(
        _mixer_kernel,
        grid=(BATCH, nc),
        in_specs=[table(1), table(blk), table(blk),
                  wide(COL_QA), kv_prev(COL_KA), kv_cur(COL_KA), kv_prev(COL_VA), kv_cur(COL_VA),
                  wide(COL_QR), wide(COL_KR), wide(COL_VR), wide(COL_GR), vec, vec],
        out_specs=pl.BlockSpec((blk, D_MODEL), lambda b, n: (b * nc + n, 0)),
        out_shape=jax.ShapeDtypeStruct((TOKENS, D_MODEL), BF16),
        scratch_shapes=[pltpu.VMEM((N_RET, HEAD_DIM_RET, HEAD_DIM_RET), F32),
                        pltpu.VMEM((D_ATTN, blk), F32)],
        compiler_params=_cparams(("parallel", "arbitrary")),
        name="mixer",
    )(sink_rows, bias_cur, bias_prev, proj, proj, proj, proj, proj, proj, proj, proj, proj,
      attn_norm_w, ret_norm_w)


def _out_ln_kernel(y_ref, w_ref, x_ref, lnw_ref, lnb_ref, o_ref):
    h = jnp.dot(y_ref[...], w_ref[...], preferred_element_type=F32)
    o_ref[...] = _layer_norm(DN_ALPHA * x_ref[...] + h, lnw_ref[...], lnb_ref[...])


def _out_ln(y, w, x, lnw, lnb):
    tm = LN_TM
    row = pl.BlockSpec((tm, D_MODEL), lambda i: (i, 0))
    vec = pl.BlockSpec((1, D_MODEL), lambda i: (0, 0))
    return pl.pallas_call(
        _out_ln_kernel,
        grid=(TOKENS // tm,),
        in_specs=[row,
                  pl.BlockSpec((D_MODEL, D_MODEL), lambda i: (0, 0), pipeline_mode=pl.Buffered(1)),
                  row, vec, vec],
        out_specs=row,
        out_shape=jax.ShapeDtypeStruct((TOKENS, D_MODEL), F32),
        compiler_params=_cparams(("parallel",)),
        name="out_ln",
    )(y, w, x, lnw, lnb)


def _mem_attn_kernel(x_ref, wq_ref, k_ref, v_ref, o_ref):
    q = jnp.dot(x_ref[...].astype(BF16), wq_ref[...], preferred_element_type=F32)
    q = (q * (HEAD_DIM_MEM ** -0.5)).astype(BF16)
    for h in range(N_MEM_HEADS):
        lo, hi = h * HEAD_DIM_MEM, (h + 1) * HEAD_DIM_MEM
        s = _dot_nt(q[:, lo:hi], k_ref[:, lo:hi])
        m = jnp.max(s, axis=-1, keepdims=True)
        p = jnp.exp(s - m)
        denom = jnp.sum(p, axis=-1, keepdims=True)
        o = jnp.dot(p.astype(BF16), v_ref[:, lo:hi], preferred_element_type=F32)
        o_ref[:, lo:hi] = (o / denom).astype(o_ref.dtype)


def _mem_attn(x, wq, kv):
    tm = MEM_TM
    nt = SEQ // tm
    row = pl.BlockSpec((tm, D_MODEL), lambda b, i: (b * nt + i, 0))
    return pl.pallas_call(
        _mem_attn_kernel,
        grid=(BATCH, nt),
        in_specs=[row,
                  pl.BlockSpec((D_MODEL, D_MODEL), lambda b, i: (0, 0), pipeline_mode=pl.Buffered(1)),
                  pl.BlockSpec((N_MEM, D_MODEL), lambda b, i: (b, 0)),
                  pl.BlockSpec((N_MEM, D_MODEL), lambda b, i: (b, 1))],
        out_specs=row,
        out_shape=jax.ShapeDtypeStruct((TOKENS, D_MODEL), BF16),
        compiler_params=_cparams(("parallel", "parallel")),
        name="mem_attn",
    )(x, wq, kv, kv)


def _route_kernel(x_ref, wr_ref, br_ref, idx_ref, gate_ref, rank_ref, count_ref, carry_ref):
    tm = x_ref.shape[0]

    @pl.when(pl.program_id(0) == 0)
    def _():
        carry_ref[...] = jnp.zeros_like(carry_ref)

    x = x_ref[...]
    x_hi = x.astype(BF16)
    x_lo = (x - x_hi.astype(F32)).astype(BF16)
    part_hi = jnp.dot(x_hi, wr_ref[...], preferred_element_type=F32)
    part_lo = jnp.dot(x_lo, wr_ref[...], preferred_element_type=F32)
    logits = (part_hi[:, :LANES] + (part_hi[:, LANES:] + part_lo[:, :LANES]) + part_lo[:, LANES:]
              + br_ref[...])
    lane = lax.broadcasted_iota(jnp.int32, (tm, LANES), 1)
    lane_f = lane.astype(F32)
    work = logits
    sels, vals, idxs = [], [], []
    for _ in range(TOP_K):
        v = jnp.max(work, axis=-1, keepdims=True)
        i = jnp.min(jnp.where(work == v, lane_f, float(LANES)), axis=-1, keepdims=True)
        sel = lane_f == i
        work = jnp.where(sel, -jnp.inf, work)
        sels.append(sel)
        vals.append(v)
        idxs.append(i)
    exps = [jnp.exp(v - vals[0]) for v in vals]
    total = exps[0] + exps[1] + exps[2] + exps[3]

    onehot = jnp.zeros((tm, LANES), F32)
    for sel in sels:
        onehot = jnp.where(sel, 1.0, onehot)
    r = lax.broadcasted_iota(jnp.int32, (tm, tm), 0)
    c = lax.broadcasted_iota(jnp.int32, (tm, tm), 1)
    below = jnp.where(c < r, 1.0, 0.0).astype(BF16)
    prefix = jnp.dot(below, onehot.astype(BF16), preferred_element_type=F32)
    rankmat = prefix + carry_ref[...]
    carry_ref[...] = carry_ref[...] + jnp.sum(onehot, axis=0, keepdims=True)
    count_ref[...] = carry_ref[...]

    idx_out = jnp.zeros((tm, LANES), jnp.int32)
    gate_out = jnp.zeros((tm, LANES), F32)
    rank_out = jnp.zeros((tm, LANES), jnp.int32)
    for k in range(TOP_K):
        rk = jnp.sum(jnp.where(sels[k], rankmat, 0.0), axis=-1, keepdims=True).astype(jnp.int32)
        idx_out = jnp.where(lane == k, idxs[k].astype(jnp.int32), idx_out)
        gate_out = jnp.where(lane == k, exps[k] / total, gate_out)
        rank_out = jnp.where(lane == k, rk, rank_out)
    idx_ref[...] = idx_out
    gate_ref[...] = gate_out
    rank_ref[...] = rank_out


def _route(x, wr, br):
    tm = ROUTE_TM
    wide = pl.BlockSpec((tm, LANES), lambda i: (i, 0))
    return pl.pallas_call(
        _route_kernel,
        grid=(TOKENS // tm,),
        in_specs=[pl.BlockSpec((tm, D_MODEL), lambda i: (i, 0)),
                  pl.BlockSpec((D_MODEL, 2 * LANES), lambda i: (0, 0)),
                  pl.BlockSpec((1, LANES), lambda i: (0, 0))],
        out_specs=[wide, wide, wide, pl.BlockSpec((1, LANES), lambda i: (0, 0))],
        out_shape=[jax.ShapeDtypeStruct((TOKENS, LANES), jnp.int32),
                   jax.ShapeDtypeStruct((TOKENS, LANES), F32),
                   jax.ShapeDtypeStruct((TOKENS, LANES), jnp.int32),
                   jax.ShapeDtypeStruct((1, LANES), F32)],
        scratch_shapes=[pltpu.VMEM((1, LANES), F32)],
        compiler_params=_cparams(("arbitrary",)),
        name="route",
    )(x, wr, br)


HALF = D_MODEL // 2


def _pack_rows(x):
    return _pack_pair(x[:, :HALF], x[:, HALF:])


def _pack_pair(lo, hi):
    lo = lax.bitcast_convert_type(lo.astype(BF16).astype(F32), jnp.int32)
    hi = lax.bitcast_convert_type(hi.astype(BF16).astype(F32), jnp.int32)
    return hi | lax.shift_right_logical(lo, 16)


def _unpack_rows(w):
    lo = lax.bitcast_convert_type(lax.shift_left(w, 16), F32)
    hi = lax.bitcast_convert_type(w & jnp.int32(-65536), F32)
    return lo, hi


def _wait_row_copies(rows_ref, sem):
    pltpu.make_async_copy(rows_ref.at[pl.ds(0, ROW_TILE)], rows_ref.at[pl.ds(0, ROW_TILE)], sem).wait()


ROW_SUB = HALF // LANES
SUBLANES = 8
ROW_GROUPS = ROW_TILE // SUBLANES


FILL_SIZES = tuple(MOE_TM >> (b + 1) for b in range(MOE_TM.bit_length() - 1))


def _fill_copies(fill_start_ref, fill_len_ref, zero_ref, xs_ref, sem, act):
    def body(e, carry):
        pos = fill_start_ref[e]
        n = fill_len_ref[e]
        for size in FILL_SIZES:
            take = (n & size) != 0

            @pl.when(take)
            def _():
                cp = pltpu.make_async_copy(zero_ref.at[pl.ds(0, size)], xs_ref.at[pl.ds(pos, size)], sem)
                getattr(cp, act)()

            pos = pos + jnp.where(take, size, 0)
        return carry

    lax.fori_loop(0, N_EXPERTS, body, 0)

    def tail(j, carry):
        pos = fill_start_ref[N_EXPERTS] + j * FILL_SIZES[0]
        cp = pltpu.make_async_copy(zero_ref, xs_ref.at[pl.ds(pos, FILL_SIZES[0])], sem)
        getattr(cp, act)()
        return carry

    lax.fori_loop(0, fill_len_ref[N_EXPERTS], tail, 0)


def _scatter_kernel(fill_start_ref, fill_len_ref, dest_ref, x_ref, xs_ref, packed_ref, zero_ref, sem, fill_sem):
    i = pl.program_id(0)
    slot = i % 2

    @pl.when(i == 0)
    def _():
        zero_ref[...] = jnp.zeros_like(zero_ref)
        _fill_copies(fill_start_ref, fill_len_ref, zero_ref, xs_ref, fill_sem, "start")

    words = _pack_rows(x_ref[...])
    for g in range(ROW_GROUPS):
        for c in range(ROW_SUB):
            packed_ref[slot, g, c] = words[g * SUBLANES:(g + 1) * SUBLANES, c * LANES:(c + 1) * LANES]

    def body(g, carry):
        for s in range(SUBLANES):
            for k in range(TOP_K):
                d = dest_ref[0, 0, (g * SUBLANES + s) * TOP_K + k]
                pltpu.make_async_copy(packed_ref.at[slot, g, :, s, :], xs_ref.at[d],
                                      sem.at[slot]).start(priority=k % 2)
        return carry

    lax.fori_loop(0, ROW_GROUPS, body, 0)

    @pl.when(i > 0)
    def _():
        for k in range(TOP_K):
            _wait_row_copies(xs_ref, sem.at[1 - slot])

    @pl.when(i == pl.num_programs(0) - 1)
    def _():
        for k in range(TOP_K):
            _wait_row_copies(xs_ref, sem.at[slot])

    @pl.when(i == 0)
    def _():
        _fill_copies(fill_start_ref, fill_len_ref, zero_ref, xs_ref, fill_sem, "wait")


def _scatter_rows(x, dest, fill_start, fill_len):
    nt = TOKENS // ROW_TILE
    grid_spec = pltpu.PrefetchScalarGridSpec(
        num_scalar_prefetch=2,
        grid=(nt,),
        in_specs=[pl.BlockSpec((1, 1, ROW_TILE * TOP_K), lambda i, fs, fl: (i, 0, 0), memory_space=pltpu.SMEM),
                  pl.BlockSpec((ROW_TILE, D_MODEL), lambda i, fs, fl: (i, 0))],
        out_specs=pl.BlockSpec(memory_space=pl.ANY),
        scratch_shapes=[pltpu.VMEM((2, ROW_GROUPS, ROW_SUB, SUBLANES, LANES), jnp.int32),
                        pltpu.VMEM((FILL_SIZES[0], ROW_SUB, LANES), jnp.int32),
                        pltpu.SemaphoreType.DMA((2,)), pltpu.SemaphoreType.DMA(())],
    )
    return pl.pallas_call(
        _scatter_kernel,
        grid_spec=grid_spec,
        out_shape=jax.ShapeDtypeStruct((MOE_ROWS, ROW_SUB, LANES), jnp.int32),
        compiler_params=_cparams(("arbitrary",), has_side_effects=True),
        name="scatter_rows",
    )(fill_start, fill_len, dest.reshape(nt, 1, ROW_TILE * TOP_K), x)


MOE_NF = D_EXPERT // MOE_TF
MOE_ROW_STEP = 256


def _experts_kernel(ge_ref, gn_ref, de_ref, dn_ref, xs_ref, wg_ref, wu_ref, wd_ref, bgu_ref, bd_ref,
                    ys_ref, xb_ref, act_ref, hold_ref, words_ref, ow_ref, pend_ref, xsem, ysem):
    v = pl.program_id(0)
    f = pl.program_id(1)
    gn = gn_ref[v]
    dn = dn_ref[v]
    slot = v % 2
    half_nf = MOE_NF // 2

    sub_per_tile = MOE_TF // LANES

    def ys_copies(visit, tile):
        row0 = pl.multiple_of(visit * MOE_TM, MOE_TM)
        return [pltpu.make_async_copy(ow_ref.at[:, j * LANES:(j + 1) * LANES],
                                      ys_ref.at[pl.ds(row0, MOE_TM), tile * sub_per_tile + j, :], ysem)
                for j in range(sub_per_tile)]

    def emit_output_tile():
        for cp in ys_copies(v - 1, f - half_nf):
            cp.start()
        pend_ref[0] = 1

    def drain_output_tile():
        @pl.when(pend_ref[0] == 1)
        def _():
            for cp in ys_copies(0, 0):
                cp.wait()
            pend_ref[0] = 0

    @pl.when(jnp.logical_and(v == 0, f == 0))
    def _():
        hold_ref[...] = jnp.zeros_like(hold_ref)
        ow_ref[...] = jnp.zeros_like(ow_ref)
        pend_ref[0] = 0

    def xs_copies(visit, s):
        row0 = pl.multiple_of(visit * MOE_TM, MOE_TM)
        return [pltpu.make_async_copy(xs_ref.at[pl.ds(row0, MOE_TM), c, :],
                                      words_ref.at[s, :, c * LANES:(c + 1) * LANES], xsem.at[s])
                for c in range(ROW_SUB)]

    @pl.when(f == 0)
    def _():
        @pl.when(v == 0)
        def _():
            for cp in xs_copies(0, 0):
                cp.start()

        @pl.when(v < N_VISITS)
        def _():
            for cp in xs_copies(v, slot):
                cp.wait()

        @pl.when(v + 1 < N_VISITS)
        def _():
            for cp in xs_copies(v + 1, 1 - slot):
                cp.start()

    @pl.when(jnp.logical_and(gn > 0, f == 0))
    def _():
        r = lax.broadcasted_iota(jnp.int32, (MOE_TM, 1), 0)
        lo, hi = _unpack_rows(jnp.where(r < gn, words_ref[slot], 0))
        xb_ref[:, :HALF] = lo.astype(BF16)
        xb_ref[:, HALF:] = hi.astype(BF16)

    def gate_phase(rows):
        xb = xb_ref[:rows, :]
        e = pl.ds(ge_ref[v], 1)
        col = pl.multiple_of(f * MOE_TF, MOE_TF)
        gl = jnp.dot(xb, wg_ref[...].astype(BF16), preferred_element_type=F32) + bgu_ref[e, pl.ds(col, MOE_TF)]
        up = (jnp.dot(xb, wu_ref[...].astype(BF16), preferred_element_type=F32)
              + bgu_ref[e, pl.ds(D_EXPERT + col, MOE_TF)])
        gl = jnp.minimum(gl, SWIGLU_LIMIT)
        up = jnp.clip(up, -SWIGLU_LIMIT, SWIGLU_LIMIT)
        act = ((up + 1.0) * (gl * jax.nn.sigmoid(SWIGLU_ALPHA * gl))).astype(BF16)
        act_ref[slot, :rows, pl.ds(pl.multiple_of(f * MOE_TF, MOE_TF), MOE_TF)] = act

    def down_phase(rows):
        drain_output_tile()
        y = jnp.dot(act_ref[1 - slot, :rows, :], wd_ref[...].astype(BF16),
                    preferred_element_type=F32) + bd_ref[pl.ds(de_ref[v], 1),
                                                         pl.ds(pl.multiple_of(f * MOE_TF, MOE_TF), MOE_TF)]
        h = f % half_nf
        ow_ref[:rows, :] = _pack_pair(hold_ref[h, :rows, :], y)
        hold_ref[h, :rows, :] = y
        pl.when(f >= half_nf)(emit_output_tile)

    for rows in range(MOE_TM, 0, -MOE_ROW_STEP):
        lower = rows - MOE_ROW_STEP
        pl.when(jnp.logical_and(gn > lower, gn <= rows))(functools.partial(gate_phase, rows))
        pl.when(jnp.logical_and(dn > lower, dn <= rows))(functools.partial(down_phase, rows))

    @pl.when(jnp.logical_and(jnp.logical_and(dn == 0, v >= 1), f >= half_nf))
    def _():
        drain_output_tile()
        ow_ref[...] = jnp.zeros_like(ow_ref)
        emit_output_tile()

    @pl.when(jnp.logical_and(v == pl.num_programs(0) - 1, f == MOE_NF - 1))
    def _():
        drain_output_tile()


def _experts(xs, visit_expert, visit_nvalid, w_gate_up, b_gate_up, w_down, b_down):
    nf = MOE_NF
    gate_e = jnp.concatenate([visit_expert, visit_expert[-1:]])
    gate_n = jnp.concatenate([visit_nvalid, jnp.zeros((1,), jnp.int32)])
    down_e = jnp.concatenate([visit_expert[:1], visit_expert])
    down_n = jnp.concatenate([jnp.zeros((1,), jnp.int32), visit_nvalid])

    def tile(f, n):
        return jnp.where(n > 0, f, nf - 1)

    grid_spec = pltpu.PrefetchScalarGridSpec(
        num_scalar_prefetch=4,
        grid=(N_VISITS + 1, nf),
        in_specs=[
            pl.BlockSpec(memory_space=pl.ANY),
            pl.BlockSpec((None, D_MODEL, MOE_TF), lambda v, f, ge, gn, de, dn: (ge[v], 0, tile(f, gn[v]))),
            pl.BlockSpec((None, D_MODEL, MOE_TF), lambda v, f, ge, gn, de, dn: (ge[v], 0, nf + tile(f, gn[v]))),
            pl.BlockSpec((None, D_EXPERT, MOE_TF), lambda v, f, ge, gn, de, dn: (de[v], 0, tile(f, dn[v]))),
            pl.BlockSpec((N_EXPERTS, 2 * D_EXPERT), lambda v, f, ge, gn, de, dn: (0, 0)),
            pl.BlockSpec((N_EXPERTS, D_MODEL), lambda v, f, ge, gn, de, dn: (0, 0)),
        ],
        out_specs=pl.BlockSpec(memory_space=pl.ANY),
        scratch_shapes=[pltpu.VMEM((MOE_TM, D_MODEL), BF16),
                        pltpu.VMEM((2, MOE_TM, D_EXPERT), BF16),
                        pltpu.VMEM((nf // 2, MOE_TM, MOE_TF), F32),
                        pltpu.VMEM((2, MOE_TM, HALF), jnp.int32),
                        pltpu.VMEM((MOE_TM, MOE_TF), jnp.int32),
                        pltpu.SMEM((1,), jnp.int32),
                        pltpu.SemaphoreType.DMA((2,)),
                        pltpu.SemaphoreType.DMA(())],
    )
    return pl.pallas_call(
        _experts_kernel,
        grid_spec=grid_spec,
        out_shape=jax.ShapeDtypeStruct((MOE_ROWS, ROW_SUB, LANES), jnp.int32),
        compiler_params=_cparams(("arbitrary", "arbitrary")),
        name="experts",
    )(gate_e, gate_n, down_e, down_n, xs, w_gate_up, w_gate_up, w_down, b_gate_up, b_down)


def _combine_kernel(dest_ref, dest_next_ref, ys_ref, gate_ref, x_ref, lnw_ref, lnb_ref, o_ref,
                    buf_ref, z_ref, sem):
    i = pl.program_id(0)
    slot = i % 2

    def start_group(idx_ref, s, g):
        for t in range(SUBLANES):
            for k in range(TOP_K):
                d = idx_ref[0, 0, (g * SUBLANES + t) * TOP_K + k]
                pltpu.make_async_copy(ys_ref.at[d], buf_ref.at[s, k, g, :, t, :],
                                      sem.at[s]).start(priority=k % 2)

    @pl.when(i == 0)
    def _():
        def body(g, carry):
            start_group(dest_ref, slot, g)
            return carry

        lax.fori_loop(0, ROW_GROUPS, body, 0)

    for k in range(TOP_K):
        _wait_row_copies(ys_ref, sem.at[slot])

    def group(g, carry, prefetch):
        if prefetch:
            start_group(dest_next_ref, 1 - slot, g)
        rows = pl.ds(pl.multiple_of(g * SUBLANES, SUBLANES), SUBLANES)
        gates = gate_ref[rows, :]
        gate_k = [jnp.broadcast_to(gates[:, k:k + 1], (SUBLANES, LANES)) for k in range(TOP_K)]
        for c in range(ROW_SUB):
            lo_cols = slice(c * LANES, (c + 1) * LANES)
            hi_cols = slice(HALF + c * LANES, HALF + (c + 1) * LANES)
            z_lo = DN_ALPHA * x_ref[rows, lo_cols]
            z_hi = DN_ALPHA * x_ref[rows, hi_cols]
            for k in range(TOP_K):
                lo, hi = _unpack_rows(buf_ref[slot, k, g, c])
                z_lo = z_lo + gate_k[k] * lo
                z_hi = z_hi + gate_k[k] * hi
            z_ref[rows, lo_cols] = z_lo
            z_ref[rows, hi_cols] = z_hi
        return carry

    last = i + 1 == pl.num_programs(0)

    @pl.when(jnp.logical_not(last))
    def _():
        lax.fori_loop(0, ROW_GROUPS, functools.partial(group, prefetch=True), 0)

    @pl.when(last)
    def _():
        lax.fori_loop(0, ROW_GROUPS, functools.partial(group, prefetch=False), 0)

    o_ref[...] = _layer_norm(z_ref[...], lnw_ref[...], lnb_ref[...])


def _combine(ys, dest, gates, x, lnw, lnb):
    nt = TOKENS // ROW_TILE
    row = pl.BlockSpec((ROW_TILE, D_MODEL), lambda i: (i, 0))
    vec = pl.BlockSpec((1, D_MODEL), lambda i: (0, 0))
    dest_tiles = dest.reshape(nt, 1, ROW_TILE * TOP_K)
    return pl.pallas_call(
        _combine_kernel,
        grid=(nt,),
        in_specs=[pl.BlockSpec((1, 1, ROW_TILE * TOP_K), lambda i: (i, 0, 0), memory_space=pltpu.SMEM),
                  pl.BlockSpec((1, 1, ROW_TILE * TOP_K), lambda i: (jnp.minimum(i + 1, nt - 1), 0, 0),
                               memory_space=pltpu.SMEM),
                  pl.BlockSpec(memory_space=pl.ANY),
                  pl.BlockSpec((ROW_TILE, LANES), lambda i: (i, 0)),
                  row, vec, vec],
        out_specs=row,
        out_shape=jax.ShapeDtypeStruct((TOKENS, D_MODEL), F32),
        scratch_shapes=[pltpu.VMEM((2, TOP_K, ROW_GROUPS, ROW_SUB, SUBLANES, LANES), jnp.int32),
                        pltpu.VMEM((ROW_TILE, D_MODEL), F32), pltpu.SemaphoreType.DMA((2,))],
        compiler_params=_cparams(("arbitrary",)),
        name="combine",
    )(dest_tiles, dest_tiles, ys, gates, x, lnw, lnb)


def _layer(x, mem, w_in, attn_sinks, attn_norm_w, ret_norm_w, w_mix_out, ln_mix_w, ln_mix_b,
           w_mem_q, w_mem_kv, w_mem_out, ln_mem_w, ln_mem_b, w_router, b_router,
           w_gate_up, b_gate_up, w_down, b_down, ln_moe_w, ln_moe_b):
    o_k, o_v, o_qr = D_ATTN, D_ATTN + D_KV, D_ATTN + 2 * D_KV
    w_in_p = jnp.concatenate([w_in[:, :o_k], w_in[:, o_qr:], w_in[:, o_k:o_v], w_in[:, o_v:o_qr]],
                             axis=1).astype(BF16)
    proj = _project(x, w_in_p, PROJ_TM, PROJ_TN)
    y = _mixer(proj, attn_sinks.astype(F32), attn_norm_w.reshape(1, D_ATTN), ret_norm_w.reshape(1, D_RET))
    x1 = _out_ln(y, w_mix_out.astype(BF16), x, ln_mix_w.reshape(1, D_MODEL), ln_mix_b.reshape(1, D_MODEL))

    kv = _project(mem.reshape(BATCH * N_MEM, D_MODEL), w_mem_kv.astype(BF16), BATCH * N_MEM, D_MODEL)
    o = _mem_attn(x1, w_mem_q.astype(BF16), kv)
    x2 = _out_ln(o, w_mem_out.astype(BF16), x1, ln_mem_w.reshape(1, D_MODEL), ln_mem_b.reshape(1, D_MODEL))

    wr = jnp.pad(w_router, ((0, 0), (0, LANES - N_EXPERTS)))
    wr_hi = wr.astype(BF16)
    wr = jnp.concatenate([wr_hi, (wr - wr_hi.astype(F32)).astype(BF16)], axis=1)
    br =jnp.pad(b_router.reshape(1, N_EXPERTS), ((0, 0), (0, LANES - N_EXPERTS)), constant_values=NEG_INF)
    idx, gates, rank, counts = _route(x2, wr, br)

    counts = counts[0, :N_EXPERTS].astype(jnp.int32)
    padded = ((counts + MOE_TM - 1) // MOE_TM) * MOE_TM
    eid = jnp.arange(N_EXPERTS, dtype=jnp.int32)
    pad_end = jnp.sum(jnp.where(eid[None, :] <= eid[:, None], padded[None, :], 0), axis=1)
    pad_start = pad_end - padded
    chosen = idx[:, :TOP_K, None] == eid[None, None, :]
    dest = jnp.sum(jnp.where(chosen, pad_start[None, None, :], 0), axis=-1) + rank[:, :TOP_K]
    visit_row = jnp.arange(N_VISITS, dtype=jnp.int32) * MOE_TM
    visit_expert = jnp.minimum(jnp.sum((pad_end[None, :] <= visit_row[:, None]).astype(jnp.int32), axis=1),
                               N_EXPERTS - 1)
    visit_nvalid = jnp.clip(counts[visit_expert] - (visit_row - pad_start[visit_expert]),
                            0, MOE_TM).astype(jnp.int32)

    fill_start = jnp.concatenate([pad_start + counts, pad_end[-1:]]).astype(jnp.int32)
    fill_len = jnp.concatenate([padded - counts, (MOE_ROWS - pad_end[-1:]) // FILL_SIZES[0]]).astype(jnp.int32)
    xs = _scatter_rows(x2, dest, fill_start, fill_len)
    ys = _experts(xs, visit_expert, visit_nvalid,
                  w_gate_up, b_gate_up, w_down, b_down)
    return _combine(ys, dest, gates, x2, ln_moe_w.reshape(1, D_MODEL), ln_moe_b.reshape(1, D_MODEL))


def kernel(x, mem, w_in, attn_sinks, attn_norm_w, ret_norm_w, w_mix_out, ln_mix_w, ln_mix_b, w_mem_q, w_mem_kv, w_mem_out, ln_mem_w, ln_mem_b, w_router, b_router, w_gate_up, b_gate_up, w_down, b_down, ln_moe_w, ln_moe_b):
    xt = x.reshape(TOKENS, D_MODEL)
    out = _layer(xt, mem, w_in[0], attn_sinks[0], attn_norm_w[0], ret_norm_w[0], w_mix_out[0],
                 ln_mix_w[0], ln_mix_b[0], w_mem_q[0], w_mem_kv[0], w_mem_out[0], ln_mem_w[0], ln_mem_b[0],
                 w_router[0], b_router[0], w_gate_up[0], b_gate_up[0], w_down[0], b_down[0],
                 ln_moe_w[0], ln_moe_b[0])
    return out.reshape(BATCH, SEQ, D_MODEL)
```

```python
import functools
import math

import jax
import jax.numpy as jnp
from jax import lax
from jax.experimental import pallas as pl
from jax.experimental.pallas import tpu as pltpu

F32 = jnp.float32
BF16 = jnp.bfloat16

D_MODEL = 2048
BATCH = 2
SEQ = 16384
TOKENS = BATCH * SEQ

HEAD_DIM_ATTN = 64
N_Q_ATTN = 16
N_KV_ATTN = 2
GQA_GROUP = N_Q_ATTN // N_KV_ATTN
ATTN_BLOCK = 128
D_ATTN = N_Q_ATTN * HEAD_DIM_ATTN
D_KV = N_KV_ATTN * HEAD_DIM_ATTN
N_RET = 8
HEAD_DIM_RET = 128
RET_CHUNK = 128
D_RET = N_RET * HEAD_DIM_RET
D_IN = D_ATTN + 2 * D_KV + 4 * D_RET
N_MEM = 256
N_MEM_HEADS = 4
HEAD_DIM_MEM = D_MODEL // N_MEM_HEADS
N_EXPERTS = 32
TOP_K = 4
D_EXPERT = D_MODEL
SWIGLU_LIMIT = 7.0
SWIGLU_ALPHA = 1.702
LN_EPS = 1e-5
DN_ALPHA = 2.0 ** 0.25
NEG_INF = -1e30

LANES = 128
VMEM_LIMIT = 58 * 1024 * 1024

PROJ_TM = 1024
PROJ_TN = 1792
LN_TM = 512
MEM_TM = 512
ROUTE_TM = 512
ROW_TILE = 512
MOE_TM = 1024
MOE_TF = 512
N_VISITS = (TOKENS * TOP_K + N_EXPERTS * (MOE_TM - 1) + MOE_TM - 1) // MOE_TM
MOE_ROWS = N_VISITS * MOE_TM

COL_QA, COL_QR, COL_KR, COL_VR, COL_GR = 0, 1, 2, 3, 4
COL_KA = 5 * D_RET // D_KV
COL_VA = COL_KA + 1


def _cparams(sem, **kw):
    return pltpu.CompilerParams(dimension_semantics=sem, vmem_limit_bytes=VMEM_LIMIT, **kw)


def _layer_norm(z, w, b):
    mu = jnp.mean(z, axis=-1, keepdims=True)
    zc = z - mu
    var = jnp.mean(zc * zc, axis=-1, keepdims=True)
    return zc * lax.rsqrt(var + LN_EPS) * w + b


def _dot_nt(a, b):
    return lax.dot_general(a, b, (((1,), (1,)), ((), ())), preferred_element_type=F32)


def _proj_kernel(x_ref, w_ref, o_ref, xb_ref):
    @pl.when(pl.program_id(1) == 0)
    def _():
        xb_ref[...] = x_ref[...].astype(BF16)

    o_ref[...] = jnp.dot(xb_ref[...], w_ref[...], preferred_element_type=F32).astype(o_ref.dtype)


def _project(x, w, tm, tn):
    m, k = x.shape
    n = w.shape[1]
    return pl.pallas_call(
        _proj_kernel,
        grid=(m // tm, n // tn),
        in_specs=[pl.BlockSpec((tm, k), lambda i, j: (i, 0)),
                  pl.BlockSpec((k, tn), lambda i, j: (0, j))],
        out_specs=pl.BlockSpec((tm, tn), lambda i, j: (i, j)),
        out_shape=jax.ShapeDtypeStruct((m, n), BF16),
        scratch_shapes=[pltpu.VMEM((tm, k), BF16)],
        compiler_params=_cparams(("parallel", "arbitrary")),
        name="project",
    )(x, w)


def _mixer_kernel(sinkrow_ref, bias_cur_ref, bias_prev_ref, decay_ref, qa_ref, kp_ref, kc_ref, vp_ref, vc_ref,
                  qr_ref, kr_ref, vr_ref, gr_ref, anw_ref, rnw_ref, y_ref, state_ref, a_ref):
    n = pl.program_id(1)

    scale = HEAD_DIM_ATTN ** -0.5
    for j in range(N_KV_ATTN):
        klo, khi = j * HEAD_DIM_ATTN, (j + 1) * HEAD_DIM_ATTN
        heads = range(j * GQA_GROUP, (j + 1) * GQA_GROUP)
        q_rows = jnp.concatenate([qa_ref[:, h * HEAD_DIM_ATTN:(h + 1) * HEAD_DIM_ATTN] for h in heads], axis=0)
        q_rows = q_rows * scale
        sink = sinkrow_ref[j]
        s_cur = _dot_nt(kc_ref[:, klo:khi], q_rows) + bias_cur_ref[j]
        s_prev = _dot_nt(kp_ref[:, klo:khi], q_rows) + bias_prev_ref[j]
        s_prev = jnp.where(n > 0, s_prev, NEG_INF)
        m = jnp.maximum(jnp.max(s_cur, axis=0, keepdims=True), jnp.max(s_prev, axis=0, keepdims=True))
        m = jnp.maximum(m, sink)
        p_cur = jnp.exp(s_cur - m)
        p_prev = jnp.exp(s_prev - m)
        denom = (jnp.sum(p_cur, axis=0, keepdims=True) + jnp.sum(p_prev, axis=0, keepdims=True)
                 + jnp.exp(sink - m))
        v_cur_t = vc_ref[:, klo:khi].astype(F32).T.astype(BF16)
        v_prev_t = vp_ref[:, klo:khi].astype(F32).T.astype(BF16)
        o_t = (jnp.dot(v_cur_t, p_cur.astype(BF16), preferred_element_type=F32)
               + jnp.dot(v_prev_t, p_prev.astype(BF16), preferred_element_type=F32))
        o_t = o_t * (1.0 / denom)
        for g, h in enumerate(heads):
            a_ref[h * HEAD_DIM_ATTN:(h + 1) * HEAD_DIM_ATTN, :] = o_t[:, g * ATTN_BLOCK:(g + 1) * ATTN_BLOCK]
    a_t = a_ref[...]
    a_t = a_t * lax.rsqrt(jnp.mean(a_t * a_t, axis=0, keepdims=True) + LN_EPS)
    y_ref[:, :D_ATTN] = (a_t.T * anw_ref[...]).astype(y_ref.dtype)

    @pl.when(n == 0)
    def _():
        state_ref[...] = jnp.zeros_like(state_ref)

    for h in range(N_RET):
        lo, hi = h * HEAD_DIM_RET, (h + 1) * HEAD_DIM_RET
        g_c = math.exp(math.log1p(-(2.0 ** (-5.0 - h))) * RET_CHUNK)
        q = qr_ref[:, lo:hi]
        k = kr_ref[:, lo:hi]
        v = vr_ref[:, lo:hi]
        s = _dot_nt(q, k) * decay_ref[h]
        inner = jnp.dot(s.astype(BF16), v, preferred_element_type=F32)
        state = state_ref[h]
        qw = (q.astype(F32) * decay_ref[N_RET + h]).astype(BF16)
        cross = jnp.dot(qw, state.astype(BF16), preferred_element_type=F32)
        kw_t = (k.astype(F32) * decay_ref[2 * N_RET + h]).T.astype(BF16)
        state_ref[h] = g_c * state + jnp.dot(kw_t, v, preferred_element_type=F32)
        o = inner + cross
        mu = jnp.mean(o, axis=-1, keepdims=True)
        oc = o - mu
        var = jnp.mean(oc * oc, axis=-1, keepdims=True)
        o = oc * lax.rsqrt(var + LN_EPS) * rnw_ref[:, lo:hi]
        g = gr_ref[:, lo:hi].astype(F32)
        o = o * (g * jax.nn.sigmoid(g))
        y_ref[:, D_ATTN + lo:D_ATTN + hi] = o.astype(y_ref.dtype)


def _mixer(proj, sinks, attn_norm_w, ret_norm_w):
    nc = SEQ // ATTN_BLOCK
    blk = ATTN_BLOCK

    def wide(colblock):
        return pl.BlockSpec((blk, D_RET), lambda b, n: (b * nc + n, colblock))

    def kv_cur(colblock):
        return pl.BlockSpec((blk, D_KV), lambda b, n: (b * nc + n, colblock))

    def kv_prev(colblock):
        return pl.BlockSpec((blk, D_KV), lambda b, n: (b * nc + jnp.maximum(n - 1, 0), colblock))

    vec = pl.BlockSpec((1, D_RET), lambda b, n: (0, 0))
    gq = GQA_GROUP * blk

    def table(rows):
        return pl.BlockSpec((N_KV_ATTN, rows, gq), lambda b, n: (0, 0, 0))

    slopes = jnp.exp2(-8.0 * (jnp.arange(N_Q_ATTN, dtype=F32) + 1.0) / N_Q_ATTN).reshape(N_KV_ATTN, 1, GQA_GROUP, 1)
    key = jnp.arange(blk, dtype=F32).reshape(1, blk, 1, 1)
    query = jnp.arange(blk, dtype=F32).reshape(1, 1, 1, blk)
    dist_cur = jnp.broadcast_to(query - key, (N_KV_ATTN, blk, GQA_GROUP, blk))
    bias_cur = jnp.where(dist_cur >= 0, -slopes * dist_cur, NEG_INF).reshape(N_KV_ATTN, blk, gq)
    bias_prev = jnp.where(dist_cur < 0, -slopes * (dist_cur + blk), NEG_INF).reshape(N_KV_ATTN, blk, gq)
    sink_rows = jnp.repeat(sinks.astype(F32).reshape(N_KV_ATTN, 1, GQA_GROUP), blk, axis=2)

    log_g = jnp.log1p(-jnp.exp2(-5.0 - jnp.arange(N_RET, dtype=F32))).reshape(N_RET, 1, 1)
    pos = jnp.arange(RET_CHUNK, dtype=F32)
    lag = pos[:, None] - pos[None, :]
    kscale = HEAD_DIM_RET ** -0.5
    in_chunk = jnp.where(lag >= 0, jnp.exp(log_g * jnp.maximum(lag, 0.0)), 0.0) * kscale
    ones = jnp.ones((1, 1, HEAD_DIM_RET), F32)
    w_query = jnp.exp(log_g * (pos + 1.0).reshape(1, RET_CHUNK, 1)) * ones
    w_key = jnp.exp(log_g * (RET_CHUNK - 1.0 - pos).reshape(1, RET_CHUNK, 1)) * (kscale * ones)
    decay = jnp.concatenate([in_chunk, w_query, w_key], axis=0)
    return pl.pallas_call(
        _mixer_kernel,
        grid=(BATCH, nc),
        in_specs=[table(1), table(blk), table(blk),
                  pl.BlockSpec((3 * N_RET, RET_CHUNK, HEAD_DIM_RET), lambda b, n: (0, 0, 0)),
                  wide(COL_QA), kv_prev(COL_KA), kv_cur(COL_KA), kv_prev(COL_VA), kv_cur(COL_VA),
                  wide(COL_QR), wide(COL_KR), wide(COL_VR), wide(COL_GR), vec, vec],
        out_specs=pl.BlockSpec((blk, D_MODEL), lambda b, n: (b * nc + n, 0)),
        out_shape=jax.ShapeDtypeStruct((TOKENS, D_MODEL), BF16),
        scratch_shapes=[pltpu.VMEM((N_RET, HEAD_DIM_RET, HEAD_DIM_RET), F32),
                        pltpu.VMEM((D_ATTN, blk), F32)],
        compiler_params=_cparams(("parallel", "arbitrary")),
        name="mixer",
    )(sink_rows, bias_cur, bias_prev, decay, proj, proj, proj, proj, proj, proj, proj, proj, proj,
      attn_norm_w, ret_norm_w)


def _out_ln_kernel(y_ref, w_ref, x_ref, lnw_ref, lnb_ref, o_ref):
    h = jnp.dot(y_ref[...], w_ref[...], preferred_element_type=F32)
    o_ref[...] = _layer_norm(DN_ALPHA * x_ref[...] + h, lnw_ref[...], lnb_ref[...])


def _out_ln(y, w, x, lnw, lnb):
    tm = LN_TM
    row = pl.BlockSpec((tm, D_MODEL), lambda i: (i, 0))
    vec = pl.BlockSpec((1, D_MODEL), lambda i: (0, 0))
    return pl.pallas_call(
        _out_ln_kernel,
        grid=(TOKENS // tm,),
        in_specs=[row,
                  pl.BlockSpec((D_MODEL, D_MODEL), lambda i: (0, 0), pipeline_mode=pl.Buffered(1)),
                  row, vec, vec],
        out_specs=row,
        out_shape=jax.ShapeDtypeStruct((TOKENS, D_MODEL), F32),
        compiler_params=_cparams(("parallel",)),
        name="out_ln",
    )(y, w, x, lnw, lnb)


def _mem_attn_kernel(x_ref, wq_ref, k_ref, v_ref, o_ref):
    q = jnp.dot(x_ref[...].astype(BF16), wq_ref[...], preferred_element_type=F32)
    q = (q * (HEAD_DIM_MEM ** -0.5)).astype(BF16)
    for h in range(N_MEM_HEADS):
        lo, hi = h * HEAD_DIM_MEM, (h + 1) * HEAD_DIM_MEM
        s = _dot_nt(q[:, lo:hi], k_ref[:, lo:hi])
        m = jnp.max(s, axis=-1, keepdims=True)
        p = jnp.exp(s - m)
        denom = jnp.sum(p, axis=-1, keepdims=True)
        o = jnp.dot(p.astype(BF16), v_ref[:, lo:hi], preferred_element_type=F32)
        o_ref[:, lo:hi] = (o / denom).astype(o_ref.dtype)


def _mem_attn(x, wq, kv):
    tm = MEM_TM
    nt = SEQ // tm
    row = pl.BlockSpec((tm, D_MODEL), lambda b, i: (b * nt + i, 0))
    return pl.pallas_call(
        _mem_attn_kernel,
        grid=(BATCH, nt),
        in_specs=[row,
                  pl.BlockSpec((D_MODEL, D_MODEL), lambda b, i: (0, 0), pipeline_mode=pl.Buffered(1)),
                  pl.BlockSpec((N_MEM, D_MODEL), lambda b, i: (b, 0)),
                  pl.BlockSpec((N_MEM, D_MODEL), lambda b, i: (b, 1))],
        out_specs=row,
        out_shape=jax.ShapeDtypeStruct((TOKENS, D_MODEL), BF16),
        compiler_params=_cparams(("parallel", "parallel")),
        name="mem_attn",
    )(x, wq, kv, kv)


def _route_kernel(x_ref, wr_ref, br_ref, idx_ref, gate_ref, rank_ref, count_ref, carry_ref):
    tm = x_ref.shape[0]

    @pl.when(pl.program_id(0) == 0)
    def _():
        carry_ref[...] = jnp.zeros_like(carry_ref)

    x = x_ref[...]
    x_hi = x.astype(BF16)
    x_lo = (x - x_hi.astype(F32)).astype(BF16)
    part_hi = jnp.dot(x_hi, wr_ref[...], preferred_element_type=F32)
    part_lo = jnp.dot(x_lo, wr_ref[...], preferred_element_type=F32)
    logits = (part_hi[:, :LANES] + (part_hi[:, LANES:] + part_lo[:, :LANES]) + part_lo[:, LANES:]
              + br_ref[...])
    lane = lax.broadcasted_iota(jnp.int32, (tm, LANES), 1)
    lane_f = lane.astype(F32)
    work = logits
    sels, vals, idxs = [], [], []
    for _ in range(TOP_K):
        v = jnp.max(work, axis=-1, keepdims=True)
        i = jnp.min(jnp.where(work == v, lane_f, float(LANES)), axis=-1, keepdims=True)
        sel = lane_f == i
        work = jnp.where(sel, -jnp.inf, work)
        sels.append(sel)
        vals.append(v)
        idxs.append(i)
    exps = [jnp.exp(v - vals[0]) for v in vals]
    total = exps[0] + exps[1] + exps[2] + exps[3]

    onehot = jnp.zeros((tm, LANES), F32)
    for sel in sels:
        onehot = jnp.where(sel, 1.0, onehot)
    r = lax.broadcasted_iota(jnp.int32, (tm, tm), 0)
    c = lax.broadcasted_iota(jnp.int32, (tm, tm), 1)
    below = jnp.where(c < r, 1.0, 0.0).astype(BF16)
    prefix = jnp.dot(below, onehot.astype(BF16), preferred_element_type=F32)
    rankmat = prefix + carry_ref[...]
    carry_ref[...] = carry_ref[...] + jnp.sum(onehot, axis=0, keepdims=True)
    count_ref[...] = carry_ref[...]

    idx_out = jnp.zeros((tm, LANES), jnp.int32)
    gate_out = jnp.zeros((tm, LANES), F32)
    rank_out = jnp.zeros((tm, LANES), jnp.int32)
    for k in range(TOP_K):
        rk = jnp.sum(jnp.where(sels[k], rankmat, 0.0), axis=-1, keepdims=True).astype(jnp.int32)
        idx_out = jnp.where(lane == k, idxs[k].astype(jnp.int32), idx_out)
        gate_out = jnp.where(lane == k, exps[k] / total, gate_out)
        rank_out = jnp.where(lane == k, rk, rank_out)
    idx_ref[...] = idx_out
    gate_ref[...] = gate_out
    rank_ref[...] = rank_out


def _route(x, wr, br):
    tm = ROUTE_TM
    wide = pl.BlockSpec((tm, LANES), lambda i: (i, 0))
    return pl.pallas_call(
        _route_kernel,
        grid=(TOKENS // tm,),
        in_specs=[pl.BlockSpec((tm, D_MODEL), lambda i: (i, 0)),
                  pl.BlockSpec((D_MODEL, 2 * LANES), lambda i: (0, 0)),
                  pl.BlockSpec((1, LANES), lambda i: (0, 0))],
        out_specs=[wide, wide, wide, pl.BlockSpec((1, LANES), lambda i: (0, 0))],
        out_shape=[jax.ShapeDtypeStruct((TOKENS, LANES), jnp.int32),
                   jax.ShapeDtypeStruct((TOKENS, LANES), F32),
                   jax.ShapeDtypeStruct((TOKENS, LANES), jnp.int32),
                   jax.ShapeDtypeStruct((1, LANES), F32)],
        scratch_shapes=[pltpu.VMEM((1, LANES), F32)],
        compiler_params=_cparams(("arbitrary",)),
        name="route",
    )(x, wr, br)


HALF = D_MODEL // 2


def _pack_rows(x):
    return _pack_pair(x[:, :HALF], x[:, HALF:])


def _pack_pair(lo, hi):
    lo = lax.bitcast_convert_type(lo.astype(BF16).astype(F32), jnp.int32)
    hi = lax.bitcast_convert_type(hi.astype(BF16).astype(F32), jnp.int32)
    return hi | lax.shift_right_logical(lo, 16)


def _unpack_rows(w):
    lo = lax.bitcast_convert_type(lax.shift_left(w, 16), F32)
    hi = lax.bitcast_convert_type(w & jnp.int32(-65536), F32)
    return lo, hi


def _wait_row_copies(rows_ref, sem):
    pltpu.make_async_copy(rows_ref.at[pl.ds(0, ROW_TILE)], rows_ref.at[pl.ds(0, ROW_TILE)], sem).wait()


ROW_SUB = HALF // LANES
SUBLANES = 8
ROW_GROUPS = ROW_TILE // SUBLANES


FILL_SIZES = tuple(MOE_TM >> (b + 1) for b in range(MOE_TM.bit_length() - 1))


def _fill_copies(fill_start_ref, fill_len_ref, zero_ref, xs_ref, sem, act):
    def body(e, carry):
        pos = fill_start_ref[e]
        n = fill_len_ref[e]
        for size in FILL_SIZES:
            take = (n & size) != 0

            @pl.when(take)
            def _():
                cp = pltpu.make_async_copy(zero_ref.at[pl.ds(0, size)], xs_ref.at[pl.ds(pos, size)], sem)
                getattr(cp, act)()

            pos = pos + jnp.where(take, size, 0)
        return carry

    lax.fori_loop(0, N_EXPERTS, body, 0)

    def tail(j, carry):
        pos = fill_start_ref[N_EXPERTS] + j * FILL_SIZES[0]
        cp = pltpu.make_async_copy(zero_ref, xs_ref.at[pl.ds(pos, FILL_SIZES[0])], sem)
        getattr(cp, act)()
        return carry

    lax.fori_loop(0, fill_len_ref[N_EXPERTS], tail, 0)


def _scatter_kernel(fill_start_ref, fill_len_ref, dest_ref, x_ref, xs_ref, packed_ref, zero_ref, sem, fill_sem):
    i = pl.program_id(0)
    slot = i % 2

    @pl.when(i == 0)
    def _():
        zero_ref[...] = jnp.zeros_like(zero_ref)
        _fill_copies(fill_start_ref, fill_len_ref, zero_ref, xs_ref, fill_sem, "start")

    words = _pack_rows(x_ref[...])
    for g in range(ROW_GROUPS):
        for c in range(ROW_SUB):
            packed_ref[slot, g, c] = words[g * SUBLANES:(g + 1) * SUBLANES, c * LANES:(c + 1) * LANES]

    def body(g, carry):
        for s in range(SUBLANES):
            for k in range(TOP_K):
                d = dest_ref[0, 0, (g * SUBLANES + s) * TOP_K + k]
                pltpu.make_async_copy(packed_ref.at[slot, g, :, s, :], xs_ref.at[d],
                                      sem.at[slot]).start(priority=k % 2)
        return carry

    lax.fori_loop(0, ROW_GROUPS, body, 0)

    @pl.when(i > 0)
    def _():
        for k in range(TOP_K):
            _wait_row_copies(xs_ref, sem.at[1 - slot])

    @pl.when(i == pl.num_programs(0) - 1)
    def _():
        for k in range(TOP_K):
            _wait_row_copies(xs_ref, sem.at[slot])

    @pl.when(i == 0)
    def _():
        _fill_copies(fill_start_ref, fill_len_ref, zero_ref, xs_ref, fill_sem, "wait")


def _scatter_rows(x, dest, fill_start, fill_len):
    nt = TOKENS // ROW_TILE
    grid_spec = pltpu.PrefetchScalarGridSpec(
        num_scalar_prefetch=2,
        grid=(nt,),
        in_specs=[pl.BlockSpec((1, 1, ROW_TILE * TOP_K), lambda i, fs, fl: (i, 0, 0), memory_space=pltpu.SMEM),
                  pl.BlockSpec((ROW_TILE, D_MODEL), lambda i, fs, fl: (i, 0))],
        out_specs=pl.BlockSpec(memory_space=pl.ANY),
        scratch_shapes=[pltpu.VMEM((2, ROW_GROUPS, ROW_SUB, SUBLANES, LANES), jnp.int32),
                        pltpu.VMEM((FILL_SIZES[0], ROW_SUB, LANES), jnp.int32),
                        pltpu.SemaphoreType.DMA((2,)), pltpu.SemaphoreType.DMA(())],
    )
    return pl.pallas_call(
        _scatter_kernel,
        grid_spec=grid_spec,
        out_shape=jax.ShapeDtypeStruct((MOE_ROWS, ROW_SUB, LANES), jnp.int32),
        compiler_params=_cparams(("arbitrary",), has_side_effects=True),
        name="scatter_rows",
    )(fill_start, fill_len, dest.reshape(nt, 1, ROW_TILE * TOP_K), x)


MOE_NF = D_EXPERT // MOE_TF
MOE_ROW_STEP = 256


def _experts_kernel(ge_ref, gn_ref, de_ref, dn_ref, xs_ref, wg_ref, wu_ref, wd_ref, bgu_ref, bd_ref,
                    ys_ref, xb_ref, act_ref, hold_ref, words_ref, ow_ref, pend_ref, xsem, ysem):
    v = pl.program_id(0)
    f = pl.program_id(1)
    gn = gn_ref[v]
    dn = dn_ref[v]
    slot = v % 2
    half_nf = MOE_NF // 2

    sub_per_tile = MOE_TF // LANES

    def ys_copies(visit, tile):
        row0 = pl.multiple_of(visit * MOE_TM, MOE_TM)
        return [pltpu.make_async_copy(ow_ref.at[:, j * LANES:(j + 1) * LANES],
                                      ys_ref.at[pl.ds(row0, MOE_TM), tile * sub_per_tile + j, :], ysem)
                for j in range(sub_per_tile)]

    def emit_output_tile():
        for cp in ys_copies(v - 1, f - half_nf):
            cp.start()
        pend_ref[0] = 1

    def drain_output_tile():
        @pl.when(pend_ref[0] == 1)
        def _():
            for cp in ys_copies(0, 0):
                cp.wait()
            pend_ref[0] = 0

    @pl.when(jnp.logical_and(v == 0, f == 0))
    def _():
        hold_ref[...] = jnp.zeros_like(hold_ref)
        ow_ref[...] = jnp.zeros_like(ow_ref)
        pend_ref[0] = 0

    def xs_copies(visit, s):
        row0 = pl.multiple_of(visit * MOE_TM, MOE_TM)
        return [pltpu.make_async_copy(xs_ref.at[pl.ds(row0, MOE_TM), c, :],
                                      words_ref.at[s, :, c * LANES:(c + 1) * LANES], xsem.at[s])
                for c in range(ROW_SUB)]

    @pl.when(f == 0)
    def _():
        @pl.when(v == 0)
        def _():
            for cp in xs_copies(0, 0):
                cp.start()

        @pl.when(v < N_VISITS)
        def _():
            for cp in xs_copies(v, slot):
                cp.wait()

        @pl.when(v + 1 < N_VISITS)
        def _():
            for cp in xs_copies(v + 1, 1 - slot):
                cp.start()

    @pl.when(jnp.logical_and(gn > 0, f == 0))
    def _():
        r = lax.broadcasted_iota(jnp.int32, (MOE_TM, 1), 0)
        lo, hi = _unpack_rows(jnp.where(r < gn, words_ref[slot], 0))
        xb_ref[:, :HALF] = lo.astype(BF16)
        xb_ref[:, HALF:] = hi.astype(BF16)

    def gate_phase(rows):
        xb = xb_ref[:rows, :]
        e = pl.ds(ge_ref[v], 1)
        col = pl.multiple_of(f * MOE_TF, MOE_TF)
        gl = jnp.dot(xb, wg_ref[...].astype(BF16), preferred_element_type=F32) + bgu_ref[e, pl.ds(col, MOE_TF)]
        up = (jnp.dot(xb, wu_ref[...].astype(BF16), preferred_element_type=F32)
              + bgu_ref[e, pl.ds(D_EXPERT + col, MOE_TF)])
        gl = jnp.minimum(gl, SWIGLU_LIMIT)
        up = jnp.clip(up, -SWIGLU_LIMIT, SWIGLU_LIMIT)
        act = ((up + 1.0) * (gl * jax.nn.sigmoid(SWIGLU_ALPHA * gl))).astype(BF16)
        act_ref[slot, :rows, pl.ds(pl.multiple_of(f * MOE_TF, MOE_TF), MOE_TF)] = act

    def down_phase(rows):
        drain_output_tile()
        y = jnp.dot(act_ref[1 - slot, :rows, :], wd_ref[...].astype(BF16),
                    preferred_element_type=F32) + bd_ref[pl.ds(de_ref[v], 1),
                                                         pl.ds(pl.multiple_of(f * MOE_TF, MOE_TF), MOE_TF)]
        h = f % half_nf
        ow_ref[:rows, :] = _pack_pair(hold_ref[h, :rows, :], y)
        hold_ref[h, :rows, :] = y
        pl.when(f >= half_nf)(emit_output_tile)

    for rows in range(MOE_TM, 0, -MOE_ROW_STEP):
        lower = rows - MOE_ROW_STEP
        pl.when(jnp.logical_and(gn > lower, gn <= rows))(functools.partial(gate_phase, rows))
        pl.when(jnp.logical_and(dn > lower, dn <= rows))(functools.partial(down_phase, rows))

    @pl.when(jnp.logical_and(jnp.logical_and(dn == 0, v >= 1), f >= half_nf))
    def _():
        drain_output_tile()
        ow_ref[...] = jnp.zeros_like(ow_ref)
        emit_output_tile()

    @pl.when(jnp.logical_and(v == pl.num_programs(0) - 1, f == MOE_NF - 1))
    def _():
        drain_output_tile()


def _experts(xs, visit_expert, visit_nvalid, w_gate_up, b_gate_up, w_down, b_down):
    nf = MOE_NF
    gate_e = jnp.concatenate([visit_expert, visit_expert[-1:]])
    gate_n = jnp.concatenate([visit_nvalid, jnp.zeros((1,), jnp.int32)])
    down_e = jnp.concatenate([visit_expert[:1], visit_expert])
    down_n = jnp.concatenate([jnp.zeros((1,), jnp.int32), visit_nvalid])

    def tile(f, n):
        return jnp.where(n > 0, f, nf - 1)

    grid_spec = pltpu.PrefetchScalarGridSpec(
        num_scalar_prefetch=4,
        grid=(N_VISITS + 1, nf),
        in_specs=[
            pl.BlockSpec(memory_space=pl.ANY),
            pl.BlockSpec((None, D_MODEL, MOE_TF), lambda v, f, ge, gn, de, dn: (ge[v], 0, tile(f, gn[v]))),
            pl.BlockSpec((None, D_MODEL, MOE_TF), lambda v, f, ge, gn, de, dn: (ge[v], 0, nf + tile(f, gn[v]))),
            pl.BlockSpec((None, D_EXPERT, MOE_TF), lambda v, f, ge, gn, de, dn: (de[v], 0, tile(f, dn[v]))),
            pl.BlockSpec((N_EXPERTS, 2 * D_EXPERT), lambda v, f, ge, gn, de, dn: (0, 0)),
            pl.BlockSpec((N_EXPERTS, D_MODEL), lambda v, f, ge, gn, de, dn: (0, 0)),
        ],
        out_specs=pl.BlockSpec(memory_space=pl.ANY),
        scratch_shapes=[pltpu.VMEM((MOE_TM, D_MODEL), BF16),
                        pltpu.VMEM((2, MOE_TM, D_EXPERT), BF16),
                        pltpu.VMEM((nf // 2, MOE_TM, MOE_TF), F32),
                        pltpu.VMEM((2, MOE_TM, HALF), jnp.int32),
                        pltpu.VMEM((MOE_TM, MOE_TF), jnp.int32),
                        pltpu.SMEM((1,), jnp.int32),
                        pltpu.SemaphoreType.DMA((2,)),
                        pltpu.SemaphoreType.DMA(())],
    )
    return pl.pallas_call(
        _experts_kernel,
        grid_spec=grid_spec,
        out_shape=jax.ShapeDtypeStruct((MOE_ROWS, ROW_SUB, LANES), jnp.int32),
        compiler_params=_cparams(("arbitrary", "arbitrary")),
        name="experts",
    )(gate_e, gate_n, down_e, down_n, xs, w_gate_up, w_gate_up, w_down, b_gate_up, b_down)


def _combine_kernel(dest_ref, dest_next_ref, ys_ref, gate_ref, x_ref, lnw_ref, lnb_ref, o_ref,
                    buf_ref, z_ref, sem):
    i = pl.program_id(0)
    slot = i % 2

    def start_group(idx_ref, s, g):
        for t in range(SUBLANES):
            for k in range(TOP_K):
                d = idx_ref[0, 0, (g * SUBLANES + t) * TOP_K + k]
                pltpu.make_async_copy(ys_ref.at[d], buf_ref.at[s, k, g, :, t, :],
                                      sem.at[s]).start(priority=k % 2)

    @pl.when(i == 0)
    def _():
        def body(g, carry):
            start_group(dest_ref, slot, g)
            return carry

        lax.fori_loop(0, ROW_GROUPS, body, 0)

    for k in range(TOP_K):
        _wait_row_copies(ys_ref, sem.at[slot])

    def group(g, carry, prefetch):
        if prefetch:
            start_group(dest_next_ref, 1 - slot, g)
        rows = pl.ds(pl.multiple_of(g * SUBLANES, SUBLANES), SUBLANES)
        gates = gate_ref[rows, :]
        gate_k = [jnp.broadcast_to(gates[:, k:k + 1], (SUBLANES, LANES)) for k in range(TOP_K)]
        for c in range(ROW_SUB):
            lo_cols = slice(c * LANES, (c + 1) * LANES)
            hi_cols = slice(HALF + c * LANES, HALF + (c + 1) * LANES)
            z_lo = DN_ALPHA * x_ref[rows, lo_cols]
            z_hi = DN_ALPHA * x_ref[rows, hi_cols]
            for k in range(TOP_K):
                lo, hi = _unpack_rows(buf_ref[slot, k, g, c])
                z_lo = z_lo + gate_k[k] * lo
                z_hi = z_hi + gate_k[k] * hi
            z_ref[rows, lo_cols] = z_lo
            z_ref[rows, hi_cols] = z_hi
        return carry

    last = i + 1 == pl.num_programs(0)

    @pl.when(jnp.logical_not(last))
    def _():
        lax.fori_loop(0, ROW_GROUPS, functools.partial(group, prefetch=True), 0)

    @pl.when(last)
    def _():
        lax.fori_loop(0, ROW_GROUPS, functools.partial(group, prefetch=False), 0)

    o_ref[...] = _layer_norm(z_ref[...], lnw_ref[...], lnb_ref[...])


def _combine(ys, dest, gates, x, lnw, lnb):
    nt = TOKENS // ROW_TILE
    row = pl.BlockSpec((ROW_TILE, D_MODEL), lambda i: (i, 0))
    vec = pl.BlockSpec((1, D_MODEL), lambda i: (0, 0))
    dest_tiles = dest.reshape(nt, 1, ROW_TILE * TOP_K)
    return pl.pallas_call(
        _combine_kernel,
        grid=(nt,),
        in_specs=[pl.BlockSpec((1, 1, ROW_TILE * TOP_K), lambda i: (i, 0, 0), memory_space=pltpu.SMEM),
                  pl.BlockSpec((1, 1, ROW_TILE * TOP_K), lambda i: (jnp.minimum(i + 1, nt - 1), 0, 0),
                               memory_space=pltpu.SMEM),
                  pl.BlockSpec(memory_space=pl.ANY),
                  pl.BlockSpec((ROW_TILE, LANES), lambda i: (i, 0)),
                  row, vec, vec],
        out_specs=row,
        out_shape=jax.ShapeDtypeStruct((TOKENS, D_MODEL), F32),
        scratch_shapes=[pltpu.VMEM((2, TOP_K, ROW_GROUPS, ROW_SUB, SUBLANES, LANES), jnp.int32),
                        pltpu.VMEM((ROW_TILE, D_MODEL), F32), pltpu.SemaphoreType.DMA((2,))],
        compiler_params=_cparams(("arbitrary",)),
        name="combine",
    )(dest_tiles, dest_tiles, ys, gates, x, lnw, lnb)


def _layer(x, mem, w_in, attn_sinks, attn_norm_w, ret_norm_w, w_mix_out, ln_mix_w, ln_mix_b,
           w_mem_q, w_mem_kv, w_mem_out, ln_mem_w, ln_mem_b, w_router, b_router,
           w_gate_up, b_gate_up, w_down, b_down, ln_moe_w, ln_moe_b):
    o_k, o_v, o_qr = D_ATTN, D_ATTN + D_KV, D_ATTN + 2 * D_KV
    w_in_p = jnp.concatenate([w_in[:, :o_k], w_in[:, o_qr:], w_in[:, o_k:o_v], w_in[:, o_v:o_qr]],
                             axis=1).astype(BF16)
    proj = _project(x, w_in_p, PROJ_TM, PROJ_TN)
    y = _mixer(proj, attn_sinks.astype(F32), attn_norm_w.reshape(1, D_ATTN), ret_norm_w.reshape(1, D_RET))
    x1 = _out_ln(y, w_mix_out.astype(BF16), x, ln_mix_w.reshape(1, D_MODEL), ln_mix_b.reshape(1, D_MODEL))

    kv = _project(mem.reshape(BATCH * N_MEM, D_MODEL), w_mem_kv.astype(BF16), BATCH * N_MEM, D_MODEL)
    o = _mem_attn(x1, w_mem_q.astype(BF16), kv)
    x2 = _out_ln(o, w_mem_out.astype(BF16), x1, ln_mem_w.reshape(1, D_MODEL), ln_mem_b.reshape(1, D_MODEL))

    wr = jnp.pad(w_router, ((0, 0), (0, LANES - N_EXPERTS)))
    wr_hi = wr.astype(BF16)
    wr = jnp.concatenate([wr_hi, (wr - wr_hi.astype(F32)).astype(BF16)], axis=1)
    br =jnp.pad(b_router.reshape(1, N_EXPERTS), ((0, 0), (0, LANES - N_EXPERTS)), constant_values=NEG_INF)
    idx, gates, rank, counts = _route(x2, wr, br)

    counts = counts[0, :N_EXPERTS].astype(jnp.int32)
    padded = ((counts + MOE_TM - 1) // MOE_TM) * MOE_TM
    eid = jnp.arange(N_EXPERTS, dtype=jnp.int32)
    pad_end = jnp.sum(jnp.where(eid[None, :] <= eid[:, None], padded[None, :], 0), axis=1)
    pad_start = pad_end - padded
    chosen = idx[:, :TOP_K, None] == eid[None, None, :]
    dest = jnp.sum(jnp.where(chosen, pad_start[None, None, :], 0), axis=-1) + rank[:, :TOP_K]
    visit_row = jnp.arange(N_VISITS, dtype=jnp.int32) * MOE_TM
    visit_expert = jnp.minimum(jnp.sum((pad_end[None, :] <= visit_row[:, None]).astype(jnp.int32), axis=1),
                               N_EXPERTS - 1)
    visit_nvalid = jnp.clip(counts[visit_expert] - (visit_row - pad_start[visit_expert]),
                            0, MOE_TM).astype(jnp.int32)

    fill_start = jnp.concatenate([pad_start + counts, pad_end[-1:]]).astype(jnp.int32)
    fill_len = jnp.concatenate([padded - counts, (MOE_ROWS - pad_end[-1:]) // FILL_SIZES[0]]).astype(jnp.int32)
    xs = _scatter_rows(x2, dest, fill_start, fill_len)
    ys = _experts(xs, visit_expert, visit_nvalid,
                  w_gate_up, b_gate_up, w_down, b_down)
    return _combine(ys, dest, gates, x2, ln_moe_w.reshape(1, D_MODEL), ln_moe_b.reshape(1, D_MODEL))


def kernel(x, mem, w_in, attn_sinks, attn_norm_w, ret_norm_w, w_mix_out, ln_mix_w, ln_mix_b, w_mem_q, w_mem_kv, w_mem_out, ln_mem_w, ln_mem_b, w_router, b_router, w_gate_up, b_gate_up, w_down, b_down, ln_moe_w, ln_moe_b):
    xt = x.reshape(TOKENS, D_MODEL)
    out = _layer(xt, mem, w_in[0], attn_sinks[0], attn_norm_w[0], ret_norm_w[0], w_mix_out[0],
                 ln_mix_w[0], ln_mix_b[0], w_mem_q[0], w_mem_kv[0], w_mem_out[0], ln_mem_w[0], ln_mem_b[0],
                 w_router[0], b_router[0], w_gate_up[0], b_gate_up[0], w_down[0], b_down[0],
                 ln_moe_w[0], ln_moe_b[0])
    return out.reshape(BATCH, SEQ, D_MODEL)
```

```python
import functools
import math

import jax
import jax.numpy as jnp
from jax import lax
from jax.experimental import pallas as pl
from jax.experimental.pallas import tpu as pltpu

F32 = jnp.float32
BF16 = jnp.bfloat16

D_MODEL = 2048
BATCH = 2
SEQ = 16384
TOKENS = BATCH * SEQ

HEAD_DIM_ATTN = 64
N_Q_ATTN = 16
N_KV_ATTN = 2
GQA_GROUP = N_Q_ATTN // N_KV_ATTN
ATTN_BLOCK = 128
D_ATTN = N_Q_ATTN * HEAD_DIM_ATTN
D_KV = N_KV_ATTN * HEAD_DIM_ATTN
N_RET = 8
HEAD_DIM_RET = 128
RET_CHUNK = 128
D_RET = N_RET * HEAD_DIM_RET
D_IN = D_ATTN + 2 * D_KV + 4 * D_RET
N_MEM = 256
N_MEM_HEADS = 4
HEAD_DIM_MEM = D_MODEL // N_MEM_HEADS
N_EXPERTS = 32
TOP_K = 4
D_EXPERT = D_MODEL
SWIGLU_LIMIT = 7.0
SWIGLU_ALPHA = 1.702
LN_EPS = 1e-5
DN_ALPHA = 2.0 ** 0.25
NEG_INF = -1e30

LANES = 128
SUBLANES = 8
V7X_VMEM_BYTES = 64 * 1024 * 1024
VMEM_LIMIT = V7X_VMEM_BYTES - 6 * 1024 * 1024

PROJ_TM = 1024
PROJ_TN = 1792
LN_TM = 512
MEM_TM = 512
ROUTE_TM = 512
ROW_TILE = 256
MOE_TM = 1024
MOE_TF = 512
N_VISITS = (TOKENS * TOP_K + N_EXPERTS * (MOE_TM - 1) + MOE_TM - 1) // MOE_TM
MOE_ROWS = N_VISITS * MOE_TM

COL_QA, COL_QR, COL_KR, COL_VR, COL_GR = 0, 1, 2, 3, 4
COL_KA = 5 * D_RET // D_KV
COL_VA = COL_KA + 1


def _cparams(sem, **kw):
    return pltpu.CompilerParams(dimension_semantics=sem, vmem_limit_bytes=VMEM_LIMIT, **kw)


def _layer_norm(z, w, b):
    mu = jnp.mean(z, axis=-1, keepdims=True)
    zc = z - mu
    var = jnp.mean(zc * zc, axis=-1, keepdims=True)
    return zc * lax.rsqrt(var + LN_EPS) * w + b


def _dot_nt(a, b):
    return lax.dot_general(a, b, (((1,), (1,)), ((), ())), preferred_element_type=F32)


def _proj_kernel(x_ref, w_ref, o_ref, xb_ref):
    @pl.when(pl.program_id(1) == 0)
    def _():
        xb_ref[...] = x_ref[...].astype(BF16)

    o_ref[...] = jnp.dot(xb_ref[...], w_ref[...], preferred_element_type=F32).astype(o_ref.dtype)


def _project(x, w, tm, tn):
    m, k = x.shape
    n = w.shape[1]
    return pl.pallas_call(
        _proj_kernel,
        grid=(m // tm, n // tn),
        in_specs=[pl.BlockSpec((tm, k), lambda i, j: (i, 0)),
                  pl.BlockSpec((k, tn), lambda i, j: (0, j))],
        out_specs=pl.BlockSpec((tm, tn), lambda i, j: (i, j)),
        out_shape=jax.ShapeDtypeStruct((m, n), BF16),
        scratch_shapes=[pltpu.VMEM((tm, k), BF16)],
        compiler_params=_cparams(("parallel", "arbitrary")),
        name="project",
    )(x, w)


def _mixer_kernel(sinkrow_ref, bias_cur_ref, bias_prev_ref, decay_ref, qa_ref, kp_ref, kc_ref, vp_ref, vc_ref,
                  qr_ref, kr_ref, vr_ref, gr_ref, anw_ref, rnw_ref, y_ref, state_ref, a_ref):
    n = pl.program_id(1)

    scale = HEAD_DIM_ATTN ** -0.5
    for j in range(N_KV_ATTN):
        klo, khi = j * HEAD_DIM_ATTN, (j + 1) * HEAD_DIM_ATTN
        heads = range(j * GQA_GROUP, (j + 1) * GQA_GROUP)
        q_rows = jnp.concatenate([qa_ref[:, h * HEAD_DIM_ATTN:(h + 1) * HEAD_DIM_ATTN] for h in heads], axis=0)
        q_rows = q_rows * scale
        sink = sinkrow_ref[j]
        s_cur = _dot_nt(kc_ref[:, klo:khi], q_rows) + bias_cur_ref[j]
        s_prev = _dot_nt(kp_ref[:, klo:khi], q_rows) + bias_prev_ref[j]
        s_prev = jnp.where(n > 0, s_prev, NEG_INF)
        m = jnp.maximum(jnp.max(s_cur, axis=0, keepdims=True), jnp.max(s_prev, axis=0, keepdims=True))
        m = jnp.maximum(m, sink)
        p_cur = jnp.exp(s_cur - m)
        p_prev = jnp.exp(s_prev - m)
        denom = (jnp.sum(p_cur, axis=0, keepdims=True) + jnp.sum(p_prev, axis=0, keepdims=True)
                 + jnp.exp(sink - m))
        v_cur_t = vc_ref[:, klo:khi].astype(F32).T.astype(BF16)
        v_prev_t = vp_ref[:, klo:khi].astype(F32).T.astype(BF16)
        o_t = (jnp.dot(v_cur_t, p_cur.astype(BF16), preferred_element_type=F32)
               + jnp.dot(v_prev_t, p_prev.astype(BF16), preferred_element_type=F32))
        o_t = o_t * (1.0 / denom)
        for g, h in enumerate(heads):
            a_ref[h * HEAD_DIM_ATTN:(h + 1) * HEAD_DIM_ATTN, :] = o_t[:, g * ATTN_BLOCK:(g + 1) * ATTN_BLOCK]
    a_t = a_ref[...]
    a_t = a_t * lax.rsqrt(jnp.mean(a_t * a_t, axis=0, keepdims=True) + LN_EPS)
    y_ref[:, :D_ATTN] = (a_t.T * anw_ref[...]).astype(y_ref.dtype)

    @pl.when(n == 0)
    def _():
        state_ref[...] = jnp.zeros_like(state_ref)

    for h in range(N_RET):
        lo, hi = h * HEAD_DIM_RET, (h + 1) * HEAD_DIM_RET
        g_c = math.exp(math.log1p(-(2.0 ** (-5.0 - h))) * RET_CHUNK)
        q = qr_ref[:, lo:hi]
        k = kr_ref[:, lo:hi]
        v = vr_ref[:, lo:hi]
        s = _dot_nt(q, k) * decay_ref[h]
        inner = jnp.dot(s.astype(BF16), v, preferred_element_type=F32)
        state = state_ref[h]
        qw = (q.astype(F32) * decay_ref[N_RET + h]).astype(BF16)
        cross = jnp.dot(qw, state.astype(BF16), preferred_element_type=F32)
        kw_t = (k.astype(F32) * decay_ref[2 * N_RET + h]).T.astype(BF16)
        state_ref[h] = g_c * state + jnp.dot(kw_t, v, preferred_element_type=F32)
        o = inner + cross
        mu = jnp.mean(o, axis=-1, keepdims=True)
        oc = o - mu
        var = jnp.mean(oc * oc, axis=-1, keepdims=True)
        o = oc * lax.rsqrt(var + LN_EPS) * rnw_ref[:, lo:hi]
        g = gr_ref[:, lo:hi].astype(F32)
        o = o * (g * jax.nn.sigmoid(g))
        y_ref[:, D_ATTN + lo:D_ATTN + hi] = o.astype(y_ref.dtype)


def _mixer(proj, sinks, attn_norm_w, ret_norm_w):
    nc = SEQ // ATTN_BLOCK
    blk = ATTN_BLOCK

    def wide(colblock):
        return pl.BlockSpec((blk, D_RET), lambda b, n: (b * nc + n, colblock))

    def kv_cur(colblock):
        return pl.BlockSpec((blk, D_KV), lambda b, n: (b * nc + n, colblock))

    def kv_prev(colblock):
        return pl.BlockSpec((blk, D_KV), lambda b, n: (b * nc + jnp.maximum(n - 1, 0), colblock))

    vec = pl.BlockSpec((1, D_RET), lambda b, n: (0, 0))
    gq = GQA_GROUP * blk

    def table(rows):
        return pl.BlockSpec((N_KV_ATTN, rows, gq), lambda b, n: (0, 0, 0))

    slopes = jnp.exp2(-8.0 * (jnp.arange(N_Q_ATTN, dtype=F32) + 1.0) / N_Q_ATTN).reshape(N_KV_ATTN, 1, GQA_GROUP, 1)
    key = jnp.arange(blk, dtype=F32).reshape(1, blk, 1, 1)
    query = jnp.arange(blk, dtype=F32).reshape(1, 1, 1, blk)
    dist_cur = jnp.broadcast_to(query - key, (N_KV_ATTN, blk, GQA_GROUP, blk))
    bias_cur = jnp.where(dist_cur >= 0, -slopes * dist_cur, NEG_INF).reshape(N_KV_ATTN, blk, gq)
    bias_prev = jnp.where(dist_cur < 0, -slopes * (dist_cur + blk), NEG_INF).reshape(N_KV_ATTN, blk, gq)
    sink_rows = jnp.repeat(sinks.astype(F32).reshape(N_KV_ATTN, 1, GQA_GROUP), blk, axis=2)

    log_g = jnp.log1p(-jnp.exp2(-5.0 - jnp.arange(N_RET, dtype=F32))).reshape(N_RET, 1, 1)
    pos = jnp.arange(RET_CHUNK, dtype=F32)
    lag = pos[:, None] - pos[None, :]
    kscale = HEAD_DIM_RET ** -0.5
    in_chunk = jnp.where(lag >= 0, jnp.exp(log_g * jnp.maximum(lag, 0.0)), 0.0) * kscale
    ones = jnp.ones((1, 1, HEAD_DIM_RET), F32)
    w_query = jnp.exp(log_g * (pos + 1.0).reshape(1, RET_CHUNK, 1)) * ones
    w_key = jnp.exp(log_g * (RET_CHUNK - 1.0 - pos).reshape(1, RET_CHUNK, 1)) * (kscale * ones)
    decay = jnp.concatenate([in_chunk, w_query, w_key], axis=0)
    return pl.pallas_call(
        _mixer_kernel,
        grid=(BATCH, nc),
        in_specs=[table(1), table(blk), table(blk),
                  pl.BlockSpec((3 * N_RET, RET_CHUNK, HEAD_DIM_RET), lambda b, n: (0, 0, 0)),
                  wide(COL_QA), kv_prev(COL_KA), kv_cur(COL_KA), kv_prev(COL_VA), kv_cur(COL_VA),
                  wide(COL_QR), wide(COL_KR), wide(COL_VR), wide(COL_GR), vec, vec],
        out_specs=pl.BlockSpec((blk, D_MODEL), lambda b, n: (b * nc + n, 0)),
        out_shape=jax.ShapeDtypeStruct((TOKENS, D_MODEL), BF16),
        scratch_shapes=[pltpu.VMEM((N_RET, HEAD_DIM_RET, HEAD_DIM_RET), F32),
                        pltpu.VMEM((D_ATTN, blk), F32)],
        compiler_params=_cparams(("parallel", "arbitrary")),
        name="mixer",
    )(sink_rows, bias_cur, bias_prev, decay, proj, proj, proj, proj, proj, proj, proj, proj, proj,
      attn_norm_w, ret_norm_w)


def _out_ln_kernel(y_ref, w_ref, x_ref, lnw_ref, lnb_ref, o_ref):
    h = jnp.dot(y_ref[...], w_ref[...], preferred_element_type=F32)
    o_ref[...] = _layer_norm(DN_ALPHA * x_ref[...] + h, lnw_ref[...], lnb_ref[...])


def _out_ln(y, w, x, lnw, lnb):
    tm = LN_TM
    row = pl.BlockSpec((tm, D_MODEL), lambda i: (i, 0))
    vec = pl.BlockSpec((1, D_MODEL), lambda i: (0, 0))
    return pl.pallas_call(
        _out_ln_kernel,
        grid=(TOKENS // tm,),
        in_specs=[row,
                  pl.BlockSpec((D_MODEL, D_MODEL), lambda i: (0, 0), pipeline_mode=pl.Buffered(1)),
                  row, vec, vec],
        out_specs=row,
        out_shape=jax.ShapeDtypeStruct((TOKENS, D_MODEL), F32),
        compiler_params=_cparams(("parallel",)),
        name="out_ln",
    )(y, w, x, lnw, lnb)


def _mem_attn_kernel(x_ref, wq_ref, k_ref, v_ref, o_ref):
    q = jnp.dot(x_ref[...].astype(BF16), wq_ref[...], preferred_element_type=F32)
    q = (q * (HEAD_DIM_MEM ** -0.5)).astype(BF16)
    for h in range(N_MEM_HEADS):
        lo, hi = h * HEAD_DIM_MEM, (h + 1) * HEAD_DIM_MEM
        s = _dot_nt(q[:, lo:hi], k_ref[:, lo:hi])
        m = jnp.max(s, axis=-1, keepdims=True)
        p = jnp.exp(s - m)
        denom = jnp.sum(p, axis=-1, keepdims=True)
        o = jnp.dot(p.astype(BF16), v_ref[:, lo:hi], preferred_element_type=F32)
        o_ref[:, lo:hi] = (o / denom).astype(o_ref.dtype)


def _mem_attn(x, wq, kv):
    tm = MEM_TM
    nt = SEQ // tm
    row = pl.BlockSpec((tm, D_MODEL), lambda b, i: (b * nt + i, 0))
    return pl.pallas_call(
        _mem_attn_kernel,
        grid=(BATCH, nt),
        in_specs=[row,
                  pl.BlockSpec((D_MODEL, D_MODEL), lambda b, i: (0, 0), pipeline_mode=pl.Buffered(1)),
                  pl.BlockSpec((N_MEM, D_MODEL), lambda b, i: (b, 0)),
                  pl.BlockSpec((N_MEM, D_MODEL), lambda b, i: (b, 1))],
        out_specs=row,
        out_shape=jax.ShapeDtypeStruct((TOKENS, D_MODEL), BF16),
        compiler_params=_cparams(("parallel", "parallel")),
        name="mem_attn",
    )(x, wq, kv, kv)


def _route_kernel(x_ref, wr_ref, br_ref, idx_ref, gate_ref, rank_ref, count_ref, carry_ref):
    tm = x_ref.shape[0]

    @pl.when(pl.program_id(0) == 0)
    def _():
        carry_ref[...] = jnp.zeros_like(carry_ref)

    x = x_ref[...]
    x_hi = x.astype(BF16)
    x_lo = (x - x_hi.astype(F32)).astype(BF16)
    part_hi = jnp.dot(x_hi, wr_ref[...], preferred_element_type=F32)
    part_lo = jnp.dot(x_lo, wr_ref[...], preferred_element_type=F32)
    logits = (part_hi[:, :LANES] + (part_hi[:, LANES:] + part_lo[:, :LANES]) + part_lo[:, LANES:]
              + br_ref[...])
    lane = lax.broadcasted_iota(jnp.int32, (tm, LANES), 1)
    lane_f = lane.astype(F32)
    work = logits
    sels, vals, idxs = [], [], []
    for _ in range(TOP_K):
        v = jnp.max(work, axis=-1, keepdims=True)
        i = jnp.min(jnp.where(work == v, lane_f, float(LANES)), axis=-1, keepdims=True)
        sel = lane_f == i
        work = jnp.where(sel, -jnp.inf, work)
        sels.append(sel)
        vals.append(v)
        idxs.append(i)
    exps = [jnp.exp(v - vals[0]) for v in vals]
    total = exps[0] + exps[1] + exps[2] + exps[3]

    onehot = jnp.zeros((tm, LANES), F32)
    for sel in sels:
        onehot = jnp.where(sel, 1.0, onehot)
    r = lax.broadcasted_iota(jnp.int32, (tm, tm), 0)
    c = lax.broadcasted_iota(jnp.int32, (tm, tm), 1)
    below = jnp.where(c < r, 1.0, 0.0).astype(BF16)
    prefix = jnp.dot(below, onehot.astype(BF16), preferred_element_type=F32)
    rankmat = prefix + carry_ref[...]
    carry_ref[...] = carry_ref[...] + jnp.sum(onehot, axis=0, keepdims=True)
    count_ref[...] = carry_ref[...]

    idx_out = jnp.zeros((tm, LANES), jnp.int32)
    gate_out = jnp.zeros((tm, LANES), F32)
    rank_out = jnp.zeros((tm, LANES), jnp.int32)
    for k in range(TOP_K):
        rk = jnp.sum(jnp.where(sels[k], rankmat, 0.0), axis=-1, keepdims=True).astype(jnp.int32)
        idx_out = jnp.where(lane == k, idxs[k].astype(jnp.int32), idx_out)
        gate_out = jnp.where(lane == k, exps[k] / total, gate_out)
        rank_out = jnp.where(lane == k, rk, rank_out)
    idx_ref[...] = idx_out
    gate_ref[...] = gate_out
    rank_ref[...] = rank_out


def _route(x, wr, br):
    tm = ROUTE_TM
    wide = pl.BlockSpec((tm, LANES), lambda i: (i, 0))
    return pl.pallas_call(
        _route_kernel,
        grid=(TOKENS // tm,),
        in_specs=[pl.BlockSpec((tm, D_MODEL), lambda i: (i, 0)),
                  pl.BlockSpec((D_MODEL, 2 * LANES), lambda i: (0, 0)),
                  pl.BlockSpec((1, LANES), lambda i: (0, 0))],
        out_specs=[wide, wide, wide, pl.BlockSpec((1, LANES), lambda i: (0, 0))],
        out_shape=[jax.ShapeDtypeStruct((TOKENS, LANES), jnp.int32),
                   jax.ShapeDtypeStruct((TOKENS, LANES), F32),
                   jax.ShapeDtypeStruct((TOKENS, LANES), jnp.int32),
                   jax.ShapeDtypeStruct((1, LANES), F32)],
        scratch_shapes=[pltpu.VMEM((1, LANES), F32)],
        compiler_params=_cparams(("arbitrary",)),
        name="route",
    )(x, wr, br)


HALF = D_MODEL // 2


def _pack_rows(x):
    return _pack_pair(x[:, :HALF], x[:, HALF:])


def _pack_pair(lo, hi):
    lo = lax.bitcast_convert_type(lo.astype(BF16).astype(F32), jnp.int32)
    hi = lax.bitcast_convert_type(hi.astype(BF16).astype(F32), jnp.int32)
    return hi | lax.shift_right_logical(lo, 16)


def _unpack_rows(w):
    lo = lax.bitcast_convert_type(lax.shift_left(w, 16), F32)
    hi = lax.bitcast_convert_type(w & jnp.int32(-65536), F32)
    return lo, hi


def _wait_row_copies(rows_ref, sem):
    pltpu.make_async_copy(rows_ref.at[pl.ds(0, ROW_TILE)], rows_ref.at[pl.ds(0, ROW_TILE)], sem).wait()


ROW_SUB = HALF // LANES
ROW_GROUPS = ROW_TILE // SUBLANES


FILL_SIZES = tuple(MOE_TM >> (b + 1) for b in range(MOE_TM.bit_length() - 1))


def _fill_copies(fill_start_ref, fill_len_ref, zero_ref, xs_ref, sem, act):
    def body(e, carry):
        pos = fill_start_ref[e]
        n = fill_len_ref[e]
        for size in FILL_SIZES:
            take = (n & size) != 0

            @pl.when(take)
            def _():
                cp = pltpu.make_async_copy(zero_ref.at[pl.ds(0, size)], xs_ref.at[pl.ds(pos, size)], sem)
                getattr(cp, act)()

            pos = pos + jnp.where(take, size, 0)
        return carry

    lax.fori_loop(0, N_EXPERTS, body, 0)

    def tail(j, carry):
        pos = fill_start_ref[N_EXPERTS] + j * FILL_SIZES[0]
        cp = pltpu.make_async_copy(zero_ref, xs_ref.at[pl.ds(pos, FILL_SIZES[0])], sem)
        getattr(cp, act)()
        return carry

    lax.fori_loop(0, fill_len_ref[N_EXPERTS], tail, 0)


def _scatter_kernel(fill_start_ref, fill_len_ref, dest_ref, x_ref, xs_ref, packed_ref, zero_ref, sem, fill_sem):
    i = pl.program_id(0)
    slot = i % 2

    @pl.when(i == 0)
    def _():
        zero_ref[...] = jnp.zeros_like(zero_ref)
        _fill_copies(fill_start_ref, fill_len_ref, zero_ref, xs_ref, fill_sem, "start")

    words = _pack_rows(x_ref[...])
    for g in range(ROW_GROUPS):
        for c in range(ROW_SUB):
            packed_ref[slot, g, c] = words[g * SUBLANES:(g + 1) * SUBLANES, c * LANES:(c + 1) * LANES]

    def body(g, carry):
        for s in range(SUBLANES):
            for k in range(TOP_K):
                d = dest_ref[0, 0, (g * SUBLANES + s) * TOP_K + k]
                pltpu.make_async_copy(packed_ref.at[slot, g, :, s, :], xs_ref.at[d],
                                      sem.at[slot]).start(priority=k % 2)
        return carry

    lax.fori_loop(0, ROW_GROUPS, body, 0)

    @pl.when(i > 0)
    def _():
        for k in range(TOP_K):
            _wait_row_copies(xs_ref, sem.at[1 - slot])

    @pl.when(i == pl.num_programs(0) - 1)
    def _():
        for k in range(TOP_K):
            _wait_row_copies(xs_ref, sem.at[slot])

    @pl.when(i == 0)
    def _():
        _fill_copies(fill_start_ref, fill_len_ref, zero_ref, xs_ref, fill_sem, "wait")


def _scatter_rows(x, dest, fill_start, fill_len):
    nt = TOKENS // ROW_TILE
    grid_spec = pltpu.PrefetchScalarGridSpec(
        num_scalar_prefetch=2,
        grid=(nt,),
        in_specs=[pl.BlockSpec((1, 1, ROW_TILE * TOP_K), lambda i, fs, fl: (i, 0, 0), memory_space=pltpu.SMEM),
                  pl.BlockSpec((ROW_TILE, D_MODEL), lambda i, fs, fl: (i, 0))],
        out_specs=pl.BlockSpec(memory_space=pl.ANY),
        scratch_shapes=[pltpu.VMEM((2, ROW_GROUPS, ROW_SUB, SUBLANES, LANES), jnp.int32),
                        pltpu.VMEM((FILL_SIZES[0], ROW_SUB, LANES), jnp.int32),
                        pltpu.SemaphoreType.DMA((2,)), pltpu.SemaphoreType.DMA(())],
    )
    return pl.pallas_call(
        _scatter_kernel,
        grid_spec=grid_spec,
        out_shape=jax.ShapeDtypeStruct((MOE_ROWS, ROW_SUB, LANES), jnp.int32),
        compiler_params=_cparams(("arbitrary",), has_side_effects=True),
        name="scatter_rows",
    )(fill_start, fill_len, dest.reshape(nt, 1, ROW_TILE * TOP_K), x)


MOE_NF = D_EXPERT // MOE_TF
MOE_ROW_STEP = 256


def _experts_kernel(ge_ref, gn_ref, de_ref, dn_ref, xs_ref, wg_ref, wu_ref, wd_ref, bgu_ref, bd_ref,
                    ys_ref, xb_ref, act_ref, hold_ref, words_ref, ow_ref, pend_ref, xsem, ysem):
    v = pl.program_id(0)
    f = pl.program_id(1)
    gn = gn_ref[v]
    dn = dn_ref[v]
    slot = v % 2
    half_nf = MOE_NF // 2

    sub_per_tile = MOE_TF // LANES

    def ys_copies(visit, tile):
        row0 = pl.multiple_of(visit * MOE_TM, MOE_TM)
        return [pltpu.make_async_copy(ow_ref.at[:, j * LANES:(j + 1) * LANES],
                                      ys_ref.at[pl.ds(row0, MOE_TM), tile * sub_per_tile + j, :], ysem)
                for j in range(sub_per_tile)]

    def emit_output_tile():
        for cp in ys_copies(v - 1, f - half_nf):
            cp.start()
        pend_ref[0] = 1

    def drain_output_tile():
        @pl.when(pend_ref[0] == 1)
        def _():
            for cp in ys_copies(0, 0):
                cp.wait()
            pend_ref[0] = 0

    @pl.when(jnp.logical_and(v == 0, f == 0))
    def _():
        hold_ref[...] = jnp.zeros_like(hold_ref)
        ow_ref[...] = jnp.zeros_like(ow_ref)
        pend_ref[0] = 0

    def xs_copies(visit, s):
        row0 = pl.multiple_of(visit * MOE_TM, MOE_TM)
        return [pltpu.make_async_copy(xs_ref.at[pl.ds(row0, MOE_TM), c, :],
                                      words_ref.at[s, :, c * LANES:(c + 1) * LANES], xsem.at[s])
                for c in range(ROW_SUB)]

    @pl.when(f == 0)
    def _():
        @pl.when(v == 0)
        def _():
            for cp in xs_copies(0, 0):
                cp.start()

        @pl.when(v < N_VISITS)
        def _():
            for cp in xs_copies(v, slot):
                cp.wait()

        @pl.when(v + 1 < N_VISITS)
        def _():
            for cp in xs_copies(v + 1, 1 - slot):
                cp.start()

    @pl.when(jnp.logical_and(gn > 0, f == 0))
    def _():
        r = lax.broadcasted_iota(jnp.int32, (MOE_TM, 1), 0)
        lo, hi = _unpack_rows(jnp.where(r < gn, words_ref[slot], 0))
        xb_ref[:, :HALF] = lo.astype(BF16)
        xb_ref[:, HALF:] = hi.astype(BF16)

    def gate_phase(rows):
        xb = xb_ref[:rows, :]
        e = pl.ds(ge_ref[v], 1)
        col = pl.multiple_of(f * MOE_TF, MOE_TF)
        gl = jnp.dot(xb, wg_ref[...].astype(BF16), preferred_element_type=F32) + bgu_ref[e, pl.ds(col, MOE_TF)]
        up = (jnp.dot(xb, wu_ref[...].astype(BF16), preferred_element_type=F32)
              + bgu_ref[e, pl.ds(D_EXPERT + col, MOE_TF)])
        gl = jnp.minimum(gl, SWIGLU_LIMIT)
        up = jnp.clip(up, -SWIGLU_LIMIT, SWIGLU_LIMIT)
        act = ((up + 1.0) * (gl * jax.nn.sigmoid(SWIGLU_ALPHA * gl))).astype(BF16)
        act_ref[slot, :rows, pl.ds(pl.multiple_of(f * MOE_TF, MOE_TF), MOE_TF)] = act

    def down_phase(rows):
        drain_output_tile()
        y = jnp.dot(act_ref[1 - slot, :rows, :], wd_ref[...].astype(BF16),
                    preferred_element_type=F32) + bd_ref[pl.ds(de_ref[v], 1),
                                                         pl.ds(pl.multiple_of(f * MOE_TF, MOE_TF), MOE_TF)]
        h = f % half_nf
        ow_ref[:rows, :] = _pack_pair(hold_ref[h, :rows, :], y)
        hold_ref[h, :rows, :] = y
        pl.when(f >= half_nf)(emit_output_tile)

    for rows in range(MOE_TM, 0, -MOE_ROW_STEP):
        lower = rows - MOE_ROW_STEP
        pl.when(jnp.logical_and(gn > lower, gn <= rows))(functools.partial(gate_phase, rows))
        pl.when(jnp.logical_and(dn > lower, dn <= rows))(functools.partial(down_phase, rows))

    @pl.when(jnp.logical_and(jnp.logical_and(dn == 0, v >= 1), f >= half_nf))
    def _():
        drain_output_tile()
        ow_ref[...] = jnp.zeros_like(ow_ref)
        emit_output_tile()

    @pl.when(jnp.logical_and(v == pl.num_programs(0) - 1, f == MOE_NF - 1))
    def _():
        drain_output_tile()


def _experts(xs, visit_expert, visit_nvalid, w_gate_up, b_gate_up, w_down, b_down):
    nf = MOE_NF
    gate_e = jnp.concatenate([visit_expert, visit_expert[-1:]])
    gate_n = jnp.concatenate([visit_nvalid, jnp.zeros((1,), jnp.int32)])
    down_e = jnp.concatenate([visit_expert[:1], visit_expert])
    down_n = jnp.concatenate([jnp.zeros((1,), jnp.int32), visit_nvalid])

    def tile(f, n):
        return jnp.where(n > 0, f, nf - 1)

    grid_spec = pltpu.PrefetchScalarGridSpec(
        num_scalar_prefetch=4,
        grid=(N_VISITS + 1, nf),
        in_specs=[
            pl.BlockSpec(memory_space=pl.ANY),
            pl.BlockSpec((None, D_MODEL, MOE_TF), lambda v, f, ge, gn, de, dn: (ge[v], 0, tile(f, gn[v]))),
            pl.BlockSpec((None, D_MODEL, MOE_TF), lambda v, f, ge, gn, de, dn: (ge[v], 0, nf + tile(f, gn[v]))),
            pl.BlockSpec((None, D_EXPERT, MOE_TF), lambda v, f, ge, gn, de, dn: (de[v], 0, tile(f, dn[v]))),
            pl.BlockSpec((N_EXPERTS, 2 * D_EXPERT), lambda v, f, ge, gn, de, dn: (0, 0)),
            pl.BlockSpec((N_EXPERTS, D_MODEL), lambda v, f, ge, gn, de, dn: (0, 0)),
        ],
        out_specs=pl.BlockSpec(memory_space=pl.ANY),
        scratch_shapes=[pltpu.VMEM((MOE_TM, D_MODEL), BF16),
                        pltpu.VMEM((2, MOE_TM, D_EXPERT), BF16),
                        pltpu.VMEM((nf // 2, MOE_TM, MOE_TF), F32),
                        pltpu.VMEM((2, MOE_TM, HALF), jnp.int32),
                        pltpu.VMEM((MOE_TM, MOE_TF), jnp.int32),
                        pltpu.SMEM((1,), jnp.int32),
                        pltpu.SemaphoreType.DMA((2,)),
                        pltpu.SemaphoreType.DMA(())],
    )
    return pl.pallas_call(
        _experts_kernel,
        grid_spec=grid_spec,
        out_shape=jax.ShapeDtypeStruct((MOE_ROWS, ROW_SUB, LANES), jnp.int32),
        compiler_params=_cparams(("arbitrary", "arbitrary")),
        name="experts",
    )(gate_e, gate_n, down_e, down_n, xs, w_gate_up, w_gate_up, w_down, b_gate_up, b_down)


def _combine_kernel(dest_ref, dest_next_ref, ys_ref, gate_ref, x_ref, lnw_ref, lnb_ref, o_ref,
                    buf_ref, z_ref, sem):
    i = pl.program_id(0)
    slot = i % 2

    def start_group(idx_ref, s, g):
        for t in range(SUBLANES):
            for k in range(TOP_K):
                d = idx_ref[0, 0, (g * SUBLANES + t) * TOP_K + k]
                pltpu.make_async_copy(ys_ref.at[d], buf_ref.at[s, k, g, :, t, :],
                                      sem.at[s]).start(priority=k % 2)

    @pl.when(i == 0)
    def _():
        def body(g, carry):
            start_group(dest_ref, slot, g)
            return carry

        lax.fori_loop(0, ROW_GROUPS, body, 0)

    for k in range(TOP_K):
        _wait_row_copies(ys_ref, sem.at[slot])

    def group(g, carry, prefetch):
        if prefetch:
            start_group(dest_next_ref, 1 - slot, g)
        rows = pl.ds(pl.multiple_of(g * SUBLANES, SUBLANES), SUBLANES)
        gates = gate_ref[rows, :]
        gate_k = [jnp.broadcast_to(gates[:, k:k + 1], (SUBLANES, LANES)) for k in range(TOP_K)]
        for c in range(ROW_SUB):
            lo_cols = slice(c * LANES, (c + 1) * LANES)
            hi_cols = slice(HALF + c * LANES, HALF + (c + 1) * LANES)
            z_lo = DN_ALPHA * x_ref[rows, lo_cols]
            z_hi = DN_ALPHA * x_ref[rows, hi_cols]
            for k in range(TOP_K):
                lo, hi = _unpack_rows(buf_ref[slot, k, g, c])
                z_lo = z_lo + gate_k[k] * lo
                z_hi = z_hi + gate_k[k] * hi
            z_ref[rows, lo_cols] = z_lo
            z_ref[rows, hi_cols] = z_hi
        return carry

    last = i + 1 == pl.num_programs(0)

    @pl.when(jnp.logical_not(last))
    def _():
        lax.fori_loop(0, ROW_GROUPS, functools.partial(group, prefetch=True), 0)

    @pl.when(last)
    def _():
        lax.fori_loop(0, ROW_GROUPS, functools.partial(group, prefetch=False), 0)

    o_ref[...] = _layer_norm(z_ref[...], lnw_ref[...], lnb_ref[...])


def _combine(ys, dest, gates, x, lnw, lnb):
    nt = TOKENS // ROW_TILE
    row = pl.BlockSpec((ROW_TILE, D_MODEL), lambda i: (i, 0))
    vec = pl.BlockSpec((1, D_MODEL), lambda i: (0, 0))
    dest_tiles = dest.reshape(nt, 1, ROW_TILE * TOP_K)
    return pl.pallas_call(
        _combine_kernel,
        grid=(nt,),
        in_specs=[pl.BlockSpec((1, 1, ROW_TILE * TOP_K), lambda i: (i, 0, 0), memory_space=pltpu.SMEM),
                  pl.BlockSpec((1, 1, ROW_TILE * TOP_K), lambda i: (jnp.minimum(i + 1, nt - 1), 0, 0),
                               memory_space=pltpu.SMEM),
                  pl.BlockSpec(memory_space=pl.ANY),
                  pl.BlockSpec((ROW_TILE, LANES), lambda i: (i, 0)),
                  row, vec, vec],
        out_specs=row,
        out_shape=jax.ShapeDtypeStruct((TOKENS, D_MODEL), F32),
        scratch_shapes=[pltpu.VMEM((2, TOP_K, ROW_GROUPS, ROW_SUB, SUBLANES, LANES), jnp.int32),
                        pltpu.VMEM((ROW_TILE, D_MODEL), F32), pltpu.SemaphoreType.DMA((2,))],
        compiler_params=_cparams(("arbitrary",)),
        name="combine",
    )(dest_tiles, dest_tiles, ys, gates, x, lnw, lnb)


def _layer(x, mem, w_in, attn_sinks, attn_norm_w, ret_norm_w, w_mix_out, ln_mix_w, ln_mix_b,
           w_mem_q, w_mem_kv, w_mem_out, ln_mem_w, ln_mem_b, w_router, b_router,
           w_gate_up, b_gate_up, w_down, b_down, ln_moe_w, ln_moe_b):
    o_k, o_v, o_qr = D_ATTN, D_ATTN + D_KV, D_ATTN + 2 * D_KV
    w_in_p = jnp.concatenate([w_in[:, :o_k], w_in[:, o_qr:], w_in[:, o_k:o_v], w_in[:, o_v:o_qr]],
                             axis=1).astype(BF16)
    proj = _project(x, w_in_p, PROJ_TM, PROJ_TN)
    y = _mixer(proj, attn_sinks.astype(F32), attn_norm_w.reshape(1, D_ATTN), ret_norm_w.reshape(1, D_RET))
    x1 = _out_ln(y, w_mix_out.astype(BF16), x, ln_mix_w.reshape(1, D_MODEL), ln_mix_b.reshape(1, D_MODEL))

    kv = _project(mem.reshape(BATCH * N_MEM, D_MODEL), w_mem_kv.astype(BF16), BATCH * N_MEM, D_MODEL)
    o = _mem_attn(x1, w_mem_q.astype(BF16), kv)
    x2 = _out_ln(o, w_mem_out.astype(BF16), x1, ln_mem_w.reshape(1, D_MODEL), ln_mem_b.reshape(1, D_MODEL))

    wr = jnp.pad(w_router, ((0, 0), (0, LANES - N_EXPERTS)))
    wr_hi = wr.astype(BF16)
    wr = jnp.concatenate([wr_hi, (wr - wr_hi.astype(F32)).astype(BF16)], axis=1)
    br = jnp.pad(b_router.reshape(1, N_EXPERTS), ((0, 0), (0, LANES - N_EXPERTS)), constant_values=NEG_INF)
    idx, gates, rank, counts = _route(x2, wr, br)

    counts = counts[0, :N_EXPERTS].astype(jnp.int32)
    padded = ((counts + MOE_TM - 1) // MOE_TM) * MOE_TM
    eid = jnp.arange(N_EXPERTS, dtype=jnp.int32)
    pad_end = jnp.sum(jnp.where(eid[None, :] <= eid[:, None], padded[None, :], 0), axis=1)
    pad_start = pad_end - padded
    chosen = idx[:, :TOP_K, None] == eid[None, None, :]
    dest = jnp.sum(jnp.where(chosen, pad_start[None, None, :], 0), axis=-1) + rank[:, :TOP_K]
    visit_row = jnp.arange(N_VISITS, dtype=jnp.int32) * MOE_TM
    visit_expert = jnp.minimum(jnp.sum((pad_end[None, :] <= visit_row[:, None]).astype(jnp.int32), axis=1),
                               N_EXPERTS - 1)
    visit_nvalid = jnp.clip(counts[visit_expert] - (visit_row - pad_start[visit_expert]),
                            0, MOE_TM).astype(jnp.int32)

    fill_start = jnp.concatenate([pad_start + counts, pad_end[-1:]]).astype(jnp.int32)
    fill_len = jnp.concatenate([padded - counts, (MOE_ROWS - pad_end[-1:]) // FILL_SIZES[0]]).astype(jnp.int32)
    xs = _scatter_rows(x2, dest, fill_start, fill_len)
    ys = _experts(xs, visit_expert, visit_nvalid,
                  w_gate_up, b_gate_up, w_down, b_down)
    return _combine(ys, dest, gates, x2, ln_moe_w.reshape(1, D_MODEL), ln_moe_b.reshape(1, D_MODEL))


def kernel(x, mem, w_in, attn_sinks, attn_norm_w, ret_norm_w, w_mix_out, ln_mix_w, ln_mix_b, w_mem_q, w_mem_kv, w_mem_out, ln_mem_w, ln_mem_b, w_router, b_router, w_gate_up, b_gate_up, w_down, b_down, ln_moe_w, ln_moe_b):
    xt = x.reshape(TOKENS, D_MODEL)
    out = _layer(xt, mem, w_in[0], attn_sinks[0], attn_norm_w[0], ret_norm_w[0], w_mix_out[0],
                 ln_mix_w[0], ln_mix_b[0], w_mem_q[0], w_mem_kv[0], w_mem_out[0], ln_mem_w[0], ln_mem_b[0],
                 w_router[0], b_router[0], w_gate_up[0], b_gate_up[0], w_down[0], b_down[0],
                 ln_moe_w[0], ln_moe_b[0])
    return out.reshape(BATCH, SEQ, D_MODEL)
```

```python
import functools
import math

import jax
import jax.numpy as jnp
from jax import lax
from jax.experimental import pallas as pl
from jax.experimental.pallas import tpu as pltpu

F32 = jnp.float32
BF16 = jnp.bfloat16

D_MODEL = 2048
BATCH = 2
SEQ = 16384
TOKENS = BATCH * SEQ

HEAD_DIM_ATTN = 64
N_Q_ATTN = 16
N_KV_ATTN = 2
GQA_GROUP = N_Q_ATTN // N_KV_ATTN
ATTN_BLOCK = 128
D_ATTN = N_Q_ATTN * HEAD_DIM_ATTN
D_KV = N_KV_ATTN * HEAD_DIM_ATTN
N_RET = 8
HEAD_DIM_RET = 128
RET_CHUNK = 128
D_RET = N_RET * HEAD_DIM_RET
D_IN = D_ATTN + 2 * D_KV + 4 * D_RET
N_MEM = 256
N_MEM_HEADS = 4
HEAD_DIM_MEM = D_MODEL // N_MEM_HEADS
N_EXPERTS = 32
TOP_K = 4
D_EXPERT = D_MODEL
SWIGLU_LIMIT = 7.0
SWIGLU_ALPHA = 1.702
LN_EPS = 1e-5
DN_ALPHA = 2.0 ** 0.25
NEG_INF = -1e30

LANES = 128
SUBLANES = 8
V7X_VMEM_BYTES = 64 * 1024 * 1024
VMEM_LIMIT = V7X_VMEM_BYTES - 6 * 1024 * 1024

PROJ_TM = 1024
PROJ_TN = 1792
LN_TM = 512
MEM_TM = 512
ROUTE_TM = 512
ROW_TILE = 256
MOE_TM = 1024
MOE_TF = 512
N_VISITS = (TOKENS * TOP_K + N_EXPERTS * (MOE_TM - 1) + MOE_TM - 1) // MOE_TM
MOE_ROWS = N_VISITS * MOE_TM

COL_QA, COL_QR, COL_KR, COL_VR, COL_GR = 0, 1, 2, 3, 4
COL_KA = 5 * D_RET // D_KV
COL_VA = COL_KA + 1


def _cparams(sem, **kw):
    return pltpu.CompilerParams(dimension_semantics=sem, vmem_limit_bytes=VMEM_LIMIT, **kw)


def _layer_norm(z, w, b):
    mu = jnp.mean(z, axis=-1, keepdims=True)
    zc = z - mu
    var = jnp.mean(zc * zc, axis=-1, keepdims=True)
    return zc * lax.rsqrt(var + LN_EPS) * w + b


def _dot_nt(a, b):
    return lax.dot_general(a, b, (((1,), (1,)), ((), ())), preferred_element_type=F32)


def _proj_kernel(x_ref, w_ref, o_ref, xb_ref):
    @pl.when(pl.program_id(1) == 0)
    def _():
        xb_ref[...] = x_ref[...].astype(BF16)

    o_ref[...] = jnp.dot(xb_ref[...], w_ref[...], preferred_element_type=F32).astype(o_ref.dtype)


def _project(x, w, tm, tn):
    m, k = x.shape
    n = w.shape[1]
    return pl.pallas_call(
        _proj_kernel,
        grid=(m // tm, n // tn),
        in_specs=[pl.BlockSpec((tm, k), lambda i, j: (i, 0)),
                  pl.BlockSpec((k, tn), lambda i, j: (0, j))],
        out_specs=pl.BlockSpec((tm, tn), lambda i, j: (i, j)),
        out_shape=jax.ShapeDtypeStruct((m, n), BF16),
        scratch_shapes=[pltpu.VMEM((tm, k), BF16)],
        compiler_params=_cparams(("parallel", "arbitrary")),
        name="project",
    )(x, w)


def _mixer_kernel(sinkrow_ref, bias_cur_ref, bias_prev_ref, decay_ref, qa_ref, kp_ref, kc_ref, vp_ref, vc_ref,
                  qr_ref, kr_ref, vr_ref, gr_ref, anw_ref, rnw_ref, y_ref, state_ref, a_ref):
    n = pl.program_id(1)

    scale = HEAD_DIM_ATTN ** -0.5
    for j in range(N_KV_ATTN):
        klo, khi = j * HEAD_DIM_ATTN, (j + 1) * HEAD_DIM_ATTN
        heads = range(j * GQA_GROUP, (j + 1) * GQA_GROUP)
        q_rows = jnp.concatenate([qa_ref[:, h * HEAD_DIM_ATTN:(h + 1) * HEAD_DIM_ATTN] for h in heads], axis=0)
        q_rows = q_rows * scale
        sink = sinkrow_ref[j]
        s_cur = _dot_nt(kc_ref[:, klo:khi], q_rows) + bias_cur_ref[j]
        s_prev = _dot_nt(kp_ref[:, klo:khi], q_rows) + bias_prev_ref[j]
        s_prev = jnp.where(n > 0, s_prev, NEG_INF)
        m = jnp.maximum(jnp.max(s_cur, axis=0, keepdims=True), jnp.max(s_prev, axis=0, keepdims=True))
        m = jnp.maximum(m, sink)
        p_cur = jnp.exp(s_cur - m)
        p_prev = jnp.exp(s_prev - m)
        denom = (jnp.sum(p_cur, axis=0, keepdims=True) + jnp.sum(p_prev, axis=0, keepdims=True)
                 + jnp.exp(sink - m))
        v_cur_t = vc_ref[:, klo:khi].astype(F32).T.astype(BF16)
        v_prev_t = vp_ref[:, klo:khi].astype(F32).T.astype(BF16)
        o_t = (jnp.dot(v_cur_t, p_cur.astype(BF16), preferred_element_type=F32)
               + jnp.dot(v_prev_t, p_prev.astype(BF16), preferred_element_type=F32))
        o_t = o_t * (1.0 / denom)
        for g, h in enumerate(heads):
            a_ref[h * HEAD_DIM_ATTN:(h + 1) * HEAD_DIM_ATTN, :] = o_t[:, g * ATTN_BLOCK:(g + 1) * ATTN_BLOCK]
    a_t = a_ref[...]
    a_t = a_t * lax.rsqrt(jnp.mean(a_t * a_t, axis=0, keepdims=True) + LN_EPS)
    y_ref[:, :D_ATTN] = (a_t.T * anw_ref[...]).astype(y_ref.dtype)

    @pl.when(n == 0)
    def _():
        state_ref[...] = jnp.zeros_like(state_ref)

    for h in range(N_RET):
        lo, hi = h * HEAD_DIM_RET, (h + 1) * HEAD_DIM_RET
        g_c = math.exp(math.log1p(-(2.0 ** (-5.0 - h))) * RET_CHUNK)
        q = qr_ref[:, lo:hi]
        k = kr_ref[:, lo:hi]
        v = vr_ref[:, lo:hi]
        s = _dot_nt(q, k) * decay_ref[h]
        inner = jnp.dot(s.astype(BF16), v, preferred_element_type=F32)
        state = state_ref[h]
        qw = (q.astype(F32) * decay_ref[N_RET + h]).astype(BF16)
        cross = jnp.dot(qw, state.astype(BF16), preferred_element_type=F32)
        kw_t = (k.astype(F32) * decay_ref[2 * N_RET + h]).T.astype(BF16)
        state_ref[h] = g_c * state + jnp.dot(kw_t, v, preferred_element_type=F32)
        o = inner + cross
        mu = jnp.mean(o, axis=-1, keepdims=True)
        oc = o - mu
        var = jnp.mean(oc * oc, axis=-1, keepdims=True)
        o = oc * lax.rsqrt(var + LN_EPS) * rnw_ref[:, lo:hi]
        g = gr_ref[:, lo:hi].astype(F32)
        o = o * (g * jax.nn.sigmoid(g))
        y_ref[:, D_ATTN + lo:D_ATTN + hi] = o.astype(y_ref.dtype)


def _mixer(proj, sinks, attn_norm_w, ret_norm_w):
    nc = SEQ // ATTN_BLOCK
    blk = ATTN_BLOCK

    def wide(colblock):
        return pl.BlockSpec((blk, D_RET), lambda b, n: (b * nc + n, colblock))

    def kv_cur(colblock):
        return pl.BlockSpec((blk, D_KV), lambda b, n: (b * nc + n, colblock))

    def kv_prev(colblock):
        return pl.BlockSpec((blk, D_KV), lambda b, n: (b * nc + jnp.maximum(n - 1, 0), colblock))

    vec = pl.BlockSpec((1, D_RET), lambda b, n: (0, 0))
    gq = GQA_GROUP * blk

    def table(rows):
        return pl.BlockSpec((N_KV_ATTN, rows, gq), lambda b, n: (0, 0, 0))

    slopes = jnp.exp2(-8.0 * (jnp.arange(N_Q_ATTN, dtype=F32) + 1.0) / N_Q_ATTN).reshape(N_KV_ATTN, 1, GQA_GROUP, 1)
    key = jnp.arange(blk, dtype=F32).reshape(1, blk, 1, 1)
    query = jnp.arange(blk, dtype=F32).reshape(1, 1, 1, blk)
    dist_cur = jnp.broadcast_to(query - key, (N_KV_ATTN, blk, GQA_GROUP, blk))
    bias_cur = jnp.where(dist_cur >= 0, -slopes * dist_cur, NEG_INF).reshape(N_KV_ATTN, blk, gq)
    bias_prev = jnp.where(dist_cur < 0, -slopes * (dist_cur + blk), NEG_INF).reshape(N_KV_ATTN, blk, gq)
    sink_rows = jnp.repeat(sinks.astype(F32).reshape(N_KV_ATTN, 1, GQA_GROUP), blk, axis=2)

    log_g = jnp.log1p(-jnp.exp2(-5.0 - jnp.arange(N_RET, dtype=F32))).reshape(N_RET, 1, 1)
    pos = jnp.arange(RET_CHUNK, dtype=F32)
    lag = pos[:, None] - pos[None, :]
    kscale = HEAD_DIM_RET ** -0.5
    in_chunk = jnp.where(lag >= 0, jnp.exp(log_g * jnp.maximum(lag, 0.0)), 0.0) * kscale
    ones = jnp.ones((1, 1, HEAD_DIM_RET), F32)
    w_query = jnp.exp(log_g * (pos + 1.0).reshape(1, RET_CHUNK, 1)) * ones
    w_key = jnp.exp(log_g * (RET_CHUNK - 1.0 - pos).reshape(1, RET_CHUNK, 1)) * (kscale * ones)
    decay = jnp.concatenate([in_chunk, w_query, w_key], axis=0)
    return pl.pallas_call(
        _mixer_kernel,
        grid=(BATCH, nc),
        in_specs=[table(1), table(blk), table(blk),
                  pl.BlockSpec((3 * N_RET, RET_CHUNK, HEAD_DIM_RET), lambda b, n: (0, 0, 0)),
                  wide(COL_QA), kv_prev(COL_KA), kv_cur(COL_KA), kv_prev(COL_VA), kv_cur(COL_VA),
                  wide(COL_QR), wide(COL_KR), wide(COL_VR), wide(COL_GR), vec, vec],
        out_specs=pl.BlockSpec((blk, D_MODEL), lambda b, n: (b * nc + n, 0)),
        out_shape=jax.ShapeDtypeStruct((TOKENS, D_MODEL), BF16),
        scratch_shapes=[pltpu.VMEM((N_RET, HEAD_DIM_RET, HEAD_DIM_RET), F32),
                        pltpu.VMEM((D_ATTN, blk), F32)],
        compiler_params=_cparams(("parallel", "arbitrary")),
        name="mixer",
    )(sink_rows, bias_cur, bias_prev, decay, proj, proj, proj, proj, proj, proj, proj, proj, proj,
      attn_norm_w, ret_norm_w)


def _out_ln_kernel(y_ref, w_ref, x_ref, lnw_ref, lnb_ref, o_ref):
    h = jnp.dot(y_ref[...], w_ref[...], preferred_element_type=F32)
    o_ref[...] = _layer_norm(DN_ALPHA * x_ref[...] + h, lnw_ref[...], lnb_ref[...])


def _out_ln(y, w, x, lnw, lnb):
    tm = LN_TM
    row = pl.BlockSpec((tm, D_MODEL), lambda i: (i, 0))
    vec = pl.BlockSpec((1, D_MODEL), lambda i: (0, 0))
    return pl.pallas_call(
        _out_ln_kernel,
        grid=(TOKENS // tm,),
        in_specs=[row,
                  pl.BlockSpec((D_MODEL, D_MODEL), lambda i: (0, 0), pipeline_mode=pl.Buffered(1)),
                  row, vec, vec],
        out_specs=row,
        out_shape=jax.ShapeDtypeStruct((TOKENS, D_MODEL), F32),
        compiler_params=_cparams(("parallel",)),
        name="out_ln",
    )(y, w, x, lnw, lnb)


def _mem_attn_kernel(x_ref, wq_ref, k_ref, v_ref, o_ref):
    q = jnp.dot(x_ref[...].astype(BF16), wq_ref[...], preferred_element_type=F32)
    q = (q * (HEAD_DIM_MEM ** -0.5)).astype(BF16)
    for h in range(N_MEM_HEADS):
        lo, hi = h * HEAD_DIM_MEM, (h + 1) * HEAD_DIM_MEM
        s = _dot_nt(q[:, lo:hi], k_ref[:, lo:hi])
        m = jnp.max(s, axis=-1, keepdims=True)
        p = jnp.exp(s - m)
        denom = jnp.sum(p, axis=-1, keepdims=True)
        o = jnp.dot(p.astype(BF16), v_ref[:, lo:hi], preferred_element_type=F32)
        o_ref[:, lo:hi] = (o / denom).astype(o_ref.dtype)


def _mem_attn(x, wq, kv):
    tm = MEM_TM
    nt = SEQ // tm
    row = pl.BlockSpec((tm, D_MODEL), lambda b, i: (b * nt + i, 0))
    return pl.pallas_call(
        _mem_attn_kernel,
        grid=(BATCH, nt),
        in_specs=[row,
                  pl.BlockSpec((D_MODEL, D_MODEL), lambda b, i: (0, 0), pipeline_mode=pl.Buffered(1)),
                  pl.BlockSpec((N_MEM, D_MODEL), lambda b, i: (b, 0)),
                  pl.BlockSpec((N_MEM, D_MODEL), lambda b, i: (b, 1))],
        out_specs=row,
        out_shape=jax.ShapeDtypeStruct((TOKENS, D_MODEL), BF16),
        compiler_params=_cparams(("parallel", "parallel")),
        name="mem_attn",
    )(x, wq, kv, kv)


def _route_kernel(x_ref, wr_ref, br_ref, idx_ref, gate_ref, rank_ref, count_ref, carry_ref):
    tm = x_ref.shape[0]

    @pl.when(pl.program_id(0) == 0)
    def _():
        carry_ref[...] = jnp.zeros_like(carry_ref)

    x = x_ref[...]
    x_hi = x.astype(BF16)
    x_lo = (x - x_hi.astype(F32)).astype(BF16)
    part_hi = jnp.dot(x_hi, wr_ref[...], preferred_element_type=F32)
    part_lo = jnp.dot(x_lo, wr_ref[...], preferred_element_type=F32)
    logits = (part_hi[:, :LANES] + (part_hi[:, LANES:] + part_lo[:, :LANES]) + part_lo[:, LANES:]
              + br_ref[...])
    lane = lax.broadcasted_iota(jnp.int32, (tm, LANES), 1)
    lane_f = lane.astype(F32)
    work = logits
    sels, vals, idxs = [], [], []
    for _ in range(TOP_K):
        v = jnp.max(work, axis=-1, keepdims=True)
        i = jnp.min(jnp.where(work == v, lane_f, float(LANES)), axis=-1, keepdims=True)
        sel = lane_f == i
        work = jnp.where(sel, -jnp.inf, work)
        sels.append(sel)
        vals.append(v)
        idxs.append(i)
    exps = [jnp.exp(v - vals[0]) for v in vals]
    total = exps[0] + exps[1] + exps[2] + exps[3]

    onehot = jnp.zeros((tm, LANES), F32)
    for sel in sels:
        onehot = jnp.where(sel, 1.0, onehot)
    r = lax.broadcasted_iota(jnp.int32, (tm, tm), 0)
    c = lax.broadcasted_iota(jnp.int32, (tm, tm), 1)
    below = jnp.where(c < r, 1.0, 0.0).astype(BF16)
    prefix = jnp.dot(below, onehot.astype(BF16), preferred_element_type=F32)
    rankmat = prefix + carry_ref[...]
    carry_ref[...] = carry_ref[...] + jnp.sum(onehot, axis=0, keepdims=True)
    count_ref[...] = carry_ref[...]

    idx_out = jnp.zeros((tm, LANES), jnp.int32)
    gate_out = jnp.zeros((tm, LANES), F32)
    rank_out = jnp.zeros((tm, LANES), jnp.int32)
    for k in range(TOP_K):
        rk = jnp.sum(jnp.where(sels[k], rankmat, 0.0), axis=-1, keepdims=True).astype(jnp.int32)
        idx_out = jnp.where(lane == k, idxs[k].astype(jnp.int32), idx_out)
        gate_out = jnp.where(lane == k, exps[k] / total, gate_out)
        rank_out = jnp.where(lane == k, rk, rank_out)
    idx_ref[...] = idx_out
    gate_ref[...] = gate_out
    rank_ref[...] = rank_out


def _route(x, wr, br):
    tm = ROUTE_TM
    wide = pl.BlockSpec((tm, LANES), lambda i: (i, 0))
    return pl.pallas_call(
        _route_kernel,
        grid=(TOKENS // tm,),
        in_specs=[pl.BlockSpec((tm, D_MODEL), lambda i: (i, 0)),
                  pl.BlockSpec((D_MODEL, 2 * LANES), lambda i: (0, 0)),
                  pl.BlockSpec((1, LANES), lambda i: (0, 0))],
        out_specs=[wide, wide, wide, pl.BlockSpec((1, LANES), lambda i: (0, 0))],
        out_shape=[jax.ShapeDtypeStruct((TOKENS, LANES), jnp.int32),
                   jax.ShapeDtypeStruct((TOKENS, LANES), F32),
                   jax.ShapeDtypeStruct((TOKENS, LANES), jnp.int32),
                   jax.ShapeDtypeStruct((1, LANES), F32)],
        scratch_shapes=[pltpu.VMEM((1, LANES), F32)],
        compiler_params=_cparams(("arbitrary",)),
        name="route",
    )(x, wr, br)


HALF = D_MODEL // 2


def _pack_rows(x):
    return _pack_pair(x[:, :HALF], x[:, HALF:])


def _pack_pair(lo, hi):
    lo = lax.bitcast_convert_type(lo.astype(BF16).astype(F32), jnp.int32)
    hi = lax.bitcast_convert_type(hi.astype(BF16).astype(F32), jnp.int32)
    return hi | lax.shift_right_logical(lo, 16)


def _unpack_rows(w):
    lo = lax.bitcast_convert_type(lax.shift_left(w, 16), F32)
    hi = lax.bitcast_convert_type(w & jnp.int32(-65536), F32)
    return lo, hi


def _wait_row_copies(rows_ref, sem):
    pltpu.make_async_copy(rows_ref.at[pl.ds(0, ROW_TILE)], rows_ref.at[pl.ds(0, ROW_TILE)], sem).wait()


ROW_SUB = HALF // LANES
ROW_GROUPS = ROW_TILE // SUBLANES


FILL_SIZES = tuple(MOE_TM >> (b + 1) for b in range(MOE_TM.bit_length() - 1))


def _fill_copies(fill_start_ref, fill_len_ref, zero_ref, xs_ref, sem, act):
    def body(e, carry):
        pos = fill_start_ref[e]
        n = fill_len_ref[e]
        for size in FILL_SIZES:
            take = (n & size) != 0

            @pl.when(take)
            def _():
                cp = pltpu.make_async_copy(zero_ref.at[pl.ds(0, size)], xs_ref.at[pl.ds(pos, size)], sem)
                getattr(cp, act)()

            pos = pos + jnp.where(take, size, 0)
        return carry

    lax.fori_loop(0, N_EXPERTS, body, 0)

    def tail(j, carry):
        pos = fill_start_ref[N_EXPERTS] + j * FILL_SIZES[0]
        cp = pltpu.make_async_copy(zero_ref, xs_ref.at[pl.ds(pos, FILL_SIZES[0])], sem)
        getattr(cp, act)()
        return carry

    lax.fori_loop(0, fill_len_ref[N_EXPERTS], tail, 0)


def _scatter_kernel(fill_start_ref, fill_len_ref, dest_ref, x_ref, xs_ref, packed_ref, zero_ref, sem, fill_sem):
    i = pl.program_id(0)
    slot = i % 2

    @pl.when(i == 0)
    def _():
        zero_ref[...] = jnp.zeros_like(zero_ref)
        _fill_copies(fill_start_ref, fill_len_ref, zero_ref, xs_ref, fill_sem, "start")

    words = _pack_rows(x_ref[...])
    for g in range(ROW_GROUPS):
        for c in range(ROW_SUB):
            packed_ref[slot, g, c] = words[g * SUBLANES:(g + 1) * SUBLANES, c * LANES:(c + 1) * LANES]

    def body(g, carry):
        for s in range(SUBLANES):
            for k in range(TOP_K):
                d = dest_ref[0, 0, (g * SUBLANES + s) * TOP_K + k]
                pltpu.make_async_copy(packed_ref.at[slot, g, :, s, :], xs_ref.at[d],
                                      sem.at[slot]).start(priority=k % 2)
        return carry

    lax.fori_loop(0, ROW_GROUPS, body, 0)

    @pl.when(i > 0)
    def _():
        for k in range(TOP_K):
            _wait_row_copies(xs_ref, sem.at[1 - slot])

    @pl.when(i == pl.num_programs(0) - 1)
    def _():
        for k in range(TOP_K):
            _wait_row_copies(xs_ref, sem.at[slot])

    @pl.when(i == 0)
    def _():
        _fill_copies(fill_start_ref, fill_len_ref, zero_ref, xs_ref, fill_sem, "wait")


def _scatter_rows(x, dest, fill_start, fill_len):
    nt = TOKENS // ROW_TILE
    grid_spec = pltpu.PrefetchScalarGridSpec(
        num_scalar_prefetch=2,
        grid=(nt,),
        in_specs=[pl.BlockSpec((1, 1, ROW_TILE * TOP_K), lambda i, fs, fl: (i, 0, 0), memory_space=pltpu.SMEM),
                  pl.BlockSpec((ROW_TILE, D_MODEL), lambda i, fs, fl: (i, 0))],
        out_specs=pl.BlockSpec(memory_space=pl.ANY),
        scratch_shapes=[pltpu.VMEM((2, ROW_GROUPS, ROW_SUB, SUBLANES, LANES), jnp.int32),
                        pltpu.VMEM((FILL_SIZES[0], ROW_SUB, LANES), jnp.int32),
                        pltpu.SemaphoreType.DMA((2,)), pltpu.SemaphoreType.DMA(())],
    )
    return pl.pallas_call(
        _scatter_kernel,
        grid_spec=grid_spec,
        out_shape=jax.ShapeDtypeStruct((MOE_ROWS, ROW_SUB, LANES), jnp.int32),
        compiler_params=_cparams(("arbitrary",), has_side_effects=True),
        name="scatter_rows",
    )(fill_start, fill_len, dest.reshape(nt, 1, ROW_TILE * TOP_K), x)


MOE_NF = D_EXPERT // MOE_TF
OUT_WORDS = MOE_TF // 2
OUT_SUB = OUT_WORDS // LANES


def _out_word_cols(c):
    tile, j = divmod(c, OUT_SUB)
    lo = tile * MOE_TF + j * LANES
    return slice(lo, lo + LANES), slice(lo + OUT_WORDS, lo + OUT_WORDS + LANES)
MOE_ROW_STEP = 256


def _experts_kernel(ge_ref, gn_ref, de_ref, dn_ref, xs_ref, wg_ref, wu_ref, wd_ref, bgu_ref, bd_ref,
                    ys_ref, xb_ref, act_ref, words_ref, ow_ref, pend_ref, xsem, ysem):
    v = pl.program_id(0)
    f = pl.program_id(1)
    gn = gn_ref[v]
    dn = dn_ref[v]
    slot = v % 2

    def ys_copies(visit, tile):
        row0 = pl.multiple_of(visit * MOE_TM, MOE_TM)
        return [pltpu.make_async_copy(ow_ref.at[:, j * LANES:(j + 1) * LANES],
                                      ys_ref.at[pl.ds(row0, MOE_TM), tile * OUT_SUB + j, :], ysem)
                for j in range(OUT_SUB)]

    def emit_output_tile():
        for cp in ys_copies(v - 1, f):
            cp.start()
        pend_ref[0] = 1

    def drain_output_tile():
        @pl.when(pend_ref[0] == 1)
        def _():
            for cp in ys_copies(0, 0):
                cp.wait()
            pend_ref[0] = 0

    @pl.when(jnp.logical_and(v == 0, f == 0))
    def _():
        ow_ref[...] = jnp.zeros_like(ow_ref)
        pend_ref[0] = 0

    def xs_copies(visit, s):
        row0 = pl.multiple_of(visit * MOE_TM, MOE_TM)
        return [pltpu.make_async_copy(xs_ref.at[pl.ds(row0, MOE_TM), c, :],
                                      words_ref.at[s, :, c * LANES:(c + 1) * LANES], xsem.at[s])
                for c in range(ROW_SUB)]

    @pl.when(f == 0)
    def _():
        @pl.when(v == 0)
        def _():
            for cp in xs_copies(0, 0):
                cp.start()

        @pl.when(v < N_VISITS)
        def _():
            for cp in xs_copies(v, slot):
                cp.wait()

        @pl.when(v + 1 < N_VISITS)
        def _():
            for cp in xs_copies(v + 1, 1 - slot):
                cp.start()

    @pl.when(jnp.logical_and(gn > 0, f == 0))
    def _():
        r = lax.broadcasted_iota(jnp.int32, (MOE_TM, 1), 0)
        lo, hi = _unpack_rows(jnp.where(r < gn, words_ref[slot], 0))
        xb_ref[:, :HALF] = lo.astype(BF16)
        xb_ref[:, HALF:] = hi.astype(BF16)

    def gate_phase(rows):
        xb = xb_ref[:rows, :]
        e = pl.ds(ge_ref[v], 1)
        col = pl.multiple_of(f * MOE_TF, MOE_TF)
        gl = jnp.dot(xb, wg_ref[...].astype(BF16), preferred_element_type=F32) + bgu_ref[e, pl.ds(col, MOE_TF)]
        up = (jnp.dot(xb, wu_ref[...].astype(BF16), preferred_element_type=F32)
              + bgu_ref[e, pl.ds(D_EXPERT + col, MOE_TF)])
        gl = jnp.minimum(gl, SWIGLU_LIMIT)
        up = jnp.clip(up, -SWIGLU_LIMIT, SWIGLU_LIMIT)
        act = ((up + 1.0) * (gl * jax.nn.sigmoid(SWIGLU_ALPHA * gl))).astype(BF16)
        act_ref[slot, :rows, pl.ds(pl.multiple_of(f * MOE_TF, MOE_TF), MOE_TF)] = act

    def down_phase(rows):
        drain_output_tile()
        y = jnp.dot(act_ref[1 - slot, :rows, :], wd_ref[...].astype(BF16),
                    preferred_element_type=F32) + bd_ref[pl.ds(de_ref[v], 1),
                                                         pl.ds(pl.multiple_of(f * MOE_TF, MOE_TF), MOE_TF)]
        ow_ref[:rows, :] = _pack_pair(y[:, :OUT_WORDS], y[:, OUT_WORDS:])
        emit_output_tile()

    for rows in range(MOE_TM, 0, -MOE_ROW_STEP):
        lower = rows - MOE_ROW_STEP
        pl.when(jnp.logical_and(gn > lower, gn <= rows))(functools.partial(gate_phase, rows))
        pl.when(jnp.logical_and(dn > lower, dn <= rows))(functools.partial(down_phase, rows))

    @pl.when(jnp.logical_and(dn == 0, v >= 1))
    def _():
        drain_output_tile()
        ow_ref[...] = jnp.zeros_like(ow_ref)
        emit_output_tile()

    @pl.when(jnp.logical_and(v == pl.num_programs(0) - 1, f == MOE_NF - 1))
    def _():
        drain_output_tile()


def _experts(xs, visit_expert, visit_nvalid, w_gate_up, b_gate_up, w_down, b_down):
    nf = MOE_NF
    gate_e = jnp.concatenate([visit_expert, visit_expert[-1:]])
    gate_n = jnp.concatenate([visit_nvalid, jnp.zeros((1,), jnp.int32)])
    down_e = jnp.concatenate([visit_expert[:1], visit_expert])
    down_n = jnp.concatenate([jnp.zeros((1,), jnp.int32), visit_nvalid])

    def tile(f, n):
        return jnp.where(n > 0, f, nf - 1)

    grid_spec = pltpu.PrefetchScalarGridSpec(
        num_scalar_prefetch=4,
        grid=(N_VISITS + 1, nf),
        in_specs=[
            pl.BlockSpec(memory_space=pl.ANY),
            pl.BlockSpec((None, D_MODEL, MOE_TF), lambda v, f, ge, gn, de, dn: (ge[v], 0, tile(f, gn[v]))),
            pl.BlockSpec((None, D_MODEL, MOE_TF), lambda v, f, ge, gn, de, dn: (ge[v], 0, nf + tile(f, gn[v]))),
            pl.BlockSpec((None, D_EXPERT, MOE_TF), lambda v, f, ge, gn, de, dn: (de[v], 0, tile(f, dn[v]))),
            pl.BlockSpec((N_EXPERTS, 2 * D_EXPERT), lambda v, f, ge, gn, de, dn: (0, 0)),
            pl.BlockSpec((N_EXPERTS, D_MODEL), lambda v, f, ge, gn, de, dn: (0, 0)),
        ],
        out_specs=pl.BlockSpec(memory_space=pl.ANY),
        scratch_shapes=[pltpu.VMEM((MOE_TM, D_MODEL), BF16),
                        pltpu.VMEM((2, MOE_TM, D_EXPERT), BF16),
                        pltpu.VMEM((2, MOE_TM, HALF), jnp.int32),
                        pltpu.VMEM((MOE_TM, OUT_WORDS), jnp.int32),
                        pltpu.SMEM((1,), jnp.int32),
                        pltpu.SemaphoreType.DMA((2,)),
                        pltpu.SemaphoreType.DMA(())],
    )
    return pl.pallas_call(
        _experts_kernel,
        grid_spec=grid_spec,
        out_shape=jax.ShapeDtypeStruct((MOE_ROWS, ROW_SUB, LANES), jnp.int32),
        compiler_params=_cparams(("arbitrary", "arbitrary")),
        name="experts",
    )(gate_e, gate_n, down_e, down_n, xs, w_gate_up, w_gate_up, w_down, b_gate_up, b_down)


def _combine_kernel(dest_ref, dest_next_ref, ys_ref, gate_ref, x_ref, lnw_ref, lnb_ref, o_ref,
                    buf_ref, z_ref, sem):
    i = pl.program_id(0)
    slot = i % 2

    def start_group(idx_ref, s, g):
        for t in range(SUBLANES):
            for k in range(TOP_K):
                d = idx_ref[0, 0, (g * SUBLANES + t) * TOP_K + k]
                pltpu.make_async_copy(ys_ref.at[d], buf_ref.at[s, k, g, :, t, :],
                                      sem.at[s]).start(priority=k % 2)

    @pl.when(i == 0)
    def _():
        def body(g, carry):
            start_group(dest_ref, slot, g)
            return carry

        lax.fori_loop(0, ROW_GROUPS, body, 0)

    for k in range(TOP_K):
        _wait_row_copies(ys_ref, sem.at[slot])

    def group(g, carry, prefetch):
        if prefetch:
            start_group(dest_next_ref, 1 - slot, g)
        rows = pl.ds(pl.multiple_of(g * SUBLANES, SUBLANES), SUBLANES)
        gates = gate_ref[rows, :]
        gate_k = [jnp.broadcast_to(gates[:, k:k + 1], (SUBLANES, LANES)) for k in range(TOP_K)]
        for c in range(ROW_SUB):
            lo_cols, hi_cols = _out_word_cols(c)
            z_lo = DN_ALPHA * x_ref[rows, lo_cols]
            z_hi = DN_ALPHA * x_ref[rows, hi_cols]
            for k in range(TOP_K):
                lo, hi = _unpack_rows(buf_ref[slot, k, g, c])
                z_lo = z_lo + gate_k[k] * lo
                z_hi = z_hi + gate_k[k] * hi
            z_ref[rows, lo_cols] = z_lo
            z_ref[rows, hi_cols] = z_hi
        return carry

    last = i + 1 == pl.num_programs(0)

    @pl.when(jnp.logical_not(last))
    def _():
        lax.fori_loop(0, ROW_GROUPS, functools.partial(group, prefetch=True), 0)

    @pl.when(last)
    def _():
        lax.fori_loop(0, ROW_GROUPS, functools.partial(group, prefetch=False), 0)

    o_ref[...] = _layer_norm(z_ref[...], lnw_ref[...], lnb_ref[...])


def _combine(ys, dest, gates, x, lnw, lnb):
    nt = TOKENS // ROW_TILE
    row = pl.BlockSpec((ROW_TILE, D_MODEL), lambda i: (i, 0))
    vec = pl.BlockSpec((1, D_MODEL), lambda i: (0, 0))
    dest_tiles = dest.reshape(nt, 1, ROW_TILE * TOP_K)
    return pl.pallas_call(
        _combine_kernel,
        grid=(nt,),
        in_specs=[pl.BlockSpec((1, 1, ROW_TILE * TOP_K), lambda i: (i, 0, 0), memory_space=pltpu.SMEM),
                  pl.BlockSpec((1, 1, ROW_TILE * TOP_K), lambda i: (jnp.minimum(i + 1, nt - 1), 0, 0),
                               memory_space=pltpu.SMEM),
                  pl.BlockSpec(memory_space=pl.ANY),
                  pl.BlockSpec((ROW_TILE, LANES), lambda i: (i, 0)),
                  row, vec, vec],
        out_specs=row,
        out_shape=jax.ShapeDtypeStruct((TOKENS, D_MODEL), F32),
        scratch_shapes=[pltpu.VMEM((2, TOP_K, ROW_GROUPS, ROW_SUB, SUBLANES, LANES), jnp.int32),
                        pltpu.VMEM((ROW_TILE, D_MODEL), F32), pltpu.SemaphoreType.DMA((2,))],
        compiler_params=_cparams(("arbitrary",)),
        name="combine",
    )(dest_tiles, dest_tiles, ys, gates, x, lnw, lnb)


def _layer(x, mem, w_in, attn_sinks, attn_norm_w, ret_norm_w, w_mix_out, ln_mix_w, ln_mix_b,
           w_mem_q, w_mem_kv, w_mem_out, ln_mem_w, ln_mem_b, w_router, b_router,
           w_gate_up, b_gate_up, w_down, b_down, ln_moe_w, ln_moe_b):
    o_k, o_v, o_qr = D_ATTN, D_ATTN + D_KV, D_ATTN + 2 * D_KV
    w_in_p = jnp.concatenate([w_in[:, :o_k], w_in[:, o_qr:], w_in[:, o_k:o_v], w_in[:, o_v:o_qr]],
                             axis=1).astype(BF16)
    proj = _project(x, w_in_p, PROJ_TM, PROJ_TN)
    y = _mixer(proj, attn_sinks.astype(F32), attn_norm_w.reshape(1, D_ATTN), ret_norm_w.reshape(1, D_RET))
    x1 = _out_ln(y, w_mix_out.astype(BF16), x, ln_mix_w.reshape(1, D_MODEL), ln_mix_b.reshape(1, D_MODEL))

    kv = _project(mem.reshape(BATCH * N_MEM, D_MODEL), w_mem_kv.astype(BF16), BATCH * N_MEM, D_MODEL)
    o = _mem_attn(x1, w_mem_q.astype(BF16), kv)
    x2 = _out_ln(o, w_mem_out.astype(BF16), x1, ln_mem_w.reshape(1, D_MODEL), ln_mem_b.reshape(1, D_MODEL))

    wr = jnp.pad(w_router, ((0, 0), (0, LANES - N_EXPERTS)))
    wr_hi = wr.astype(BF16)
    wr = jnp.concatenate([wr_hi, (wr - wr_hi.astype(F32)).astype(BF16)], axis=1)
    br = jnp.pad(b_router.reshape(1, N_EXPERTS), ((0, 0), (0, LANES - N_EXPERTS)), constant_values=NEG_INF)
    idx, gates, rank, counts = _route(x2, wr, br)

    counts = counts[0, :N_EXPERTS].astype(jnp.int32)
    padded = ((counts + MOE_TM - 1) // MOE_TM) * MOE_TM
    eid = jnp.arange(N_EXPERTS, dtype=jnp.int32)
    pad_end = jnp.sum(jnp.where(eid[None, :] <= eid[:, None], padded[None, :], 0), axis=1)
    pad_start = pad_end - padded
    chosen = idx[:, :TOP_K, None] == eid[None, None, :]
    dest = jnp.sum(jnp.where(chosen, pad_start[None, None, :], 0), axis=-1) + rank[:, :TOP_K]
    visit_row = jnp.arange(N_VISITS, dtype=jnp.int32) * MOE_TM
    visit_expert = jnp.minimum(jnp.sum((pad_end[None, :] <= visit_row[:, None]).astype(jnp.int32), axis=1),
                               N_EXPERTS - 1)
    visit_nvalid = jnp.clip(counts[visit_expert] - (visit_row - pad_start[visit_expert]),
                            0, MOE_TM).astype(jnp.int32)

    fill_start = jnp.concatenate([pad_start + counts, pad_end[-1:]]).astype(jnp.int32)
    fill_len = jnp.concatenate([padded - counts, (MOE_ROWS - pad_end[-1:]) // FILL_SIZES[0]]).astype(jnp.int32)
    xs = _scatter_rows(x2, dest, fill_start, fill_len)
    ys = _experts(xs, visit_expert, visit_nvalid,
                  w_gate_up, b_gate_up, w_down, b_down)
    return _combine(ys, dest, gates, x2, ln_moe_w.reshape(1, D_MODEL), ln_moe_b.reshape(1, D_MODEL))


def kernel(x, mem, w_in, attn_sinks, attn_norm_w, ret_norm_w, w_mix_out, ln_mix_w, ln_mix_b, w_mem_q, w_mem_kv, w_mem_out, ln_mem_w, ln_mem_b, w_router, b_router, w_gate_up, b_gate_up, w_down, b_down, ln_moe_w, ln_moe_b):
    xt = x.reshape(TOKENS, D_MODEL)
    out = _layer(xt, mem, w_in[0], attn_sinks[0], attn_norm_w[0], ret_norm_w[0], w_mix_out[0],
                 ln_mix_w[0], ln_mix_b[0], w_mem_q[0], w_mem_kv[0], w_mem_out[0], ln_mem_w[0], ln_mem_b[0],
                 w_router[0], b_router[0], w_gate_up[0], b_gate_up[0], w_down[0], b_down[0],
                 ln_moe_w[0], ln_moe_b[0])
    return out.reshape(BATCH, SEQ, D_MODEL)
```

```python
import functools
import math

import jax
import jax.numpy as jnp
from jax import lax
from jax.experimental import pallas as pl
from jax.experimental.pallas import tpu as pltpu

F32 = jnp.float32
BF16 = jnp.bfloat16

D_MODEL = 2048
BATCH = 2
SEQ = 16384
TOKENS = BATCH * SEQ

HEAD_DIM_ATTN = 64
N_Q_ATTN = 16
N_KV_ATTN = 2
GQA_GROUP = N_Q_ATTN // N_KV_ATTN
ATTN_BLOCK = 128
D_ATTN = N_Q_ATTN * HEAD_DIM_ATTN
D_KV = N_KV_ATTN * HEAD_DIM_ATTN
N_RET = 8
HEAD_DIM_RET = 128
RET_CHUNK = 128
D_RET = N_RET * HEAD_DIM_RET
D_IN = D_ATTN + 2 * D_KV + 4 * D_RET
N_MEM = 256
N_MEM_HEADS = 4
HEAD_DIM_MEM = D_MODEL // N_MEM_HEADS
N_EXPERTS = 32
TOP_K = 4
D_EXPERT = D_MODEL
SWIGLU_LIMIT = 7.0
SWIGLU_ALPHA = 1.702
LN_EPS = 1e-5
DN_ALPHA = 2.0 ** 0.25
NEG_INF = -1e30

LANES = 128
SUBLANES = 8
V7X_VMEM_BYTES = 64 * 1024 * 1024
VMEM_LIMIT = V7X_VMEM_BYTES - 6 * 1024 * 1024

PROJ_TM = 1024
PROJ_TN = 1792
LN_TM = 512
MEM_TM = 512
ROUTE_TM = 512
ROW_TILE = 256
MOE_TM = 1024
MOE_TF = 512
N_VISITS = (TOKENS * TOP_K + N_EXPERTS * (MOE_TM - 1) + MOE_TM - 1) // MOE_TM
MOE_ROWS = N_VISITS * MOE_TM

COL_QA, COL_QR, COL_KR, COL_VR, COL_GR = 0, 1, 2, 3, 4
COL_KA = 5 * D_RET // D_KV
COL_VA = COL_KA + 1


def _cparams(sem, **kw):
    return pltpu.CompilerParams(dimension_semantics=sem, vmem_limit_bytes=VMEM_LIMIT, **kw)


def _layer_norm(z, w, b):
    mu = jnp.mean(z, axis=-1, keepdims=True)
    zc = z - mu
    var = jnp.mean(zc * zc, axis=-1, keepdims=True)
    return zc * lax.rsqrt(var + LN_EPS) * w + b


def _dot_nt(a, b):
    return lax.dot_general(a, b, (((1,), (1,)), ((), ())), preferred_element_type=F32)


def _proj_kernel(x_ref, w_ref, o_ref, xb_ref):
    @pl.when(pl.program_id(1) == 0)
    def _():
        xb_ref[...] = x_ref[...].astype(BF16)

    o_ref[...] = jnp.dot(xb_ref[...], w_ref[...], preferred_element_type=F32).astype(o_ref.dtype)


def _project(x, w, tm, tn):
    m, k = x.shape
    n = w.shape[1]
    return pl.pallas_call(
        _proj_kernel,
        grid=(m // tm, n // tn),
        in_specs=[pl.BlockSpec((tm, k), lambda i, j: (i, 0)),
                  pl.BlockSpec((k, tn), lambda i, j: (0, j))],
        out_specs=pl.BlockSpec((tm, tn), lambda i, j: (i, j)),
        out_shape=jax.ShapeDtypeStruct((m, n), BF16),
        scratch_shapes=[pltpu.VMEM((tm, k), BF16)],
        compiler_params=_cparams(("parallel", "arbitrary")),
        name="project",
    )(x, w)


def _mixer_kernel(sinkrow_ref, bias_cur_ref, bias_prev_ref, decay_ref, qa_ref, kp_ref, kc_ref, vp_ref, vc_ref,
                  qr_ref, kr_ref, vr_ref, gr_ref, anw_ref, rnw_ref, y_ref, state_ref, a_ref):
    n = pl.program_id(1)

    scale = HEAD_DIM_ATTN ** -0.5
    for j in range(N_KV_ATTN):
        klo, khi = j * HEAD_DIM_ATTN, (j + 1) * HEAD_DIM_ATTN
        heads = range(j * GQA_GROUP, (j + 1) * GQA_GROUP)
        q_rows = jnp.concatenate([qa_ref[:, h * HEAD_DIM_ATTN:(h + 1) * HEAD_DIM_ATTN] for h in heads], axis=0)
        q_rows = q_rows * scale
        sink = sinkrow_ref[j]
        s_cur = _dot_nt(kc_ref[:, klo:khi], q_rows) + bias_cur_ref[j]
        s_prev = _dot_nt(kp_ref[:, klo:khi], q_rows) + bias_prev_ref[j]
        s_prev = jnp.where(n > 0, s_prev, NEG_INF)
        m = jnp.maximum(jnp.max(s_cur, axis=0, keepdims=True), jnp.max(s_prev, axis=0, keepdims=True))
        m = jnp.maximum(m, sink)
        p_cur = jnp.exp(s_cur - m)
        p_prev = jnp.exp(s_prev - m)
        denom = (jnp.sum(p_cur, axis=0, keepdims=True) + jnp.sum(p_prev, axis=0, keepdims=True)
                 + jnp.exp(sink - m))
        v_cur_t = vc_ref[:, klo:khi].astype(F32).T.astype(BF16)
        v_prev_t = vp_ref[:, klo:khi].astype(F32).T.astype(BF16)
        o_t = (jnp.dot(v_cur_t, p_cur.astype(BF16), preferred_element_type=F32)
               + jnp.dot(v_prev_t, p_prev.astype(BF16), preferred_element_type=F32))
        o_t = o_t * (1.0 / denom)
        for g, h in enumerate(heads):
            a_ref[h * HEAD_DIM_ATTN:(h + 1) * HEAD_DIM_ATTN, :] = o_t[:, g * ATTN_BLOCK:(g + 1) * ATTN_BLOCK]
    a_t = a_ref[...]
    a_t = a_t * lax.rsqrt(jnp.mean(a_t * a_t, axis=0, keepdims=True) + LN_EPS)
    y_ref[:, :D_ATTN] = (a_t.T * anw_ref[...]).astype(y_ref.dtype)

    @pl.when(n == 0)
    def _():
        state_ref[...] = jnp.zeros_like(state_ref)

    for h in range(N_RET):
        lo, hi = h * HEAD_DIM_RET, (h + 1) * HEAD_DIM_RET
        g_c = math.exp(math.log1p(-(2.0 ** (-5.0 - h))) * RET_CHUNK)
        q = qr_ref[:, lo:hi]
        k = kr_ref[:, lo:hi]
        v = vr_ref[:, lo:hi]
        s = _dot_nt(q, k) * decay_ref[h]
        inner = jnp.dot(s.astype(BF16), v, preferred_element_type=F32)
        state = state_ref[h]
        qw = (q.astype(F32) * decay_ref[N_RET + h]).astype(BF16)
        cross = jnp.dot(qw, state.astype(BF16), preferred_element_type=F32)
        kw_t = (k.astype(F32) * decay_ref[2 * N_RET + h]).T.astype(BF16)
        state_ref[h] = g_c * state + jnp.dot(kw_t, v, preferred_element_type=F32)
        o = inner + cross
        mu = jnp.mean(o, axis=-1, keepdims=True)
        oc = o - mu
        var = jnp.mean(oc * oc, axis=-1, keepdims=True)
        o = oc * lax.rsqrt(var + LN_EPS) * rnw_ref[:, lo:hi]
        g = gr_ref[:, lo:hi].astype(F32)
        o = o * (g * jax.nn.sigmoid(g))
        y_ref[:, D_ATTN + lo:D_ATTN + hi] = o.astype(y_ref.dtype)


def _mixer(proj, sinks, attn_norm_w, ret_norm_w):
    nc = SEQ // ATTN_BLOCK
    blk = ATTN_BLOCK

    def wide(colblock):
        return pl.BlockSpec((blk, D_RET), lambda b, n: (b * nc + n, colblock))

    def kv_cur(colblock):
        return pl.BlockSpec((blk, D_KV), lambda b, n: (b * nc + n, colblock))

    def kv_prev(colblock):
        return pl.BlockSpec((blk, D_KV), lambda b, n: (b * nc + jnp.maximum(n - 1, 0), colblock))

    vec = pl.BlockSpec((1, D_RET), lambda b, n: (0, 0))
    gq = GQA_GROUP * blk

    def table(rows):
        return pl.BlockSpec((N_KV_ATTN, rows, gq), lambda b, n: (0, 0, 0))

    slopes = jnp.exp2(-8.0 * (jnp.arange(N_Q_ATTN, dtype=F32) + 1.0) / N_Q_ATTN).reshape(N_KV_ATTN, 1, GQA_GROUP, 1)
    key = jnp.arange(blk, dtype=F32).reshape(1, blk, 1, 1)
    query = jnp.arange(blk, dtype=F32).reshape(1, 1, 1, blk)
    dist_cur = jnp.broadcast_to(query - key, (N_KV_ATTN, blk, GQA_GROUP, blk))
    bias_cur = jnp.where(dist_cur >= 0, -slopes * dist_cur, NEG_INF).reshape(N_KV_ATTN, blk, gq)
    bias_prev = jnp.where(dist_cur < 0, -slopes * (dist_cur + blk), NEG_INF).reshape(N_KV_ATTN, blk, gq)
    sink_rows = jnp.repeat(sinks.astype(F32).reshape(N_KV_ATTN, 1, GQA_GROUP), blk, axis=2)

    log_g = jnp.log1p(-jnp.exp2(-5.0 - jnp.arange(N_RET, dtype=F32))).reshape(N_RET, 1, 1)
    pos = jnp.arange(RET_CHUNK, dtype=F32)
    lag = pos[:, None] - pos[None, :]
    kscale = HEAD_DIM_RET ** -0.5
    in_chunk = jnp.where(lag >= 0, jnp.exp(log_g * jnp.maximum(lag, 0.0)), 0.0) * kscale
    ones = jnp.ones((1, 1, HEAD_DIM_RET), F32)
    w_query = jnp.exp(log_g * (pos + 1.0).reshape(1, RET_CHUNK, 1)) * ones
    w_key = jnp.exp(log_g * (RET_CHUNK - 1.0 - pos).reshape(1, RET_CHUNK, 1)) * (kscale * ones)
    decay = jnp.concatenate([in_chunk, w_query, w_key], axis=0)
    return pl.pallas_call(
        _mixer_kernel,
        grid=(BATCH, nc),
        in_specs=[table(1), table(blk), table(blk),
                  pl.BlockSpec((3 * N_RET, RET_CHUNK, HEAD_DIM_RET), lambda b, n: (0, 0, 0)),
                  wide(COL_QA), kv_prev(COL_KA), kv_cur(COL_KA), kv_prev(COL_VA), kv_cur(COL_VA),
                  wide(COL_QR), wide(COL_KR), wide(COL_VR), wide(COL_GR), vec, vec],
        out_specs=pl.BlockSpec((blk, D_MODEL), lambda b, n: (b * nc + n, 0)),
        out_shape=jax.ShapeDtypeStruct((TOKENS, D_MODEL), BF16),
        scratch_shapes=[pltpu.VMEM((N_RET, HEAD_DIM_RET, HEAD_DIM_RET), F32),
                        pltpu.VMEM((D_ATTN, blk), F32)],
        compiler_params=_cparams(("parallel", "arbitrary")),
        name="mixer",
    )(sink_rows, bias_cur, bias_prev, decay, proj, proj, proj, proj, proj, proj, proj, proj, proj,
      attn_norm_w, ret_norm_w)


def _out_ln_kernel(y_ref, w_ref, x_ref, lnw_ref, lnb_ref, o_ref):
    h = jnp.dot(y_ref[...], w_ref[...], preferred_element_type=F32)
    o_ref[...] = _layer_norm(DN_ALPHA * x_ref[...] + h, lnw_ref[...], lnb_ref[...])


def _out_ln(y, w, x, lnw, lnb):
    tm = LN_TM
    row = pl.BlockSpec((tm, D_MODEL), lambda i: (i, 0))
    vec = pl.BlockSpec((1, D_MODEL), lambda i: (0, 0))
    return pl.pallas_call(
        _out_ln_kernel,
        grid=(TOKENS // tm,),
        in_specs=[row,
                  pl.BlockSpec((D_MODEL, D_MODEL), lambda i: (0, 0), pipeline_mode=pl.Buffered(1)),
                  row, vec, vec],
        out_specs=row,
        out_shape=jax.ShapeDtypeStruct((TOKENS, D_MODEL), F32),
        compiler_params=_cparams(("parallel",)),
        name="out_ln",
    )(y, w, x, lnw, lnb)


def _mem_attn_kernel(x_ref, wq_ref, k_ref, v_ref, o_ref):
    q = jnp.dot(x_ref[...].astype(BF16), wq_ref[...], preferred_element_type=F32)
    q = (q * (HEAD_DIM_MEM ** -0.5)).astype(BF16)
    for h in range(N_MEM_HEADS):
        lo, hi = h * HEAD_DIM_MEM, (h + 1) * HEAD_DIM_MEM
        s = _dot_nt(q[:, lo:hi], k_ref[:, lo:hi])
        m = jnp.max(s, axis=-1, keepdims=True)
        p = jnp.exp(s - m)
        denom = jnp.sum(p, axis=-1, keepdims=True)
        o = jnp.dot(p.astype(BF16), v_ref[:, lo:hi], preferred_element_type=F32)
        o_ref[:, lo:hi] = (o / denom).astype(o_ref.dtype)


def _mem_attn(x, wq, kv):
    tm = MEM_TM
    nt = SEQ // tm
    row = pl.BlockSpec((tm, D_MODEL), lambda b, i: (b * nt + i, 0))
    return pl.pallas_call(
        _mem_attn_kernel,
        grid=(BATCH, nt),
        in_specs=[row,
                  pl.BlockSpec((D_MODEL, D_MODEL), lambda b, i: (0, 0), pipeline_mode=pl.Buffered(1)),
                  pl.BlockSpec((N_MEM, D_MODEL), lambda b, i: (b, 0)),
                  pl.BlockSpec((N_MEM, D_MODEL), lambda b, i: (b, 1))],
        out_specs=row,
        out_shape=jax.ShapeDtypeStruct((TOKENS, D_MODEL), BF16),
        compiler_params=_cparams(("parallel", "parallel")),
        name="mem_attn",
    )(x, wq, kv, kv)


def _route_kernel(x_ref, wr_ref, br_ref, idx_ref, gate_ref, rank_ref, count_ref, carry_ref):
    tm = x_ref.shape[0]

    @pl.when(pl.program_id(0) == 0)
    def _():
        carry_ref[...] = jnp.zeros_like(carry_ref)

    x = x_ref[...]
    x_hi = x.astype(BF16)
    x_lo = (x - x_hi.astype(F32)).astype(BF16)
    part_hi = jnp.dot(x_hi, wr_ref[...], preferred_element_type=F32)
    part_lo = jnp.dot(x_lo, wr_ref[...], preferred_element_type=F32)
    logits = (part_hi[:, :LANES] + (part_hi[:, LANES:] + part_lo[:, :LANES]) + part_lo[:, LANES:]
              + br_ref[...])
    lane = lax.broadcasted_iota(jnp.int32, (tm, LANES), 1)
    lane_f = lane.astype(F32)
    work = logits
    sels, vals, idxs = [], [], []
    for _ in range(TOP_K):
        v = jnp.max(work, axis=-1, keepdims=True)
        i = jnp.min(jnp.where(work == v, lane_f, float(LANES)), axis=-1, keepdims=True)
        sel = lane_f == i
        work = jnp.where(sel, -jnp.inf, work)
        sels.append(sel)
        vals.append(v)
        idxs.append(i)
    exps = [jnp.exp(v - vals[0]) for v in vals]
    total = exps[0] + exps[1] + exps[2] + exps[3]

    onehot = jnp.zeros((tm, LANES), F32)
    for sel in sels:
        onehot = jnp.where(sel, 1.0, onehot)
    r = lax.broadcasted_iota(jnp.int32, (tm, tm), 0)
    c = lax.broadcasted_iota(jnp.int32, (tm, tm), 1)
    below = jnp.where(c < r, 1.0, 0.0).astype(BF16)
    prefix = jnp.dot(below, onehot.astype(BF16), preferred_element_type=F32)
    rankmat = prefix + carry_ref[...]
    carry_ref[...] = carry_ref[...] + jnp.sum(onehot, axis=0, keepdims=True)
    count_ref[...] = carry_ref[...]

    idx_out = jnp.zeros((tm, LANES), F32)
    gate_out = jnp.zeros((tm, LANES), F32)
    rank_out = jnp.zeros((tm, LANES), F32)
    for k in range(TOP_K):
        rk = jnp.sum(jnp.where(sels[k], rankmat, 0.0), axis=-1, keepdims=True)
        idx_out = jnp.where(lane == k, idxs[k], idx_out)
        gate_out = jnp.where(lane == k, exps[k] / total, gate_out)
        rank_out = jnp.where(lane == k, rk, rank_out)
    gate_ref[...] = gate_out
    idx_ref[...] = idx_out.T[:SUBLANES, :].astype(jnp.int32)
    rank_ref[...] = rank_out.T[:SUBLANES, :].astype(jnp.int32)


def _route(x, wr, br):
    tm = ROUTE_TM
    wide = pl.BlockSpec((tm, LANES), lambda i: (i, 0))
    rows = pl.BlockSpec((SUBLANES, tm), lambda i: (0, i))
    return pl.pallas_call(
        _route_kernel,
        grid=(TOKENS // tm,),
        in_specs=[pl.BlockSpec((tm, D_MODEL), lambda i: (i, 0)),
                  pl.BlockSpec((D_MODEL, 2 * LANES), lambda i: (0, 0)),
                  pl.BlockSpec((1, LANES), lambda i: (0, 0))],
        out_specs=[rows, wide, rows, pl.BlockSpec((1, LANES), lambda i: (0, 0))],
        out_shape=[jax.ShapeDtypeStruct((SUBLANES, TOKENS), jnp.int32),
                   jax.ShapeDtypeStruct((TOKENS, LANES), F32),
                   jax.ShapeDtypeStruct((SUBLANES, TOKENS), jnp.int32),
                   jax.ShapeDtypeStruct((1, LANES), F32)],
        scratch_shapes=[pltpu.VMEM((1, LANES), F32)],
        compiler_params=_cparams(("arbitrary",)),
        name="route",
    )(x, wr, br)


HALF = D_MODEL // 2


def _pack_rows(x):
    return _pack_pair(x[:, :HALF], x[:, HALF:])


def _pack_pair(lo, hi):
    lo = lax.bitcast_convert_type(lo.astype(BF16).astype(F32), jnp.int32)
    hi = lax.bitcast_convert_type(hi.astype(BF16).astype(F32), jnp.int32)
    return hi | lax.shift_right_logical(lo, 16)


def _unpack_rows(w):
    lo = lax.bitcast_convert_type(lax.shift_left(w, 16), F32)
    hi = lax.bitcast_convert_type(w & jnp.int32(-65536), F32)
    return lo, hi


def _wait_row_copies(rows_ref, sem):
    pltpu.make_async_copy(rows_ref.at[pl.ds(0, ROW_TILE)], rows_ref.at[pl.ds(0, ROW_TILE)], sem).wait()


ROW_SUB = HALF // LANES
ROW_GROUPS = ROW_TILE // SUBLANES


FILL_SIZES = tuple(MOE_TM >> (b + 1) for b in range(MOE_TM.bit_length() - 1))


def _fill_copies(fill_start_ref, fill_len_ref, zero_ref, xs_ref, sem, act):
    def body(e, carry):
        pos = fill_start_ref[e]
        n = fill_len_ref[e]
        for size in FILL_SIZES:
            take = (n & size) != 0

            @pl.when(take)
            def _():
                cp = pltpu.make_async_copy(zero_ref.at[pl.ds(0, size)], xs_ref.at[pl.ds(pos, size)], sem)
                getattr(cp, act)()

            pos = pos + jnp.where(take, size, 0)
        return carry

    lax.fori_loop(0, N_EXPERTS, body, 0)

    def tail(j, carry):
        pos = fill_start_ref[N_EXPERTS] + j * FILL_SIZES[0]
        cp = pltpu.make_async_copy(zero_ref, xs_ref.at[pl.ds(pos, FILL_SIZES[0])], sem)
        getattr(cp, act)()
        return carry

    lax.fori_loop(0, fill_len_ref[N_EXPERTS], tail, 0)


def _scatter_kernel(fill_start_ref, fill_len_ref, dest_ref, x_ref, xs_ref, packed_ref, zero_ref, sem, fill_sem):
    i = pl.program_id(0)
    slot = i % 2

    @pl.when(i == 0)
    def _():
        zero_ref[...] = jnp.zeros_like(zero_ref)
        _fill_copies(fill_start_ref, fill_len_ref, zero_ref, xs_ref, fill_sem, "start")

    words = _pack_rows(x_ref[...])
    for g in range(ROW_GROUPS):
        for c in range(ROW_SUB):
            packed_ref[slot, g, c] = words[g * SUBLANES:(g + 1) * SUBLANES, c * LANES:(c + 1) * LANES]

    def body(g, carry):
        for s in range(SUBLANES):
            for k in range(TOP_K):
                d = dest_ref[0, 0, (g * SUBLANES + s) * TOP_K + k]
                pltpu.make_async_copy(packed_ref.at[slot, g, :, s, :], xs_ref.at[d],
                                      sem.at[slot]).start(priority=k % 2)
        return carry

    lax.fori_loop(0, ROW_GROUPS, body, 0)

    @pl.when(i > 0)
    def _():
        for k in range(TOP_K):
            _wait_row_copies(xs_ref, sem.at[1 - slot])

    @pl.when(i == pl.num_programs(0) - 1)
    def _():
        for k in range(TOP_K):
            _wait_row_copies(xs_ref, sem.at[slot])

    @pl.when(i == 0)
    def _():
        _fill_copies(fill_start_ref, fill_len_ref, zero_ref, xs_ref, fill_sem, "wait")


def _scatter_rows(x, dest, fill_start, fill_len):
    nt = TOKENS // ROW_TILE
    grid_spec = pltpu.PrefetchScalarGridSpec(
        num_scalar_prefetch=2,
        grid=(nt,),
        in_specs=[pl.BlockSpec((1, 1, ROW_TILE * TOP_K), lambda i, fs, fl: (i, 0, 0), memory_space=pltpu.SMEM),
                  pl.BlockSpec((ROW_TILE, D_MODEL), lambda i, fs, fl: (i, 0))],
        out_specs=pl.BlockSpec(memory_space=pl.ANY),
        scratch_shapes=[pltpu.VMEM((2, ROW_GROUPS, ROW_SUB, SUBLANES, LANES), jnp.int32),
                        pltpu.VMEM((FILL_SIZES[0], ROW_SUB, LANES), jnp.int32),
                        pltpu.SemaphoreType.DMA((2,)), pltpu.SemaphoreType.DMA(())],
    )
    return pl.pallas_call(
        _scatter_kernel,
        grid_spec=grid_spec,
        out_shape=jax.ShapeDtypeStruct((MOE_ROWS, ROW_SUB, LANES), jnp.int32),
        compiler_params=_cparams(("arbitrary",), has_side_effects=True),
        name="scatter_rows",
    )(fill_start, fill_len, dest.reshape(nt, 1, ROW_TILE * TOP_K), x)


MOE_NF = D_EXPERT // MOE_TF
OUT_WORDS = MOE_TF // 2
OUT_SUB = OUT_WORDS // LANES


def _out_word_cols(c):
    tile, j = divmod(c, OUT_SUB)
    lo = tile * MOE_TF + j * LANES
    return slice(lo, lo + LANES), slice(lo + OUT_WORDS, lo + OUT_WORDS + LANES)
MOE_ROW_STEP = 256


def _experts_kernel(ge_ref, gn_ref, de_ref, dn_ref, xs_ref, wg_ref, wu_ref, wd_ref, bgu_ref, bd_ref,
                    ys_ref, xb_ref, act_ref, words_ref, ow_ref, pend_ref, xsem, ysem):
    v = pl.program_id(0)
    f = pl.program_id(1)
    gn = gn_ref[v]
    dn = dn_ref[v]
    slot = v % 2

    def ys_copies(visit, tile):
        row0 = pl.multiple_of(visit * MOE_TM, MOE_TM)
        return [pltpu.make_async_copy(ow_ref.at[:, j * LANES:(j + 1) * LANES],
                                      ys_ref.at[pl.ds(row0, MOE_TM), tile * OUT_SUB + j, :], ysem)
                for j in range(OUT_SUB)]

    def emit_output_tile():
        for cp in ys_copies(v - 1, f):
            cp.start()
        pend_ref[0] = 1

    def drain_output_tile():
        @pl.when(pend_ref[0] == 1)
        def _():
            for cp in ys_copies(0, 0):
                cp.wait()
            pend_ref[0] = 0

    @pl.when(jnp.logical_and(v == 0, f == 0))
    def _():
        ow_ref[...] = jnp.zeros_like(ow_ref)
        pend_ref[0] = 0

    def xs_copies(visit, s):
        row0 = pl.multiple_of(visit * MOE_TM, MOE_TM)
        return [pltpu.make_async_copy(xs_ref.at[pl.ds(row0, MOE_TM), c, :],
                                      words_ref.at[s, :, c * LANES:(c + 1) * LANES], xsem.at[s])
                for c in range(ROW_SUB)]

    @pl.when(f == 0)
    def _():
        @pl.when(v == 0)
        def _():
            for cp in xs_copies(0, 0):
                cp.start()

        @pl.when(v < N_VISITS)
        def _():
            for cp in xs_copies(v, slot):
                cp.wait()

        @pl.when(v + 1 < N_VISITS)
        def _():
            for cp in xs_copies(v + 1, 1 - slot):
                cp.start()

    @pl.when(jnp.logical_and(gn > 0, f == 0))
    def _():
        r = lax.broadcasted_iota(jnp.int32, (MOE_TM, 1), 0)
        lo, hi = _unpack_rows(jnp.where(r < gn, words_ref[slot], 0))
        xb_ref[:, :HALF] = lo.astype(BF16)
        xb_ref[:, HALF:] = hi.astype(BF16)

    def gate_phase(rows):
        xb = xb_ref[:rows, :]
        e = pl.ds(ge_ref[v], 1)
        col = pl.multiple_of(f * MOE_TF, MOE_TF)
        gl = jnp.dot(xb, wg_ref[...].astype(BF16), preferred_element_type=F32) + bgu_ref[e, pl.ds(col, MOE_TF)]
        up = (jnp.dot(xb, wu_ref[...].astype(BF16), preferred_element_type=F32)
              + bgu_ref[e, pl.ds(D_EXPERT + col, MOE_TF)])
        gl = jnp.minimum(gl, SWIGLU_LIMIT)
        up = jnp.clip(up, -SWIGLU_LIMIT, SWIGLU_LIMIT)
        act = ((up + 1.0) * (gl * jax.nn.sigmoid(SWIGLU_ALPHA * gl))).astype(BF16)
        act_ref[slot, :rows, pl.ds(pl.multiple_of(f * MOE_TF, MOE_TF), MOE_TF)] = act

    def down_phase(rows):
        drain_output_tile()
        y = jnp.dot(act_ref[1 - slot, :rows, :], wd_ref[...].astype(BF16),
                    preferred_element_type=F32) + bd_ref[pl.ds(de_ref[v], 1),
                                                         pl.ds(pl.multiple_of(f * MOE_TF, MOE_TF), MOE_TF)]
        ow_ref[:rows, :] = _pack_pair(y[:, :OUT_WORDS], y[:, OUT_WORDS:])
        emit_output_tile()

    for rows in range(MOE_TM, 0, -MOE_ROW_STEP):
        lower = rows - MOE_ROW_STEP
        pl.when(jnp.logical_and(gn > lower, gn <= rows))(functools.partial(gate_phase, rows))
        pl.when(jnp.logical_and(dn > lower, dn <= rows))(functools.partial(down_phase, rows))

    @pl.when(jnp.logical_and(dn == 0, v >= 1))
    def _():
        drain_output_tile()
        ow_ref[...] = jnp.zeros_like(ow_ref)
        emit_output_tile()

    @pl.when(jnp.logical_and(v == pl.num_programs(0) - 1, f == MOE_NF - 1))
    def _():
        drain_output_tile()


def _experts(xs, visit_expert, visit_nvalid, w_gate_up, b_gate_up, w_down, b_down):
    nf = MOE_NF
    gate_e = jnp.concatenate([visit_expert, visit_expert[-1:]])
    gate_n = jnp.concatenate([visit_nvalid, jnp.zeros((1,), jnp.int32)])
    down_e = jnp.concatenate([visit_expert[:1], visit_expert])
    down_n = jnp.concatenate([jnp.zeros((1,), jnp.int32), visit_nvalid])

    def tile(f, n):
        return jnp.where(n > 0, f, nf - 1)

    grid_spec = pltpu.PrefetchScalarGridSpec(
        num_scalar_prefetch=4,
        grid=(N_VISITS + 1, nf),
        in_specs=[
            pl.BlockSpec(memory_space=pl.ANY),
            pl.BlockSpec((None, D_MODEL, MOE_TF), lambda v, f, ge, gn, de, dn: (ge[v], 0, tile(f, gn[v]))),
            pl.BlockSpec((None, D_MODEL, MOE_TF), lambda v, f, ge, gn, de, dn: (ge[v], 0, nf + tile(f, gn[v]))),
            pl.BlockSpec((None, D_EXPERT, MOE_TF), lambda v, f, ge, gn, de, dn: (de[v], 0, tile(f, dn[v]))),
            pl.BlockSpec((N_EXPERTS, 2 * D_EXPERT), lambda v, f, ge, gn, de, dn: (0, 0)),
            pl.BlockSpec((N_EXPERTS, D_MODEL), lambda v, f, ge, gn, de, dn: (0, 0)),
        ],
        out_specs=pl.BlockSpec(memory_space=pl.ANY),
        scratch_shapes=[pltpu.VMEM((MOE_TM, D_MODEL), BF16),
                        pltpu.VMEM((2, MOE_TM, D_EXPERT), BF16),
                        pltpu.VMEM((2, MOE_TM, HALF), jnp.int32),
                        pltpu.VMEM((MOE_TM, OUT_WORDS), jnp.int32),
                        pltpu.SMEM((1,), jnp.int32),
                        pltpu.SemaphoreType.DMA((2,)),
                        pltpu.SemaphoreType.DMA(())],
    )
    return pl.pallas_call(
        _experts_kernel,
        grid_spec=grid_spec,
        out_shape=jax.ShapeDtypeStruct((MOE_ROWS, ROW_SUB, LANES), jnp.int32),
        compiler_params=_cparams(("arbitrary", "arbitrary")),
        name="experts",
    )(gate_e, gate_n, down_e, down_n, xs, w_gate_up, w_gate_up, w_down, b_gate_up, b_down)


def _combine_kernel(dest_ref, dest_next_ref, ys_ref, gate_ref, x_ref, lnw_ref, lnb_ref, o_ref,
                    buf_ref, z_ref, sem):
    i = pl.program_id(0)
    slot = i % 2

    def start_group(idx_ref, s, g):
        for t in range(SUBLANES):
            for k in range(TOP_K):
                d = idx_ref[0, 0, (g * SUBLANES + t) * TOP_K + k]
                pltpu.make_async_copy(ys_ref.at[d], buf_ref.at[s, k, g, :, t, :],
                                      sem.at[s]).start(priority=k % 2)

    @pl.when(i == 0)
    def _():
        def body(g, carry):
            start_group(dest_ref, slot, g)
            return carry

        lax.fori_loop(0, ROW_GROUPS, body, 0)

    for k in range(TOP_K):
        _wait_row_copies(ys_ref, sem.at[slot])

    def group(g, carry, prefetch):
        if prefetch:
            start_group(dest_next_ref, 1 - slot, g)
        rows = pl.ds(pl.multiple_of(g * SUBLANES, SUBLANES), SUBLANES)
        gates = gate_ref[rows, :]
        gate_k = [jnp.broadcast_to(gates[:, k:k + 1], (SUBLANES, LANES)) for k in range(TOP_K)]
        for c in range(ROW_SUB):
            lo_cols, hi_cols = _out_word_cols(c)
            z_lo = DN_ALPHA * x_ref[rows, lo_cols]
            z_hi = DN_ALPHA * x_ref[rows, hi_cols]
            for k in range(TOP_K):
                lo, hi = _unpack_rows(buf_ref[slot, k, g, c])
                z_lo = z_lo + gate_k[k] * lo
                z_hi = z_hi + gate_k[k] * hi
            z_ref[rows, lo_cols] = z_lo
            z_ref[rows, hi_cols] = z_hi
        return carry

    last = i + 1 == pl.num_programs(0)

    @pl.when(jnp.logical_not(last))
    def _():
        lax.fori_loop(0, ROW_GROUPS, functools.partial(group, prefetch=True), 0)

    @pl.when(last)
    def _():
        lax.fori_loop(0, ROW_GROUPS, functools.partial(group, prefetch=False), 0)

    o_ref[...] = _layer_norm(z_ref[...], lnw_ref[...], lnb_ref[...])


def _combine(ys, dest, gates, x, lnw, lnb):
    nt = TOKENS // ROW_TILE
    row = pl.BlockSpec((ROW_TILE, D_MODEL), lambda i: (i, 0))
    vec = pl.BlockSpec((1, D_MODEL), lambda i: (0, 0))
    dest_tiles = dest.reshape(nt, 1, ROW_TILE * TOP_K)
    return pl.pallas_call(
        _combine_kernel,
        grid=(nt,),
        in_specs=[pl.BlockSpec((1, 1, ROW_TILE * TOP_K), lambda i: (i, 0, 0), memory_space=pltpu.SMEM),
                  pl.BlockSpec((1, 1, ROW_TILE * TOP_K), lambda i: (jnp.minimum(i + 1, nt - 1), 0, 0),
                               memory_space=pltpu.SMEM),
                  pl.BlockSpec(memory_space=pl.ANY),
                  pl.BlockSpec((ROW_TILE, LANES), lambda i: (i, 0)),
                  row, vec, vec],
        out_specs=row,
        out_shape=jax.ShapeDtypeStruct((TOKENS, D_MODEL), F32),
        scratch_shapes=[pltpu.VMEM((2, TOP_K, ROW_GROUPS, ROW_SUB, SUBLANES, LANES), jnp.int32),
                        pltpu.VMEM((ROW_TILE, D_MODEL), F32), pltpu.SemaphoreType.DMA((2,))],
        compiler_params=_cparams(("arbitrary",)),
        name="combine",
    )(dest_tiles, dest_tiles, ys, gates, x, lnw, lnb)


def _layer(x, mem, w_in, attn_sinks, attn_norm_w, ret_norm_w, w_mix_out, ln_mix_w, ln_mix_b,
           w_mem_q, w_mem_kv, w_mem_out, ln_mem_w, ln_mem_b, w_router, b_router,
           w_gate_up, b_gate_up, w_down, b_down, ln_moe_w, ln_moe_b):
    o_k, o_v, o_qr = D_ATTN, D_ATTN + D_KV, D_ATTN + 2 * D_KV
    w_in_p = jnp.concatenate([w_in[:, :o_k], w_in[:, o_qr:], w_in[:, o_k:o_v], w_in[:, o_v:o_qr]],
                             axis=1).astype(BF16)
    proj = _project(x, w_in_p, PROJ_TM, PROJ_TN)
    y = _mixer(proj, attn_sinks.astype(F32), attn_norm_w.reshape(1, D_ATTN), ret_norm_w.reshape(1, D_RET))
    x1 = _out_ln(y, w_mix_out.astype(BF16), x, ln_mix_w.reshape(1, D_MODEL), ln_mix_b.reshape(1, D_MODEL))

    kv = _project(mem.reshape(BATCH * N_MEM, D_MODEL), w_mem_kv.astype(BF16), BATCH * N_MEM, D_MODEL)
    o = _mem_attn(x1, w_mem_q.astype(BF16), kv)
    x2 = _out_ln(o, w_mem_out.astype(BF16), x1, ln_mem_w.reshape(1, D_MODEL), ln_mem_b.reshape(1, D_MODEL))

    wr = jnp.pad(w_router, ((0, 0), (0, LANES - N_EXPERTS)))
    wr_hi = wr.astype(BF16)
    wr = jnp.concatenate([wr_hi, (wr - wr_hi.astype(F32)).astype(BF16)], axis=1)
    br = jnp.pad(b_router.reshape(1, N_EXPERTS), ((0, 0), (0, LANES - N_EXPERTS)), constant_values=NEG_INF)
    idx, gates, rank, counts = _route(x2, wr, br)

    counts = counts[0, :N_EXPERTS].astype(jnp.int32)
    padded = ((counts + MOE_TM - 1) // MOE_TM) * MOE_TM
    eid = jnp.arange(N_EXPERTS, dtype=jnp.int32)
    pad_end = jnp.sum(jnp.where(eid[None, :] <= eid[:, None], padded[None, :], 0), axis=1)
    pad_start = pad_end - padded
    chosen = idx[:TOP_K, :, None] == eid[None, None, :]
    dest = (jnp.sum(jnp.where(chosen, pad_start[None, None, :], 0), axis=-1) + rank[:TOP_K]).T
    visit_row = jnp.arange(N_VISITS, dtype=jnp.int32) * MOE_TM
    visit_expert = jnp.minimum(jnp.sum((pad_end[None, :] <= visit_row[:, None]).astype(jnp.int32), axis=1),
                               N_EXPERTS - 1)
    visit_nvalid = jnp.clip(counts[visit_expert] - (visit_row - pad_start[visit_expert]),
                            0, MOE_TM).astype(jnp.int32)

    fill_start = jnp.concatenate([pad_start + counts, pad_end[-1:]]).astype(jnp.int32)
    fill_len = jnp.concatenate([padded - counts, (MOE_ROWS - pad_end[-1:]) // FILL_SIZES[0]]).astype(jnp.int32)
    xs = _scatter_rows(x2, dest, fill_start, fill_len)
    ys = _experts(xs, visit_expert, visit_nvalid,
                  w_gate_up, b_gate_up, w_down, b_down)
    return _combine(ys, dest, gates, x2, ln_moe_w.reshape(1, D_MODEL), ln_moe_b.reshape(1, D_MODEL))


def kernel(x, mem, w_in, attn_sinks, attn_norm_w, ret_norm_w, w_mix_out, ln_mix_w, ln_mix_b, w_mem_q, w_mem_kv, w_mem_out, ln_mem_w, ln_mem_b, w_router, b_router, w_gate_up, b_gate_up, w_down, b_down, ln_moe_w, ln_moe_b):
    xt = x.reshape(TOKENS, D_MODEL)
    out = _layer(xt, mem, w_in[0], attn_sinks[0], attn_norm_w[0], ret_norm_w[0], w_mix_out[0],
                 ln_mix_w[0], ln_mix_b[0], w_mem_q[0], w_mem_kv[0], w_mem_out[0], ln_mem_w[0], ln_mem_b[0],
                 w_router[0], b_router[0], w_gate_up[0], b_gate_up[0], w_down[0], b_down[0],
                 ln_moe_w[0], ln_moe_b[0])
    return out.reshape(BATCH, SEQ, D_MODEL)
```

```python
import functools
import math

import jax
import jax.numpy as jnp
import numpy as np
from jax import lax
from jax.experimental import pallas as pl
from jax.experimental.pallas import tpu as pltpu

F32 = jnp.float32
BF16 = jnp.bfloat16

D_MODEL = 2048
BATCH = 2
SEQ = 16384
TOKENS = BATCH * SEQ

HEAD_DIM_ATTN = 64
N_Q_ATTN = 16
N_KV_ATTN = 2
GQA_GROUP = N_Q_ATTN // N_KV_ATTN
ATTN_BLOCK = 128
D_ATTN = N_Q_ATTN * HEAD_DIM_ATTN
D_KV = N_KV_ATTN * HEAD_DIM_ATTN
N_RET = 8
HEAD_DIM_RET = 128
RET_CHUNK = 128
D_RET = N_RET * HEAD_DIM_RET
D_IN = D_ATTN + 2 * D_KV + 4 * D_RET
N_MEM = 256
N_MEM_HEADS = 4
HEAD_DIM_MEM = D_MODEL // N_MEM_HEADS
N_EXPERTS = 32
TOP_K = 4
D_EXPERT = D_MODEL
SWIGLU_LIMIT = 7.0
SWIGLU_ALPHA = 1.702
LN_EPS = 1e-5
DN_ALPHA = 2.0 ** 0.25
NEG_INF = -1e30

LANES = 128
SUBLANES = 8
V7X_VMEM_BYTES = 64 * 1024 * 1024
VMEM_LIMIT = V7X_VMEM_BYTES - 6 * 1024 * 1024

PROJ_TM = 1024
PROJ_TN = 1792
LN_TM = 512
MEM_TM = 512
ROUTE_TM = 512
ROW_TILE = 256
MOE_TM = 1024
MOE_TF = 512
N_VISITS = (TOKENS * TOP_K + N_EXPERTS * (MOE_TM - 1) + MOE_TM - 1) // MOE_TM
MOE_ROWS = N_VISITS * MOE_TM

COL_QA, COL_QR, COL_KR, COL_VR, COL_GR = 0, 1, 2, 3, 4
COL_KA = 5 * D_RET // D_KV
COL_VA = COL_KA + 1


def _cparams(sem, **kw):
    return pltpu.CompilerParams(dimension_semantics=sem, vmem_limit_bytes=VMEM_LIMIT, **kw)


def _layer_norm(z, w, b):
    mu = jnp.mean(z, axis=-1, keepdims=True)
    zc = z - mu
    var = jnp.mean(zc * zc, axis=-1, keepdims=True)
    return zc * lax.rsqrt(var + LN_EPS) * w + b


def _dot_nt(a, b):
    return lax.dot_general(a, b, (((1,), (1,)), ((), ())), preferred_element_type=F32)


def _proj_kernel(x_ref, w_ref, o_ref, xb_ref):
    @pl.when(pl.program_id(1) == 0)
    def _():
        xb_ref[...] = x_ref[...].astype(BF16)

    o_ref[...] = jnp.dot(xb_ref[...], w_ref[...], preferred_element_type=F32).astype(o_ref.dtype)


def _project(x, w, tm, tn):
    m, k = x.shape
    n = w.shape[1]
    return pl.pallas_call(
        _proj_kernel,
        grid=(m // tm, n // tn),
        in_specs=[pl.BlockSpec((tm, k), lambda i, j: (i, 0)),
                  pl.BlockSpec((k, tn), lambda i, j: (0, j))],
        out_specs=pl.BlockSpec((tm, tn), lambda i, j: (i, j)),
        out_shape=jax.ShapeDtypeStruct((m, n), BF16),
        scratch_shapes=[pltpu.VMEM((tm, k), BF16)],
        compiler_params=_cparams(("parallel", "arbitrary")),
        name="project",
    )(x, w)


def _mixer_kernel(sinkrow_ref, bias_cur_ref, bias_prev_ref, decay_ref, qa_ref, kp_ref, kc_ref, vp_ref, vc_ref,
                  qr_ref, kr_ref, vr_ref, gr_ref, anw_ref, rnw_ref, y_ref, state_ref, a_ref):
    n = pl.program_id(1)

    scale = HEAD_DIM_ATTN ** -0.5
    for j in range(N_KV_ATTN):
        klo, khi = j * HEAD_DIM_ATTN, (j + 1) * HEAD_DIM_ATTN
        heads = range(j * GQA_GROUP, (j + 1) * GQA_GROUP)
        q_rows = jnp.concatenate([qa_ref[:, h * HEAD_DIM_ATTN:(h + 1) * HEAD_DIM_ATTN] for h in heads], axis=0)
        q_rows = q_rows * scale
        sink = sinkrow_ref[j]
        s_cur = _dot_nt(kc_ref[:, klo:khi], q_rows) + bias_cur_ref[j]
        s_prev = _dot_nt(kp_ref[:, klo:khi], q_rows) + bias_prev_ref[j]
        s_prev = jnp.where(n > 0, s_prev, NEG_INF)
        m = jnp.maximum(jnp.max(s_cur, axis=0, keepdims=True), jnp.max(s_prev, axis=0, keepdims=True))
        m = jnp.maximum(m, sink)
        p_cur = jnp.exp(s_cur - m)
        p_prev = jnp.exp(s_prev - m)
        denom = (jnp.sum(p_cur, axis=0, keepdims=True) + jnp.sum(p_prev, axis=0, keepdims=True)
                 + jnp.exp(sink - m))
        v_cur_t = vc_ref[:, klo:khi].astype(F32).T.astype(BF16)
        v_prev_t = vp_ref[:, klo:khi].astype(F32).T.astype(BF16)
        o_t = (jnp.dot(v_cur_t, p_cur.astype(BF16), preferred_element_type=F32)
               + jnp.dot(v_prev_t, p_prev.astype(BF16), preferred_element_type=F32))
        o_t = o_t * (1.0 / denom)
        for g, h in enumerate(heads):
            a_ref[h * HEAD_DIM_ATTN:(h + 1) * HEAD_DIM_ATTN, :] = o_t[:, g * ATTN_BLOCK:(g + 1) * ATTN_BLOCK]
    a_t = a_ref[...]
    a_t = a_t * lax.rsqrt(jnp.mean(a_t * a_t, axis=0, keepdims=True) + LN_EPS)
    y_ref[:, :D_ATTN] = (a_t.T * anw_ref[...]).astype(y_ref.dtype)

    @pl.when(n == 0)
    def _():
        state_ref[...] = jnp.zeros_like(state_ref)

    for h in range(N_RET):
        lo, hi = h * HEAD_DIM_RET, (h + 1) * HEAD_DIM_RET
        g_c = math.exp(math.log1p(-(2.0 ** (-5.0 - h))) * RET_CHUNK)
        q = qr_ref[:, lo:hi]
        k = kr_ref[:, lo:hi]
        v = vr_ref[:, lo:hi]
        s = _dot_nt(q, k) * decay_ref[h]
        inner = jnp.dot(s.astype(BF16), v, preferred_element_type=F32)
        state = state_ref[h]
        qw = (q.astype(F32) * decay_ref[N_RET + h]).astype(BF16)
        cross = jnp.dot(qw, state.astype(BF16), preferred_element_type=F32)
        kw_t = (k.astype(F32) * decay_ref[2 * N_RET + h]).T.astype(BF16)
        state_ref[h] = g_c * state + jnp.dot(kw_t, v, preferred_element_type=F32)
        o = inner + cross
        mu = jnp.mean(o, axis=-1, keepdims=True)
        oc = o - mu
        var = jnp.mean(oc * oc, axis=-1, keepdims=True)
        o = oc * lax.rsqrt(var + LN_EPS) * rnw_ref[:, lo:hi]
        g = gr_ref[:, lo:hi].astype(F32)
        o = o * (g * jax.nn.sigmoid(g))
        y_ref[:, D_ATTN + lo:D_ATTN + hi] = o.astype(y_ref.dtype)


def _mixer_tables():
    blk, gq = ATTN_BLOCK, GQA_GROUP * ATTN_BLOCK
    f32 = np.float32
    slopes = np.exp2(-8.0 * (np.arange(N_Q_ATTN, dtype=f32) + 1.0) / N_Q_ATTN).astype(f32)
    slopes = slopes.reshape(N_KV_ATTN, 1, GQA_GROUP, 1)
    key = np.arange(blk, dtype=f32).reshape(1, blk, 1, 1)
    query = np.arange(blk, dtype=f32).reshape(1, 1, 1, blk)
    dist_cur = np.broadcast_to(query - key, (N_KV_ATTN, blk, GQA_GROUP, blk))
    bias_cur = np.where(dist_cur >= 0, -slopes * dist_cur, NEG_INF).astype(f32).reshape(N_KV_ATTN, blk, gq)
    bias_prev = np.where(dist_cur < 0, -slopes * (dist_cur + blk), NEG_INF).astype(f32).reshape(N_KV_ATTN, blk, gq)

    log_g = np.log1p(-np.exp2(-5.0 - np.arange(N_RET, dtype=f32))).astype(f32).reshape(N_RET, 1, 1)
    pos = np.arange(RET_CHUNK, dtype=f32)
    lag = pos[:, None] - pos[None, :]
    kscale = f32(HEAD_DIM_RET ** -0.5)
    in_chunk = np.where(lag >= 0, np.exp(log_g * np.maximum(lag, 0.0)), 0.0) * kscale
    ones = np.ones((1, 1, HEAD_DIM_RET), f32)
    w_query = np.exp(log_g * (pos + 1.0).reshape(1, RET_CHUNK, 1)) * ones
    w_key = np.exp(log_g * (RET_CHUNK - 1.0 - pos).reshape(1, RET_CHUNK, 1)) * (kscale * ones)
    decay = np.concatenate([in_chunk, w_query, w_key], axis=0).astype(f32)
    return bias_cur, bias_prev, decay


_MIXER_TABLES = _mixer_tables()


def _mixer(proj, sinks, attn_norm_w, ret_norm_w):
    nc = SEQ // ATTN_BLOCK
    blk = ATTN_BLOCK

    def wide(colblock):
        return pl.BlockSpec((blk, D_RET), lambda b, n: (b * nc + n, colblock))

    def kv_cur(colblock):
        return pl.BlockSpec((blk, D_KV), lambda b, n: (b * nc + n, colblock))

    def kv_prev(colblock):
        return pl.BlockSpec((blk, D_KV), lambda b, n: (b * nc + jnp.maximum(n - 1, 0), colblock))

    vec = pl.BlockSpec((1, D_RET), lambda b, n: (0, 0))
    gq = GQA_GROUP * blk

    def table(rows):
        return pl.BlockSpec((N_KV_ATTN, rows, gq), lambda b, n: (0, 0, 0))

    bias_cur, bias_prev, decay = _MIXER_TABLES
    sink_rows = jnp.repeat(sinks.astype(F32).reshape(N_KV_ATTN, 1, GQA_GROUP), blk, axis=2)
    return pl.pallas_call(
        _mixer_kernel,
        grid=(BATCH, nc),
        in_specs=[table(1), table(blk), table(blk),
                  pl.BlockSpec((3 * N_RET, RET_CHUNK, HEAD_DIM_RET), lambda b, n: (0, 0, 0)),
                  wide(COL_QA), kv_prev(COL_KA), kv_cur(COL_KA), kv_prev(COL_VA), kv_cur(COL_VA),
                  wide(COL_QR), wide(COL_KR), wide(COL_VR), wide(COL_GR), vec, vec],
        out_specs=pl.BlockSpec((blk, D_MODEL), lambda b, n: (b * nc + n, 0)),
        out_shape=jax.ShapeDtypeStruct((TOKENS, D_MODEL), BF16),
        scratch_shapes=[pltpu.VMEM((N_RET, HEAD_DIM_RET, HEAD_DIM_RET), F32),
                        pltpu.VMEM((D_ATTN, blk), F32)],
        compiler_params=_cparams(("parallel", "arbitrary")),
        name="mixer",
    )(sink_rows, bias_cur, bias_prev, decay, proj, proj, proj, proj, proj, proj, proj, proj, proj,
      attn_norm_w, ret_norm_w)


def _out_ln_kernel(y_ref, w_ref, x_ref, lnw_ref, lnb_ref, o_ref):
    h = jnp.dot(y_ref[...], w_ref[...], preferred_element_type=F32)
    o_ref[...] = _layer_norm(DN_ALPHA * x_ref[...] + h, lnw_ref[...], lnb_ref[...])


def _out_ln(y, w, x, lnw, lnb):
    tm = LN_TM
    row = pl.BlockSpec((tm, D_MODEL), lambda i: (i, 0))
    vec = pl.BlockSpec((1, D_MODEL), lambda i: (0, 0))
    return pl.pallas_call(
        _out_ln_kernel,
        grid=(TOKENS // tm,),
        in_specs=[row,
                  pl.BlockSpec((D_MODEL, D_MODEL), lambda i: (0, 0), pipeline_mode=pl.Buffered(1)),
                  row, vec, vec],
        out_specs=row,
        out_shape=jax.ShapeDtypeStruct((TOKENS, D_MODEL), F32),
        compiler_params=_cparams(("parallel",)),
        name="out_ln",
    )(y, w, x, lnw, lnb)


def _mem_attn_kernel(x_ref, wq_ref, k_ref, v_ref, o_ref):
    q = jnp.dot(x_ref[...].astype(BF16), wq_ref[...], preferred_element_type=F32)
    q = (q * (HEAD_DIM_MEM ** -0.5)).astype(BF16)
    for h in range(N_MEM_HEADS):
        lo, hi = h * HEAD_DIM_MEM, (h + 1) * HEAD_DIM_MEM
        s = _dot_nt(q[:, lo:hi], k_ref[:, lo:hi])
        m = jnp.max(s, axis=-1, keepdims=True)
        p = jnp.exp(s - m)
        denom = jnp.sum(p, axis=-1, keepdims=True)
        o = jnp.dot(p.astype(BF16), v_ref[:, lo:hi], preferred_element_type=F32)
        o_ref[:, lo:hi] = (o / denom).astype(o_ref.dtype)


def _mem_attn(x, wq, kv):
    tm = MEM_TM
    nt = SEQ // tm
    row = pl.BlockSpec((tm, D_MODEL), lambda b, i: (b * nt + i, 0))
    return pl.pallas_call(
        _mem_attn_kernel,
        grid=(BATCH, nt),
        in_specs=[row,
                  pl.BlockSpec((D_MODEL, D_MODEL), lambda b, i: (0, 0), pipeline_mode=pl.Buffered(1)),
                  pl.BlockSpec((N_MEM, D_MODEL), lambda b, i: (b, 0)),
                  pl.BlockSpec((N_MEM, D_MODEL), lambda b, i: (b, 1))],
        out_specs=row,
        out_shape=jax.ShapeDtypeStruct((TOKENS, D_MODEL), BF16),
        compiler_params=_cparams(("parallel", "parallel")),
        name="mem_attn",
    )(x, wq, kv, kv)


def _route_kernel(x_ref, wr_ref, br_ref, idx_ref, gate_ref, rank_ref, count_ref, carry_ref):
    tm = x_ref.shape[0]

    @pl.when(pl.program_id(0) == 0)
    def _():
        carry_ref[...] = jnp.zeros_like(carry_ref)

    x = x_ref[...]
    x_hi = x.astype(BF16)
    x_lo = (x - x_hi.astype(F32)).astype(BF16)
    part_hi = jnp.dot(x_hi, wr_ref[...], preferred_element_type=F32)
    part_lo = jnp.dot(x_lo, wr_ref[...], preferred_element_type=F32)
    logits = (part_hi[:, :LANES] + (part_hi[:, LANES:] + part_lo[:, :LANES]) + part_lo[:, LANES:]
              + br_ref[...])
    lane = lax.broadcasted_iota(jnp.int32, (tm, LANES), 1)
    lane_f = lane.astype(F32)
    work = logits
    sels, vals, idxs = [], [], []
    for _ in range(TOP_K):
        v = jnp.max(work, axis=-1, keepdims=True)
        i = jnp.min(jnp.where(work == v, lane_f, float(LANES)), axis=-1, keepdims=True)
        sel = lane_f == i
        work = jnp.where(sel, -jnp.inf, work)
        sels.append(sel)
        vals.append(v)
        idxs.append(i)
    exps = [jnp.exp(v - vals[0]) for v in vals]
    total = exps[0] + exps[1] + exps[2] + exps[3]

    onehot = jnp.zeros((tm, LANES), F32)
    for sel in sels:
        onehot = jnp.where(sel, 1.0, onehot)
    r = lax.broadcasted_iota(jnp.int32, (tm, tm), 0)
    c = lax.broadcasted_iota(jnp.int32, (tm, tm), 1)
    below = jnp.where(c < r, 1.0, 0.0).astype(BF16)
    prefix = jnp.dot(below, onehot.astype(BF16), preferred_element_type=F32)
    rankmat = prefix + carry_ref[...]
    carry_ref[...] = carry_ref[...] + jnp.sum(onehot, axis=0, keepdims=True)
    count_ref[...] = carry_ref[...]

    idx_out = jnp.zeros((tm, LANES), F32)
    gate_out = jnp.zeros((tm, LANES), F32)
    rank_out = jnp.zeros((tm, LANES), F32)
    for k in range(TOP_K):
        rk = jnp.sum(jnp.where(sels[k], rankmat, 0.0), axis=-1, keepdims=True)
        idx_out = jnp.where(lane == k, idxs[k], idx_out)
        gate_out = jnp.where(lane == k, exps[k] / total, gate_out)
        rank_out = jnp.where(lane == k, rk, rank_out)
    gate_ref[...] = gate_out
    idx_ref[...] = idx_out.T[:SUBLANES, :].astype(jnp.int32)
    rank_ref[...] = rank_out.T[:SUBLANES, :].astype(jnp.int32)


def _route(x, wr, br):
    tm = ROUTE_TM
    wide = pl.BlockSpec((tm, LANES), lambda i: (i, 0))
    rows = pl.BlockSpec((SUBLANES, tm), lambda i: (0, i))
    return pl.pallas_call(
        _route_kernel,
        grid=(TOKENS // tm,),
        in_specs=[pl.BlockSpec((tm, D_MODEL), lambda i: (i, 0)),
                  pl.BlockSpec((D_MODEL, 2 * LANES), lambda i: (0, 0)),
                  pl.BlockSpec((1, LANES), lambda i: (0, 0))],
        out_specs=[rows, wide, rows, pl.BlockSpec((1, LANES), lambda i: (0, 0))],
        out_shape=[jax.ShapeDtypeStruct((SUBLANES, TOKENS), jnp.int32),
                   jax.ShapeDtypeStruct((TOKENS, LANES), F32),
                   jax.ShapeDtypeStruct((SUBLANES, TOKENS), jnp.int32),
                   jax.ShapeDtypeStruct((1, LANES), F32)],
        scratch_shapes=[pltpu.VMEM((1, LANES), F32)],
        compiler_params=_cparams(("arbitrary",)),
        name="route",
    )(x, wr, br)


HALF = D_MODEL // 2


def _pack_rows(x):
    return _pack_pair(x[:, :HALF], x[:, HALF:])


def _pack_pair(lo, hi):
    lo = lax.bitcast_convert_type(lo.astype(BF16).astype(F32), jnp.int32)
    hi = lax.bitcast_convert_type(hi.astype(BF16).astype(F32), jnp.int32)
    return hi | lax.shift_right_logical(lo, 16)


def _unpack_rows(w):
    lo = lax.bitcast_convert_type(lax.shift_left(w, 16), F32)
    hi = lax.bitcast_convert_type(w & jnp.int32(-65536), F32)
    return lo, hi


def _wait_row_copies(rows_ref, sem):
    pltpu.make_async_copy(rows_ref.at[pl.ds(0, ROW_TILE)], rows_ref.at[pl.ds(0, ROW_TILE)], sem).wait()


ROW_SUB = HALF // LANES
ROW_GROUPS = ROW_TILE // SUBLANES


FILL_SIZES = tuple(MOE_TM >> (b + 1) for b in range(MOE_TM.bit_length() - 1))


def _fill_copies(fill_start_ref, fill_len_ref, zero_ref, xs_ref, sem, act):
    def body(e, carry):
        pos = fill_start_ref[e]
        n = fill_len_ref[e]
        for size in FILL_SIZES:
            take = (n & size) != 0

            @pl.when(take)
            def _():
                cp = pltpu.make_async_copy(zero_ref.at[pl.ds(0, size)], xs_ref.at[pl.ds(pos, size)], sem)
                getattr(cp, act)()

            pos = pos + jnp.where(take, size, 0)
        return carry

    lax.fori_loop(0, N_EXPERTS, body, 0)

    def tail(j, carry):
        pos = fill_start_ref[N_EXPERTS] + j * FILL_SIZES[0]
        cp = pltpu.make_async_copy(zero_ref, xs_ref.at[pl.ds(pos, FILL_SIZES[0])], sem)
        getattr(cp, act)()
        return carry

    lax.fori_loop(0, fill_len_ref[N_EXPERTS], tail, 0)


def _scatter_kernel(fill_start_ref, fill_len_ref, dest_ref, x_ref, xs_ref, packed_ref, zero_ref, sem, fill_sem):
    i = pl.program_id(0)
    slot = i % 2

    @pl.when(i == 0)
    def _():
        zero_ref[...] = jnp.zeros_like(zero_ref)
        _fill_copies(fill_start_ref, fill_len_ref, zero_ref, xs_ref, fill_sem, "start")

    words = _pack_rows(x_ref[...])
    for g in range(ROW_GROUPS):
        for c in range(ROW_SUB):
            packed_ref[slot, g, c] = words[g * SUBLANES:(g + 1) * SUBLANES, c * LANES:(c + 1) * LANES]

    def body(g, carry):
        for s in range(SUBLANES):
            for k in range(TOP_K):
                d = dest_ref[0, 0, (g * SUBLANES + s) * TOP_K + k]
                pltpu.make_async_copy(packed_ref.at[slot, g, :, s, :], xs_ref.at[d],
                                      sem.at[slot]).start(priority=k % 2)
        return carry

    lax.fori_loop(0, ROW_GROUPS, body, 0)

    @pl.when(i > 0)
    def _():
        for k in range(TOP_K):
            _wait_row_copies(xs_ref, sem.at[1 - slot])

    @pl.when(i == pl.num_programs(0) - 1)
    def _():
        for k in range(TOP_K):
            _wait_row_copies(xs_ref, sem.at[slot])

    @pl.when(i == 0)
    def _():
        _fill_copies(fill_start_ref, fill_len_ref, zero_ref, xs_ref, fill_sem, "wait")


def _scatter_rows(x, dest, fill_start, fill_len):
    nt = TOKENS // ROW_TILE
    grid_spec = pltpu.PrefetchScalarGridSpec(
        num_scalar_prefetch=2,
        grid=(nt,),
        in_specs=[pl.BlockSpec((1, 1, ROW_TILE * TOP_K), lambda i, fs, fl: (i, 0, 0), memory_space=pltpu.SMEM),
                  pl.BlockSpec((ROW_TILE, D_MODEL), lambda i, fs, fl: (i, 0))],
        out_specs=pl.BlockSpec(memory_space=pl.ANY),
        scratch_shapes=[pltpu.VMEM((2, ROW_GROUPS, ROW_SUB, SUBLANES, LANES), jnp.int32),
                        pltpu.VMEM((FILL_SIZES[0], ROW_SUB, LANES), jnp.int32),
                        pltpu.SemaphoreType.DMA((2,)), pltpu.SemaphoreType.DMA(())],
    )
    return pl.pallas_call(
        _scatter_kernel,
        grid_spec=grid_spec,
        out_shape=jax.ShapeDtypeStruct((MOE_ROWS, ROW_SUB, LANES), jnp.int32),
        compiler_params=_cparams(("arbitrary",), has_side_effects=True),
        name="scatter_rows",
    )(fill_start, fill_len, dest.reshape(nt, 1, ROW_TILE * TOP_K), x)


MOE_NF = D_EXPERT // MOE_TF
OUT_WORDS = MOE_TF // 2
OUT_SUB = OUT_WORDS // LANES


def _out_word_cols(c):
    tile, j = divmod(c, OUT_SUB)
    lo = tile * MOE_TF + j * LANES
    return slice(lo, lo + LANES), slice(lo + OUT_WORDS, lo + OUT_WORDS + LANES)
MOE_ROW_STEP = 256


def _experts_kernel(ge_ref, gn_ref, de_ref, dn_ref, xs_ref, wg_ref, wu_ref, wd_ref, bgu_ref, bd_ref,
                    ys_ref, xb_ref, act_ref, words_ref, ow_ref, pend_ref, xsem, ysem):
    v = pl.program_id(0)
    f = pl.program_id(1)
    gn = gn_ref[v]
    dn = dn_ref[v]
    slot = v % 2

    def ys_copies(visit, tile):
        row0 = pl.multiple_of(visit * MOE_TM, MOE_TM)
        return [pltpu.make_async_copy(ow_ref.at[:, j * LANES:(j + 1) * LANES],
                                      ys_ref.at[pl.ds(row0, MOE_TM), tile * OUT_SUB + j, :], ysem)
                for j in range(OUT_SUB)]

    def emit_output_tile():
        for cp in ys_copies(v - 1, f):
            cp.start()
        pend_ref[0] = 1

    def drain_output_tile():
        @pl.when(pend_ref[0] == 1)
        def _():
            for cp in ys_copies(0, 0):
                cp.wait()
            pend_ref[0] = 0

    @pl.when(jnp.logical_and(v == 0, f == 0))
    def _():
        ow_ref[...] = jnp.zeros_like(ow_ref)
        pend_ref[0] = 0

    def xs_copies(visit, s):
        row0 = pl.multiple_of(visit * MOE_TM, MOE_TM)
        return [pltpu.make_async_copy(xs_ref.at[pl.ds(row0, MOE_TM), c, :],
                                      words_ref.at[s, :, c * LANES:(c + 1) * LANES], xsem.at[s])
                for c in range(ROW_SUB)]

    @pl.when(f == 0)
    def _():
        @pl.when(v == 0)
        def _():
            for cp in xs_copies(0, 0):
                cp.start()

        @pl.when(v < N_VISITS)
        def _():
            for cp in xs_copies(v, slot):
                cp.wait()

        @pl.when(v + 1 < N_VISITS)
        def _():
            for cp in xs_copies(v + 1, 1 - slot):
                cp.start()

    @pl.when(jnp.logical_and(gn > 0, f == 0))
    def _():
        r = lax.broadcasted_iota(jnp.int32, (MOE_TM, 1), 0)
        lo, hi = _unpack_rows(jnp.where(r < gn, words_ref[slot], 0))
        xb_ref[:, :HALF] = lo.astype(BF16)
        xb_ref[:, HALF:] = hi.astype(BF16)

    def gate_phase(rows):
        xb = xb_ref[:rows, :]
        e = pl.ds(ge_ref[v], 1)
        col = pl.multiple_of(f * MOE_TF, MOE_TF)
        gl = jnp.dot(xb, wg_ref[...].astype(BF16), preferred_element_type=F32) + bgu_ref[e, pl.ds(col, MOE_TF)]
        up = (jnp.dot(xb, wu_ref[...].astype(BF16), preferred_element_type=F32)
              + bgu_ref[e, pl.ds(D_EXPERT + col, MOE_TF)])
        gl = jnp.minimum(gl, SWIGLU_LIMIT)
        up = jnp.clip(up, -SWIGLU_LIMIT, SWIGLU_LIMIT)
        act = ((up + 1.0) * (gl * jax.nn.sigmoid(SWIGLU_ALPHA * gl))).astype(BF16)
        act_ref[slot, :rows, pl.ds(pl.multiple_of(f * MOE_TF, MOE_TF), MOE_TF)] = act

    def down_phase(rows):
        drain_output_tile()
        y = jnp.dot(act_ref[1 - slot, :rows, :], wd_ref[...].astype(BF16),
                    preferred_element_type=F32) + bd_ref[pl.ds(de_ref[v], 1),
                                                         pl.ds(pl.multiple_of(f * MOE_TF, MOE_TF), MOE_TF)]
        ow_ref[:rows, :] = _pack_pair(y[:, :OUT_WORDS], y[:, OUT_WORDS:])
        emit_output_tile()

    for rows in range(MOE_TM, 0, -MOE_ROW_STEP):
        lower = rows - MOE_ROW_STEP
        pl.when(jnp.logical_and(gn > lower, gn <= rows))(functools.partial(gate_phase, rows))
        pl.when(jnp.logical_and(dn > lower, dn <= rows))(functools.partial(down_phase, rows))

    @pl.when(jnp.logical_and(dn == 0, v >= 1))
    def _():
        drain_output_tile()
        ow_ref[...] = jnp.zeros_like(ow_ref)
        emit_output_tile()

    @pl.when(jnp.logical_and(v == pl.num_programs(0) - 1, f == MOE_NF - 1))
    def _():
        drain_output_tile()


def _experts(xs, visit_expert, visit_nvalid, w_gate_up, b_gate_up, w_down, b_down):
    nf = MOE_NF
    gate_e = jnp.concatenate([visit_expert, visit_expert[-1:]])
    gate_n = jnp.concatenate([visit_nvalid, jnp.zeros((1,), jnp.int32)])
    down_e = jnp.concatenate([visit_expert[:1], visit_expert])
    down_n = jnp.concatenate([jnp.zeros((1,), jnp.int32), visit_nvalid])

    def tile(f, n):
        return jnp.where(n > 0, f, nf - 1)

    grid_spec = pltpu.PrefetchScalarGridSpec(
        num_scalar_prefetch=4,
        grid=(N_VISITS + 1, nf),
        in_specs=[
            pl.BlockSpec(memory_space=pl.ANY),
            pl.BlockSpec((None, D_MODEL, MOE_TF), lambda v, f, ge, gn, de, dn: (ge[v], 0, tile(f, gn[v]))),
            pl.BlockSpec((None, D_MODEL, MOE_TF), lambda v, f, ge, gn, de, dn: (ge[v], 0, nf + tile(f, gn[v]))),
            pl.BlockSpec((None, D_EXPERT, MOE_TF), lambda v, f, ge, gn, de, dn: (de[v], 0, tile(f, dn[v]))),
            pl.BlockSpec((N_EXPERTS, 2 * D_EXPERT), lambda v, f, ge, gn, de, dn: (0, 0)),
            pl.BlockSpec((N_EXPERTS, D_MODEL), lambda v, f, ge, gn, de, dn: (0, 0)),
        ],
        out_specs=pl.BlockSpec(memory_space=pl.ANY),
        scratch_shapes=[pltpu.VMEM((MOE_TM, D_MODEL), BF16),
                        pltpu.VMEM((2, MOE_TM, D_EXPERT), BF16),
                        pltpu.VMEM((2, MOE_TM, HALF), jnp.int32),
                        pltpu.VMEM((MOE_TM, OUT_WORDS), jnp.int32),
                        pltpu.SMEM((1,), jnp.int32),
                        pltpu.SemaphoreType.DMA((2,)),
                        pltpu.SemaphoreType.DMA(())],
    )
    return pl.pallas_call(
        _experts_kernel,
        grid_spec=grid_spec,
        out_shape=jax.ShapeDtypeStruct((MOE_ROWS, ROW_SUB, LANES), jnp.int32),
        compiler_params=_cparams(("arbitrary", "arbitrary")),
        name="experts",
    )(gate_e, gate_n, down_e, down_n, xs, w_gate_up, w_gate_up, w_down, b_gate_up, b_down)


def _combine_kernel(dest_ref, dest_next_ref, ys_ref, gate_ref, x_ref, lnw_ref, lnb_ref, o_ref,
                    buf_ref, z_ref, sem):
    i = pl.program_id(0)
    slot = i % 2

    def start_group(idx_ref, s, g):
        for t in range(SUBLANES):
            for k in range(TOP_K):
                d = idx_ref[0, 0, (g * SUBLANES + t) * TOP_K + k]
                pltpu.make_async_copy(ys_ref.at[d], buf_ref.at[s, k, g, :, t, :],
                                      sem.at[s]).start(priority=k % 2)

    @pl.when(i == 0)
    def _():
        def body(g, carry):
            start_group(dest_ref, slot, g)
            return carry

        lax.fori_loop(0, ROW_GROUPS, body, 0)

    for k in range(TOP_K):
        _wait_row_copies(ys_ref, sem.at[slot])

    def group(g, carry, prefetch):
        if prefetch:
            start_group(dest_next_ref, 1 - slot, g)
        rows = pl.ds(pl.multiple_of(g * SUBLANES, SUBLANES), SUBLANES)
        gates = gate_ref[rows, :]
        gate_k = [jnp.broadcast_to(gates[:, k:k + 1], (SUBLANES, LANES)) for k in range(TOP_K)]
        for c in range(ROW_SUB):
            lo_cols, hi_cols = _out_word_cols(c)
            z_lo = DN_ALPHA * x_ref[rows, lo_cols]
            z_hi = DN_ALPHA * x_ref[rows, hi_cols]
            for k in range(TOP_K):
                lo, hi = _unpack_rows(buf_ref[slot, k, g, c])
                z_lo = z_lo + gate_k[k] * lo
                z_hi = z_hi + gate_k[k] * hi
            z_ref[rows, lo_cols] = z_lo
            z_ref[rows, hi_cols] = z_hi
        return carry

    last = i + 1 == pl.num_programs(0)

    @pl.when(jnp.logical_not(last))
    def _():
        lax.fori_loop(0, ROW_GROUPS, functools.partial(group, prefetch=True), 0)

    @pl.when(last)
    def _():
        lax.fori_loop(0, ROW_GROUPS, functools.partial(group, prefetch=False), 0)

    o_ref[...] = _layer_norm(z_ref[...], lnw_ref[...], lnb_ref[...])


def _combine(ys, dest, gates, x, lnw, lnb):
    nt = TOKENS // ROW_TILE
    row = pl.BlockSpec((ROW_TILE, D_MODEL), lambda i: (i, 0))
    vec = pl.BlockSpec((1, D_MODEL), lambda i: (0, 0))
    dest_tiles = dest.reshape(nt, 1, ROW_TILE * TOP_K)
    return pl.pallas_call(
        _combine_kernel,
        grid=(nt,),
        in_specs=[pl.BlockSpec((1, 1, ROW_TILE * TOP_K), lambda i: (i, 0, 0), memory_space=pltpu.SMEM),
                  pl.BlockSpec((1, 1, ROW_TILE * TOP_K), lambda i: (jnp.minimum(i + 1, nt - 1), 0, 0),
                               memory_space=pltpu.SMEM),
                  pl.BlockSpec(memory_space=pl.ANY),
                  pl.BlockSpec((ROW_TILE, LANES), lambda i: (i, 0)),
                  row, vec, vec],
        out_specs=row,
        out_shape=jax.ShapeDtypeStruct((TOKENS, D_MODEL), F32),
        scratch_shapes=[pltpu.VMEM((2, TOP_K, ROW_GROUPS, ROW_SUB, SUBLANES, LANES), jnp.int32),
                        pltpu.VMEM((ROW_TILE, D_MODEL), F32), pltpu.SemaphoreType.DMA((2,))],
        compiler_params=_cparams(("arbitrary",)),
        name="combine",
    )(dest_tiles, dest_tiles, ys, gates, x, lnw, lnb)


def _layer(x, mem, w_in, attn_sinks, attn_norm_w, ret_norm_w, w_mix_out, ln_mix_w, ln_mix_b,
           w_mem_q, w_mem_kv, w_mem_out, ln_mem_w, ln_mem_b, w_router, b_router,
           w_gate_up, b_gate_up, w_down, b_down, ln_moe_w, ln_moe_b):
    o_k, o_v, o_qr = D_ATTN, D_ATTN + D_KV, D_ATTN + 2 * D_KV
    w_in_p = jnp.concatenate([w_in[:, :o_k], w_in[:, o_qr:], w_in[:, o_k:o_v], w_in[:, o_v:o_qr]],
                             axis=1).astype(BF16)
    proj = _project(x, w_in_p, PROJ_TM, PROJ_TN)
    y = _mixer(proj, attn_sinks.astype(F32), attn_norm_w.reshape(1, D_ATTN), ret_norm_w.reshape(1, D_RET))
    x1 = _out_ln(y, w_mix_out.astype(BF16), x, ln_mix_w.reshape(1, D_MODEL), ln_mix_b.reshape(1, D_MODEL))

    kv = _project(mem.reshape(BATCH * N_MEM, D_MODEL), w_mem_kv.astype(BF16), BATCH * N_MEM, D_MODEL)
    o = _mem_attn(x1, w_mem_q.astype(BF16), kv)
    x2 = _out_ln(o, w_mem_out.astype(BF16), x1, ln_mem_w.reshape(1, D_MODEL), ln_mem_b.reshape(1, D_MODEL))

    wr = jnp.pad(w_router, ((0, 0), (0, LANES - N_EXPERTS)))
    wr_hi = wr.astype(BF16)
    wr = jnp.concatenate([wr_hi, (wr - wr_hi.astype(F32)).astype(BF16)], axis=1)
    br = jnp.pad(b_router.reshape(1, N_EXPERTS), ((0, 0), (0, LANES - N_EXPERTS)), constant_values=NEG_INF)
    idx, gates, rank, counts = _route(x2, wr, br)

    counts = counts[0, :N_EXPERTS].astype(jnp.int32)
    padded = ((counts + MOE_TM - 1) // MOE_TM) * MOE_TM
    eid = jnp.arange(N_EXPERTS, dtype=jnp.int32)
    pad_end = jnp.sum(jnp.where(eid[None, :] <= eid[:, None], padded[None, :], 0), axis=1)
    pad_start = pad_end - padded
    chosen = idx[:TOP_K, :, None] == eid[None, None, :]
    dest = (jnp.sum(jnp.where(chosen, pad_start[None, None, :], 0), axis=-1) + rank[:TOP_K]).T
    visit_row = jnp.arange(N_VISITS, dtype=jnp.int32) * MOE_TM
    visit_expert = jnp.minimum(jnp.sum((pad_end[None, :] <= visit_row[:, None]).astype(jnp.int32), axis=1),
                               N_EXPERTS - 1)
    visit_nvalid = jnp.clip(counts[visit_expert] - (visit_row - pad_start[visit_expert]),
                            0, MOE_TM).astype(jnp.int32)

    fill_start = jnp.concatenate([pad_start + counts, pad_end[-1:]]).astype(jnp.int32)
    fill_len = jnp.concatenate([padded - counts, (MOE_ROWS - pad_end[-1:]) // FILL_SIZES[0]]).astype(jnp.int32)
    xs = _scatter_rows(x2, dest, fill_start, fill_len)
    ys = _experts(xs, visit_expert, visit_nvalid,
                  w_gate_up, b_gate_up, w_down, b_down)
    return _combine(ys, dest, gates, x2, ln_moe_w.reshape(1, D_MODEL), ln_moe_b.reshape(1, D_MODEL))


def kernel(x, mem, w_in, attn_sinks, attn_norm_w, ret_norm_w, w_mix_out, ln_mix_w, ln_mix_b, w_mem_q, w_mem_kv, w_mem_out, ln_mem_w, ln_mem_b, w_router, b_router, w_gate_up, b_gate_up, w_down, b_down, ln_moe_w, ln_moe_b):
    xt = x.reshape(TOKENS, D_MODEL)
    out = _layer(xt, mem, w_in[0], attn_sinks[0], attn_norm_w[0], ret_norm_w[0], w_mix_out[0],
                 ln_mix_w[0], ln_mix_b[0], w_mem_q[0], w_mem_kv[0], w_mem_out[0], ln_mem_w[0], ln_mem_b[0],
                 w_router[0], b_router[0], w_gate_up[0], b_gate_up[0], w_down[0], b_down[0],
                 ln_moe_w[0], ln_moe_b[0])
    return out.reshape(BATCH, SEQ, D_MODEL)
```

```python
import functools
import math

import jax
import jax.numpy as jnp
import numpy as np
from jax import lax
from jax.experimental import pallas as pl
from jax.experimental.pallas import tpu as pltpu

F32 = jnp.float32
BF16 = jnp.bfloat16

D_MODEL = 2048
BATCH = 2
SEQ = 16384
TOKENS = BATCH * SEQ

HEAD_DIM_ATTN = 64
N_Q_ATTN = 16
N_KV_ATTN = 2
GQA_GROUP = N_Q_ATTN // N_KV_ATTN
ATTN_BLOCK = 128
D_ATTN = N_Q_ATTN * HEAD_DIM_ATTN
D_KV = N_KV_ATTN * HEAD_DIM_ATTN
N_RET = 8
HEAD_DIM_RET = 128
RET_CHUNK = 128
D_RET = N_RET * HEAD_DIM_RET
D_IN = D_ATTN + 2 * D_KV + 4 * D_RET
N_MEM = 256
N_MEM_HEADS = 4
HEAD_DIM_MEM = D_MODEL // N_MEM_HEADS
N_EXPERTS = 32
TOP_K = 4
D_EXPERT = D_MODEL
SWIGLU_LIMIT = 7.0
SWIGLU_ALPHA = 1.702
LN_EPS = 1e-5
DN_ALPHA = 2.0 ** 0.25
NEG_INF = -1e30

LANES = 128
SUBLANES = 8
V7X_VMEM_BYTES = 64 * 1024 * 1024
VMEM_LIMIT = V7X_VMEM_BYTES - 6 * 1024 * 1024

PROJ_TM = 1024
PROJ_TN = 1792
LN_TM = 512
MEM_TM = 512
ROUTE_TM = 512
ROW_TILE = 256
MOE_TM = 1024
MOE_TF = 512
N_VISITS = (TOKENS * TOP_K + N_EXPERTS * (MOE_TM - 1) + MOE_TM - 1) // MOE_TM
MOE_ROWS = N_VISITS * MOE_TM

COL_QA, COL_QR, COL_KR, COL_VR, COL_GR = 0, 1, 2, 3, 4
COL_KA = 5 * D_RET // D_KV
COL_VA = COL_KA + 1


def _cparams(sem, **kw):
    return pltpu.CompilerParams(dimension_semantics=sem, vmem_limit_bytes=VMEM_LIMIT, **kw)


def _layer_norm(z, w, b):
    mu = jnp.mean(z, axis=-1, keepdims=True)
    zc = z - mu
    var = jnp.mean(zc * zc, axis=-1, keepdims=True)
    return zc * lax.rsqrt(var + LN_EPS) * w + b


def _dot_nt(a, b):
    return lax.dot_general(a, b, (((1,), (1,)), ((), ())), preferred_element_type=F32)


def _proj_kernel(x_ref, w_ref, o_ref, xb_ref):
    @pl.when(pl.program_id(1) == 0)
    def _():
        xb_ref[...] = x_ref[...].astype(BF16)

    o_ref[...] = jnp.dot(xb_ref[...], w_ref[...], preferred_element_type=F32).astype(o_ref.dtype)


def _project(x, w, tm, tn):
    m, k = x.shape
    n = w.shape[1]
    return pl.pallas_call(
        _proj_kernel,
        grid=(m // tm, n // tn),
        in_specs=[pl.BlockSpec((tm, k), lambda i, j: (i, 0)),
                  pl.BlockSpec((k, tn), lambda i, j: (0, j))],
        out_specs=pl.BlockSpec((tm, tn), lambda i, j: (i, j)),
        out_shape=jax.ShapeDtypeStruct((m, n), BF16),
        scratch_shapes=[pltpu.VMEM((tm, k), BF16)],
        compiler_params=_cparams(("parallel", "arbitrary")),
        name="project",
    )(x, w)


def _mixer_kernel(sinkrow_ref, bias_cur_ref, bias_prev_ref, decay_ref, qa_ref, kp_ref, kc_ref, vp_ref, vc_ref,
                  qr_ref, kr_ref, vr_ref, gr_ref, anw_ref, rnw_ref, y_ref, state_ref, a_ref):
    n = pl.program_id(1)

    scale = HEAD_DIM_ATTN ** -0.5
    for j in range(N_KV_ATTN):
        klo, khi = j * HEAD_DIM_ATTN, (j + 1) * HEAD_DIM_ATTN
        heads = range(j * GQA_GROUP, (j + 1) * GQA_GROUP)
        q_rows = jnp.concatenate([qa_ref[:, h * HEAD_DIM_ATTN:(h + 1) * HEAD_DIM_ATTN] for h in heads], axis=0)
        q_rows = q_rows * scale
        sink = sinkrow_ref[j]
        s_cur = _dot_nt(kc_ref[:, klo:khi], q_rows) + bias_cur_ref[j]
        s_prev = _dot_nt(kp_ref[:, klo:khi], q_rows) + bias_prev_ref[j]
        s_prev = jnp.where(n > 0, s_prev, NEG_INF)
        m = jnp.maximum(jnp.max(s_cur, axis=0, keepdims=True), jnp.max(s_prev, axis=0, keepdims=True))
        m = jnp.maximum(m, sink)
        p_cur = jnp.exp(s_cur - m)
        p_prev = jnp.exp(s_prev - m)
        denom = (jnp.sum(p_cur, axis=0, keepdims=True) + jnp.sum(p_prev, axis=0, keepdims=True)
                 + jnp.exp(sink - m))
        v_cur_t = vc_ref[:, klo:khi].astype(F32).T.astype(BF16)
        v_prev_t = vp_ref[:, klo:khi].astype(F32).T.astype(BF16)
        o_t = (jnp.dot(v_cur_t, p_cur.astype(BF16), preferred_element_type=F32)
               + jnp.dot(v_prev_t, p_prev.astype(BF16), preferred_element_type=F32))
        o_t = o_t * (1.0 / denom)
        for g, h in enumerate(heads):
            a_ref[h * HEAD_DIM_ATTN:(h + 1) * HEAD_DIM_ATTN, :] = o_t[:, g * ATTN_BLOCK:(g + 1) * ATTN_BLOCK]
    a_t = a_ref[...]
    a_t = a_t * lax.rsqrt(jnp.mean(a_t * a_t, axis=0, keepdims=True) + LN_EPS)
    y_ref[:, :D_ATTN] = (a_t.T * anw_ref[...]).astype(y_ref.dtype)

    @pl.when(n == 0)
    def _():
        state_ref[...] = jnp.zeros_like(state_ref)

    for h in range(N_RET):
        lo, hi = h * HEAD_DIM_RET, (h + 1) * HEAD_DIM_RET
        g_c = math.exp(math.log1p(-(2.0 ** (-5.0 - h))) * RET_CHUNK)
        q = qr_ref[:, lo:hi]
        k = kr_ref[:, lo:hi]
        v = vr_ref[:, lo:hi]
        s = _dot_nt(q, k) * decay_ref[h]
        inner = jnp.dot(s.astype(BF16), v, preferred_element_type=F32)
        state = state_ref[h]
        qw = (q.astype(F32) * decay_ref[N_RET + h]).astype(BF16)
        cross = jnp.dot(qw, state.astype(BF16), preferred_element_type=F32)
        kw_t = (k.astype(F32) * decay_ref[2 * N_RET + h]).T.astype(BF16)
        state_ref[h] = g_c * state + jnp.dot(kw_t, v, preferred_element_type=F32)
        o = inner + cross
        mu = jnp.mean(o, axis=-1, keepdims=True)
        oc = o - mu
        var = jnp.mean(oc * oc, axis=-1, keepdims=True)
        o = oc * lax.rsqrt(var + LN_EPS) * rnw_ref[:, lo:hi]
        g = gr_ref[:, lo:hi].astype(F32)
        o = o * (g * jax.nn.sigmoid(g))
        y_ref[:, D_ATTN + lo:D_ATTN + hi] = o.astype(y_ref.dtype)


def _mixer_tables():
    blk, gq = ATTN_BLOCK, GQA_GROUP * ATTN_BLOCK
    f32 = np.float32
    slopes = np.exp2(-8.0 * (np.arange(N_Q_ATTN, dtype=f32) + 1.0) / N_Q_ATTN).astype(f32)
    slopes = slopes.reshape(N_KV_ATTN, 1, GQA_GROUP, 1)
    key = np.arange(blk, dtype=f32).reshape(1, blk, 1, 1)
    query = np.arange(blk, dtype=f32).reshape(1, 1, 1, blk)
    dist_cur = np.broadcast_to(query - key, (N_KV_ATTN, blk, GQA_GROUP, blk))
    bias_cur = np.where(dist_cur >= 0, -slopes * dist_cur, NEG_INF).astype(f32).reshape(N_KV_ATTN, blk, gq)
    bias_prev = np.where(dist_cur < 0, -slopes * (dist_cur + blk), NEG_INF).astype(f32).reshape(N_KV_ATTN, blk, gq)

    log_g = np.log1p(-np.exp2(-5.0 - np.arange(N_RET, dtype=f32))).astype(f32).reshape(N_RET, 1, 1)
    pos = np.arange(RET_CHUNK, dtype=f32)
    lag = pos[:, None] - pos[None, :]
    kscale = f32(HEAD_DIM_RET ** -0.5)
    in_chunk = np.where(lag >= 0, np.exp(log_g * np.maximum(lag, 0.0)), 0.0) * kscale
    ones = np.ones((1, 1, HEAD_DIM_RET), f32)
    w_query = np.exp(log_g * (pos + 1.0).reshape(1, RET_CHUNK, 1)) * ones
    w_key = np.exp(log_g * (RET_CHUNK - 1.0 - pos).reshape(1, RET_CHUNK, 1)) * (kscale * ones)
    decay = np.concatenate([in_chunk, w_query, w_key], axis=0).astype(f32)
    return bias_cur, bias_prev, decay


_MIXER_TABLES = _mixer_tables()


def _mixer(proj, sinks, attn_norm_w, ret_norm_w):
    nc = SEQ // ATTN_BLOCK
    blk = ATTN_BLOCK

    def wide(colblock):
        return pl.BlockSpec((blk, D_RET), lambda b, n: (b * nc + n, colblock))

    def kv_cur(colblock):
        return pl.BlockSpec((blk, D_KV), lambda b, n: (b * nc + n, colblock))

    def kv_prev(colblock):
        return pl.BlockSpec((blk, D_KV), lambda b, n: (b * nc + jnp.maximum(n - 1, 0), colblock))

    vec = pl.BlockSpec((1, D_RET), lambda b, n: (0, 0))
    gq = GQA_GROUP * blk

    def table(rows):
        return pl.BlockSpec((N_KV_ATTN, rows, gq), lambda b, n: (0, 0, 0))

    bias_cur, bias_prev, decay = _MIXER_TABLES
    sink_rows = jnp.repeat(sinks.astype(F32).reshape(N_KV_ATTN, 1, GQA_GROUP), blk, axis=2)
    return pl.pallas_call(
        _mixer_kernel,
        grid=(BATCH, nc),
        in_specs=[table(1), table(blk), table(blk),
                  pl.BlockSpec((3 * N_RET, RET_CHUNK, HEAD_DIM_RET), lambda b, n: (0, 0, 0)),
                  wide(COL_QA), kv_prev(COL_KA), kv_cur(COL_KA), kv_prev(COL_VA), kv_cur(COL_VA),
                  wide(COL_QR), wide(COL_KR), wide(COL_VR), wide(COL_GR), vec, vec],
        out_specs=pl.BlockSpec((blk, D_MODEL), lambda b, n: (b * nc + n, 0)),
        out_shape=jax.ShapeDtypeStruct((TOKENS, D_MODEL), BF16),
        scratch_shapes=[pltpu.VMEM((N_RET, HEAD_DIM_RET, HEAD_DIM_RET), F32),
                        pltpu.VMEM((D_ATTN, blk), F32)],
        compiler_params=_cparams(("parallel", "arbitrary")),
        name="mixer",
    )(sink_rows, bias_cur, bias_prev, decay, proj, proj, proj, proj, proj, proj, proj, proj, proj,
      attn_norm_w, ret_norm_w)


def _out_ln_kernel(y_ref, w_ref, x_ref, lnw_ref, lnb_ref, o_ref):
    h = jnp.dot(y_ref[...], w_ref[...], preferred_element_type=F32)
    o_ref[...] = _layer_norm(DN_ALPHA * x_ref[...] + h, lnw_ref[...], lnb_ref[...])


def _out_ln(y, w, x, lnw, lnb):
    tm = LN_TM
    row = pl.BlockSpec((tm, D_MODEL), lambda i: (i, 0))
    vec = pl.BlockSpec((1, D_MODEL), lambda i: (0, 0))
    return pl.pallas_call(
        _out_ln_kernel,
        grid=(TOKENS // tm,),
        in_specs=[row,
                  pl.BlockSpec((D_MODEL, D_MODEL), lambda i: (0, 0), pipeline_mode=pl.Buffered(1)),
                  row, vec, vec],
        out_specs=row,
        out_shape=jax.ShapeDtypeStruct((TOKENS, D_MODEL), F32),
        compiler_params=_cparams(("parallel",)),
        name="out_ln",
    )(y, w, x, lnw, lnb)


def _mem_attn_kernel(x_ref, wq_ref, k_ref, v_ref, o_ref):
    q = jnp.dot(x_ref[...].astype(BF16), wq_ref[...], preferred_element_type=F32)
    q = (q * (HEAD_DIM_MEM ** -0.5)).astype(BF16)
    for h in range(N_MEM_HEADS):
        lo, hi = h * HEAD_DIM_MEM, (h + 1) * HEAD_DIM_MEM
        s = _dot_nt(q[:, lo:hi], k_ref[:, lo:hi])
        m = jnp.max(s, axis=-1, keepdims=True)
        p = jnp.exp(s - m)
        denom = jnp.sum(p, axis=-1, keepdims=True)
        o = jnp.dot(p.astype(BF16), v_ref[:, lo:hi], preferred_element_type=F32)
        o_ref[:, lo:hi] = (o / denom).astype(o_ref.dtype)


def _mem_attn(x, wq, kv):
    tm = MEM_TM
    nt = SEQ // tm
    row = pl.BlockSpec((tm, D_MODEL), lambda b, i: (b * nt + i, 0))
    return pl.pallas_call(
        _mem_attn_kernel,
        grid=(BATCH, nt),
        in_specs=[row,
                  pl.BlockSpec((D_MODEL, D_MODEL), lambda b, i: (0, 0), pipeline_mode=pl.Buffered(1)),
                  pl.BlockSpec((N_MEM, D_MODEL), lambda b, i: (b, 0)),
                  pl.BlockSpec((N_MEM, D_MODEL), lambda b, i: (b, 1))],
        out_specs=row,
        out_shape=jax.ShapeDtypeStruct((TOKENS, D_MODEL), BF16),
        compiler_params=_cparams(("parallel", "parallel")),
        name="mem_attn",
    )(x, wq, kv, kv)


def _route_kernel(x_ref, wr_ref, br_ref, idx_ref, gate_ref, rank_ref, count_ref, carry_ref):
    tm = x_ref.shape[0]

    @pl.when(pl.program_id(0) == 0)
    def _():
        carry_ref[...] = jnp.zeros_like(carry_ref)

    x = x_ref[...]
    x_hi = x.astype(BF16)
    x_lo = (x - x_hi.astype(F32)).astype(BF16)
    part_hi = jnp.dot(x_hi, wr_ref[...], preferred_element_type=F32)
    part_lo = jnp.dot(x_lo, wr_ref[...], preferred_element_type=F32)
    logits = (part_hi[:, :LANES] + (part_hi[:, LANES:] + part_lo[:, :LANES]) + part_lo[:, LANES:]
              + br_ref[...])
    lane = lax.broadcasted_iota(jnp.int32, (tm, LANES), 1)
    lane_f = lane.astype(F32)
    work = logits
    sels, vals, idxs = [], [], []
    for _ in range(TOP_K):
        v = jnp.max(work, axis=-1, keepdims=True)
        i = jnp.min(jnp.where(work == v, lane_f, float(LANES)), axis=-1, keepdims=True)
        sel = lane_f == i
        work = jnp.where(sel, -jnp.inf, work)
        sels.append(sel)
        vals.append(v)
        idxs.append(i)
    exps = [jnp.exp(v - vals[0]) for v in vals]
    total = exps[0] + exps[1] + exps[2] + exps[3]

    onehot = jnp.zeros((tm, LANES), F32)
    for sel in sels:
        onehot = jnp.where(sel, 1.0, onehot)
    r = lax.broadcasted_iota(jnp.int32, (tm, tm), 0)
    c = lax.broadcasted_iota(jnp.int32, (tm, tm), 1)
    below = jnp.where(c < r, 1.0, 0.0).astype(BF16)
    prefix = jnp.dot(below, onehot.astype(BF16), preferred_element_type=F32)
    rankmat = prefix + carry_ref[...]
    carry_ref[...] = carry_ref[...] + jnp.sum(onehot, axis=0, keepdims=True)
    count_ref[...] = carry_ref[...]

    idx_out = jnp.zeros((tm, LANES), F32)
    gate_out = jnp.zeros((tm, LANES), F32)
    rank_out = jnp.zeros((tm, LANES), F32)
    for k in range(TOP_K):
        rk = jnp.sum(jnp.where(sels[k], rankmat, 0.0), axis=-1, keepdims=True)
        idx_out = jnp.where(lane == k, idxs[k], idx_out)
        gate_out = jnp.where(lane == k, exps[k] / total, gate_out)
        rank_out = jnp.where(lane == k, rk, rank_out)
    gate_ref[...] = gate_out
    idx_ref[...] = idx_out.T[:SUBLANES, :].astype(jnp.int32)
    rank_ref[...] = rank_out.T[:SUBLANES, :].astype(jnp.int32)


def _route(x, wr, br):
    tm = ROUTE_TM
    wide = pl.BlockSpec((tm, LANES), lambda i: (i, 0))
    rows = pl.BlockSpec((SUBLANES, tm), lambda i: (0, i))
    return pl.pallas_call(
        _route_kernel,
        grid=(TOKENS // tm,),
        in_specs=[pl.BlockSpec((tm, D_MODEL), lambda i: (i, 0)),
                  pl.BlockSpec((D_MODEL, 2 * LANES), lambda i: (0, 0)),
                  pl.BlockSpec((1, LANES), lambda i: (0, 0))],
        out_specs=[rows, wide, rows, pl.BlockSpec((1, LANES), lambda i: (0, 0))],
        out_shape=[jax.ShapeDtypeStruct((SUBLANES, TOKENS), jnp.int32),
                   jax.ShapeDtypeStruct((TOKENS, LANES), F32),
                   jax.ShapeDtypeStruct((SUBLANES, TOKENS), jnp.int32),
                   jax.ShapeDtypeStruct((1, LANES), F32)],
        scratch_shapes=[pltpu.VMEM((1, LANES), F32)],
        compiler_params=_cparams(("arbitrary",)),
        name="route",
    )(x, wr, br)


HALF = D_MODEL // 2


def _pack_rows(x):
    return _pack_pair(x[:, :HALF], x[:, HALF:])


def _pack_pair(lo, hi):
    lo = lax.bitcast_convert_type(lo.astype(BF16).astype(F32), jnp.int32)
    hi = lax.bitcast_convert_type(hi.astype(BF16).astype(F32), jnp.int32)
    return hi | lax.shift_right_logical(lo, 16)


def _unpack_rows(w):
    lo = lax.bitcast_convert_type(lax.shift_left(w, 16), F32)
    hi = lax.bitcast_convert_type(w & jnp.int32(-65536), F32)
    return lo, hi


def _wait_row_copies(rows_ref, sem):
    pltpu.make_async_copy(rows_ref.at[pl.ds(0, ROW_TILE)], rows_ref.at[pl.ds(0, ROW_TILE)], sem).wait()


ROW_SUB = HALF // LANES
ROW_GROUPS = ROW_TILE // SUBLANES


FILL_SIZES = tuple(MOE_TM >> (b + 1) for b in range(MOE_TM.bit_length() - 1))


def _fill_copies(fill_start_ref, fill_len_ref, zero_ref, xs_ref, sem, act):
    def body(e, carry):
        pos = fill_start_ref[e]
        n = fill_len_ref[e]
        for size in FILL_SIZES:
            take = (n & size) != 0

            @pl.when(take)
            def _():
                cp = pltpu.make_async_copy(zero_ref.at[pl.ds(0, size)], xs_ref.at[pl.ds(pos, size)], sem)
                getattr(cp, act)()

            pos = pos + jnp.where(take, size, 0)
        return carry

    lax.fori_loop(0, N_EXPERTS, body, 0)

    def tail(j, carry):
        pos = fill_start_ref[N_EXPERTS] + j * FILL_SIZES[0]
        cp = pltpu.make_async_copy(zero_ref, xs_ref.at[pl.ds(pos, FILL_SIZES[0])], sem)
        getattr(cp, act)()
        return carry

    lax.fori_loop(0, fill_len_ref[N_EXPERTS], tail, 0)


def _scatter_kernel(fill_start_ref, fill_len_ref, dest_ref, x_ref, xs_ref, packed_ref, zero_ref, sem, fill_sem):
    i = pl.program_id(0)
    slot = i % 2

    @pl.when(i == 0)
    def _():
        zero_ref[...] = jnp.zeros_like(zero_ref)
        _fill_copies(fill_start_ref, fill_len_ref, zero_ref, xs_ref, fill_sem, "start")

    words = _pack_rows(x_ref[...])
    for g in range(ROW_GROUPS):
        for c in range(ROW_SUB):
            packed_ref[slot, g, c] = words[g * SUBLANES:(g + 1) * SUBLANES, c * LANES:(c + 1) * LANES]

    def body(g, carry):
        for s in range(SUBLANES):
            for k in range(TOP_K):
                d = dest_ref[0, 0, (g * SUBLANES + s) * TOP_K + k]
                pltpu.make_async_copy(packed_ref.at[slot, g, :, s, :], xs_ref.at[d],
                                      sem.at[slot]).start(priority=k % 2)
        return carry

    lax.fori_loop(0, ROW_GROUPS, body, 0)

    @pl.when(i > 0)
    def _():
        for k in range(TOP_K):
            _wait_row_copies(xs_ref, sem.at[1 - slot])

    @pl.when(i == pl.num_programs(0) - 1)
    def _():
        for k in range(TOP_K):
            _wait_row_copies(xs_ref, sem.at[slot])

    @pl.when(i == 0)
    def _():
        _fill_copies(fill_start_ref, fill_len_ref, zero_ref, xs_ref, fill_sem, "wait")


def _scatter_rows(x, dest, fill_start, fill_len):
    nt = TOKENS // ROW_TILE
    grid_spec = pltpu.PrefetchScalarGridSpec(
        num_scalar_prefetch=2,
        grid=(nt,),
        in_specs=[pl.BlockSpec((1, 1, ROW_TILE * TOP_K), lambda i, fs, fl: (i, 0, 0), memory_space=pltpu.SMEM),
                  pl.BlockSpec((ROW_TILE, D_MODEL), lambda i, fs, fl: (i, 0))],
        out_specs=pl.BlockSpec(memory_space=pl.ANY),
        scratch_shapes=[pltpu.VMEM((2, ROW_GROUPS, ROW_SUB, SUBLANES, LANES), jnp.int32),
                        pltpu.VMEM((FILL_SIZES[0], ROW_SUB, LANES), jnp.int32),
                        pltpu.SemaphoreType.DMA((2,)), pltpu.SemaphoreType.DMA(())],
    )
    return pl.pallas_call(
        _scatter_kernel,
        grid_spec=grid_spec,
        out_shape=jax.ShapeDtypeStruct((MOE_ROWS, ROW_SUB, LANES), jnp.int32),
        compiler_params=_cparams(("arbitrary",), has_side_effects=True),
        name="scatter_rows",
    )(fill_start, fill_len, dest, x)


MOE_NF = D_EXPERT // MOE_TF
OUT_WORDS = MOE_TF // 2
OUT_SUB = OUT_WORDS // LANES


def _out_word_cols(c):
    tile, j = divmod(c, OUT_SUB)
    lo = tile * MOE_TF + j * LANES
    return slice(lo, lo + LANES), slice(lo + OUT_WORDS, lo + OUT_WORDS + LANES)
MOE_ROW_STEP = 256


def _experts_kernel(ge_ref, gn_ref, de_ref, dn_ref, xs_ref, wg_ref, wu_ref, wd_ref, bgu_ref, bd_ref,
                    ys_ref, xb_ref, act_ref, words_ref, ow_ref, pend_ref, xsem, ysem):
    v = pl.program_id(0)
    f = pl.program_id(1)
    gn = gn_ref[v]
    dn = dn_ref[v]
    slot = v % 2

    def ys_copies(visit, tile):
        row0 = pl.multiple_of(visit * MOE_TM, MOE_TM)
        return [pltpu.make_async_copy(ow_ref.at[:, j * LANES:(j + 1) * LANES],
                                      ys_ref.at[pl.ds(row0, MOE_TM), tile * OUT_SUB + j, :], ysem)
                for j in range(OUT_SUB)]

    def emit_output_tile():
        for cp in ys_copies(v - 1, f):
            cp.start()
        pend_ref[0] = 1

    def drain_output_tile():
        @pl.when(pend_ref[0] == 1)
        def _():
            for cp in ys_copies(0, 0):
                cp.wait()
            pend_ref[0] = 0

    @pl.when(jnp.logical_and(v == 0, f == 0))
    def _():
        ow_ref[...] = jnp.zeros_like(ow_ref)
        pend_ref[0] = 0

    def xs_copies(visit, s):
        row0 = pl.multiple_of(visit * MOE_TM, MOE_TM)
        return [pltpu.make_async_copy(xs_ref.at[pl.ds(row0, MOE_TM), c, :],
                                      words_ref.at[s, :, c * LANES:(c + 1) * LANES], xsem.at[s])
                for c in range(ROW_SUB)]

    @pl.when(f == 0)
    def _():
        @pl.when(v == 0)
        def _():
            for cp in xs_copies(0, 0):
                cp.start()

        @pl.when(v < N_VISITS)
        def _():
            for cp in xs_copies(v, slot):
                cp.wait()

        @pl.when(v + 1 < N_VISITS)
        def _():
            for cp in xs_copies(v + 1, 1 - slot):
                cp.start()

    @pl.when(jnp.logical_and(gn > 0, f == 0))
    def _():
        r = lax.broadcasted_iota(jnp.int32, (MOE_TM, 1), 0)
        lo, hi = _unpack_rows(jnp.where(r < gn, words_ref[slot], 0))
        xb_ref[:, :HALF] = lo.astype(BF16)
        xb_ref[:, HALF:] = hi.astype(BF16)

    def gate_phase(rows):
        xb = xb_ref[:rows, :]
        e = pl.ds(ge_ref[v], 1)
        col = pl.multiple_of(f * MOE_TF, MOE_TF)
        gl = jnp.dot(xb, wg_ref[...].astype(BF16), preferred_element_type=F32) + bgu_ref[e, pl.ds(col, MOE_TF)]
        up = (jnp.dot(xb, wu_ref[...].astype(BF16), preferred_element_type=F32)
              + bgu_ref[e, pl.ds(D_EXPERT + col, MOE_TF)])
        gl = jnp.minimum(gl, SWIGLU_LIMIT)
        up = jnp.clip(up, -SWIGLU_LIMIT, SWIGLU_LIMIT)
        act = ((up + 1.0) * (gl * jax.nn.sigmoid(SWIGLU_ALPHA * gl))).astype(BF16)
        act_ref[slot, :rows, pl.ds(pl.multiple_of(f * MOE_TF, MOE_TF), MOE_TF)] = act

    def down_phase(rows):
        drain_output_tile()
        y = jnp.dot(act_ref[1 - slot, :rows, :], wd_ref[...].astype(BF16),
                    preferred_element_type=F32) + bd_ref[pl.ds(de_ref[v], 1),
                                                         pl.ds(pl.multiple_of(f * MOE_TF, MOE_TF), MOE_TF)]
        ow_ref[:rows, :] = _pack_pair(y[:, :OUT_WORDS], y[:, OUT_WORDS:])
        emit_output_tile()

    for rows in range(MOE_TM, 0, -MOE_ROW_STEP):
        lower = rows - MOE_ROW_STEP
        pl.when(jnp.logical_and(gn > lower, gn <= rows))(functools.partial(gate_phase, rows))
        pl.when(jnp.logical_and(dn > lower, dn <= rows))(functools.partial(down_phase, rows))

    @pl.when(jnp.logical_and(dn == 0, v >= 1))
    def _():
        drain_output_tile()
        ow_ref[...] = jnp.zeros_like(ow_ref)
        emit_output_tile()

    @pl.when(jnp.logical_and(v == pl.num_programs(0) - 1, f == MOE_NF - 1))
    def _():
        drain_output_tile()


def _experts(xs, visit_expert, visit_nvalid, w_gate_up, b_gate_up, w_down, b_down):
    nf = MOE_NF
    gate_e = jnp.concatenate([visit_expert, visit_expert[-1:]])
    gate_n = jnp.concatenate([visit_nvalid, jnp.zeros((1,), jnp.int32)])
    down_e = jnp.concatenate([visit_expert[:1], visit_expert])
    down_n = jnp.concatenate([jnp.zeros((1,), jnp.int32), visit_nvalid])

    def tile(f, n):
        return jnp.where(n > 0, f, nf - 1)

    grid_spec = pltpu.PrefetchScalarGridSpec(
        num_scalar_prefetch=4,
        grid=(N_VISITS + 1, nf),
        in_specs=[
            pl.BlockSpec(memory_space=pl.ANY),
            pl.BlockSpec((None, D_MODEL, MOE_TF), lambda v, f, ge, gn, de, dn: (ge[v], 0, tile(f, gn[v]))),
            pl.BlockSpec((None, D_MODEL, MOE_TF), lambda v, f, ge, gn, de, dn: (ge[v], 0, nf + tile(f, gn[v]))),
            pl.BlockSpec((None, D_EXPERT, MOE_TF), lambda v, f, ge, gn, de, dn: (de[v], 0, tile(f, dn[v]))),
            pl.BlockSpec((N_EXPERTS, 2 * D_EXPERT), lambda v, f, ge, gn, de, dn: (0, 0)),
            pl.BlockSpec((N_EXPERTS, D_MODEL), lambda v, f, ge, gn, de, dn: (0, 0)),
        ],
        out_specs=pl.BlockSpec(memory_space=pl.ANY),
        scratch_shapes=[pltpu.VMEM((MOE_TM, D_MODEL), BF16),
                        pltpu.VMEM((2, MOE_TM, D_EXPERT), BF16),
                        pltpu.VMEM((2, MOE_TM, HALF), jnp.int32),
                        pltpu.VMEM((MOE_TM, OUT_WORDS), jnp.int32),
                        pltpu.SMEM((1,), jnp.int32),
                        pltpu.SemaphoreType.DMA((2,)),
                        pltpu.SemaphoreType.DMA(())],
    )
    return pl.pallas_call(
        _experts_kernel,
        grid_spec=grid_spec,
        out_shape=jax.ShapeDtypeStruct((MOE_ROWS, ROW_SUB, LANES), jnp.int32),
        compiler_params=_cparams(("arbitrary", "arbitrary")),
        name="experts",
    )(gate_e, gate_n, down_e, down_n, xs, w_gate_up, w_gate_up, w_down, b_gate_up, b_down)


def _combine_kernel(dest_ref, dest_next_ref, ys_ref, gate_ref, x_ref, lnw_ref, lnb_ref, o_ref,
                    buf_ref, z_ref, sem):
    i = pl.program_id(0)
    slot = i % 2

    def start_group(idx_ref, s, g):
        for t in range(SUBLANES):
            for k in range(TOP_K):
                d = idx_ref[0, 0, (g * SUBLANES + t) * TOP_K + k]
                pltpu.make_async_copy(ys_ref.at[d], buf_ref.at[s, k, g, :, t, :],
                                      sem.at[s]).start(priority=k % 2)

    @pl.when(i == 0)
    def _():
        def body(g, carry):
            start_group(dest_ref, slot, g)
            return carry

        lax.fori_loop(0, ROW_GROUPS, body, 0)

    for k in range(TOP_K):
        _wait_row_copies(ys_ref, sem.at[slot])

    def group(g, carry, prefetch):
        if prefetch:
            start_group(dest_next_ref, 1 - slot, g)
        rows = pl.ds(pl.multiple_of(g * SUBLANES, SUBLANES), SUBLANES)
        gates = gate_ref[rows, :]
        gate_k = [jnp.broadcast_to(gates[:, k:k + 1], (SUBLANES, LANES)) for k in range(TOP_K)]
        for c in range(ROW_SUB):
            lo_cols, hi_cols = _out_word_cols(c)
            z_lo = DN_ALPHA * x_ref[rows, lo_cols]
            z_hi = DN_ALPHA * x_ref[rows, hi_cols]
            for k in range(TOP_K):
                lo, hi = _unpack_rows(buf_ref[slot, k, g, c])
                z_lo = z_lo + gate_k[k] * lo
                z_hi = z_hi + gate_k[k] * hi
            z_ref[rows, lo_cols] = z_lo
            z_ref[rows, hi_cols] = z_hi
        return carry

    last = i + 1 == pl.num_programs(0)

    @pl.when(jnp.logical_not(last))
    def _():
        lax.fori_loop(0, ROW_GROUPS, functools.partial(group, prefetch=True), 0)

    @pl.when(last)
    def _():
        lax.fori_loop(0, ROW_GROUPS, functools.partial(group, prefetch=False), 0)

    o_ref[...] = _layer_norm(z_ref[...], lnw_ref[...], lnb_ref[...])


def _combine(ys, dest, gates, x, lnw, lnb):
    nt = TOKENS // ROW_TILE
    row = pl.BlockSpec((ROW_TILE, D_MODEL), lambda i: (i, 0))
    vec = pl.BlockSpec((1, D_MODEL), lambda i: (0, 0))
    dest_tiles = dest
    return pl.pallas_call(
        _combine_kernel,
        grid=(nt,),
        in_specs=[pl.BlockSpec((1, 1, ROW_TILE * TOP_K), lambda i: (i, 0, 0), memory_space=pltpu.SMEM),
                  pl.BlockSpec((1, 1, ROW_TILE * TOP_K), lambda i: (jnp.minimum(i + 1, nt - 1), 0, 0),
                               memory_space=pltpu.SMEM),
                  pl.BlockSpec(memory_space=pl.ANY),
                  pl.BlockSpec((ROW_TILE, LANES), lambda i: (i, 0)),
                  row, vec, vec],
        out_specs=row,
        out_shape=jax.ShapeDtypeStruct((TOKENS, D_MODEL), F32),
        scratch_shapes=[pltpu.VMEM((2, TOP_K, ROW_GROUPS, ROW_SUB, SUBLANES, LANES), jnp.int32),
                        pltpu.VMEM((ROW_TILE, D_MODEL), F32), pltpu.SemaphoreType.DMA((2,))],
        compiler_params=_cparams(("arbitrary",)),
        name="combine",
    )(dest_tiles, dest_tiles, ys, gates, x, lnw, lnb)


def _layer(x, mem, w_in, attn_sinks, attn_norm_w, ret_norm_w, w_mix_out, ln_mix_w, ln_mix_b,
           w_mem_q, w_mem_kv, w_mem_out, ln_mem_w, ln_mem_b, w_router, b_router,
           w_gate_up, b_gate_up, w_down, b_down, ln_moe_w, ln_moe_b):
    o_k, o_v, o_qr = D_ATTN, D_ATTN + D_KV, D_ATTN + 2 * D_KV
    w_in_p = jnp.concatenate([w_in[:, :o_k], w_in[:, o_qr:], w_in[:, o_k:o_v], w_in[:, o_v:o_qr]],
                             axis=1).astype(BF16)
    proj = _project(x, w_in_p, PROJ_TM, PROJ_TN)
    y = _mixer(proj, attn_sinks.astype(F32), attn_norm_w.reshape(1, D_ATTN), ret_norm_w.reshape(1, D_RET))
    x1 = _out_ln(y, w_mix_out.astype(BF16), x, ln_mix_w.reshape(1, D_MODEL), ln_mix_b.reshape(1, D_MODEL))

    kv = _project(mem.reshape(BATCH * N_MEM, D_MODEL), w_mem_kv.astype(BF16), BATCH * N_MEM, D_MODEL)
    o = _mem_attn(x1, w_mem_q.astype(BF16), kv)
    x2 = _out_ln(o, w_mem_out.astype(BF16), x1, ln_mem_w.reshape(1, D_MODEL), ln_mem_b.reshape(1, D_MODEL))

    wr = jnp.pad(w_router, ((0, 0), (0, LANES - N_EXPERTS)))
    wr_hi = wr.astype(BF16)
    wr = jnp.concatenate([wr_hi, (wr - wr_hi.astype(F32)).astype(BF16)], axis=1)
    br = jnp.pad(b_router.reshape(1, N_EXPERTS), ((0, 0), (0, LANES - N_EXPERTS)), constant_values=NEG_INF)
    idx, gates, rank, counts = _route(x2, wr, br)

    counts = counts[0, :N_EXPERTS].astype(jnp.int32)
    padded = ((counts + MOE_TM - 1) // MOE_TM) * MOE_TM
    eid = jnp.arange(N_EXPERTS, dtype=jnp.int32)
    pad_end = jnp.sum(jnp.where(eid[None, :] <= eid[:, None], padded[None, :], 0), axis=1)
    pad_start = pad_end - padded
    chosen = idx[:TOP_K, :, None] == eid[None, None, :]
    dest_kt = jnp.sum(jnp.where(chosen, pad_start[None, None, :], 0), axis=-1) + rank[:TOP_K]
    nt = TOKENS // ROW_TILE
    dest = dest_kt.reshape(TOP_K, nt, ROW_TILE).transpose(1, 2, 0).reshape(nt, 1, ROW_TILE * TOP_K)
    visit_row = jnp.arange(N_VISITS, dtype=jnp.int32) * MOE_TM
    visit_expert = jnp.minimum(jnp.sum((pad_end[None, :] <= visit_row[:, None]).astype(jnp.int32), axis=1),
                               N_EXPERTS - 1)
    of_visit = visit_expert[:, None] == eid[None, :]
    visit_count = jnp.sum(jnp.where(of_visit, counts[None, :], 0), axis=1)
    visit_start = jnp.sum(jnp.where(of_visit, pad_start[None, :], 0), axis=1)
    visit_nvalid = jnp.clip(visit_count - (visit_row - visit_start), 0, MOE_TM).astype(jnp.int32)

    fill_start = jnp.concatenate([pad_start + counts, pad_end[-1:]]).astype(jnp.int32)
    fill_len = jnp.concatenate([padded - counts, (MOE_ROWS - pad_end[-1:]) // FILL_SIZES[0]]).astype(jnp.int32)
    xs = _scatter_rows(x2, dest, fill_start, fill_len)
    ys = _experts(xs, visit_expert, visit_nvalid,
                  w_gate_up, b_gate_up, w_down, b_down)
    return _combine(ys, dest, gates, x2, ln_moe_w.reshape(1, D_MODEL), ln_moe_b.reshape(1, D_MODEL))


def kernel(x, mem, w_in, attn_sinks, attn_norm_w, ret_norm_w, w_mix_out, ln_mix_w, ln_mix_b, w_mem_q, w_mem_kv, w_mem_out, ln_mem_w, ln_mem_b, w_router, b_router, w_gate_up, b_gate_up, w_down, b_down, ln_moe_w, ln_moe_b):
    xt = x.reshape(TOKENS, D_MODEL)
    out = _layer(xt, mem, w_in[0], attn_sinks[0], attn_norm_w[0], ret_norm_w[0], w_mix_out[0],
                 ln_mix_w[0], ln_mix_b[0], w_mem_q[0], w_mem_kv[0], w_mem_out[0], ln_mem_w[0], ln_mem_b[0],
                 w_router[0], b_router[0], w_gate_up[0], b_gate_up[0], w_down[0], b_down[0],
                 ln_moe_w[0], ln_moe_b[0])
    return out.reshape(BATCH, SEQ, D_MODEL)
```

```python
import functools
import math

import jax
import jax.numpy as jnp
import numpy as np
from jax import lax
from jax.experimental import pallas as pl
from jax.experimental.pallas import tpu as pltpu

F32 = jnp.float32
BF16 = jnp.bfloat16

D_MODEL = 2048
BATCH = 2
SEQ = 16384
TOKENS = BATCH * SEQ

HEAD_DIM_ATTN = 64
N_Q_ATTN = 16
N_KV_ATTN = 2
GQA_GROUP = N_Q_ATTN // N_KV_ATTN
ATTN_BLOCK = 128
D_ATTN = N_Q_ATTN * HEAD_DIM_ATTN
D_KV = N_KV_ATTN * HEAD_DIM_ATTN
N_RET = 8
HEAD_DIM_RET = 128
RET_CHUNK = 128
D_RET = N_RET * HEAD_DIM_RET
D_IN = D_ATTN + 2 * D_KV + 4 * D_RET
N_MEM = 256
N_MEM_HEADS = 4
HEAD_DIM_MEM = D_MODEL // N_MEM_HEADS
N_EXPERTS = 32
TOP_K = 4
D_EXPERT = D_MODEL
SWIGLU_LIMIT = 7.0
SWIGLU_ALPHA = 1.702
LN_EPS = 1e-5
DN_ALPHA = 2.0 ** 0.25
NEG_INF = -1e30

LANES = 128
SUBLANES = 8
V7X_VMEM_BYTES = 64 * 1024 * 1024
VMEM_LIMIT = V7X_VMEM_BYTES - 6 * 1024 * 1024

PROJ_TM = 1024
PROJ_TN = 1792
LN_TM = 512
MEM_TM = 512
ROUTE_TM = 512
ROW_TILE = 256
MOE_TM = 1024
MOE_TF = 512
N_VISITS = (TOKENS * TOP_K + N_EXPERTS * (MOE_TM - 1) + MOE_TM - 1) // MOE_TM
MOE_ROWS = N_VISITS * MOE_TM

COL_QA, COL_QR, COL_KR, COL_VR, COL_GR = 0, 1, 2, 3, 4
COL_KA = 5 * D_RET // D_KV
COL_VA = COL_KA + 1


def _cparams(sem, **kw):
    return pltpu.CompilerParams(dimension_semantics=sem, vmem_limit_bytes=VMEM_LIMIT, **kw)


def _layer_norm(z, w, b):
    mu = jnp.mean(z, axis=-1, keepdims=True)
    zc = z - mu
    var = jnp.mean(zc * zc, axis=-1, keepdims=True)
    return zc * lax.rsqrt(var + LN_EPS) * w + b


def _dot_nt(a, b):
    return lax.dot_general(a, b, (((1,), (1,)), ((), ())), preferred_element_type=F32)


def _proj_kernel(x_ref, w_ref, o_ref, xb_ref):
    @pl.when(pl.program_id(1) == 0)
    def _():
        xb_ref[...] = x_ref[...].astype(BF16)

    o_ref[...] = jnp.dot(xb_ref[...], w_ref[...], preferred_element_type=F32).astype(o_ref.dtype)


def _project(x, w, tm, tn):
    m, k = x.shape
    n = w.shape[1]
    return pl.pallas_call(
        _proj_kernel,
        grid=(m // tm, n // tn),
        in_specs=[pl.BlockSpec((tm, k), lambda i, j: (i, 0)),
                  pl.BlockSpec((k, tn), lambda i, j: (0, j))],
        out_specs=pl.BlockSpec((tm, tn), lambda i, j: (i, j)),
        out_shape=jax.ShapeDtypeStruct((m, n), BF16),
        scratch_shapes=[pltpu.VMEM((tm, k), BF16)],
        compiler_params=_cparams(("parallel", "arbitrary")),
        name="project",
    )(x, w)


def _mixer_kernel(sinkrow_ref, bias_cur_ref, bias_prev_ref, decay_ref, qa_ref, kp_ref, kc_ref, vp_ref, vc_ref,
                  qr_ref, kr_ref, vr_ref, gr_ref, anw_ref, rnw_ref, y_ref, state_ref, a_ref):
    n = pl.program_id(1)

    scale = HEAD_DIM_ATTN ** -0.5
    for j in range(N_KV_ATTN):
        klo, khi = j * HEAD_DIM_ATTN, (j + 1) * HEAD_DIM_ATTN
        heads = range(j * GQA_GROUP, (j + 1) * GQA_GROUP)
        q_rows = jnp.concatenate([qa_ref[:, h * HEAD_DIM_ATTN:(h + 1) * HEAD_DIM_ATTN] for h in heads], axis=0)
        q_rows = q_rows * scale
        sink = sinkrow_ref[j]
        s_cur = _dot_nt(kc_ref[:, klo:khi], q_rows) + bias_cur_ref[j]
        s_prev = _dot_nt(kp_ref[:, klo:khi], q_rows) + bias_prev_ref[j]
        s_prev = jnp.where(n > 0, s_prev, NEG_INF)
        m = jnp.maximum(jnp.max(s_cur, axis=0, keepdims=True), jnp.max(s_prev, axis=0, keepdims=True))
        m = jnp.maximum(m, sink)
        p_cur = jnp.exp(s_cur - m)
        p_prev = jnp.exp(s_prev - m)
        denom = (jnp.sum(p_cur, axis=0, keepdims=True) + jnp.sum(p_prev, axis=0, keepdims=True)
                 + jnp.exp(sink - m))
        v_cur_t = vc_ref[:, klo:khi].astype(F32).T.astype(BF16)
        v_prev_t = vp_ref[:, klo:khi].astype(F32).T.astype(BF16)
        o_t = (jnp.dot(v_cur_t, p_cur.astype(BF16), preferred_element_type=F32)
               + jnp.dot(v_prev_t, p_prev.astype(BF16), preferred_element_type=F32))
        o_t = o_t * (1.0 / denom)
        for g, h in enumerate(heads):
            a_ref[h * HEAD_DIM_ATTN:(h + 1) * HEAD_DIM_ATTN, :] = o_t[:, g * ATTN_BLOCK:(g + 1) * ATTN_BLOCK]
    a_t = a_ref[...]
    a_t = a_t * lax.rsqrt(jnp.mean(a_t * a_t, axis=0, keepdims=True) + LN_EPS)
    y_ref[:, :D_ATTN] = (a_t.T * anw_ref[...]).astype(y_ref.dtype)

    @pl.when(n == 0)
    def _():
        state_ref[...] = jnp.zeros_like(state_ref)

    for h in range(N_RET):
        lo, hi = h * HEAD_DIM_RET, (h + 1) * HEAD_DIM_RET
        g_c = math.exp(math.log1p(-(2.0 ** (-5.0 - h))) * RET_CHUNK)
        q = qr_ref[:, lo:hi]
        k = kr_ref[:, lo:hi]
        v = vr_ref[:, lo:hi]
        s = _dot_nt(q, k) * decay_ref[h]
        inner = jnp.dot(s.astype(BF16), v, preferred_element_type=F32)
        state = state_ref[h]
        qw = (q.astype(F32) * decay_ref[N_RET + h]).astype(BF16)
        cross = jnp.dot(qw, state.astype(BF16), preferred_element_type=F32)
        kw_t = (k.astype(F32) * decay_ref[2 * N_RET + h]).T.astype(BF16)
        state_ref[h] = g_c * state + jnp.dot(kw_t, v, preferred_element_type=F32)
        o = inner + cross
        mu = jnp.mean(o, axis=-1, keepdims=True)
        oc = o - mu
        var = jnp.mean(oc * oc, axis=-1, keepdims=True)
        o = oc * lax.rsqrt(var + LN_EPS) * rnw_ref[:, lo:hi]
        g = gr_ref[:, lo:hi].astype(F32)
        o = o * (g * jax.nn.sigmoid(g))
        y_ref[:, D_ATTN + lo:D_ATTN + hi] = o.astype(y_ref.dtype)


def _mixer_tables():
    blk, gq = ATTN_BLOCK, GQA_GROUP * ATTN_BLOCK
    f32 = np.float32
    slopes = np.exp2(-8.0 * (np.arange(N_Q_ATTN, dtype=f32) + 1.0) / N_Q_ATTN).astype(f32)
    slopes = slopes.reshape(N_KV_ATTN, 1, GQA_GROUP, 1)
    key = np.arange(blk, dtype=f32).reshape(1, blk, 1, 1)
    query = np.arange(blk, dtype=f32).reshape(1, 1, 1, blk)
    dist_cur = np.broadcast_to(query - key, (N_KV_ATTN, blk, GQA_GROUP, blk))
    bias_cur = np.where(dist_cur >= 0, -slopes * dist_cur, NEG_INF).astype(f32).reshape(N_KV_ATTN, blk, gq)
    bias_prev = np.where(dist_cur < 0, -slopes * (dist_cur + blk), NEG_INF).astype(f32).reshape(N_KV_ATTN, blk, gq)

    log_g = np.log1p(-np.exp2(-5.0 - np.arange(N_RET, dtype=f32))).astype(f32).reshape(N_RET, 1, 1)
    pos = np.arange(RET_CHUNK, dtype=f32)
    lag = pos[:, None] - pos[None, :]
    kscale = f32(HEAD_DIM_RET ** -0.5)
    in_chunk = np.where(lag >= 0, np.exp(log_g * np.maximum(lag, 0.0)), 0.0) * kscale
    ones = np.ones((1, 1, HEAD_DIM_RET), f32)
    w_query = np.exp(log_g * (pos + 1.0).reshape(1, RET_CHUNK, 1)) * ones
    w_key = np.exp(log_g * (RET_CHUNK - 1.0 - pos).reshape(1, RET_CHUNK, 1)) * (kscale * ones)
    decay = np.concatenate([in_chunk, w_query, w_key], axis=0).astype(f32)
    return bias_cur, bias_prev, decay


_MIXER_TABLES = _mixer_tables()


def _mixer(proj, sinks, attn_norm_w, ret_norm_w):
    nc = SEQ // ATTN_BLOCK
    blk = ATTN_BLOCK

    def wide(colblock):
        return pl.BlockSpec((blk, D_RET), lambda b, n: (b * nc + n, colblock))

    def kv_cur(colblock):
        return pl.BlockSpec((blk, D_KV), lambda b, n: (b * nc + n, colblock))

    def kv_prev(colblock):
        return pl.BlockSpec((blk, D_KV), lambda b, n: (b * nc + jnp.maximum(n - 1, 0), colblock))

    vec = pl.BlockSpec((1, D_RET), lambda b, n: (0, 0))
    gq = GQA_GROUP * blk

    def table(rows):
        return pl.BlockSpec((N_KV_ATTN, rows, gq), lambda b, n: (0, 0, 0))

    bias_cur, bias_prev, decay = _MIXER_TABLES
    sink_rows = jnp.repeat(sinks.astype(F32).reshape(N_KV_ATTN, 1, GQA_GROUP), blk, axis=2)
    return pl.pallas_call(
        _mixer_kernel,
        grid=(BATCH, nc),
        in_specs=[table(1), table(blk), table(blk),
                  pl.BlockSpec((3 * N_RET, RET_CHUNK, HEAD_DIM_RET), lambda b, n: (0, 0, 0)),
                  wide(COL_QA), kv_prev(COL_KA), kv_cur(COL_KA), kv_prev(COL_VA), kv_cur(COL_VA),
                  wide(COL_QR), wide(COL_KR), wide(COL_VR), wide(COL_GR), vec, vec],
        out_specs=pl.BlockSpec((blk, D_MODEL), lambda b, n: (b * nc + n, 0)),
        out_shape=jax.ShapeDtypeStruct((TOKENS, D_MODEL), BF16),
        scratch_shapes=[pltpu.VMEM((N_RET, HEAD_DIM_RET, HEAD_DIM_RET), F32),
                        pltpu.VMEM((D_ATTN, blk), F32)],
        compiler_params=_cparams(("parallel", "arbitrary")),
        name="mixer",
    )(sink_rows, bias_cur, bias_prev, decay, proj, proj, proj, proj, proj, proj, proj, proj, proj,
      attn_norm_w, ret_norm_w)


def _out_ln_kernel(y_ref, w_ref, x_ref, lnw_ref, lnb_ref, o_ref):
    h = jnp.dot(y_ref[...], w_ref[...], preferred_element_type=F32)
    o_ref[...] = _layer_norm(DN_ALPHA * x_ref[...] + h, lnw_ref[...], lnb_ref[...])


def _out_ln(y, w, x, lnw, lnb):
    tm = LN_TM
    row = pl.BlockSpec((tm, D_MODEL), lambda i: (i, 0))
    vec = pl.BlockSpec((1, D_MODEL), lambda i: (0, 0))
    return pl.pallas_call(
        _out_ln_kernel,
        grid=(TOKENS // tm,),
        in_specs=[row,
                  pl.BlockSpec((D_MODEL, D_MODEL), lambda i: (0, 0), pipeline_mode=pl.Buffered(1)),
                  row, vec, vec],
        out_specs=row,
        out_shape=jax.ShapeDtypeStruct((TOKENS, D_MODEL), F32),
        compiler_params=_cparams(("parallel",)),
        name="out_ln",
    )(y, w, x, lnw, lnb)


def _mem_attn_kernel(x_ref, wq_ref, k_ref, v_ref, o_ref):
    q = jnp.dot(x_ref[...].astype(BF16), wq_ref[...], preferred_element_type=F32)
    q = (q * (HEAD_DIM_MEM ** -0.5)).astype(BF16)
    for h in range(N_MEM_HEADS):
        lo, hi = h * HEAD_DIM_MEM, (h + 1) * HEAD_DIM_MEM
        s = _dot_nt(q[:, lo:hi], k_ref[:, lo:hi])
        m = jnp.max(s, axis=-1, keepdims=True)
        p = jnp.exp(s - m)
        denom = jnp.sum(p, axis=-1, keepdims=True)
        o = jnp.dot(p.astype(BF16), v_ref[:, lo:hi], preferred_element_type=F32)
        o_ref[:, lo:hi] = (o / denom).astype(o_ref.dtype)


def _mem_attn(x, wq, kv):
    tm = MEM_TM
    nt = SEQ // tm
    row = pl.BlockSpec((tm, D_MODEL), lambda b, i: (b * nt + i, 0))
    return pl.pallas_call(
        _mem_attn_kernel,
        grid=(BATCH, nt),
        in_specs=[row,
                  pl.BlockSpec((D_MODEL, D_MODEL), lambda b, i: (0, 0), pipeline_mode=pl.Buffered(1)),
                  pl.BlockSpec((N_MEM, D_MODEL), lambda b, i: (b, 0)),
                  pl.BlockSpec((N_MEM, D_MODEL), lambda b, i: (b, 1))],
        out_specs=row,
        out_shape=jax.ShapeDtypeStruct((TOKENS, D_MODEL), BF16),
        compiler_params=_cparams(("parallel", "parallel")),
        name="mem_attn",
    )(x, wq, kv, kv)


def _route_kernel(x_ref, wr_ref, br_ref, idx_ref, gate_ref, rank_ref, count_ref, carry_ref):
    tm = x_ref.shape[0]

    @pl.when(pl.program_id(0) == 0)
    def _():
        carry_ref[...] = jnp.zeros_like(carry_ref)

    x = x_ref[...]
    x_hi = x.astype(BF16)
    x_lo = (x - x_hi.astype(F32)).astype(BF16)
    part_hi = jnp.dot(x_hi, wr_ref[...], preferred_element_type=F32)
    part_lo = jnp.dot(x_lo, wr_ref[...], preferred_element_type=F32)
    logits = (part_hi[:, :LANES] + (part_hi[:, LANES:] + part_lo[:, :LANES]) + part_lo[:, LANES:]
              + br_ref[...])
    lane = lax.broadcasted_iota(jnp.int32, (tm, LANES), 1)
    lane_f = lane.astype(F32)
    work = logits
    sels, vals, idxs = [], [], []
    for _ in range(TOP_K):
        v = jnp.max(work, axis=-1, keepdims=True)
        i = jnp.min(jnp.where(work == v, lane_f, float(LANES)), axis=-1, keepdims=True)
        sel = lane_f == i
        work = jnp.where(sel, -jnp.inf, work)
        sels.append(sel)
        vals.append(v)
        idxs.append(i)
    exps = [jnp.exp(v - vals[0]) for v in vals]
    total = exps[0] + exps[1] + exps[2] + exps[3]

    onehot = jnp.zeros((tm, LANES), F32)
    for sel in sels:
        onehot = jnp.where(sel, 1.0, onehot)
    r = lax.broadcasted_iota(jnp.int32, (tm, tm), 0)
    c = lax.broadcasted_iota(jnp.int32, (tm, tm), 1)
    below = jnp.where(c < r, 1.0, 0.0).astype(BF16)
    prefix = jnp.dot(below, onehot.astype(BF16), preferred_element_type=F32)
    rankmat = prefix + carry_ref[...]
    carry_ref[...] = carry_ref[...] + jnp.sum(onehot, axis=0, keepdims=True)
    count_ref[...] = carry_ref[...]

    idx_out = jnp.zeros((tm, LANES), F32)
    gate_out = jnp.zeros((tm, LANES), F32)
    rank_out = jnp.zeros((tm, LANES), F32)
    for k in range(TOP_K):
        rk = jnp.sum(jnp.where(sels[k], rankmat, 0.0), axis=-1, keepdims=True)
        idx_out = jnp.where(lane == k, idxs[k], idx_out)
        gate_out = jnp.where(lane == k, exps[k] / total, gate_out)
        rank_out = jnp.where(lane == k, rk, rank_out)
    gate_ref[...] = gate_out
    idx_ref[...] = idx_out.T[:SUBLANES, :].astype(jnp.int32)
    rank_ref[...] = rank_out.T[:SUBLANES, :].astype(jnp.int32)


def _route(x, wr, br):
    tm = ROUTE_TM
    wide = pl.BlockSpec((tm, LANES), lambda i: (i, 0))
    rows = pl.BlockSpec((SUBLANES, tm), lambda i: (0, i))
    return pl.pallas_call(
        _route_kernel,
        grid=(TOKENS // tm,),
        in_specs=[pl.BlockSpec((tm, D_MODEL), lambda i: (i, 0)),
                  pl.BlockSpec((D_MODEL, 2 * LANES), lambda i: (0, 0)),
                  pl.BlockSpec((1, LANES), lambda i: (0, 0))],
        out_specs=[rows, wide, rows, pl.BlockSpec((1, LANES), lambda i: (0, 0))],
        out_shape=[jax.ShapeDtypeStruct((SUBLANES, TOKENS), jnp.int32),
                   jax.ShapeDtypeStruct((TOKENS, LANES), F32),
                   jax.ShapeDtypeStruct((SUBLANES, TOKENS), jnp.int32),
                   jax.ShapeDtypeStruct((1, LANES), F32)],
        scratch_shapes=[pltpu.VMEM((1, LANES), F32)],
        compiler_params=_cparams(("arbitrary",)),
        name="route",
    )(x, wr, br)


HALF = D_MODEL // 2


def _pack_rows(x):
    return _pack_pair(x[:, :HALF], x[:, HALF:])


def _pack_pair(lo, hi):
    lo = lax.bitcast_convert_type(lo.astype(BF16).astype(F32), jnp.int32)
    hi = lax.bitcast_convert_type(hi.astype(BF16).astype(F32), jnp.int32)
    return hi | lax.shift_right_logical(lo, 16)


def _unpack_rows(w):
    lo = lax.bitcast_convert_type(lax.shift_left(w, 16), F32)
    hi = lax.bitcast_convert_type(w & jnp.int32(-65536), F32)
    return lo, hi


def _wait_row_copies(rows_ref, sem):
    pltpu.make_async_copy(rows_ref.at[pl.ds(0, ROW_TILE)], rows_ref.at[pl.ds(0, ROW_TILE)], sem).wait()


ROW_SUB = HALF // LANES
ROW_GROUPS = ROW_TILE // SUBLANES


FILL_SIZES = tuple(MOE_TM >> (b + 1) for b in range(MOE_TM.bit_length() - 1))


def _fill_copies(fill_start_ref, fill_len_ref, zero_ref, xs_ref, sem, act):
    def body(e, carry):
        pos = fill_start_ref[e]
        n = fill_len_ref[e]
        for size in FILL_SIZES:
            take = (n & size) != 0

            @pl.when(take)
            def _():
                cp = pltpu.make_async_copy(zero_ref.at[pl.ds(0, size)], xs_ref.at[pl.ds(pos, size)], sem)
                getattr(cp, act)()

            pos = pos + jnp.where(take, size, 0)
        return carry

    lax.fori_loop(0, N_EXPERTS, body, 0)

    def tail(j, carry):
        pos = fill_start_ref[N_EXPERTS] + j * FILL_SIZES[0]
        cp = pltpu.make_async_copy(zero_ref, xs_ref.at[pl.ds(pos, FILL_SIZES[0])], sem)
        getattr(cp, act)()
        return carry

    lax.fori_loop(0, fill_len_ref[N_EXPERTS], tail, 0)


def _scatter_kernel(fill_start_ref, fill_len_ref, dest_ref, x_ref, xs_ref, packed_ref, zero_ref, sem, fill_sem):
    i = pl.program_id(0)
    slot = i % 2

    @pl.when(i == 0)
    def _():
        zero_ref[...] = jnp.zeros_like(zero_ref)
        _fill_copies(fill_start_ref, fill_len_ref, zero_ref, xs_ref, fill_sem, "start")

    words = _pack_rows(x_ref[...])
    for g in range(ROW_GROUPS):
        for c in range(ROW_SUB):
            packed_ref[slot, g, c] = words[g * SUBLANES:(g + 1) * SUBLANES, c * LANES:(c + 1) * LANES]

    def body(g, carry):
        for s in range(SUBLANES):
            for k in range(TOP_K):
                d = dest_ref[0, 0, (g * SUBLANES + s) * TOP_K + k]
                pltpu.make_async_copy(packed_ref.at[slot, g, :, s, :], xs_ref.at[d],
                                      sem.at[slot]).start(priority=k % 2)
        return carry

    lax.fori_loop(0, ROW_GROUPS, body, 0)

    @pl.when(i > 0)
    def _():
        for k in range(TOP_K):
            _wait_row_copies(xs_ref, sem.at[1 - slot])

    @pl.when(i == pl.num_programs(0) - 1)
    def _():
        for k in range(TOP_K):
            _wait_row_copies(xs_ref, sem.at[slot])

    @pl.when(i == 0)
    def _():
        _fill_copies(fill_start_ref, fill_len_ref, zero_ref, xs_ref, fill_sem, "wait")


def _scatter_rows(x, dest, fill_start, fill_len):
    nt = TOKENS // ROW_TILE
    grid_spec = pltpu.PrefetchScalarGridSpec(
        num_scalar_prefetch=2,
        grid=(nt,),
        in_specs=[pl.BlockSpec((1, 1, ROW_TILE * TOP_K), lambda i, fs, fl: (i, 0, 0), memory_space=pltpu.SMEM),
                  pl.BlockSpec((ROW_TILE, D_MODEL), lambda i, fs, fl: (i, 0))],
        out_specs=pl.BlockSpec(memory_space=pl.ANY),
        scratch_shapes=[pltpu.VMEM((2, ROW_GROUPS, ROW_SUB, SUBLANES, LANES), jnp.int32),
                        pltpu.VMEM((FILL_SIZES[0], ROW_SUB, LANES), jnp.int32),
                        pltpu.SemaphoreType.DMA((2,)), pltpu.SemaphoreType.DMA(())],
    )
    return pl.pallas_call(
        _scatter_kernel,
        grid_spec=grid_spec,
        out_shape=jax.ShapeDtypeStruct((MOE_ROWS, ROW_SUB, LANES), jnp.int32),
        compiler_params=_cparams(("arbitrary",), has_side_effects=True),
        name="scatter_rows",
    )(fill_start, fill_len, dest, x)


MOE_NF = D_EXPERT // MOE_TF
OUT_WORDS = MOE_TF // 2
OUT_SUB = OUT_WORDS // LANES


def _out_word_cols(c):
    tile, j = divmod(c, OUT_SUB)
    lo = tile * MOE_TF + j * LANES
    return slice(lo, lo + LANES), slice(lo + OUT_WORDS, lo + OUT_WORDS + LANES)
MOE_ROW_STEP = 128


def _experts_kernel(ge_ref, gn_ref, de_ref, dn_ref, xs_ref, wg_ref, wu_ref, wd_ref, bgu_ref, bd_ref,
                    ys_ref, xb_ref, act_ref, words_ref, ow_ref, pend_ref, xsem, ysem):
    v = pl.program_id(0)
    f = pl.program_id(1)
    gn = gn_ref[v]
    dn = dn_ref[v]
    slot = v % 2

    def ys_copies(visit, tile):
        row0 = pl.multiple_of(visit * MOE_TM, MOE_TM)
        return [pltpu.make_async_copy(ow_ref.at[:, j * LANES:(j + 1) * LANES],
                                      ys_ref.at[pl.ds(row0, MOE_TM), tile * OUT_SUB + j, :], ysem)
                for j in range(OUT_SUB)]

    def emit_output_tile():
        for cp in ys_copies(v - 1, f):
            cp.start()
        pend_ref[0] = 1

    def drain_output_tile():
        @pl.when(pend_ref[0] == 1)
        def _():
            for cp in ys_copies(0, 0):
                cp.wait()
            pend_ref[0] = 0

    @pl.when(jnp.logical_and(v == 0, f == 0))
    def _():
        ow_ref[...] = jnp.zeros_like(ow_ref)
        pend_ref[0] = 0

    def xs_copies(visit, s):
        row0 = pl.multiple_of(visit * MOE_TM, MOE_TM)
        return [pltpu.make_async_copy(xs_ref.at[pl.ds(row0, MOE_TM), c, :],
                                      words_ref.at[s, :, c * LANES:(c + 1) * LANES], xsem.at[s])
                for c in range(ROW_SUB)]

    @pl.when(f == 0)
    def _():
        @pl.when(v == 0)
        def _():
            for cp in xs_copies(0, 0):
                cp.start()

        @pl.when(v < N_VISITS)
        def _():
            for cp in xs_copies(v, slot):
                cp.wait()

        @pl.when(v + 1 < N_VISITS)
        def _():
            for cp in xs_copies(v + 1, 1 - slot):
                cp.start()

    @pl.when(jnp.logical_and(gn > 0, f == 0))
    def _():
        r = lax.broadcasted_iota(jnp.int32, (MOE_TM, 1), 0)
        lo, hi = _unpack_rows(jnp.where(r < gn, words_ref[slot], 0))
        xb_ref[:, :HALF] = lo.astype(BF16)
        xb_ref[:, HALF:] = hi.astype(BF16)

    def gate_phase(rows):
        xb = xb_ref[:rows, :]
        e = pl.ds(ge_ref[v], 1)
        col = pl.multiple_of(f * MOE_TF, MOE_TF)
        gl = jnp.dot(xb, wg_ref[...].astype(BF16), preferred_element_type=F32) + bgu_ref[e, pl.ds(col, MOE_TF)]
        up = (jnp.dot(xb, wu_ref[...].astype(BF16), preferred_element_type=F32)
              + bgu_ref[e, pl.ds(D_EXPERT + col, MOE_TF)])
        gl = jnp.minimum(gl, SWIGLU_LIMIT)
        up = jnp.clip(up, -SWIGLU_LIMIT, SWIGLU_LIMIT)
        act = ((up + 1.0) * (gl * jax.nn.sigmoid(SWIGLU_ALPHA * gl))).astype(BF16)
        act_ref[slot, :rows, pl.ds(pl.multiple_of(f * MOE_TF, MOE_TF), MOE_TF)] = act

    def down_phase(rows):
        drain_output_tile()
        y = jnp.dot(act_ref[1 - slot, :rows, :], wd_ref[...].astype(BF16),
                    preferred_element_type=F32) + bd_ref[pl.ds(de_ref[v], 1),
                                                         pl.ds(pl.multiple_of(f * MOE_TF, MOE_TF), MOE_TF)]
        ow_ref[:rows, :] = _pack_pair(y[:, :OUT_WORDS], y[:, OUT_WORDS:])
        emit_output_tile()

    for rows in range(MOE_TM, 0, -MOE_ROW_STEP):
        lower = rows - MOE_ROW_STEP
        pl.when(jnp.logical_and(gn > lower, gn <= rows))(functools.partial(gate_phase, rows))
        pl.when(jnp.logical_and(dn > lower, dn <= rows))(functools.partial(down_phase, rows))

    @pl.when(jnp.logical_and(dn == 0, v >= 1))
    def _():
        drain_output_tile()
        ow_ref[...] = jnp.zeros_like(ow_ref)
        emit_output_tile()

    @pl.when(jnp.logical_and(v == pl.num_programs(0) - 1, f == MOE_NF - 1))
    def _():
        drain_output_tile()


def _experts(xs, visit_expert, visit_nvalid, w_gate_up, b_gate_up, w_down, b_down):
    nf = MOE_NF
    gate_e = jnp.concatenate([visit_expert, visit_expert[-1:]])
    gate_n = jnp.concatenate([visit_nvalid, jnp.zeros((1,), jnp.int32)])
    down_e = jnp.concatenate([visit_expert[:1], visit_expert])
    down_n = jnp.concatenate([jnp.zeros((1,), jnp.int32), visit_nvalid])

    def tile(f, n):
        return jnp.where(n > 0, f, nf - 1)

    grid_spec = pltpu.PrefetchScalarGridSpec(
        num_scalar_prefetch=4,
        grid=(N_VISITS + 1, nf),
        in_specs=[
            pl.BlockSpec(memory_space=pl.ANY),
            pl.BlockSpec((None, D_MODEL, MOE_TF), lambda v, f, ge, gn, de, dn: (ge[v], 0, tile(f, gn[v]))),
            pl.BlockSpec((None, D_MODEL, MOE_TF), lambda v, f, ge, gn, de, dn: (ge[v], 0, nf + tile(f, gn[v]))),
            pl.BlockSpec((None, D_EXPERT, MOE_TF), lambda v, f, ge, gn, de, dn: (de[v], 0, tile(f, dn[v]))),
            pl.BlockSpec((N_EXPERTS, 2 * D_EXPERT), lambda v, f, ge, gn, de, dn: (0, 0)),
            pl.BlockSpec((N_EXPERTS, D_MODEL), lambda v, f, ge, gn, de, dn: (0, 0)),
        ],
        out_specs=pl.BlockSpec(memory_space=pl.ANY),
        scratch_shapes=[pltpu.VMEM((MOE_TM, D_MODEL), BF16),
                        pltpu.VMEM((2, MOE_TM, D_EXPERT), BF16),
                        pltpu.VMEM((2, MOE_TM, HALF), jnp.int32),
                        pltpu.VMEM((MOE_TM, OUT_WORDS), jnp.int32),
                        pltpu.SMEM((1,), jnp.int32),
                        pltpu.SemaphoreType.DMA((2,)),
                        pltpu.SemaphoreType.DMA(())],
    )
    return pl.pallas_call(
        _experts_kernel,
        grid_spec=grid_spec,
        out_shape=jax.ShapeDtypeStruct((MOE_ROWS, ROW_SUB, LANES), jnp.int32),
        compiler_params=_cparams(("arbitrary", "arbitrary")),
        name="experts",
    )(gate_e, gate_n, down_e, down_n, xs, w_gate_up, w_gate_up, w_down, b_gate_up, b_down)


def _combine_kernel(dest_ref, dest_next_ref, ys_ref, gate_ref, x_ref, lnw_ref, lnb_ref, o_ref,
                    buf_ref, z_ref, sem):
    i = pl.program_id(0)
    slot = i % 2

    def start_group(idx_ref, s, g):
        for t in range(SUBLANES):
            for k in range(TOP_K):
                d = idx_ref[0, 0, (g * SUBLANES + t) * TOP_K + k]
                pltpu.make_async_copy(ys_ref.at[d], buf_ref.at[s, k, g, :, t, :],
                                      sem.at[s]).start(priority=k % 2)

    @pl.when(i == 0)
    def _():
        def body(g, carry):
            start_group(dest_ref, slot, g)
            return carry

        lax.fori_loop(0, ROW_GROUPS, body, 0)

    for k in range(TOP_K):
        _wait_row_copies(ys_ref, sem.at[slot])

    def group(g, carry, prefetch):
        if prefetch:
            start_group(dest_next_ref, 1 - slot, g)
        rows = pl.ds(pl.multiple_of(g * SUBLANES, SUBLANES), SUBLANES)
        gates = gate_ref[rows, :]
        gate_k = [jnp.broadcast_to(gates[:, k:k + 1], (SUBLANES, LANES)) for k in range(TOP_K)]
        for c in range(ROW_SUB):
            lo_cols, hi_cols = _out_word_cols(c)
            z_lo = DN_ALPHA * x_ref[rows, lo_cols]
            z_hi = DN_ALPHA * x_ref[rows, hi_cols]
            for k in range(TOP_K):
                lo, hi = _unpack_rows(buf_ref[slot, k, g, c])
                z_lo = z_lo + gate_k[k] * lo
                z_hi = z_hi + gate_k[k] * hi
            z_ref[rows, lo_cols] = z_lo
            z_ref[rows, hi_cols] = z_hi
        return carry

    last = i + 1 == pl.num_programs(0)

    @pl.when(jnp.logical_not(last))
    def _():
        lax.fori_loop(0, ROW_GROUPS, functools.partial(group, prefetch=True), 0)

    @pl.when(last)
    def _():
        lax.fori_loop(0, ROW_GROUPS, functools.partial(group, prefetch=False), 0)

    o_ref[...] = _layer_norm(z_ref[...], lnw_ref[...], lnb_ref[...])


def _combine(ys, dest, gates, x, lnw, lnb):
    nt = TOKENS // ROW_TILE
    row = pl.BlockSpec((ROW_TILE, D_MODEL), lambda i: (i, 0))
    vec = pl.BlockSpec((1, D_MODEL), lambda i: (0, 0))
    dest_tiles = dest
    return pl.pallas_call(
        _combine_kernel,
        grid=(nt,),
        in_specs=[pl.BlockSpec((1, 1, ROW_TILE * TOP_K), lambda i: (i, 0, 0), memory_space=pltpu.SMEM),
                  pl.BlockSpec((1, 1, ROW_TILE * TOP_K), lambda i: (jnp.minimum(i + 1, nt - 1), 0, 0),
                               memory_space=pltpu.SMEM),
                  pl.BlockSpec(memory_space=pl.ANY),
                  pl.BlockSpec((ROW_TILE, LANES), lambda i: (i, 0)),
                  row, vec, vec],
        out_specs=row,
        out_shape=jax.ShapeDtypeStruct((TOKENS, D_MODEL), F32),
        scratch_shapes=[pltpu.VMEM((2, TOP_K, ROW_GROUPS, ROW_SUB, SUBLANES, LANES), jnp.int32),
                        pltpu.VMEM((ROW_TILE, D_MODEL), F32), pltpu.SemaphoreType.DMA((2,))],
        compiler_params=_cparams(("arbitrary",)),
        name="combine",
    )(dest_tiles, dest_tiles, ys, gates, x, lnw, lnb)


def _layer(x, mem, w_in, attn_sinks, attn_norm_w, ret_norm_w, w_mix_out, ln_mix_w, ln_mix_b,
           w_mem_q, w_mem_kv, w_mem_out, ln_mem_w, ln_mem_b, w_router, b_router,
           w_gate_up, b_gate_up, w_down, b_down, ln_moe_w, ln_moe_b):
    o_k, o_v, o_qr = D_ATTN, D_ATTN + D_KV, D_ATTN + 2 * D_KV
    w_in_p = jnp.concatenate([w_in[:, :o_k], w_in[:, o_qr:], w_in[:, o_k:o_v], w_in[:, o_v:o_qr]],
                             axis=1).astype(BF16)
    proj = _project(x, w_in_p, PROJ_TM, PROJ_TN)
    y = _mixer(proj, attn_sinks.astype(F32), attn_norm_w.reshape(1, D_ATTN), ret_norm_w.reshape(1, D_RET))
    x1 = _out_ln(y, w_mix_out.astype(BF16), x, ln_mix_w.reshape(1, D_MODEL), ln_mix_b.reshape(1, D_MODEL))

    kv = _project(mem.reshape(BATCH * N_MEM, D_MODEL), w_mem_kv.astype(BF16), BATCH * N_MEM, D_MODEL)
    o = _mem_attn(x1, w_mem_q.astype(BF16), kv)
    x2 = _out_ln(o, w_mem_out.astype(BF16), x1, ln_mem_w.reshape(1, D_MODEL), ln_mem_b.reshape(1, D_MODEL))

    wr = jnp.pad(w_router, ((0, 0), (0, LANES - N_EXPERTS)))
    wr_hi = wr.astype(BF16)
    wr = jnp.concatenate([wr_hi, (wr - wr_hi.astype(F32)).astype(BF16)], axis=1)
    br = jnp.pad(b_router.reshape(1, N_EXPERTS), ((0, 0), (0, LANES - N_EXPERTS)), constant_values=NEG_INF)
    idx, gates, rank, counts = _route(x2, wr, br)

    counts = counts[0, :N_EXPERTS].astype(jnp.int32)
    padded = ((counts + MOE_TM - 1) // MOE_TM) * MOE_TM
    eid = jnp.arange(N_EXPERTS, dtype=jnp.int32)
    pad_end = jnp.sum(jnp.where(eid[None, :] <= eid[:, None], padded[None, :], 0), axis=1)
    pad_start = pad_end - padded
    chosen = idx[:TOP_K, :, None] == eid[None, None, :]
    dest_kt = jnp.sum(jnp.where(chosen, pad_start[None, None, :], 0), axis=-1) + rank[:TOP_K]
    nt = TOKENS // ROW_TILE
    dest = dest_kt.reshape(TOP_K, nt, ROW_TILE).transpose(1, 2, 0).reshape(nt, 1, ROW_TILE * TOP_K)
    visit_row = jnp.arange(N_VISITS, dtype=jnp.int32) * MOE_TM
    visit_expert = jnp.minimum(jnp.sum((pad_end[None, :] <= visit_row[:, None]).astype(jnp.int32), axis=1),
                               N_EXPERTS - 1)
    of_visit = visit_expert[:, None] == eid[None, :]
    visit_count = jnp.sum(jnp.where(of_visit, counts[None, :], 0), axis=1)
    visit_start = jnp.sum(jnp.where(of_visit, pad_start[None, :], 0), axis=1)
    visit_nvalid = jnp.clip(visit_count - (visit_row - visit_start), 0, MOE_TM).astype(jnp.int32)

    fill_start = jnp.concatenate([pad_start + counts, pad_end[-1:]]).astype(jnp.int32)
    fill_len = jnp.concatenate([padded - counts, (MOE_ROWS - pad_end[-1:]) // FILL_SIZES[0]]).astype(jnp.int32)
    xs = _scatter_rows(x2, dest, fill_start, fill_len)
    ys = _experts(xs, visit_expert, visit_nvalid,
                  w_gate_up, b_gate_up, w_down, b_down)
    return _combine(ys, dest, gates, x2, ln_moe_w.reshape(1, D_MODEL), ln_moe_b.reshape(1, D_MODEL))


def kernel(x, mem, w_in, attn_sinks, attn_norm_w, ret_norm_w, w_mix_out, ln_mix_w, ln_mix_b, w_mem_q, w_mem_kv, w_mem_out, ln_mem_w, ln_mem_b, w_router, b_router, w_gate_up, b_gate_up, w_down, b_down, ln_moe_w, ln_moe_b):
    xt = x.reshape(TOKENS, D_MODEL)
    out = _layer(xt, mem, w_in[0], attn_sinks[0], attn_norm_w[0], ret_norm_w[0], w_mix_out[0],
                 ln_mix_w[0], ln_mix_b[0], w_mem_q[0], w_mem_kv[0], w_mem_out[0], ln_mem_w[0], ln_mem_b[0],
                 w_router[0], b_router[0], w_gate_up[0], b_gate_up[0], w_down[0], b_down[0],
                 ln_moe_w[0], ln_moe_b[0])
    return out.reshape(BATCH, SEQ, D_MODEL)
```

```python
import functools
import math

import jax
import jax.numpy as jnp
import numpy as np
from jax import lax
from jax.experimental import pallas as pl
from jax.experimental.pallas import tpu as pltpu

F32 = jnp.float32
BF16 = jnp.bfloat16

D_MODEL = 2048
BATCH = 2
SEQ = 16384
TOKENS = BATCH * SEQ

HEAD_DIM_ATTN = 64
N_Q_ATTN = 16
N_KV_ATTN = 2
GQA_GROUP = N_Q_ATTN // N_KV_ATTN
ATTN_BLOCK = 128
D_ATTN = N_Q_ATTN * HEAD_DIM_ATTN
D_KV = N_KV_ATTN * HEAD_DIM_ATTN
N_RET = 8
HEAD_DIM_RET = 128
RET_CHUNK = 128
D_RET = N_RET * HEAD_DIM_RET
D_IN = D_ATTN + 2 * D_KV + 4 * D_RET
N_MEM = 256
N_MEM_HEADS = 4
HEAD_DIM_MEM = D_MODEL // N_MEM_HEADS
N_EXPERTS = 32
TOP_K = 4
D_EXPERT = D_MODEL
SWIGLU_LIMIT = 7.0
SWIGLU_ALPHA = 1.702
LN_EPS = 1e-5
DN_ALPHA = 2.0 ** 0.25
NEG_INF = -1e30

LANES = 128
SUBLANES = 8
V7X_VMEM_BYTES = 64 * 1024 * 1024
VMEM_LIMIT = V7X_VMEM_BYTES - 6 * 1024 * 1024

PROJ_TM = 1024
PROJ_TN = 1792
LN_TM = 512
MEM_TM = 512
ROUTE_TM = 512
ROW_TILE = 256
MOE_TM = 1024
MOE_TF = 512
N_VISITS = (TOKENS * TOP_K + N_EXPERTS * (MOE_TM - 1) + MOE_TM - 1) // MOE_TM
MOE_ROWS = N_VISITS * MOE_TM

COL_QA, COL_QR, COL_KR, COL_VR, COL_GR = 0, 1, 2, 3, 4
COL_KA = 5 * D_RET // D_KV
COL_VA = COL_KA + 1


def _cparams(sem, **kw):
    return pltpu.CompilerParams(dimension_semantics=sem, vmem_limit_bytes=VMEM_LIMIT, **kw)


def _layer_norm(z, w, b):
    mu = jnp.mean(z, axis=-1, keepdims=True)
    zc = z - mu
    var = jnp.mean(zc * zc, axis=-1, keepdims=True)
    return zc * lax.rsqrt(var + LN_EPS) * w + b


def _dot_nt(a, b):
    return lax.dot_general(a, b, (((1,), (1,)), ((), ())), preferred_element_type=F32)


def _proj_kernel(x_ref, w_ref, o_ref, xb_ref):
    @pl.when(pl.program_id(1) == 0)
    def _():
        xb_ref[...] = x_ref[...].astype(BF16)

    o_ref[...] = jnp.dot(xb_ref[...], w_ref[...], preferred_element_type=F32).astype(o_ref.dtype)


def _project(x, w, tm, tn):
    m, k = x.shape
    n = w.shape[1]
    return pl.pallas_call(
        _proj_kernel,
        grid=(m // tm, n // tn),
        in_specs=[pl.BlockSpec((tm, k), lambda i, j: (i, 0)),
                  pl.BlockSpec((k, tn), lambda i, j: (0, j))],
        out_specs=pl.BlockSpec((tm, tn), lambda i, j: (i, j)),
        out_shape=jax.ShapeDtypeStruct((m, n), BF16),
        scratch_shapes=[pltpu.VMEM((tm, k), BF16)],
        compiler_params=_cparams(("parallel", "arbitrary")),
        name="project",
    )(x, w)


def _mixer_kernel(sinkrow_ref, bias_cur_ref, bias_prev_ref, decay_ref, qa_ref, kp_ref, kc_ref, vp_ref, vc_ref,
                  qr_ref, kr_ref, vr_ref, gr_ref, anw_ref, rnw_ref, y_ref, state_ref, a_ref):
    n = pl.program_id(1)

    scale = HEAD_DIM_ATTN ** -0.5
    for j in range(N_KV_ATTN):
        klo, khi = j * HEAD_DIM_ATTN, (j + 1) * HEAD_DIM_ATTN
        heads = range(j * GQA_GROUP, (j + 1) * GQA_GROUP)
        q_rows = jnp.concatenate([qa_ref[:, h * HEAD_DIM_ATTN:(h + 1) * HEAD_DIM_ATTN] for h in heads], axis=0)
        q_rows = q_rows * scale
        sink = sinkrow_ref[j]
        s_cur = _dot_nt(kc_ref[:, klo:khi], q_rows) + bias_cur_ref[j]
        s_prev = _dot_nt(kp_ref[:, klo:khi], q_rows) + bias_prev_ref[j]
        s_prev = jnp.where(n > 0, s_prev, NEG_INF)
        m = jnp.maximum(jnp.max(s_cur, axis=0, keepdims=True), jnp.max(s_prev, axis=0, keepdims=True))
        m = jnp.maximum(m, sink)
        p_cur = jnp.exp(s_cur - m)
        p_prev = jnp.exp(s_prev - m)
        denom = (jnp.sum(p_cur, axis=0, keepdims=True) + jnp.sum(p_prev, axis=0, keepdims=True)
                 + jnp.exp(sink - m))
        v_cur_t = vc_ref[:, klo:khi].astype(F32).T.astype(BF16)
        v_prev_t = vp_ref[:, klo:khi].astype(F32).T.astype(BF16)
        o_t = (jnp.dot(v_cur_t, p_cur.astype(BF16), preferred_element_type=F32)
               + jnp.dot(v_prev_t, p_prev.astype(BF16), preferred_element_type=F32))
        o_t = o_t * (1.0 / denom)
        for g, h in enumerate(heads):
            a_ref[h * HEAD_DIM_ATTN:(h + 1) * HEAD_DIM_ATTN, :] = o_t[:, g * ATTN_BLOCK:(g + 1) * ATTN_BLOCK]
    a_t = a_ref[...]
    a_t = a_t * lax.rsqrt(jnp.mean(a_t * a_t, axis=0, keepdims=True) + LN_EPS)
    y_ref[:, :D_ATTN] = (a_t.T * anw_ref[...]).astype(y_ref.dtype)

    @pl.when(n == 0)
    def _():
        state_ref[...] = jnp.zeros_like(state_ref)

    for h in range(N_RET):
        lo, hi = h * HEAD_DIM_RET, (h + 1) * HEAD_DIM_RET
        g_c = math.exp(math.log1p(-(2.0 ** (-5.0 - h))) * RET_CHUNK)
        q = qr_ref[:, lo:hi]
        k = kr_ref[:, lo:hi]
        v = vr_ref[:, lo:hi]
        s = _dot_nt(q, k) * decay_ref[h]
        inner = jnp.dot(s.astype(BF16), v, preferred_element_type=F32)
        state = state_ref[h]
        qw = (q.astype(F32) * decay_ref[N_RET + h]).astype(BF16)
        cross = jnp.dot(qw, state.astype(BF16), preferred_element_type=F32)
        kw_t = (k.astype(F32) * decay_ref[2 * N_RET + h]).T.astype(BF16)
        state_ref[h] = g_c * state + jnp.dot(kw_t, v, preferred_element_type=F32)
        o = inner + cross
        mu = jnp.mean(o, axis=-1, keepdims=True)
        oc = o - mu
        var = jnp.mean(oc * oc, axis=-1, keepdims=True)
        o = oc * lax.rsqrt(var + LN_EPS) * rnw_ref[:, lo:hi]
        g = gr_ref[:, lo:hi].astype(F32)
        o = o * (g * jax.nn.sigmoid(g))
        y_ref[:, D_ATTN + lo:D_ATTN + hi] = o.astype(y_ref.dtype)


def _mixer_tables():
    blk, gq = ATTN_BLOCK, GQA_GROUP * ATTN_BLOCK
    f32 = np.float32
    slopes = np.exp2(-8.0 * (np.arange(N_Q_ATTN, dtype=f32) + 1.0) / N_Q_ATTN).astype(f32)
    slopes = slopes.reshape(N_KV_ATTN, 1, GQA_GROUP, 1)
    key = np.arange(blk, dtype=f32).reshape(1, blk, 1, 1)
    query = np.arange(blk, dtype=f32).reshape(1, 1, 1, blk)
    dist_cur = np.broadcast_to(query - key, (N_KV_ATTN, blk, GQA_GROUP, blk))
    bias_cur = np.where(dist_cur >= 0, -slopes * dist_cur, NEG_INF).astype(f32).reshape(N_KV_ATTN, blk, gq)
    bias_prev = np.where(dist_cur < 0, -slopes * (dist_cur + blk), NEG_INF).astype(f32).reshape(N_KV_ATTN, blk, gq)

    log_g = np.log1p(-np.exp2(-5.0 - np.arange(N_RET, dtype=f32))).astype(f32).reshape(N_RET, 1, 1)
    pos = np.arange(RET_CHUNK, dtype=f32)
    lag = pos[:, None] - pos[None, :]
    kscale = f32(HEAD_DIM_RET ** -0.5)
    in_chunk = np.where(lag >= 0, np.exp(log_g * np.maximum(lag, 0.0)), 0.0) * kscale
    ones = np.ones((1, 1, HEAD_DIM_RET), f32)
    w_query = np.exp(log_g * (pos + 1.0).reshape(1, RET_CHUNK, 1)) * ones
    w_key = np.exp(log_g * (RET_CHUNK - 1.0 - pos).reshape(1, RET_CHUNK, 1)) * (kscale * ones)
    decay = np.concatenate([in_chunk, w_query, w_key], axis=0).astype(f32)
    return bias_cur, bias_prev, decay


_MIXER_TABLES = _mixer_tables()


def _mixer(proj, sinks, attn_norm_w, ret_norm_w):
    nc = SEQ // ATTN_BLOCK
    blk = ATTN_BLOCK

    def wide(colblock):
        return pl.BlockSpec((blk, D_RET), lambda b, n: (b * nc + n, colblock))

    def kv_cur(colblock):
        return pl.BlockSpec((blk, D_KV), lambda b, n: (b * nc + n, colblock))

    def kv_prev(colblock):
        return pl.BlockSpec((blk, D_KV), lambda b, n: (b * nc + jnp.maximum(n - 1, 0), colblock))

    vec = pl.BlockSpec((1, D_RET), lambda b, n: (0, 0))
    gq = GQA_GROUP * blk

    def table(rows):
        return pl.BlockSpec((N_KV_ATTN, rows, gq), lambda b, n: (0, 0, 0))

    bias_cur, bias_prev, decay = _MIXER_TABLES
    sink_rows = jnp.repeat(sinks.astype(F32).reshape(N_KV_ATTN, 1, GQA_GROUP), blk, axis=2)
    return pl.pallas_call(
        _mixer_kernel,
        grid=(BATCH, nc),
        in_specs=[table(1), table(blk), table(blk),
                  pl.BlockSpec((3 * N_RET, RET_CHUNK, HEAD_DIM_RET), lambda b, n: (0, 0, 0)),
                  wide(COL_QA), kv_prev(COL_KA), kv_cur(COL_KA), kv_prev(COL_VA), kv_cur(COL_VA),
                  wide(COL_QR), wide(COL_KR), wide(COL_VR), wide(COL_GR), vec, vec],
        out_specs=pl.BlockSpec((blk, D_MODEL), lambda b, n: (b * nc + n, 0)),
        out_shape=jax.ShapeDtypeStruct((TOKENS, D_MODEL), BF16),
        scratch_shapes=[pltpu.VMEM((N_RET, HEAD_DIM_RET, HEAD_DIM_RET), F32),
                        pltpu.VMEM((D_ATTN, blk), F32)],
        compiler_params=_cparams(("parallel", "arbitrary")),
        name="mixer",
    )(sink_rows, bias_cur, bias_prev, decay, proj, proj, proj, proj, proj, proj, proj, proj, proj,
      attn_norm_w, ret_norm_w)


def _out_ln_kernel(y_ref, w_ref, x_ref, lnw_ref, lnb_ref, o_ref):
    h = jnp.dot(y_ref[...], w_ref[...], preferred_element_type=F32)
    o_ref[...] = _layer_norm(DN_ALPHA * x_ref[...] + h, lnw_ref[...], lnb_ref[...])


def _out_ln(y, w, x, lnw, lnb):
    tm = LN_TM
    row = pl.BlockSpec((tm, D_MODEL), lambda i: (i, 0))
    vec = pl.BlockSpec((1, D_MODEL), lambda i: (0, 0))
    return pl.pallas_call(
        _out_ln_kernel,
        grid=(TOKENS // tm,),
        in_specs=[row,
                  pl.BlockSpec((D_MODEL, D_MODEL), lambda i: (0, 0), pipeline_mode=pl.Buffered(1)),
                  row, vec, vec],
        out_specs=row,
        out_shape=jax.ShapeDtypeStruct((TOKENS, D_MODEL), F32),
        compiler_params=_cparams(("parallel",)),
        name="out_ln",
    )(y, w, x, lnw, lnb)


def _mem_attn_kernel(x_ref, wq_ref, k_ref, v_ref, o_ref):
    q = jnp.dot(x_ref[...].astype(BF16), wq_ref[...], preferred_element_type=F32)
    q = (q * (HEAD_DIM_MEM ** -0.5)).astype(BF16)
    for h in range(N_MEM_HEADS):
        lo, hi = h * HEAD_DIM_MEM, (h + 1) * HEAD_DIM_MEM
        s = _dot_nt(q[:, lo:hi], k_ref[:, lo:hi])
        m = jnp.max(s, axis=-1, keepdims=True)
        p = jnp.exp(s - m)
        denom = jnp.sum(p, axis=-1, keepdims=True)
        o = jnp.dot(p.astype(BF16), v_ref[:, lo:hi], preferred_element_type=F32)
        o_ref[:, lo:hi] = (o * (1.0 / denom)).astype(o_ref.dtype)


def _mem_attn(x, wq, kv):
    tm = MEM_TM
    nt = SEQ // tm
    row = pl.BlockSpec((tm, D_MODEL), lambda b, i: (b * nt + i, 0))
    return pl.pallas_call(
        _mem_attn_kernel,
        grid=(BATCH, nt),
        in_specs=[row,
                  pl.BlockSpec((D_MODEL, D_MODEL), lambda b, i: (0, 0), pipeline_mode=pl.Buffered(1)),
                  pl.BlockSpec((N_MEM, D_MODEL), lambda b, i: (b, 0)),
                  pl.BlockSpec((N_MEM, D_MODEL), lambda b, i: (b, 1))],
        out_specs=row,
        out_shape=jax.ShapeDtypeStruct((TOKENS, D_MODEL), BF16),
        compiler_params=_cparams(("parallel", "parallel")),
        name="mem_attn",
    )(x, wq, kv, kv)


def _route_kernel(x_ref, wr_ref, br_ref, idx_ref, gate_ref, rank_ref, count_ref, carry_ref):
    tm = x_ref.shape[0]

    @pl.when(pl.program_id(0) == 0)
    def _():
        carry_ref[...] = jnp.zeros_like(carry_ref)

    x = x_ref[...]
    x_hi = x.astype(BF16)
    x_lo = (x - x_hi.astype(F32)).astype(BF16)
    part_hi = jnp.dot(x_hi, wr_ref[...], preferred_element_type=F32)
    part_lo = jnp.dot(x_lo, wr_ref[...], preferred_element_type=F32)
    logits = (part_hi[:, :LANES] + (part_hi[:, LANES:] + part_lo[:, :LANES]) + part_lo[:, LANES:]
              + br_ref[...])
    lane = lax.broadcasted_iota(jnp.int32, (tm, LANES), 1)
    lane_f = lane.astype(F32)
    work = logits
    sels, vals, idxs = [], [], []
    for _ in range(TOP_K):
        v = jnp.max(work, axis=-1, keepdims=True)
        i = jnp.min(jnp.where(work == v, lane_f, float(LANES)), axis=-1, keepdims=True)
        sel = lane_f == i
        work = jnp.where(sel, -jnp.inf, work)
        sels.append(sel)
        vals.append(v)
        idxs.append(i)
    exps = [jnp.exp(v - vals[0]) for v in vals]
    total = exps[0] + exps[1] + exps[2] + exps[3]

    onehot = jnp.zeros((tm, LANES), F32)
    for sel in sels:
        onehot = jnp.where(sel, 1.0, onehot)
    r = lax.broadcasted_iota(jnp.int32, (tm, tm), 0)
    c = lax.broadcasted_iota(jnp.int32, (tm, tm), 1)
    below = jnp.where(c < r, 1.0, 0.0).astype(BF16)
    prefix = jnp.dot(below, onehot.astype(BF16), preferred_element_type=F32)
    rankmat = prefix + carry_ref[...]
    carry_ref[...] = carry_ref[...] + jnp.sum(onehot, axis=0, keepdims=True)
    count_ref[...] = carry_ref[...]

    idx_out = jnp.zeros((tm, LANES), F32)
    gate_out = jnp.zeros((tm, LANES), F32)
    rank_out = jnp.zeros((tm, LANES), F32)
    for k in range(TOP_K):
        rk = jnp.sum(jnp.where(sels[k], rankmat, 0.0), axis=-1, keepdims=True)
        idx_out = jnp.where(lane == k, idxs[k], idx_out)
        gate_out = jnp.where(lane == k, exps[k] / total, gate_out)
        rank_out = jnp.where(lane == k, rk, rank_out)
    gate_ref[...] = gate_out
    idx_ref[...] = idx_out.T[:SUBLANES, :].astype(jnp.int32)
    rank_ref[...] = rank_out.T[:SUBLANES, :].astype(jnp.int32)


def _route(x, wr, br):
    tm = ROUTE_TM
    wide = pl.BlockSpec((tm, LANES), lambda i: (i, 0))
    rows = pl.BlockSpec((SUBLANES, tm), lambda i: (0, i))
    return pl.pallas_call(
        _route_kernel,
        grid=(TOKENS // tm,),
        in_specs=[pl.BlockSpec((tm, D_MODEL), lambda i: (i, 0)),
                  pl.BlockSpec((D_MODEL, 2 * LANES), lambda i: (0, 0)),
                  pl.BlockSpec((1, LANES), lambda i: (0, 0))],
        out_specs=[rows, wide, rows, pl.BlockSpec((1, LANES), lambda i: (0, 0))],
        out_shape=[jax.ShapeDtypeStruct((SUBLANES, TOKENS), jnp.int32),
                   jax.ShapeDtypeStruct((TOKENS, LANES), F32),
                   jax.ShapeDtypeStruct((SUBLANES, TOKENS), jnp.int32),
                   jax.ShapeDtypeStruct((1, LANES), F32)],
        scratch_shapes=[pltpu.VMEM((1, LANES), F32)],
        compiler_params=_cparams(("arbitrary",)),
        name="route",
    )(x, wr, br)


HALF = D_MODEL // 2


def _pack_rows(x):
    return _pack_pair(x[:, :HALF], x[:, HALF:])


def _pack_pair(lo, hi):
    lo = lax.bitcast_convert_type(lo.astype(BF16).astype(F32), jnp.int32)
    hi = lax.bitcast_convert_type(hi.astype(BF16).astype(F32), jnp.int32)
    return hi | lax.shift_right_logical(lo, 16)


def _unpack_rows(w):
    lo = lax.bitcast_convert_type(lax.shift_left(w, 16), F32)
    hi = lax.bitcast_convert_type(w & jnp.int32(-65536), F32)
    return lo, hi


def _wait_row_copies(rows_ref, sem):
    pltpu.make_async_copy(rows_ref.at[pl.ds(0, ROW_TILE)], rows_ref.at[pl.ds(0, ROW_TILE)], sem).wait()


ROW_SUB = HALF // LANES
ROW_GROUPS = ROW_TILE // SUBLANES


FILL_SIZES = tuple(MOE_TM >> (b + 1) for b in range(MOE_TM.bit_length() - 1))


def _fill_copies(fill_start_ref, fill_len_ref, zero_ref, xs_ref, sem, act):
    def body(e, carry):
        pos = fill_start_ref[e]
        n = fill_len_ref[e]
        for size in FILL_SIZES:
            take = (n & size) != 0

            @pl.when(take)
            def _():
                cp = pltpu.make_async_copy(zero_ref.at[pl.ds(0, size)], xs_ref.at[pl.ds(pos, size)], sem)
                getattr(cp, act)()

            pos = pos + jnp.where(take, size, 0)
        return carry

    lax.fori_loop(0, N_EXPERTS, body, 0)

    def tail(j, carry):
        pos = fill_start_ref[N_EXPERTS] + j * FILL_SIZES[0]
        cp = pltpu.make_async_copy(zero_ref, xs_ref.at[pl.ds(pos, FILL_SIZES[0])], sem)
        getattr(cp, act)()
        return carry

    lax.fori_loop(0, fill_len_ref[N_EXPERTS], tail, 0)


def _scatter_kernel(fill_start_ref, fill_len_ref, dest_ref, x_ref, xs_ref, packed_ref, zero_ref, sem, fill_sem):
    i = pl.program_id(0)
    slot = i % 2

    @pl.when(i == 0)
    def _():
        zero_ref[...] = jnp.zeros_like(zero_ref)
        _fill_copies(fill_start_ref, fill_len_ref, zero_ref, xs_ref, fill_sem, "start")

    words = _pack_rows(x_ref[...])
    for g in range(ROW_GROUPS):
        for c in range(ROW_SUB):
            packed_ref[slot, g, c] = words[g * SUBLANES:(g + 1) * SUBLANES, c * LANES:(c + 1) * LANES]

    def body(g, carry):
        for s in range(SUBLANES):
            for k in range(TOP_K):
                d = dest_ref[0, 0, (g * SUBLANES + s) * TOP_K + k]
                pltpu.make_async_copy(packed_ref.at[slot, g, :, s, :], xs_ref.at[d],
                                      sem.at[slot]).start(priority=k % 2)
        return carry

    lax.fori_loop(0, ROW_GROUPS, body, 0)

    @pl.when(i > 0)
    def _():
        for k in range(TOP_K):
            _wait_row_copies(xs_ref, sem.at[1 - slot])

    @pl.when(i == pl.num_programs(0) - 1)
    def _():
        for k in range(TOP_K):
            _wait_row_copies(xs_ref, sem.at[slot])

    @pl.when(i == 0)
    def _():
        _fill_copies(fill_start_ref, fill_len_ref, zero_ref, xs_ref, fill_sem, "wait")


def _scatter_rows(x, dest, fill_start, fill_len):
    nt = TOKENS // ROW_TILE
    grid_spec = pltpu.PrefetchScalarGridSpec(
        num_scalar_prefetch=2,
        grid=(nt,),
        in_specs=[pl.BlockSpec((1, 1, ROW_TILE * TOP_K), lambda i, fs, fl: (i, 0, 0), memory_space=pltpu.SMEM),
                  pl.BlockSpec((ROW_TILE, D_MODEL), lambda i, fs, fl: (i, 0))],
        out_specs=pl.BlockSpec(memory_space=pl.ANY),
        scratch_shapes=[pltpu.VMEM((2, ROW_GROUPS, ROW_SUB, SUBLANES, LANES), jnp.int32),
                        pltpu.VMEM((FILL_SIZES[0], ROW_SUB, LANES), jnp.int32),
                        pltpu.SemaphoreType.DMA((2,)), pltpu.SemaphoreType.DMA(())],
    )
    return pl.pallas_call(
        _scatter_kernel,
        grid_spec=grid_spec,
        out_shape=jax.ShapeDtypeStruct((MOE_ROWS, ROW_SUB, LANES), jnp.int32),
        compiler_params=_cparams(("arbitrary",), has_side_effects=True),
        name="scatter_rows",
    )(fill_start, fill_len, dest, x)


MOE_NF = D_EXPERT // MOE_TF
OUT_WORDS = MOE_TF // 2
OUT_SUB = OUT_WORDS // LANES


def _out_word_cols(c):
    tile, j = divmod(c, OUT_SUB)
    lo = tile * MOE_TF + j * LANES
    return slice(lo, lo + LANES), slice(lo + OUT_WORDS, lo + OUT_WORDS + LANES)
MOE_ROW_STEP = 256


def _experts_kernel(ge_ref, gn_ref, de_ref, dn_ref, xs_ref, wg_ref, wu_ref, wd_ref, bgu_ref, bd_ref,
                    ys_ref, xb_ref, act_ref, words_ref, ow_ref, pend_ref, xsem, ysem):
    v = pl.program_id(0)
    f = pl.program_id(1)
    gn = gn_ref[v]
    dn = dn_ref[v]
    slot = v % 2

    def ys_copies(visit, tile):
        row0 = pl.multiple_of(visit * MOE_TM, MOE_TM)
        return [pltpu.make_async_copy(ow_ref.at[:, j * LANES:(j + 1) * LANES],
                                      ys_ref.at[pl.ds(row0, MOE_TM), tile * OUT_SUB + j, :], ysem)
                for j in range(OUT_SUB)]

    def emit_output_tile():
        for cp in ys_copies(v - 1, f):
            cp.start()
        pend_ref[0] = 1

    def drain_output_tile():
        @pl.when(pend_ref[0] == 1)
        def _():
            for cp in ys_copies(0, 0):
                cp.wait()
            pend_ref[0] = 0

    @pl.when(jnp.logical_and(v == 0, f == 0))
    def _():
        ow_ref[...] = jnp.zeros_like(ow_ref)
        pend_ref[0] = 0

    def xs_copies(visit, s):
        row0 = pl.multiple_of(visit * MOE_TM, MOE_TM)
        return [pltpu.make_async_copy(xs_ref.at[pl.ds(row0, MOE_TM), c, :],
                                      words_ref.at[s, :, c * LANES:(c + 1) * LANES], xsem.at[s])
                for c in range(ROW_SUB)]

    @pl.when(f == 0)
    def _():
        @pl.when(v == 0)
        def _():
            for cp in xs_copies(0, 0):
                cp.start()

        @pl.when(v < N_VISITS)
        def _():
            for cp in xs_copies(v, slot):
                cp.wait()

        @pl.when(v + 1 < N_VISITS)
        def _():
            for cp in xs_copies(v + 1, 1 - slot):
                cp.start()

    @pl.when(jnp.logical_and(gn > 0, f == 0))
    def _():
        lo, hi = _unpack_rows(words_ref[slot])
        xb_ref[:, :HALF] = lo.astype(BF16)
        xb_ref[:, HALF:] = hi.astype(BF16)

    def gate_phase(rows):
        xb = xb_ref[:rows, :]
        e = pl.ds(ge_ref[v], 1)
        col = pl.multiple_of(f * MOE_TF, MOE_TF)
        gl = jnp.dot(xb, wg_ref[...].astype(BF16), preferred_element_type=F32) + bgu_ref[e, pl.ds(col, MOE_TF)]
        up = (jnp.dot(xb, wu_ref[...].astype(BF16), preferred_element_type=F32)
              + bgu_ref[e, pl.ds(D_EXPERT + col, MOE_TF)])
        gl = jnp.minimum(gl, SWIGLU_LIMIT)
        up = jnp.clip(up, -SWIGLU_LIMIT, SWIGLU_LIMIT)
        act = ((up + 1.0) * (gl * jax.nn.sigmoid(SWIGLU_ALPHA * gl))).astype(BF16)
        act_ref[slot, :rows, pl.ds(pl.multiple_of(f * MOE_TF, MOE_TF), MOE_TF)] = act

    def down_phase(rows):
        drain_output_tile()
        y = jnp.dot(act_ref[1 - slot, :rows, :], wd_ref[...].astype(BF16),
                    preferred_element_type=F32) + bd_ref[pl.ds(de_ref[v], 1),
                                                         pl.ds(pl.multiple_of(f * MOE_TF, MOE_TF), MOE_TF)]
        ow_ref[:rows, :] = _pack_pair(y[:, :OUT_WORDS], y[:, OUT_WORDS:])
        emit_output_tile()

    for rows in range(MOE_TM, 0, -MOE_ROW_STEP):
        lower = rows - MOE_ROW_STEP
        pl.when(jnp.logical_and(gn > lower, gn <= rows))(functools.partial(gate_phase, rows))
        pl.when(jnp.logical_and(dn > lower, dn <= rows))(functools.partial(down_phase, rows))

    @pl.when(jnp.logical_and(dn == 0, v >= 1))
    def _():
        drain_output_tile()
        ow_ref[...] = jnp.zeros_like(ow_ref)
        emit_output_tile()

    @pl.when(jnp.logical_and(v == pl.num_programs(0) - 1, f == MOE_NF - 1))
    def _():
        drain_output_tile()


def _experts(xs, visit_expert, visit_nvalid, w_gate_up, b_gate_up, w_down, b_down):
    nf = MOE_NF
    gate_e = jnp.concatenate([visit_expert, visit_expert[-1:]])
    gate_n = jnp.concatenate([visit_nvalid, jnp.zeros((1,), jnp.int32)])
    down_e = jnp.concatenate([visit_expert[:1], visit_expert])
    down_n = jnp.concatenate([jnp.zeros((1,), jnp.int32), visit_nvalid])

    def tile(f, n):
        return jnp.where(n > 0, f, nf - 1)

    grid_spec = pltpu.PrefetchScalarGridSpec(
        num_scalar_prefetch=4,
        grid=(N_VISITS + 1, nf),
        in_specs=[
            pl.BlockSpec(memory_space=pl.ANY),
            pl.BlockSpec((None, D_MODEL, MOE_TF), lambda v, f, ge, gn, de, dn: (ge[v], 0, tile(f, gn[v]))),
            pl.BlockSpec((None, D_MODEL, MOE_TF), lambda v, f, ge, gn, de, dn: (ge[v], 0, nf + tile(f, gn[v]))),
            pl.BlockSpec((None, D_EXPERT, MOE_TF), lambda v, f, ge, gn, de, dn: (de[v], 0, tile(f, dn[v]))),
            pl.BlockSpec((N_EXPERTS, 2 * D_EXPERT), lambda v, f, ge, gn, de, dn: (0, 0)),
            pl.BlockSpec((N_EXPERTS, D_MODEL), lambda v, f, ge, gn, de, dn: (0, 0)),
        ],
        out_specs=pl.BlockSpec(memory_space=pl.ANY),
        scratch_shapes=[pltpu.VMEM((MOE_TM, D_MODEL), BF16),
                        pltpu.VMEM((2, MOE_TM, D_EXPERT), BF16),
                        pltpu.VMEM((2, MOE_TM, HALF), jnp.int32),
                        pltpu.VMEM((MOE_TM, OUT_WORDS), jnp.int32),
                        pltpu.SMEM((1,), jnp.int32),
                        pltpu.SemaphoreType.DMA((2,)),
                        pltpu.SemaphoreType.DMA(())],
    )
    return pl.pallas_call(
        _experts_kernel,
        grid_spec=grid_spec,
        out_shape=jax.ShapeDtypeStruct((MOE_ROWS, ROW_SUB, LANES), jnp.int32),
        compiler_params=_cparams(("arbitrary", "arbitrary")),
        name="experts",
    )(gate_e, gate_n, down_e, down_n, xs, w_gate_up, w_gate_up, w_down, b_gate_up, b_down)


def _combine_kernel(dest_ref, dest_next_ref, ys_ref, gate_ref, x_ref, lnw_ref, lnb_ref, o_ref,
                    buf_ref, z_ref, sem):
    i = pl.program_id(0)
    slot = i % 2

    def start_group(idx_ref, s, g):
        for t in range(SUBLANES):
            for k in range(TOP_K):
                d = idx_ref[0, 0, (g * SUBLANES + t) * TOP_K + k]
                pltpu.make_async_copy(ys_ref.at[d], buf_ref.at[s, k, g, :, t, :],
                                      sem.at[s]).start(priority=k % 2)

    @pl.when(i == 0)
    def _():
        def body(g, carry):
            start_group(dest_ref, slot, g)
            return carry

        lax.fori_loop(0, ROW_GROUPS, body, 0)

    for k in range(TOP_K):
        _wait_row_copies(ys_ref, sem.at[slot])

    def group(g, carry, prefetch):
        if prefetch:
            start_group(dest_next_ref, 1 - slot, g)
        rows = pl.ds(pl.multiple_of(g * SUBLANES, SUBLANES), SUBLANES)
        gates = gate_ref[rows, :]
        gate_k = [jnp.broadcast_to(gates[:, k:k + 1], (SUBLANES, LANES)) for k in range(TOP_K)]
        for c in range(ROW_SUB):
            lo_cols, hi_cols = _out_word_cols(c)
            z_lo = DN_ALPHA * x_ref[rows, lo_cols]
            z_hi = DN_ALPHA * x_ref[rows, hi_cols]
            for k in range(TOP_K):
                lo, hi = _unpack_rows(buf_ref[slot, k, g, c])
                z_lo = z_lo + gate_k[k] * lo
                z_hi = z_hi + gate_k[k] * hi
            z_ref[rows, lo_cols] = z_lo
            z_ref[rows, hi_cols] = z_hi
        return carry

    last = i + 1 == pl.num_programs(0)

    @pl.when(jnp.logical_not(last))
    def _():
        lax.fori_loop(0, ROW_GROUPS, functools.partial(group, prefetch=True), 0)

    @pl.when(last)
    def _():
        lax.fori_loop(0, ROW_GROUPS, functools.partial(group, prefetch=False), 0)

    o_ref[...] = _layer_norm(z_ref[...], lnw_ref[...], lnb_ref[...])


def _combine(ys, dest, gates, x, lnw, lnb):
    nt = TOKENS // ROW_TILE
    row = pl.BlockSpec((ROW_TILE, D_MODEL), lambda i: (i, 0))
    vec = pl.BlockSpec((1, D_MODEL), lambda i: (0, 0))
    dest_tiles = dest
    return pl.pallas_call(
        _combine_kernel,
        grid=(nt,),
        in_specs=[pl.BlockSpec((1, 1, ROW_TILE * TOP_K), lambda i: (i, 0, 0), memory_space=pltpu.SMEM),
                  pl.BlockSpec((1, 1, ROW_TILE * TOP_K), lambda i: (jnp.minimum(i + 1, nt - 1), 0, 0),
                               memory_space=pltpu.SMEM),
                  pl.BlockSpec(memory_space=pl.ANY),
                  pl.BlockSpec((ROW_TILE, LANES), lambda i: (i, 0)),
                  row, vec, vec],
        out_specs=row,
        out_shape=jax.ShapeDtypeStruct((TOKENS, D_MODEL), F32),
        scratch_shapes=[pltpu.VMEM((2, TOP_K, ROW_GROUPS, ROW_SUB, SUBLANES, LANES), jnp.int32),
                        pltpu.VMEM((ROW_TILE, D_MODEL), F32), pltpu.SemaphoreType.DMA((2,))],
        compiler_params=_cparams(("arbitrary",)),
        name="combine",
    )(dest_tiles, dest_tiles, ys, gates, x, lnw, lnb)


def _layer(x, mem, w_in, attn_sinks, attn_norm_w, ret_norm_w, w_mix_out, ln_mix_w, ln_mix_b,
           w_mem_q, w_mem_kv, w_mem_out, ln_mem_w, ln_mem_b, w_router, b_router,
           w_gate_up, b_gate_up, w_down, b_down, ln_moe_w, ln_moe_b):
    o_k, o_v, o_qr = D_ATTN, D_ATTN + D_KV, D_ATTN + 2 * D_KV
    w_in_p = jnp.concatenate([w_in[:, :o_k], w_in[:, o_qr:], w_in[:, o_k:o_v], w_in[:, o_v:o_qr]],
                             axis=1).astype(BF16)
    proj = _project(x, w_in_p, PROJ_TM, PROJ_TN)
    y = _mixer(proj, attn_sinks.astype(F32), attn_norm_w.reshape(1, D_ATTN), ret_norm_w.reshape(1, D_RET))
    x1 = _out_ln(y, w_mix_out.astype(BF16), x, ln_mix_w.reshape(1, D_MODEL), ln_mix_b.reshape(1, D_MODEL))

    kv = _project(mem.reshape(BATCH * N_MEM, D_MODEL), w_mem_kv.astype(BF16), BATCH * N_MEM, D_MODEL)
    o = _mem_attn(x1, w_mem_q.astype(BF16), kv)
    x2 = _out_ln(o, w_mem_out.astype(BF16), x1, ln_mem_w.reshape(1, D_MODEL), ln_mem_b.reshape(1, D_MODEL))

    wr = jnp.pad(w_router, ((0, 0), (0, LANES - N_EXPERTS)))
    wr_hi = wr.astype(BF16)
    wr = jnp.concatenate([wr_hi, (wr - wr_hi.astype(F32)).astype(BF16)], axis=1)
    br = jnp.pad(b_router.reshape(1, N_EXPERTS), ((0, 0), (0, LANES - N_EXPERTS)), constant_values=NEG_INF)
    idx, gates, rank, counts = _route(x2, wr, br)

    counts = counts[0, :N_EXPERTS].astype(jnp.int32)
    padded = ((counts + MOE_TM - 1) // MOE_TM) * MOE_TM
    eid = jnp.arange(N_EXPERTS, dtype=jnp.int32)
    pad_end = jnp.sum(jnp.where(eid[None, :] <= eid[:, None], padded[None, :], 0), axis=1)
    pad_start = pad_end - padded
    chosen = idx[:TOP_K, :, None] == eid[None, None, :]
    dest_kt = jnp.sum(jnp.where(chosen, pad_start[None, None, :], 0), axis=-1) + rank[:TOP_K]
    nt = TOKENS // ROW_TILE
    dest = dest_kt.reshape(TOP_K, nt, ROW_TILE).transpose(1, 2, 0).reshape(nt, 1, ROW_TILE * TOP_K)
    visit_row = jnp.arange(N_VISITS, dtype=jnp.int32) * MOE_TM
    visit_expert = jnp.minimum(jnp.sum((pad_end[None, :] <= visit_row[:, None]).astype(jnp.int32), axis=1),
                               N_EXPERTS - 1)
    of_visit = visit_expert[:, None] == eid[None, :]
    visit_count = jnp.sum(jnp.where(of_visit, counts[None, :], 0), axis=1)
    visit_start = jnp.sum(jnp.where(of_visit, pad_start[None, :], 0), axis=1)
    visit_nvalid = jnp.clip(visit_count - (visit_row - visit_start), 0, MOE_TM).astype(jnp.int32)

    fill_start = jnp.concatenate([pad_start + counts, pad_end[-1:]]).astype(jnp.int32)
    fill_len = jnp.concatenate([padded - counts, (MOE_ROWS - pad_end[-1:]) // FILL_SIZES[0]]).astype(jnp.int32)
    xs = _scatter_rows(x2, dest, fill_start, fill_len)
    ys = _experts(xs, visit_expert, visit_nvalid,
                  w_gate_up, b_gate_up, w_down, b_down)
    return _combine(ys, dest, gates, x2, ln_moe_w.reshape(1, D_MODEL), ln_moe_b.reshape(1, D_MODEL))


def kernel(x, mem, w_in, attn_sinks, attn_norm_w, ret_norm_w, w_mix_out, ln_mix_w, ln_mix_b, w_mem_q, w_mem_kv, w_mem_out, ln_mem_w, ln_mem_b, w_router, b_router, w_gate_up, b_gate_up, w_down, b_down, ln_moe_w, ln_moe_b):
    xt = x.reshape(TOKENS, D_MODEL)
    out = _layer(xt, mem, w_in[0], attn_sinks[0], attn_norm_w[0], ret_norm_w[0], w_mix_out[0],
                 ln_mix_w[0], ln_mix_b[0], w_mem_q[0], w_mem_kv[0], w_mem_out[0], ln_mem_w[0], ln_mem_b[0],
                 w_router[0], b_router[0], w_gate_up[0], b_gate_up[0], w_down[0], b_down[0],
                 ln_moe_w[0], ln_moe_b[0])
    return out.reshape(BATCH, SEQ, D_MODEL)
```
